```python
import math
import jax, jax.numpy as jnp
from jax import lax
import numpy as np

D_MODEL = 4096
BATCH = 8
SEQ = 4096
DEPTH = 1

N_META = 16
HEAD_DIM = 64
N_Q_HEADS = 32
N_KV_HEADS = 4
Q_PER_KV = N_Q_HEADS // N_KV_HEADS
ATTN_WIDTH = N_Q_HEADS * HEAD_DIM
KV_WIDTH = N_KV_HEADS * HEAD_DIM
WINDOW = 128
BLOCK = 128
ATTN_SCALE = HEAD_DIM ** -0.5
ROPE_DIM = HEAD_DIM // 4
ROPE_THETA = 500000.0
NEG_INF = -1e30
POOL_WINDOWS = (2, 4, 8, 16)
N_POOL_GROUPS = len(POOL_WINDOWS)
POOL_WIDTH = D_MODEL // 2
POOL_GROUP_WIDTH = POOL_WIDTH // N_POOL_GROUPS
N_BRANCHES = 2
IN_WIDTH = ATTN_WIDTH + 2 * KV_WIDTH + POOL_WIDTH + N_BRANCHES * D_MODEL
SPLITS = [ATTN_WIDTH, ATTN_WIDTH + KV_WIDTH, ATTN_WIDTH + 2 * KV_WIDTH,
          ATTN_WIDTH + 2 * KV_WIDTH + POOL_WIDTH]
D_FF = -(-8 * D_MODEL // 768) * 256
DN_ALPHA = (2 * DEPTH) ** 0.25
DN_BETA = (8 * DEPTH) ** -0.25
LN_EPS = 1e-5

kernel_name = "hybrid_swa_sinks_multiscale_pool_gated_deepnorm"


def layer_norm(x, g, b):
    xf = x.astype(jnp.float32)
    mu = xf.mean(-1, keepdims=True)
    var = jnp.square(xf - mu).mean(-1, keepdims=True)
    y = (xf - mu) * lax.rsqrt(var + LN_EPS)
    return (y * g.astype(jnp.float32) + b.astype(jnp.float32)).astype(x.dtype)


def partial_rope(x, pos):
    half = ROPE_DIM // 2
    inv_freq = ROPE_THETA ** (-jnp.arange(half, dtype=jnp.float32) * 2.0 / ROPE_DIM)
    ang = pos.astype(jnp.float32)[:, None] * inv_freq[None, :]
    cos = jnp.cos(ang)[None, :, None, :]
    sin = jnp.sin(ang)[None, :, None, :]
    xr = x[..., :ROPE_DIM].astype(jnp.float32)
    x1, x2 = xr[..., :half], xr[..., half:]
    rot = jnp.concatenate([x1 * cos - x2 * sin, x2 * cos + x1 * sin], axis=-1).astype(x.dtype)
    return jnp.concatenate([rot, x[..., ROPE_DIM:]], axis=-1)


def _band_blocks(a, nb):
    B = a.shape[0]
    ab = a.reshape(B, nb, BLOCK, N_KV_HEADS, HEAD_DIM)
    prev = jnp.pad(ab[:, :-1], ((0, 0), (1, 0), (0, 0), (0, 0), (0, 0)))
    return jnp.concatenate([prev, ab], axis=2)


def sliding_window_attention(q, k, v, sinks):
    B, T = q.shape[0], q.shape[1]
    lead = (-N_META) % BLOCK
    tail = (-(lead + T)) % BLOCK
    Tp = lead + T + tail
    nb = Tp // BLOCK
    pad = ((0, 0), (lead, tail), (0, 0), (0, 0))
    qb = jnp.pad(q, pad).reshape(B, nb, BLOCK, N_KV_HEADS, Q_PER_KV, HEAD_DIM)
    kb = _band_blocks(jnp.pad(k, pad), nb)
    vb = _band_blocks(jnp.pad(v, pad), nb)
    k_meta, v_meta = k[:, :N_META], v[:, :N_META]

    s_band = jnp.einsum('bnqkgd,bnskd->bnkgqs', qb, kb).astype(jnp.float32) * ATTN_SCALE
    s_meta = jnp.einsum('bnqkgd,bmkd->bnkgqm', qb, k_meta).astype(jnp.float32) * ATTN_SCALE

    blk = jnp.arange(nb)
    q_idx = blk[:, None] * BLOCK + jnp.arange(BLOCK)[None, :]
    k_idx = (blk[:, None] - 1) * BLOCK + jnp.arange(2 * BLOCK)[None, :]
    diff = q_idx[:, :, None] - k_idx[:, None, :]
    band_ok = (diff >= 0) & (diff < WINDOW) & (k_idx[:, None, :] >= lead + N_META)
    meta_ok = q_idx[:, :, None] >= lead + jnp.arange(N_META)[None, None, :]
    s_band = jnp.where(band_ok[None, :, None, None], s_band, NEG_INF)
    s_meta = jnp.where(meta_ok[None, :, None, None], s_meta, NEG_INF)

    sink = sinks.astype(jnp.float32).reshape(N_KV_HEADS, Q_PER_KV)[None, None, :, :, None, None]
    m = jnp.maximum(jnp.maximum(s_band.max(-1, keepdims=True), s_meta.max(-1, keepdims=True)), sink)
    p_band = jnp.exp(s_band - m)
    p_meta = jnp.exp(s_meta - m)
    inv = 1.0 / (p_band.sum(-1, keepdims=True) + p_meta.sum(-1, keepdims=True) + jnp.exp(sink - m))
    o = (jnp.einsum('bnkgqs,bnskd->bnqkgd', (p_band * inv).astype(v.dtype), vb)
         + jnp.einsum('bnkgqm,bmkd->bnqkgd', (p_meta * inv).astype(v.dtype), v_meta))
    return o.reshape(B, Tp, ATTN_WIDTH)[:, lead:lead + T]


def multiscale_pool(u, w_grp, scale):
    B, T = u.shape[0], u.shape[1]
    ug = u.reshape(B, T, N_POOL_GROUPS, POOL_GROUP_WIDTH)
    cs = jnp.cumsum(ug.astype(jnp.float32), axis=1)
    cs = jnp.pad(cs, ((0, 0), (1, 0), (0, 0), (0, 0)))
    t = jnp.arange(T)
    outs = []
    for g, w in enumerate(POOL_WINDOWS):
        csg = cs[:, :, g]
        start = jnp.maximum(t + 1 - w, 0)
        win_sum = csg[:, 1:] - csg[:, start]
        count = jnp.minimum(t + 1, w).astype(jnp.float32)[None, :, None]
        outs.append(win_sum / count - ug[:, :, g].astype(jnp.float32))
    pooled = jnp.stack(outs, axis=2).astype(u.dtype)
    mixed = jnp.einsum('btgc,gcd->btgd', pooled, w_grp)
    return mixed.reshape(B, T, POOL_WIDTH) * scale


def _fwd_setup_inputs(seed: int = 0) -> dict:
    key = jax.random.key(seed)
    ks = jax.random.split(key, 20)
    f32 = jnp.float32

    def nrm(k, shape, s):
        return jax.random.normal(k, shape, f32) * s

    return {
        "x": nrm(ks[0], (BATCH, SEQ, D_MODEL), 1.0),
        "meta_tokens": nrm(ks[1], (N_META, D_MODEL), 1.0),
        "ln_in_g": 1.0 + nrm(ks[2], (D_MODEL,), 0.02),
        "ln_in_b": nrm(ks[3], (D_MODEL,), 0.02),
        "w_in": nrm(ks[4], (DEPTH, D_MODEL, IN_WIDTH), D_MODEL ** -0.5),
        "b_gate": nrm(ks[5], (DEPTH, N_BRANCHES, D_MODEL), 0.1),
        "attn_sinks": nrm(ks[6], (DEPTH, N_Q_HEADS), 0.5),
        "w_attn_up": nrm(ks[7], (DEPTH, ATTN_WIDTH, D_MODEL), ATTN_WIDTH ** -0.5),
        "w_pool_grp": nrm(ks[8], (DEPTH, N_POOL_GROUPS, POOL_GROUP_WIDTH, POOL_GROUP_WIDTH), POOL_GROUP_WIDTH ** -0.5),
        "pool_scale": 1.0 + nrm(ks[9], (DEPTH, POOL_WIDTH), 0.02),
        "w_pool_up": nrm(ks[10], (DEPTH, POOL_WIDTH, D_MODEL), POOL_WIDTH ** -0.5),
        "w_out": nrm(ks[11], (DEPTH, D_MODEL, D_MODEL), DN_BETA * D_MODEL ** -0.5),
        "ln1_g": 1.0 + nrm(ks[12], (DEPTH, D_MODEL), 0.02),
        "ln1_b": nrm(ks[13], (DEPTH, D_MODEL), 0.02),
        "w_ffn_in": nrm(ks[14], (DEPTH, D_MODEL, 2 * D_FF), D_MODEL ** -0.5),
        "w_ffn_down": nrm(ks[15], (DEPTH, D_FF, D_MODEL), DN_BETA * D_FF ** -0.5),
        "ln2_g": 1.0 + nrm(ks[16], (DEPTH, D_MODEL), 0.02),
        "ln2_b": nrm(ks[17], (DEPTH, D_MODEL), 0.02),
    }


def _fwd_reference(x, meta_tokens, ln_in_g, ln_in_b, w_in, b_gate, attn_sinks, w_attn_up,
              w_pool_grp, pool_scale, w_pool_up, w_out, ln1_g, ln1_b, w_ffn_in,
              w_ffn_down, ln2_g, ln2_b):
    B = x.shape[0]
    meta = jnp.broadcast_to(meta_tokens[None].astype(x.dtype), (B, N_META, D_MODEL))
    h = layer_norm(jnp.concatenate([meta, x], axis=1), ln_in_g, ln_in_b)
    T = h.shape[1]
    pos = jnp.arange(T)

    for l in range(DEPTH):
        proj = h @ w_in[l]
        q, k, v, u, gate_logits = jnp.split(proj, SPLITS, axis=-1)
        q = partial_rope(q.reshape(B, T, N_Q_HEADS, HEAD_DIM), pos)
        k = partial_rope(k.reshape(B, T, N_KV_HEADS, HEAD_DIM), pos)
        v = v.reshape(B, T, N_KV_HEADS, HEAD_DIM)

        a_out = sliding_window_attention(q, k, v, attn_sinks[l]) @ w_attn_up[l]
        p_out = multiscale_pool(u, w_pool_grp[l], pool_scale[l]) @ w_pool_up[l]

        gates = jax.nn.sigmoid(gate_logits.reshape(B, T, N_BRANCHES, D_MODEL) + b_gate[l])
        mixed = gates[:, :, 0] * a_out + gates[:, :, 1] * p_out
        h = layer_norm(DN_ALPHA * h + mixed @ w_out[l], ln1_g[l], ln1_b[l])

        f_gate, f_up = jnp.split(h @ w_ffn_in[l], 2, axis=-1)
        ffn = (jax.nn.silu(f_gate) * f_up) @ w_ffn_down[l]
        h = layer_norm(DN_ALPHA * h + ffn, ln2_g[l], ln2_b[l])

    return h[:, N_META:]


import jax as _jax
import jax.numpy as _jnp

TWIN_FORMAT = 'train_step'
FWD_PARAMS = ['x', 'meta_tokens', 'ln_in_g', 'ln_in_b', 'w_in', 'b_gate', 'attn_sinks', 'w_attn_up', 'w_pool_grp', 'pool_scale', 'w_pool_up', 'w_out', 'ln1_g', 'ln1_b', 'w_ffn_in', 'w_ffn_down', 'ln2_g', 'ln2_b']
TWIN_WEIGHTS = ['meta_tokens', 'ln_in_g', 'ln_in_b', 'w_in', 'b_gate', 'attn_sinks', 'w_attn_up', 'w_pool_grp', 'pool_scale', 'w_pool_up', 'w_out', 'ln1_g', 'ln1_b', 'w_ffn_in', 'w_ffn_down', 'ln2_g', 'ln2_b']
TWIN_DIFF_INPUT = 'x'
TWIN_INPUTS = ['x', 'meta_tokens', 'ln_in_g', 'ln_in_b', 'w_in', 'b_gate', 'attn_sinks', 'w_attn_up', 'w_pool_grp', 'pool_scale', 'w_pool_up', 'w_out', 'ln1_g', 'ln1_b', 'w_ffn_in', 'w_ffn_down', 'ln2_g', 'ln2_b', 'loss_target', 'm_meta_tokens', 'm_ln_in_g', 'm_ln_in_b', 'm_w_in', 'm_b_gate', 'm_attn_sinks', 'm_w_attn_up', 'm_w_pool_grp', 'm_pool_scale', 'm_w_pool_up', 'm_w_out', 'm_ln1_g', 'm_ln1_b', 'm_w_ffn_in', 'm_w_ffn_down', 'm_ln2_g', 'm_ln2_b', 'v_meta_tokens', 'v_ln_in_g', 'v_ln_in_b', 'v_w_in', 'v_b_gate', 'v_attn_sinks', 'v_w_attn_up', 'v_w_pool_grp', 'v_pool_scale', 'v_w_pool_up', 'v_w_out', 'v_ln1_g', 'v_ln1_b', 'v_w_ffn_in', 'v_w_ffn_down', 'v_ln2_g', 'v_ln2_b']
TWIN_OUTPUTS = ['loss', 'grad_x', 'grad_meta_tokens', 'grad_ln_in_g', 'grad_ln_in_b', 'grad_w_in', 'grad_b_gate', 'grad_attn_sinks', 'grad_w_attn_up', 'grad_w_pool_grp', 'grad_pool_scale', 'grad_w_pool_up', 'grad_w_out', 'grad_ln1_g', 'grad_ln1_b', 'grad_w_ffn_in', 'grad_w_ffn_down', 'grad_ln2_g', 'grad_ln2_b', 'delta_meta_tokens', 'delta_ln_in_g', 'delta_ln_in_b', 'delta_w_in', 'delta_b_gate', 'delta_attn_sinks', 'delta_w_attn_up', 'delta_w_pool_grp', 'delta_pool_scale', 'delta_w_pool_up', 'delta_w_out', 'delta_ln1_g', 'delta_ln1_b', 'delta_w_ffn_in', 'delta_w_ffn_down', 'delta_ln2_g', 'delta_ln2_b', 'new_m_meta_tokens', 'new_m_ln_in_g', 'new_m_ln_in_b', 'new_m_w_in', 'new_m_b_gate', 'new_m_attn_sinks', 'new_m_w_attn_up', 'new_m_w_pool_grp', 'new_m_pool_scale', 'new_m_w_pool_up', 'new_m_w_out', 'new_m_ln1_g', 'new_m_ln1_b', 'new_m_w_ffn_in', 'new_m_w_ffn_down', 'new_m_ln2_g', 'new_m_ln2_b', 'new_v_meta_tokens', 'new_v_ln_in_g', 'new_v_ln_in_b', 'new_v_w_in', 'new_v_b_gate', 'new_v_attn_sinks', 'new_v_w_attn_up', 'new_v_w_pool_grp', 'new_v_pool_scale', 'new_v_w_pool_up', 'new_v_w_out', 'new_v_ln1_g', 'new_v_ln1_b', 'new_v_w_ffn_in', 'new_v_w_ffn_down', 'new_v_ln2_g', 'new_v_ln2_b']
TWIN_LEAF_KINDS = {'loss': 'loss', 'grad_x': 'grad_x', 'grad_meta_tokens': 'grad_w', 'grad_ln_in_g': 'grad_w', 'grad_ln_in_b': 'grad_w', 'grad_w_in': 'grad_w', 'grad_b_gate': 'grad_w', 'grad_attn_sinks': 'grad_w', 'grad_w_attn_up': 'grad_w', 'grad_w_pool_grp': 'grad_w', 'grad_pool_scale': 'grad_w', 'grad_w_pool_up': 'grad_w', 'grad_w_out': 'grad_w', 'grad_ln1_g': 'grad_w', 'grad_ln1_b': 'grad_w', 'grad_w_ffn_in': 'grad_w', 'grad_w_ffn_down': 'grad_w', 'grad_ln2_g': 'grad_w', 'grad_ln2_b': 'grad_w', 'delta_meta_tokens': 'delta_w', 'delta_ln_in_g': 'delta_w', 'delta_ln_in_b': 'delta_w', 'delta_w_in': 'delta_w', 'delta_b_gate': 'delta_w', 'delta_attn_sinks': 'delta_w', 'delta_w_attn_up': 'delta_w', 'delta_w_pool_grp': 'delta_w', 'delta_pool_scale': 'delta_w', 'delta_w_pool_up': 'delta_w', 'delta_w_out': 'delta_w', 'delta_ln1_g': 'delta_w', 'delta_ln1_b': 'delta_w', 'delta_w_ffn_in': 'delta_w', 'delta_w_ffn_down': 'delta_w', 'delta_ln2_g': 'delta_w', 'delta_ln2_b': 'delta_w', 'new_m_meta_tokens': 'new_m', 'new_m_ln_in_g': 'new_m', 'new_m_ln_in_b': 'new_m', 'new_m_w_in': 'new_m', 'new_m_b_gate': 'new_m', 'new_m_attn_sinks': 'new_m', 'new_m_w_attn_up': 'new_m', 'new_m_w_pool_grp': 'new_m', 'new_m_pool_scale': 'new_m', 'new_m_w_pool_up': 'new_m', 'new_m_w_out': 'new_m', 'new_m_ln1_g': 'new_m', 'new_m_ln1_b': 'new_m', 'new_m_w_ffn_in': 'new_m', 'new_m_w_ffn_down': 'new_m', 'new_m_ln2_g': 'new_m', 'new_m_ln2_b': 'new_m', 'new_v_meta_tokens': 'new_v', 'new_v_ln_in_g': 'new_v', 'new_v_ln_in_b': 'new_v', 'new_v_w_in': 'new_v', 'new_v_b_gate': 'new_v', 'new_v_attn_sinks': 'new_v', 'new_v_w_attn_up': 'new_v', 'new_v_w_pool_grp': 'new_v', 'new_v_pool_scale': 'new_v', 'new_v_w_pool_up': 'new_v', 'new_v_w_out': 'new_v', 'new_v_ln1_g': 'new_v', 'new_v_ln1_b': 'new_v', 'new_v_w_ffn_in': 'new_v', 'new_v_w_ffn_down': 'new_v', 'new_v_ln2_g': 'new_v', 'new_v_ln2_b': 'new_v'}


def _forward(args):
    return _fwd_reference(*[args[k] for k in FWD_PARAMS])


def _output_shape():
    out = _jax.eval_shape(lambda: _forward(_fwd_setup_inputs(0)))
    return out.shape, out.dtype

N_MICROBATCH = 1
ADAM_LR = 0.001
ADAM_B1 = 0.9
ADAM_B2 = 0.999
ADAM_EPS = 1e-08
ADAM_WD = 0.01
ADAM_STEP = 10
PER_EXAMPLE_BATCH_AXIS = {'x': 0, 'loss_target': 0}
SHARED_INPUTS = []
_WEIGHT_DTYPES = {'meta_tokens': _jnp.float32, 'ln_in_g': _jnp.float32, 'ln_in_b': _jnp.float32, 'w_in': _jnp.float32, 'b_gate': _jnp.float32, 'attn_sinks': _jnp.float32, 'w_attn_up': _jnp.float32, 'w_pool_grp': _jnp.float32, 'pool_scale': _jnp.float32, 'w_pool_up': _jnp.float32, 'w_out': _jnp.float32, 'ln1_g': _jnp.float32, 'ln1_b': _jnp.float32, 'w_ffn_in': _jnp.float32, 'w_ffn_down': _jnp.float32, 'ln2_g': _jnp.float32, 'ln2_b': _jnp.float32}
MOMENT_SCALE = {'meta_tokens': 3.970395e-04, 'ln_in_g': 2.150501e-01, 'ln_in_b': 1.088467e-01, 'w_in': 6.754144e-03, 'b_gate': 3.080787e-03, 'attn_sinks': 4.009434e-04, 'w_attn_up': 2.119024e-03, 'w_pool_grp': 1.505464e-02, 'pool_scale': 1.537476e-02, 'w_pool_up': 1.065972e-02, 'w_out': 1.820674e-02, 'ln1_g': 2.295673e-01, 'ln1_b': 1.078747e-01, 'w_ffn_in': 7.842273e-03, 'w_ffn_down': 2.130493e-02, 'ln2_g': 7.994269e+00, 'ln2_b': 2.008709e-01}


def _to_microbatches(a, axis):
    t = _jnp.moveaxis(a, axis, 0)
    t = t.reshape((N_MICROBATCH, t.shape[0] // N_MICROBATCH) + t.shape[1:])
    return _jnp.moveaxis(t, 1, axis + 1)


def setup_inputs(seed: int = 0) -> dict:
    inp = _fwd_setup_inputs(seed)
    key = _jax.random.fold_in(_jax.random.key(seed), 7919)
    shape, _ = _output_shape()
    out = dict(inp)
    out["loss_target"] = _jax.random.normal(_jax.random.fold_in(key, 0), shape, _jnp.float32)
    for i, name in enumerate(TWIN_WEIGHTS):
        w = inp[name].astype(_jnp.float32)
        if MOMENT_SCALE is None:
            s = _jnp.sqrt(_jnp.mean(_jnp.square(w)) + 1e-30)
        else:
            s = MOMENT_SCALE[name]
        km, kv = _jax.random.split(_jax.random.fold_in(key, i + 1))
        out[name] = w
        out["m_" + name] = s * _jax.random.normal(km, w.shape, _jnp.float32)
        out["v_" + name] = (s * s) * _jax.random.uniform(kv, w.shape, _jnp.float32, 0.5, 1.5)
    if N_MICROBATCH > 1:
        for name, axis in PER_EXAMPLE_BATCH_AXIS.items():
            out[name] = _to_microbatches(out[name], axis)
    return {'x': out['x'], 'meta_tokens': out['meta_tokens'], 'ln_in_g': out['ln_in_g'], 'ln_in_b': out['ln_in_b'], 'w_in': out['w_in'], 'b_gate': out['b_gate'], 'attn_sinks': out['attn_sinks'], 'w_attn_up': out['w_attn_up'], 'w_pool_grp': out['w_pool_grp'], 'pool_scale': out['pool_scale'], 'w_pool_up': out['w_pool_up'], 'w_out': out['w_out'], 'ln1_g': out['ln1_g'], 'ln1_b': out['ln1_b'], 'w_ffn_in': out['w_ffn_in'], 'w_ffn_down': out['w_ffn_down'], 'ln2_g': out['ln2_g'], 'ln2_b': out['ln2_b'], 'loss_target': out['loss_target'], 'm_meta_tokens': out['m_meta_tokens'], 'm_ln_in_g': out['m_ln_in_g'], 'm_ln_in_b': out['m_ln_in_b'], 'm_w_in': out['m_w_in'], 'm_b_gate': out['m_b_gate'], 'm_attn_sinks': out['m_attn_sinks'], 'm_w_attn_up': out['m_w_attn_up'], 'm_w_pool_grp': out['m_w_pool_grp'], 'm_pool_scale': out['m_pool_scale'], 'm_w_pool_up': out['m_w_pool_up'], 'm_w_out': out['m_w_out'], 'm_ln1_g': out['m_ln1_g'], 'm_ln1_b': out['m_ln1_b'], 'm_w_ffn_in': out['m_w_ffn_in'], 'm_w_ffn_down': out['m_w_ffn_down'], 'm_ln2_g': out['m_ln2_g'], 'm_ln2_b': out['m_ln2_b'], 'v_meta_tokens': out['v_meta_tokens'], 'v_ln_in_g': out['v_ln_in_g'], 'v_ln_in_b': out['v_ln_in_b'], 'v_w_in': out['v_w_in'], 'v_b_gate': out['v_b_gate'], 'v_attn_sinks': out['v_attn_sinks'], 'v_w_attn_up': out['v_w_attn_up'], 'v_w_pool_grp': out['v_w_pool_grp'], 'v_pool_scale': out['v_pool_scale'], 'v_w_pool_up': out['v_w_pool_up'], 'v_w_out': out['v_w_out'], 'v_ln1_g': out['v_ln1_g'], 'v_ln1_b': out['v_ln1_b'], 'v_w_ffn_in': out['v_w_ffn_in'], 'v_w_ffn_down': out['v_w_ffn_down'], 'v_ln2_g': out['v_ln2_g'], 'v_ln2_b': out['v_ln2_b']}


def _loss(weights, diff, rest, loss_target):
    with _jax.named_scope("forward"):
        args = {**rest, TWIN_DIFF_INPUT: diff, **{k: w.astype(_WEIGHT_DTYPES[k]) for k, w in weights.items()}}
        y = _forward(args)
    with _jax.named_scope("loss_head"):
        err = _jnp.square(y.astype(_jnp.float32) - loss_target)
        return 0.5 * _jnp.sum(_jnp.mean(err, axis=-1)) if err.ndim else 0.5 * err


def _adamw(w, g, m, v):
    m = ADAM_B1 * m + (1.0 - ADAM_B1) * g
    v = ADAM_B2 * v + (1.0 - ADAM_B2) * _jnp.square(g)
    m_hat = m / (1.0 - ADAM_B1 ** ADAM_STEP)
    v_hat = v / (1.0 - ADAM_B2 ** ADAM_STEP)
    delta = -ADAM_LR * (m_hat / (_jnp.sqrt(v_hat) + ADAM_EPS) + ADAM_WD * w)
    return delta, m, v


def reference(x, meta_tokens, ln_in_g, ln_in_b, w_in, b_gate, attn_sinks, w_attn_up, w_pool_grp, pool_scale, w_pool_up, w_out, ln1_g, ln1_b, w_ffn_in, w_ffn_down, ln2_g, ln2_b, loss_target, m_meta_tokens, m_ln_in_g, m_ln_in_b, m_w_in, m_b_gate, m_attn_sinks, m_w_attn_up, m_w_pool_grp, m_pool_scale, m_w_pool_up, m_w_out, m_ln1_g, m_ln1_b, m_w_ffn_in, m_w_ffn_down, m_ln2_g, m_ln2_b, v_meta_tokens, v_ln_in_g, v_ln_in_b, v_w_in, v_b_gate, v_attn_sinks, v_w_attn_up, v_w_pool_grp, v_pool_scale, v_w_pool_up, v_w_out, v_ln1_g, v_ln1_b, v_w_ffn_in, v_w_ffn_down, v_ln2_g, v_ln2_b):
    given = dict(x=x, meta_tokens=meta_tokens, ln_in_g=ln_in_g, ln_in_b=ln_in_b, w_in=w_in, b_gate=b_gate, attn_sinks=attn_sinks, w_attn_up=w_attn_up, w_pool_grp=w_pool_grp, pool_scale=pool_scale, w_pool_up=w_pool_up, w_out=w_out, ln1_g=ln1_g, ln1_b=ln1_b, w_ffn_in=w_ffn_in, w_ffn_down=w_ffn_down, ln2_g=ln2_g, ln2_b=ln2_b, loss_target=loss_target, m_meta_tokens=m_meta_tokens, m_ln_in_g=m_ln_in_g, m_ln_in_b=m_ln_in_b, m_w_in=m_w_in, m_b_gate=m_b_gate, m_attn_sinks=m_attn_sinks, m_w_attn_up=m_w_attn_up, m_w_pool_grp=m_w_pool_grp, m_pool_scale=m_pool_scale, m_w_pool_up=m_w_pool_up, m_w_out=m_w_out, m_ln1_g=m_ln1_g, m_ln1_b=m_ln1_b, m_w_ffn_in=m_w_ffn_in, m_w_ffn_down=m_w_ffn_down, m_ln2_g=m_ln2_g, m_ln2_b=m_ln2_b, v_meta_tokens=v_meta_tokens, v_ln_in_g=v_ln_in_g, v_ln_in_b=v_ln_in_b, v_w_in=v_w_in, v_b_gate=v_b_gate, v_attn_sinks=v_attn_sinks, v_w_attn_up=v_w_attn_up, v_w_pool_grp=v_w_pool_grp, v_pool_scale=v_pool_scale, v_w_pool_up=v_w_pool_up, v_w_out=v_w_out, v_ln1_g=v_ln1_g, v_ln1_b=v_ln1_b, v_w_ffn_in=v_w_ffn_in, v_w_ffn_down=v_w_ffn_down, v_ln2_g=v_ln2_g, v_ln2_b=v_ln2_b)
    weights = {n: given[n] for n in TWIN_WEIGHTS}
    shared = {n: given[n] for n in SHARED_INPUTS}
    per_example = {n: given[n] for n in ['x']}
    grad_fn = _jax.value_and_grad(_loss, argnums=(0, 1))

    def one_microbatch(ex, loss_target):
        ex = dict(ex)
        diff = ex.pop(TWIN_DIFF_INPUT)
        return grad_fn(weights, diff, {**shared, **ex}, loss_target)

    if N_MICROBATCH == 1:
        loss, (grad_w, grad_x) = one_microbatch(per_example, given["loss_target"])
    else:
        def body(carry, xs):
            loss_sum, grad_sum = carry
            l_k, (gw_k, gx_k) = one_microbatch(xs[0], xs[1])
            with _jax.named_scope("update"):
                return (loss_sum + l_k, _jax.tree.map(_jnp.add, grad_sum, gw_k)), gx_k

        init = (_jnp.zeros((), _jnp.float32), _jax.tree.map(_jnp.zeros_like, weights))
        (loss, grad_w), grad_x = _jax.lax.scan(body, init, (per_example, given["loss_target"]))
    with _jax.named_scope("update"):
        delta_w, new_m, new_v = {}, {}, {}
        for n in TWIN_WEIGHTS:
            delta_w[n], new_m[n], new_v[n] = _adamw(weights[n], grad_w[n], given["m_" + n], given["v_" + n])
    return (loss, grad_x, *[grad_w[n] for n in TWIN_WEIGHTS], *[delta_w[n] for n in TWIN_WEIGHTS],
            *[new_m[n] for n in TWIN_WEIGHTS], *[new_v[n] for n in TWIN_WEIGHTS])
```

```python
import functools

import jax
import jax.numpy as jnp
from jax import lax
from jax.experimental import pallas as pl
from jax.experimental.pallas import tpu as pltpu

F32 = jnp.float32
BF16 = jnp.bfloat16
MESH = pl.DeviceIdType.MESH

N_META = 16
HEAD_DIM = 64
Q_PER_KV = 8
WINDOW = 128
BLOCK = 128
ATTN_SCALE = HEAD_DIM ** -0.5
ROPE_DIM = HEAD_DIM // 4
ROPE_THETA = 500000.0
NEG_INF = -1e30
POOL_WINDOWS = (2, 4, 8, 16)
LN_EPS = 1e-5
DN_ALPHA = 2.0 ** 0.25
ADAM_LR = 0.001
ADAM_B1 = 0.9
ADAM_B2 = 0.999
ADAM_EPS = 1e-08
ADAM_WD = 0.01
ADAM_STEP = 10

LANES = 128
META_ROW0 = BLOCK - N_META
VMEM_LIMIT = 56 * 1024 * 1024


def _params(sem=None, **kw):
    return pltpu.CompilerParams(dimension_semantics=sem, vmem_limit_bytes=VMEM_LIMIT, **kw)


def _pick(dim, pref, mult=LANES):
    best = None
    t = mult
    while t <= min(dim, pref):
        if dim % t == 0:
            best = t
        t += mult
    return dim if best is None else best


_DIMS = {"nn": (((1,), (0,)), ((), ())), "nt": (((1,), (1,)), ((), ())), "tn": (((0,), (0,)), ((), ()))}


def _mm(a, b, *, kind, out_dtype, tm, tn, tk=None, name, a_lead=None):
    a2 = a.shape[-2:]
    if kind == "tn":
        K, M = a2
    else:
        M, K = a2
    N = b.shape[0] if kind == "nt" else b.shape[1]
    tm = _pick(M, tm)
    tn = _pick(N, tn)
    tk = K if tk is None else _pick(K, tk)
    nm, nn_, nk = M // tm, N // tn, K // tk
    a_bytes = M * K * a.dtype.itemsize
    b_bytes = N * K * b.dtype.itemsize
    i_outer = (a_bytes + nm * b_bytes <= b_bytes + nn_ * a_bytes) if nk == 1 else True

    def ij(g0, g1):
        return (g0, g1) if i_outer else (g1, g0)

    def a_map(g0, g1, k):
        i, _ = ij(g0, g1)
        idx = (k, i) if kind == "tn" else (i, k)
        return idx if a_lead is None else (a_lead,) + idx

    def b_map(g0, g1, k):
        _, j = ij(g0, g1)
        return (j, k) if kind == "nt" else (k, j)

    def o_map(g0, g1, k):
        return ij(g0, g1)

    a_blk = (tk, tm) if kind == "tn" else (tm, tk)
    if a_lead is not None:
        a_blk = (None,) + a_blk
    b_blk = (tn, tk) if kind == "nt" else (tk, tn)

    def body(a_ref, b_ref, o_ref, *acc):
        p = lax.dot_general(a_ref[...], b_ref[...], _DIMS[kind], preferred_element_type=F32)
        if nk == 1:
            o_ref[...] = p.astype(o_ref.dtype)
        else:
            k = pl.program_id(2)

            @pl.when(k == 0)
            def _():
                acc[0][...] = p

            @pl.when(k > 0)
            def _():
                acc[0][...] += p

            @pl.when(k == nk - 1)
            def _():
                o_ref[...] = acc[0][...].astype(o_ref.dtype)

    grid = (nm, nn_, nk) if i_outer else (nn_, nm, nk)
    return pl.pallas_call(
        body, name=name, grid=grid,
        in_specs=[pl.BlockSpec(a_blk, a_map), pl.BlockSpec(b_blk, b_map)],
        out_specs=pl.BlockSpec((tm, tn), o_map),
        out_shape=jax.ShapeDtypeStruct((M, N), out_dtype),
        scratch_shapes=[] if nk == 1 else [pltpu.VMEM((tm, tn), F32)],
        compiler_params=_params(("parallel", "parallel", "arbitrary")),
    )(a, b)


def _ln_stats(z):
    mu = jnp.mean(z, axis=-1, keepdims=True)
    zc = z - mu
    var = jnp.mean(zc * zc, axis=-1, keepdims=True)
    rstd = lax.rsqrt(var + LN_EPS)
    return zc * rstd, rstd


def _ln_bwd(dy, xhat, rstd, g):
    dxh = dy * g
    m1 = jnp.mean(dxh, axis=-1, keepdims=True)
    m2 = jnp.mean(dxh * xhat, axis=-1, keepdims=True)
    return rstd * (dxh - m1 - xhat * m2)


def _ln_in_fwd(x, meta_pad, g, b):
    S, D = x.shape
    nb = S // BLOCK

    def body(x_ref, mp_ref, g_ref, b_ref, h_ref, hb_ref):
        is_meta = pl.program_id(0) == nb
        xin = jnp.where(is_meta, mp_ref[...], x_ref[...])
        xhat, _ = _ln_stats(xin)
        y = xhat * g_ref[...] + b_ref[...]
        h_ref[...] = y
        hb_ref[...] = y.astype(BF16)

    row = pl.BlockSpec((BLOCK, D), lambda i: (i, 0))
    vec = pl.BlockSpec((1, D), lambda i: (0, 0))
    return pl.pallas_call(
        body, name="ln_in_fwd", grid=(nb + 1,),
        in_specs=[pl.BlockSpec((BLOCK, D), lambda i: (jnp.minimum(i, nb - 1), 0)),
                  pl.BlockSpec((BLOCK, D), lambda i: (0, 0)), vec, vec],
        out_specs=[row, row],
        out_shape=[jax.ShapeDtypeStruct((S + BLOCK, D), F32), jax.ShapeDtypeStruct((S + BLOCK, D), BF16)],
        compiler_params=_params(("parallel",)),
    )(x, meta_pad, g, b)


def _ln_in_bwd(x, meta_pad, g, dh0, dz1):
    S, D = x.shape
    nb = S // BLOCK

    def body(x_ref, mp_ref, g_ref, dh_ref, dz_ref, dx_ref, dg_ref, db_ref):
        i = pl.program_id(0)
        is_meta = i == nb
        xin = jnp.where(is_meta, mp_ref[...], x_ref[...])
        xhat, rstd = _ln_stats(xin)
        dy = dh_ref[...] + jnp.where(is_meta, 0.0, DN_ALPHA) * dz_ref[...]
        dx_ref[...] = _ln_bwd(dy, xhat, rstd, g_ref[...])

        @pl.when(i == 0)
        def _():
            dg_ref[...] = jnp.zeros_like(dg_ref)
            db_ref[...] = jnp.zeros_like(db_ref)

        dg_ref[...] += jnp.sum(dy * xhat, axis=0, keepdims=True)
        db_ref[...] += jnp.sum(dy, axis=0, keepdims=True)

    row = pl.BlockSpec((BLOCK, D), lambda i: (i, 0))
    rowx = pl.BlockSpec((BLOCK, D), lambda i: (jnp.minimum(i, nb - 1), 0))
    vec = pl.BlockSpec((1, D), lambda i: (0, 0))
    return pl.pallas_call(
        body, name="ln_in_bwd", grid=(nb + 1,),
        in_specs=[rowx, pl.BlockSpec((BLOCK, D), lambda i: (0, 0)), vec, row, rowx],
        out_specs=[row, vec, vec],
        out_shape=[jax.ShapeDtypeStruct((S + BLOCK, D), F32), jax.ShapeDtypeStruct((1, D), F32),
                   jax.ShapeDtypeStruct((1, D), F32)],
        compiler_params=_params(("arbitrary",)),
    )(x, meta_pad, g, dh0, dz1)


def _ln1_fwd(h0, y1, g, b):
    S, D = y1.shape
    tm = _pick(S, BLOCK, 8)

    def body(h_ref, y_ref, g_ref, b_ref, o_ref, ob_ref):
        xhat, _ = _ln_stats(DN_ALPHA * h_ref[...] + y_ref[...])
        y = xhat * g_ref[...] + b_ref[...]
        o_ref[...] = y
        ob_ref[...] = y.astype(BF16)

    row = pl.BlockSpec((tm, D), lambda i: (i, 0))
    vec = pl.BlockSpec((1, D), lambda i: (0, 0))
    return pl.pallas_call(
        body, name="ln1_fwd", grid=(S // tm,), in_specs=[row, row, vec, vec], out_specs=[row, row],
        out_shape=[jax.ShapeDtypeStruct((S, D), F32), jax.ShapeDtypeStruct((S, D), BF16)],
        compiler_params=_params(("parallel",)),
    )(h0, y1, g, b)


def _ln1_bwd(h0, y1, g, dh1, dz2):
    S, D = y1.shape
    tm = _pick(S, BLOCK, 8)

    def body(h_ref, y_ref, g_ref, dh_ref, dz2_ref, dz_ref, dzb_ref, dg_ref, db_ref):
        i = pl.program_id(0)
        xhat, rstd = _ln_stats(DN_ALPHA * h_ref[...] + y_ref[...])
        dy = dh_ref[...] + DN_ALPHA * dz2_ref[...]
        dz = _ln_bwd(dy, xhat, rstd, g_ref[...])
        dz_ref[...] = dz
        dzb_ref[...] = dz.astype(BF16)

        @pl.when(i == 0)
        def _():
            dg_ref[...] = jnp.zeros_like(dg_ref)
            db_ref[...] = jnp.zeros_like(db_ref)

        dg_ref[...] += jnp.sum(dy * xhat, axis=0, keepdims=True)
        db_ref[...] += jnp.sum(dy, axis=0, keepdims=True)

    row = pl.BlockSpec((tm, D), lambda i: (i, 0))
    vec = pl.BlockSpec((1, D), lambda i: (0, 0))
    return pl.pallas_call(
        body, name="ln1_bwd", grid=(S // tm,), in_specs=[row, row, vec, row, row],
        out_specs=[row, row, vec, vec],
        out_shape=[jax.ShapeDtypeStruct((S, D), F32), jax.ShapeDtypeStruct((S, D), BF16),
                   jax.ShapeDtypeStruct((1, D), F32), jax.ShapeDtypeStruct((1, D), F32)],
        compiler_params=_params(("arbitrary",)),
    )(h0, y1, g, dh1, dz2)


def _ln2_loss_bwd(h1, y2, target, g, b):
    S, D = y2.shape
    tm = _pick(S, BLOCK, 8)

    def body(h_ref, y_ref, t_ref, g_ref, b_ref, dz_ref, dzb_ref, dg_ref, db_ref, loss_ref):
        i = pl.program_id(0)
        xhat, rstd = _ln_stats(DN_ALPHA * h_ref[...] + y_ref[...])
        diff = xhat * g_ref[...] + b_ref[...] - t_ref[...]
        dy = diff / D
        dz = _ln_bwd(dy, xhat, rstd, g_ref[...])
        dz_ref[...] = dz
        dzb_ref[...] = dz.astype(BF16)

        @pl.when(i == 0)
        def _():
            dg_ref[...] = jnp.zeros_like(dg_ref)
            db_ref[...] = jnp.zeros_like(db_ref)
            loss_ref[...] = jnp.zeros_like(loss_ref)

        dg_ref[...] += jnp.sum(dy * xhat, axis=0, keepdims=True)
        db_ref[...] += jnp.sum(dy, axis=0, keepdims=True)
        loss_ref[...] += jnp.sum(jnp.mean(diff * diff, axis=-1, keepdims=True), axis=0, keepdims=True)

    row = pl.BlockSpec((tm, D), lambda i: (i, 0))
    vec = pl.BlockSpec((1, D), lambda i: (0, 0))
    one = pl.BlockSpec((1, 1), lambda i: (0, 0))
    return pl.pallas_call(
        body, name="ln2_loss_bwd", grid=(S // tm,), in_specs=[row, row, row, vec, vec],
        out_specs=[row, row, vec, vec, one],
        out_shape=[jax.ShapeDtypeStruct((S, D), F32), jax.ShapeDtypeStruct((S, D), BF16),
                   jax.ShapeDtypeStruct((1, D), F32), jax.ShapeDtypeStruct((1, D), F32),
                   jax.ShapeDtypeStruct((1, 1), F32)],
        compiler_params=_params(("arbitrary",)),
    )(h1, y2, target, g, b)


def _rope_table(S):
    r = jnp.arange(S + BLOCK)
    pos = jnp.where(r < S, r + N_META, jnp.maximum(r - (S + META_ROW0), 0))
    half = ROPE_DIM // 2
    inv_freq = ROPE_THETA ** (-jnp.arange(half, dtype=F32) * 2.0 / ROPE_DIM)
    ang = pos.astype(F32)[:, None] * inv_freq[None, :]
    cos, sin = jnp.cos(ang), jnp.sin(ang)
    lane = jnp.arange(LANES) % HEAD_DIM
    idx = lane % half
    c = jnp.where(lane < ROPE_DIM, cos[:, idx], 1.0)
    sa = jnp.where(lane < half, -sin[:, idx], 0.0)
    sb = jnp.where((lane >= half) & (lane < ROPE_DIM), sin[:, idx], 0.0)
    return jnp.concatenate([c, sa, sb], axis=1).astype(F32)


def _rope(x, tab):
    h = ROPE_DIM // 2
    return (x * tab[:, :LANES] + pltpu.roll(x, LANES - h, 1) * tab[:, LANES:2 * LANES]
            + pltpu.roll(x, h, 1) * tab[:, 2 * LANES:])


def _rope_t(dy, tab):
    h = ROPE_DIM // 2
    return (dy * tab[:, :LANES] + pltpu.roll(dy * tab[:, LANES:2 * LANES], h, 1)
            + pltpu.roll(dy * tab[:, 2 * LANES:], LANES - h, 1))


def _attn_tiles(g, n, S, sink_ref, q_ref, k_ref, v_ref, tab_ref):
    NQG = Q_PER_KV // 2
    R = NQG * BLOCK
    halfsel = (g % 2).astype(F32)
    prev = jnp.maximum(n - 1, 0)
    qrow = pl.ds(pl.multiple_of(n * BLOCK, BLOCK), BLOCK)
    prow = pl.ds(pl.multiple_of(prev * BLOCK, BLOCK), BLOCK)
    mrow = pl.ds(S, BLOCK)

    tq = tab_ref[qrow, :]
    qf = q_ref[...]
    q4 = jnp.concatenate([_rope(qf[:, LANES * p:LANES * (p + 1)], tq) for p in range(NQG)], axis=0).astype(BF16)

    tk = jnp.concatenate([tab_ref[mrow, :], tab_ref[prow, :], tq], axis=0)
    kr = _rope(jnp.concatenate([k_ref[mrow, :], k_ref[prow, :], k_ref[qrow, :]], axis=0), tk)
    vr = jnp.concatenate([v_ref[mrow, :], v_ref[prow, :], v_ref[qrow, :]], axis=0)

    lane = lax.broadcasted_iota(jnp.int32, kr.shape, 1)
    own = jnp.where(lane < HEAD_DIM, 1.0 - halfsel, halfsel)

    def lo_hi(t):
        mine = t * own
        other = pltpu.roll(mine, HEAD_DIM, 1)
        lo = mine * (1.0 - halfsel) + other * halfsel
        hi = other * (1.0 - halfsel) + mine * halfsel
        return lo.astype(BF16), hi.astype(BF16)

    klo, khi = lo_hi(kr)
    vlo, vhi = lo_hi(vr)

    row = lax.broadcasted_iota(jnp.int32, (R, 3 * BLOCK), 0) & (BLOCK - 1)
    col = lax.broadcasted_iota(jnp.int32, (R, 3 * BLOCK), 1)
    jj = col & (BLOCK - 1)
    no_prev = jnp.where(n >= 1, 0, 2 * BLOCK)
    mask = (((col < BLOCK) & (col >= META_ROW0))
            | ((col >= BLOCK) & (col < 2 * BLOCK) & (jj > row + no_prev))
            | ((col >= 2 * BLOCK) & (jj <= row)))

    def soft(kk, parity):
        sk = jnp.concatenate(
            [jnp.full((BLOCK, 1), sink_ref[0, Q_PER_KV * g + 2 * p + parity], F32) for p in range(NQG)], axis=0)
        s = lax.dot_general(q4, kk, _DIMS["nt"], preferred_element_type=F32) * ATTN_SCALE
        s = jnp.where(mask, s, NEG_INF)
        m = jnp.maximum(jnp.max(s, axis=1, keepdims=True), sk)
        p = jnp.exp(s - m)
        es = jnp.exp(sk - m)
        inv = 1.0 / (jnp.sum(p, axis=1, keepdims=True) + es)
        return p * inv, es * inv

    pe, sink_e = soft(klo, 0)
    po, sink_o = soft(khi, 1)
    return q4, tk, (klo, khi), (vlo, vhi), (pe, po), (sink_e, sink_o), own


def _attn_specs(S, ATTN, KVW):
    Tp = S + BLOCK
    koff, voff = ATTN // LANES, (ATTN + KVW) // LANES
    gw = Q_PER_KV * HEAD_DIM
    return [pl.BlockSpec(memory_space=pltpu.SMEM),
            pl.BlockSpec((BLOCK, gw), lambda g, n: (n, g)),
            pl.BlockSpec((Tp, LANES), lambda g, n: (0, koff + g // 2)),
            pl.BlockSpec((Tp, LANES), lambda g, n: (0, voff + g // 2)),
            pl.BlockSpec((Tp, 3 * LANES), lambda g, n: (0, 0))]


def _attn_fwd(proj, tab, sinks, S, ATTN, KVW):
    G = KVW // HEAD_DIM
    nb = S // BLOCK
    gw = Q_PER_KV * HEAD_DIM

    def body(sink_ref, q_ref, k_ref, v_ref, tab_ref, o_ref):
        g, n = pl.program_id(0), pl.program_id(1)
        _, _, _, (vlo, vhi), (pe, po), _, _ = _attn_tiles(g, n, S, sink_ref, q_ref, k_ref, v_ref, tab_ref)
        o4 = (jnp.dot(pe.astype(BF16), vlo, preferred_element_type=F32)
              + jnp.dot(po.astype(BF16), vhi, preferred_element_type=F32))
        o_ref[...] = jnp.concatenate(
            [o4[BLOCK * p:BLOCK * (p + 1)] for p in range(Q_PER_KV // 2)], axis=1).astype(BF16)

    return pl.pallas_call(
        body, name="attn_fwd", grid=(G, nb), in_specs=_attn_specs(S, ATTN, KVW),
        out_specs=pl.BlockSpec((BLOCK, gw), lambda g, n: (n, g)),
        out_shape=jax.ShapeDtypeStruct((S, ATTN), BF16),
        compiler_params=_params(("parallel", "arbitrary")),
    )(sinks, proj, proj, proj, tab)


def _attn_bwd(proj, tab, sinks, da, dproj, S, ATTN, KVW):
    G = KVW // HEAD_DIM
    nb = S // BLOCK
    Tp = S + BLOCK
    NQG = Q_PER_KV // 2
    gw = Q_PER_KV * HEAD_DIM

    def body(sink_ref, q_ref, k_ref, v_ref, tab_ref, da_ref, dproj_in, dq_ref, dk_ref, dv_ref, ds_ref):
        del dproj_in
        g, n = pl.program_id(0), pl.program_id(1)
        q4, tk, (klo, khi), (vlo, vhi), (pe, po), (sink_e, sink_o), own = _attn_tiles(
            g, n, S, sink_ref, q_ref, k_ref, v_ref, tab_ref)
        dof = da_ref[...]
        do4 = jnp.concatenate([dof[:, LANES * p:LANES * (p + 1)] for p in range(NQG)], axis=0)

        def grads(p, vv):
            dp = lax.dot_general(do4, vv, _DIMS["nt"], preferred_element_type=F32)
            delta = jnp.sum(p * dp, axis=1, keepdims=True)
            return (p * (dp - delta) * ATTN_SCALE).astype(BF16), delta

        dse, delta_e = grads(pe, vlo)
        dso, delta_o = grads(po, vhi)

        dq4 = (jnp.dot(dse, klo, preferred_element_type=F32) + jnp.dot(dso, khi, preferred_element_type=F32))
        tq = tk[2 * BLOCK:]
        dq_ref[...] = jnp.concatenate(
            [_rope_t(dq4[BLOCK * p:BLOCK * (p + 1)], tq) for p in range(NQG)], axis=1).astype(BF16)

        lane = lax.broadcasted_iota(jnp.int32, (3 * BLOCK, LANES), 1)

        def fold(lo_part, hi_part):
            t = jnp.where(lane < HEAD_DIM, lo_part, hi_part)
            return t + pltpu.roll(t, HEAD_DIM, 1)

        dk = _rope_t(fold(lax.dot_general(dse, q4, _DIMS["tn"], preferred_element_type=F32),
                          lax.dot_general(dso, q4, _DIMS["tn"], preferred_element_type=F32)), tk) * own
        dv = fold(lax.dot_general(pe.astype(BF16), do4, _DIMS["tn"], preferred_element_type=F32),
                  lax.dot_general(po.astype(BF16), do4, _DIMS["tn"], preferred_element_type=F32)) * own

        @pl.when((n == 0) & (g % 2 == 0))
        def _():
            dk_ref[...] = jnp.zeros_like(dk_ref)
            dv_ref[...] = jnp.zeros_like(dv_ref)

        @pl.when(n == 0)
        def _():
            ds_ref[...] = jnp.zeros_like(ds_ref)

        prev = jnp.maximum(n - 1, 0)
        qrow = pl.ds(pl.multiple_of(n * BLOCK, BLOCK), BLOCK)
        prow = pl.ds(pl.multiple_of(prev * BLOCK, BLOCK), BLOCK)
        mrow = pl.ds(S, BLOCK)
        for ref, val in ((dk_ref, dk), (dv_ref, dv)):
            ref[mrow, :] += val[:BLOCK]
            ref[prow, :] += val[BLOCK:2 * BLOCK]
            ref[qrow, :] += val[2 * BLOCK:]

        srow = lax.broadcasted_iota(jnp.int32, (Q_PER_KV, LANES), 0)
        acc = jnp.zeros((Q_PER_KV, LANES), F32)
        for p in range(NQG):
            for parity, (sk, dl) in enumerate(((sink_e, delta_e), (sink_o, delta_o))):
                val = -jnp.sum(sk[BLOCK * p:BLOCK * (p + 1)] * dl[BLOCK * p:BLOCK * (p + 1)])
                acc = jnp.where(srow == 2 * p + parity, val, acc)
        ds_ref[0] += acc

    in_specs = _attn_specs(S, ATTN, KVW) + [pl.BlockSpec((BLOCK, gw), lambda g, n: (n, g)),
                                            pl.BlockSpec(memory_space=pl.ANY)]
    slab = pl.BlockSpec((Tp, LANES), lambda g, n: (0, g // 2))
    return pl.pallas_call(
        body, name="attn_bwd", grid=(G, nb), in_specs=in_specs,
        out_specs=[pl.BlockSpec((BLOCK, gw), lambda g, n: (n, g)), slab, slab,
                   pl.BlockSpec((1, Q_PER_KV, LANES), lambda g, n: (g, 0, 0))],
        out_shape=[jax.ShapeDtypeStruct(dproj.shape, BF16), jax.ShapeDtypeStruct((Tp, KVW), F32),
                   jax.ShapeDtypeStruct((Tp, KVW), F32), jax.ShapeDtypeStruct((G, Q_PER_KV, LANES), F32)],
        input_output_aliases={6: 0},
        compiler_params=_params(("arbitrary", "arbitrary")),
    )(sinks, proj, proj, proj, tab, da, dproj)


def _zero_meta_block(Tp, IN):
    tc = _pick(IN, 4096)

    def body(o_ref):
        o_ref[...] = jnp.zeros_like(o_ref)

    return pl.pallas_call(
        body, name="dproj_zero_meta", grid=(IN // tc,),
        out_specs=pl.BlockSpec((BLOCK, tc), lambda j: (Tp // BLOCK - 1, j)),
        out_shape=jax.ShapeDtypeStruct((Tp, IN), BF16),
        compiler_params=_params(("parallel",)),
    )()


def _put_dkv(dk, dv, dproj, ATTN):
    Tp, KVW = dk.shape
    nkb = KVW // LANES
    koff = ATTN // LANES

    def body(dk_ref, dv_ref, dproj_in, o_ref):
        del dproj_in
        t = pl.program_id(0)
        o_ref[...] = jnp.where(t < nkb, dk_ref[...], dv_ref[...]).astype(BF16)

    src = pl.BlockSpec((Tp, LANES), lambda t: (0, t % nkb))
    return pl.pallas_call(
        body, name="dproj_put_dkv", grid=(2 * nkb,),
        in_specs=[src, src, pl.BlockSpec(memory_space=pl.ANY)],
        out_specs=pl.BlockSpec((Tp, LANES), lambda t: (0, koff + t)),
        out_shape=jax.ShapeDtypeStruct(dproj.shape, BF16),
        input_output_aliases={2: 0},
        compiler_params=_params(("parallel",)),
    )(dk, dv, dproj)


HALO = 16


def _window_sums(x, up):
    n = x.shape[0]
    out = []
    s = x
    for k in (1, 2, 4, 8):
        s = s + pltpu.roll(s, (n - k) if up else k, 0)
        out.append(s)
    return out


def _pool_specs(S, D_unused, uoff_blocks, gw, tm):
    del D_unused
    main = [pl.BlockSpec((tm, gw), functools.partial(lambda i, g: (i, uoff_blocks + g), g=g)) for g in range(4)]
    meta_halo = (S + BLOCK - HALO) // HALO
    halo = [pl.BlockSpec((HALO, gw), functools.partial(
        lambda i, g: (jnp.where(i == 0, meta_halo, i * (tm // HALO) - 1), uoff_blocks + g), g=g)) for g in range(4)]
    return main + halo


def _pooled(main_refs, halo_refs, g):
    x = jnp.concatenate([halo_refs[g][...], main_refs[g][...]], axis=0)
    s = _window_sums(x, up=False)[g]
    return (s[HALO:] * (1.0 / POOL_WINDOWS[g]) - x[HALO:]).astype(BF16)


def _pool_fwd(proj, wgrp, scale, S, uoff, POOL):
    gw = POOL // 4
    tm = BLOCK

    def body(*refs):
        main, halo = refs[:4], refs[4:8]
        w_ref, sc_ref, o_ref = refs[8:]
        for g in range(4):
            mixed = jnp.dot(_pooled(main, halo, g), w_ref[g], preferred_element_type=F32)
            o_ref[:, gw * g:gw * (g + 1)] = (mixed * sc_ref[:, gw * g:gw * (g + 1)]).astype(BF16)

    return pl.pallas_call(
        body, name="pool_fwd", grid=(S // tm,),
        in_specs=_pool_specs(S, None, uoff // gw, gw, tm) + [
            pl.BlockSpec((4, gw, gw), lambda i: (0, 0, 0)), pl.BlockSpec((1, POOL), lambda i: (0, 0))],
        out_specs=pl.BlockSpec((tm, POOL), lambda i: (i, 0)),
        out_shape=jax.ShapeDtypeStruct((S, POOL), BF16),
        compiler_params=_params(("parallel",)),
    )(*([proj] * 8), wgrp, scale)


def _pool_bwd_mix(proj, wgrp, scale, dps, S, uoff, POOL):
    gw = POOL // 4
    tm = BLOCK

    def body(*refs):
        main, halo = refs[:4], refs[4:8]
        w_ref, sc_ref, dps_ref, dpl_ref, dw_ref, dsc_ref = refs[8:]
        i = pl.program_id(0)

        @pl.when(i == 0)
        def _():
            dw_ref[...] = jnp.zeros_like(dw_ref)
            dsc_ref[...] = jnp.zeros_like(dsc_ref)

        for g in range(4):
            cols = slice(gw * g, gw * (g + 1))
            pooled = _pooled(main, halo, g)
            mixed = jnp.dot(pooled, w_ref[g], preferred_element_type=F32)
            dps_g = dps_ref[:, cols]
            dsc_ref[:, cols] += jnp.sum(dps_g * mixed, axis=0, keepdims=True)
            dms = (dps_g * sc_ref[:, cols]).astype(BF16)
            dw_ref[g] += lax.dot_general(pooled, dms, _DIMS["tn"], preferred_element_type=F32)
            dpl_ref[:, cols] = lax.dot_general(dms, w_ref[g], _DIMS["nt"], preferred_element_type=F32)

    row = pl.BlockSpec((tm, POOL), lambda i: (i, 0))
    return pl.pallas_call(
        body, name="pool_bwd_mix", grid=(S // tm,),
        in_specs=_pool_specs(S, None, uoff // gw, gw, tm) + [
            pl.BlockSpec((4, gw, gw), lambda i: (0, 0, 0)), pl.BlockSpec((1, POOL), lambda i: (0, 0)), row],
        out_specs=[row, pl.BlockSpec((4, gw, gw), lambda i: (0, 0, 0)), pl.BlockSpec((1, POOL), lambda i: (0, 0))],
        out_shape=[jax.ShapeDtypeStruct((S, POOL), F32), jax.ShapeDtypeStruct((4, gw, gw), F32),
                   jax.ShapeDtypeStruct((1, POOL), F32)],
        compiler_params=_params(("arbitrary",)),
    )(*([proj] * 8), wgrp, scale, dps)


def _pool_bwd_window(dpl, dproj, S, uoff, POOL):
    gw = POOL // 4
    nb = S // BLOCK
    ub = uoff // gw

    def body(main_ref, halo_ref, dproj_in, o_ref):
        del dproj_in
        b, g = pl.program_id(0), pl.program_id(1)
        main = jnp.where(b < nb, main_ref[...], 0.0)
        halo = jnp.where(b == nb - 1, 0.0, halo_ref[...])
        sums = _window_sums(jnp.concatenate([main, halo], axis=0), up=True)
        du = jnp.zeros((BLOCK, gw), F32)
        for k, w in enumerate(POOL_WINDOWS):
            du = jnp.where(g == k, sums[k][:BLOCK] * (1.0 / w), du)
        du = du - main
        row = lax.broadcasted_iota(jnp.int32, du.shape, 0)
        first_valid = jnp.where(b == nb, META_ROW0, 0)
        o_ref[...] = jnp.where(row >= first_valid, du, 0.0).astype(BF16)

    return pl.pallas_call(
        body, name="pool_bwd_window", grid=(nb + 1, 4),
        in_specs=[pl.BlockSpec((BLOCK, gw), lambda b, g: (jnp.minimum(b, nb - 1), g)),
                  pl.BlockSpec((HALO, gw), lambda b, g: (
                      jnp.where(b == nb, 0, jnp.minimum((b + 1) * (BLOCK // HALO), S // HALO - 1)), g)),
                  pl.BlockSpec(memory_space=pl.ANY)],
        out_specs=pl.BlockSpec((BLOCK, gw), lambda b, g: (b, ub + g)),
        out_shape=jax.ShapeDtypeStruct(dproj.shape, BF16),
        input_output_aliases={2: 0},
        compiler_params=_params(("parallel", "parallel")),
    )(dpl, dpl, dproj)


def _sigmoid(x):
    return 1.0 / (1.0 + jnp.exp(-x))


def _gate_tiles(S, D, goff):
    tc = 512
    while goff % tc or D % tc:
        tc //= 2
    return _pick(S, 512, 8), tc


def _gate_mix(proj, bgate, a_out, p_out, S, D, goff):
    tm, tc = _gate_tiles(S, D, goff)
    g0b, nd = goff // tc, D // tc

    def body(l0_ref, l1_ref, b_ref, a_ref, p_ref, o_ref):
        g0 = _sigmoid(l0_ref[...] + b_ref[0:1, :])
        g1 = _sigmoid(l1_ref[...] + b_ref[1:2, :])
        o_ref[...] = (g0 * a_ref[...] + g1 * p_ref[...]).astype(BF16)

    tile = pl.BlockSpec((tm, tc), lambda i, j: (i, j))
    return pl.pallas_call(
        body, name="gate_mix", grid=(S // tm, nd),
        in_specs=[pl.BlockSpec((tm, tc), lambda i, j: (i, g0b + j)),
                  pl.BlockSpec((tm, tc), lambda i, j: (i, g0b + nd + j)),
                  pl.BlockSpec((2, tc), lambda i, j: (0, j)), tile, tile],
        out_specs=tile, out_shape=jax.ShapeDtypeStruct((S, D), BF16),
        compiler_params=_params(("parallel", "parallel")),
    )(proj, proj, bgate, a_out, p_out)


def _gate_bwd(proj, bgate, a_out, p_out, dmixed, dproj, S, D, goff):
    tm, tc = _gate_tiles(S, D, goff)
    g0b, nd = goff // tc, D // tc

    def body(l0_ref, l1_ref, b_ref, a_ref, p_ref, dm_ref, dproj_in, dap_ref, dl_ref, db_ref):
        del dproj_in
        i, br = pl.program_id(1), pl.program_id(2)
        first = br == 0
        logit = jnp.where(first, l0_ref[...], l1_ref[...]) + jnp.where(first, b_ref[0:1, :], b_ref[1:2, :])
        val = jnp.where(first, a_ref[...], p_ref[...])
        gate = _sigmoid(logit)
        dm = dm_ref[...]
        dap_ref[...] = (dm * gate).astype(BF16)
        dl = dm * val * gate * (1.0 - gate)
        dl_ref[...] = dl.astype(BF16)

        @pl.when((i == 0) & first)
        def _():
            db_ref[...] = jnp.zeros_like(db_ref)

        db_ref[br] += jnp.sum(dl, axis=0, keepdims=True)

    tile = pl.BlockSpec((tm, tc), lambda j, i, br: (i, j))
    return pl.pallas_call(
        body, name="gate_bwd", grid=(nd, S // tm, 2),
        in_specs=[pl.BlockSpec((tm, tc), lambda j, i, br: (i, g0b + j)),
                  pl.BlockSpec((tm, tc), lambda j, i, br: (i, g0b + nd + j)),
                  pl.BlockSpec((2, tc), lambda j, i, br: (0, j)), tile, tile, tile,
                  pl.BlockSpec(memory_space=pl.ANY)],
        out_specs=[pl.BlockSpec((None, tm, tc), lambda j, i, br: (br, i, j)),
                   pl.BlockSpec((tm, tc), lambda j, i, br: (i, g0b + br * nd + j)),
                   pl.BlockSpec((2, 1, tc), lambda j, i, br: (0, 0, j))],
        out_shape=[jax.ShapeDtypeStruct((2, S, D), BF16), jax.ShapeDtypeStruct(dproj.shape, BF16),
                   jax.ShapeDtypeStruct((2, 1, D), F32)],
        input_output_aliases={6: 1},
        compiler_params=_params(("parallel", "arbitrary", "arbitrary")),
    )(proj, proj, bgate, a_out, p_out, dmixed, dproj)


def _swiglu_fwd(f, S, FF):
    tc = _pick(FF, 5504)
    nj = FF // tc

    def body(g_ref, u_ref, o_ref):
        gt = g_ref[...]
        o_ref[...] = (gt * _sigmoid(gt) * u_ref[...]).astype(BF16)

    return pl.pallas_call(
        body, name="swiglu_fwd", grid=(S // BLOCK, nj),
        in_specs=[pl.BlockSpec((BLOCK, tc), lambda i, j: (i, j)), pl.BlockSpec((BLOCK, tc), lambda i, j: (i, nj + j))],
        out_specs=pl.BlockSpec((BLOCK, tc), lambda i, j: (i, j)),
        out_shape=jax.ShapeDtypeStruct((S, FF), BF16),
        compiler_params=_params(("parallel", "parallel")),
    )(f, f)


def _swiglu_bwd(f, dact, S, FF):
    tc = _pick(FF, 5504)
    nj = FF // tc

    def body(g_ref, u_ref, d_ref, o_ref):
        br = pl.program_id(2)
        gt, d = g_ref[...], d_ref[...]
        s = _sigmoid(gt)
        dgate = d * u_ref[...] * s * (1.0 + gt * (1.0 - s))
        dup = d * gt * s
        o_ref[...] = jnp.where(br == 0, dgate, dup).astype(BF16)

    return pl.pallas_call(
        body, name="swiglu_bwd", grid=(S // BLOCK, nj, 2),
        in_specs=[pl.BlockSpec((BLOCK, tc), lambda i, j, br: (i, j)),
                  pl.BlockSpec((BLOCK, tc), lambda i, j, br: (i, nj + j)),
                  pl.BlockSpec((BLOCK, tc), lambda i, j, br: (i, j))],
        out_specs=pl.BlockSpec((BLOCK, tc), lambda i, j, br: (i, br * nj + j)),
        out_shape=jax.ShapeDtypeStruct((S, 2 * FF), BF16),
        compiler_params=_params(("parallel", "parallel", "arbitrary")),
    )(f, f, dact)


ANY = pl.BlockSpec(memory_space=pl.ANY)


def _place():
    return lax.axis_index("x"), lax.axis_index("y"), lax.axis_index("c")


def _all_gather_rows(shards):
    nw = len(shards)

    def body(*refs):
        ins, outs = refs[:nw], refs[nw:2 * nw]
        send_sems, recv_sems, local_sems = refs[2 * nw:]
        x, y, c = _place()
        me, sib = (x, y, c), (x, y, 1 - c)
        chips = [(1 - x, y), (x, 1 - y), (1 - x, 1 - y)]

        def rows(w, px, py, pc):
            r = ins[w].shape[0]
            return outs[w].at[pl.ds((4 * px + 2 * py + pc) * r, r)]

        def copy(w, k, block, to, src=None):
            return pltpu.make_async_remote_copy(
                src_ref=rows(w, *block) if src is None else src, dst_ref=rows(w, *block),
                send_sem=send_sems.at[w, k], recv_sem=recv_sems.at[w, k], device_id=to, device_id_type=MESH)

        mine, first, passed = [], [], []
        for w in range(nw):
            cp = pltpu.make_async_copy(ins[w], rows(w, *me), local_sems.at[w])
            cp.start()
            mine.append(cp)
            f = [copy(w, 0, me, sib, src=ins[w])]
            f += [copy(w, 1 + j, me, (*chip, c), src=ins[w]) for j, chip in enumerate(chips)]
            for cp in f:
                cp.start()
            first += f
        for w in range(nw):
            for j, chip in enumerate(chips):
                copy(w, 1 + j, (*chip, c), me).wait_recv()
                cp = copy(w, 4 + j, (*chip, c), sib)
                cp.start()
                passed.append(cp)
        for w in range(nw):
            copy(w, 0, sib, me).wait_recv()
            for j, chip in enumerate(chips):
                copy(w, 4 + j, (*chip, 1 - c), me).wait_recv()
        for cp in first + passed:
            cp.wait_send()
        for cp in mine:
            cp.wait()

    return pl.pallas_call(
        body, name="weights_all_gather", in_specs=[ANY] * nw, out_specs=[ANY] * nw,
        out_shape=[jax.ShapeDtypeStruct((8 * s.shape[0], s.shape[1]), s.dtype) for s in shards],
        scratch_shapes=[pltpu.SemaphoreType.DMA((nw, 7)), pltpu.SemaphoreType.DMA((nw, 7)),
                        pltpu.SemaphoreType.DMA((nw,))],
        compiler_params=pltpu.CompilerParams(has_side_effects=True),
    )(*shards)


def _rs_pair_exchange(grads):
    nw = len(grads)

    def body(*refs):
        ins, outs = refs[:nw], refs[nw:2 * nw]
        send_sems, recv_sems = refs[2 * nw:]
        x, y, c = _place()
        copies = []
        for w in range(nw):
            r = ins[w].shape[0] // 8
            for k in range(4):
                cp = pltpu.make_async_remote_copy(
                    src_ref=ins[w].at[pl.ds((2 * k + 1 - c) * r, r)], dst_ref=outs[w].at[k],
                    send_sem=send_sems.at[w, k], recv_sem=recv_sems.at[w, k],
                    device_id=(x, y, 1 - c), device_id_type=MESH)
                cp.start()
                copies.append(cp)
        for cp in copies:
            cp.wait_recv()
        for cp in copies:
            cp.wait_send()

    return pl.pallas_call(
        body, name="grads_pair_exchange", in_specs=[ANY] * nw, out_specs=[ANY] * nw,
        out_shape=[jax.ShapeDtypeStruct((4, g.shape[0] // 8, g.shape[1]), g.dtype) for g in grads],
        scratch_shapes=[pltpu.SemaphoreType.DMA((nw, 4)), pltpu.SemaphoreType.DMA((nw, 4))],
        compiler_params=pltpu.CompilerParams(has_side_effects=True),
    )(*grads)


def _pair_sum(grad, other, cidx, name):
    r, cols = other.shape[1:]
    tr = _pick(r, max(16, (4 << 20) // (2 * cols)), 16)
    nr = r // tr

    def body(c_ref, g_ref, a_ref, o_ref):
        del c_ref
        o_ref[...] = (g_ref[...].astype(F32) + a_ref[...].astype(F32)).astype(BF16)

    blk = pl.BlockSpec((None, tr, cols), lambda k, i, c_ref: (k, i, 0))
    return pl.pallas_call(
        body, name=name,
        grid_spec=pltpu.PrefetchScalarGridSpec(
            num_scalar_prefetch=1, grid=(4, nr),
            in_specs=[pl.BlockSpec((tr, cols), lambda k, i, c_ref: ((2 * k + c_ref[0]) * nr + i, 0)), blk],
            out_specs=blk),
        out_shape=jax.ShapeDtypeStruct(other.shape, BF16),
        compiler_params=_params(("parallel", "parallel")),
    )(cidx, grad, other)


def _rs_chip_exchange(psums):
    nw = len(psums)

    def body(*refs):
        ins, outs = refs[:nw], refs[nw:2 * nw]
        send_sems, recv_sems, local_sems = refs[2 * nw:]
        x, y, c = _place()
        kme = 2 * x + y
        copies, mine = [], []
        for w in range(nw):
            cp = pltpu.make_async_copy(ins[w].at[kme], outs[w].at[kme], local_sems.at[w])
            cp.start()
            mine.append(cp)
            for j, (dx, dy) in enumerate(((0, 1), (1, 0), (1, 1))):
                px = 1 - x if dx else x
                py = 1 - y if dy else y
                cp = pltpu.make_async_remote_copy(
                    src_ref=ins[w].at[2 * px + py], dst_ref=outs[w].at[kme],
                    send_sem=send_sems.at[w, j], recv_sem=recv_sems.at[w, j],
                    device_id=(px, py, c), device_id_type=MESH)
                cp.start()
                copies.append(cp)
        for cp in copies:
            cp.wait_recv()
        for cp in copies:
            cp.wait_send()
        for cp in mine:
            cp.wait()

    return pl.pallas_call(
        body, name="grads_chip_exchange", in_specs=[ANY] * nw, out_specs=[ANY] * nw,
        out_shape=[jax.ShapeDtypeStruct(p.shape, p.dtype) for p in psums],
        scratch_shapes=[pltpu.SemaphoreType.DMA((nw, 3)), pltpu.SemaphoreType.DMA((nw, 3)),
                        pltpu.SemaphoreType.DMA((nw,))],
        compiler_params=pltpu.CompilerParams(has_side_effects=True),
    )(*psums)


def _sum4(parts, name):
    _, r, cols = parts.shape
    tr = _pick(r, max(16, (2 << 20) // (2 * cols)), 16)

    def body(p_ref, o_ref):
        o_ref[...] = ((p_ref[0].astype(F32) + p_ref[1].astype(F32)) + p_ref[2].astype(F32)) + p_ref[3].astype(F32)

    return pl.pallas_call(
        body, name=name, grid=(r // tr,),
        in_specs=[pl.BlockSpec((4, tr, cols), lambda i: (0, i, 0))],
        out_specs=pl.BlockSpec((tr, cols), lambda i: (i, 0)),
        out_shape=jax.ShapeDtypeStruct((r, cols), F32),
        compiler_params=_params(("parallel",)),
    )(parts)


def _all_reduce_small(pack, name):
    R = pack.shape[0]

    def body(p_ref, o_ref, buf, send_sems, recv_sems):
        x, y, c = _place()
        me = 4 * x + 2 * y + c
        buf[me] = p_ref[...]
        copies = []
        for k in range(1, 8):
            px = 1 - x if k & 4 else x
            py = 1 - y if k & 2 else y
            pc = 1 - c if k & 1 else c
            cp = pltpu.make_async_remote_copy(
                src_ref=p_ref, dst_ref=buf.at[me], send_sem=send_sems.at[k - 1], recv_sem=recv_sems.at[k - 1],
                device_id=(px, py, pc), device_id_type=MESH)
            cp.start()
            copies.append(cp)
        for cp in copies:
            cp.wait_recv()
        acc = buf[0]
        for d in range(1, 8):
            acc = acc + buf[d]
        o_ref[...] = acc
        for cp in copies:
            cp.wait_send()

    vm = pl.BlockSpec(memory_space=pltpu.VMEM)
    return pl.pallas_call(
        body, name=name, in_specs=[vm], out_specs=vm,
        out_shape=jax.ShapeDtypeStruct((R, LANES), F32),
        scratch_shapes=[pltpu.VMEM((8, R, LANES), F32), pltpu.SemaphoreType.DMA((7,)), pltpu.SemaphoreType.DMA((7,))],
        compiler_params=pltpu.CompilerParams(vmem_limit_bytes=VMEM_LIMIT, has_side_effects=True),
    )(pack)


def _pack(parts):
    flat = []
    for p in parts:
        v = p.reshape(-1).astype(F32)
        flat.append(jnp.pad(v, (0, (-v.shape[0]) % LANES)))
    v = jnp.concatenate(flat)
    v = jnp.pad(v, (0, (-v.shape[0]) % (8 * LANES)))
    return v.reshape(-1, LANES)


def _unpack(pack, shapes):
    v = pack.reshape(-1)
    out, off = [], 0
    for s in shapes:
        n = 1
        for d in s:
            n *= d
        out.append(v[off:off + n].reshape(s))
        off += n + (-n) % LANES
    return out


def _adamw(w, g, m, v, name):
    shape = w.shape
    cols = shape[-1]
    w2, g2, m2, v2 = (t.reshape(-1, cols) for t in (w, g, m, v))
    R = w2.shape[0]
    tr = _pick(R, max(8, (1 << 20) // (4 * cols)), 8)

    def body(w_ref, g_ref, m_ref, v_ref, d_ref, mo_ref, vo_ref):
        gr = g_ref[...]
        mn = ADAM_B1 * m_ref[...] + (1.0 - ADAM_B1) * gr
        vn = ADAM_B2 * v_ref[...] + (1.0 - ADAM_B2) * (gr * gr)
        m_hat = mn / (1.0 - ADAM_B1 ** ADAM_STEP)
        v_hat = vn / (1.0 - ADAM_B2 ** ADAM_STEP)
        d_ref[...] = -ADAM_LR * (m_hat / (jnp.sqrt(v_hat) + ADAM_EPS) + ADAM_WD * w_ref[...])
        mo_ref[...] = mn
        vo_ref[...] = vn

    blk = pl.BlockSpec((tr, cols), lambda i: (i, 0))
    outs = pl.pallas_call(
        body, name=name, grid=(R // tr,), in_specs=[blk] * 4, out_specs=[blk] * 3,
        out_shape=[jax.ShapeDtypeStruct((R, cols), F32)] * 3,
        compiler_params=_params(("parallel",)),
    )(w2, g2, m2, v2)
    return tuple(o.reshape(shape) for o in outs)


def kernel(x, meta_tokens, ln_in_g, ln_in_b, w_in, b_gate, attn_sinks, w_attn_up, w_pool_grp, pool_scale, w_pool_up, w_out, ln1_g, ln1_b, w_ffn_in, w_ffn_down, ln2_g, ln2_b, loss_target, m_meta_tokens, m_ln_in_g, m_ln_in_b, m_w_in, m_b_gate, m_attn_sinks, m_w_attn_up, m_w_pool_grp, m_pool_scale, m_w_pool_up, m_w_out, m_ln1_g, m_ln1_b, m_w_ffn_in, m_w_ffn_down, m_ln2_g, m_ln2_b, v_meta_tokens, v_ln_in_g, v_ln_in_b, v_w_in, v_b_gate, v_attn_sinks, v_w_attn_up, v_w_pool_grp, v_pool_scale, v_w_pool_up, v_w_out, v_ln1_g, v_ln1_b, v_w_ffn_in, v_w_ffn_down, v_ln2_g, v_ln2_b):
    S, D = x.shape[1], x.shape[2]
    Tp = S + BLOCK
    NQ = attn_sinks.shape[-1]
    ATTN = NQ * HEAD_DIM
    KVW = ATTN // Q_PER_KV
    POOL = pool_scale.shape[-1]
    IN = 8 * w_in.shape[2]
    FF = 8 * w_ffn_down.shape[1]
    uoff = ATTN + 2 * KVW
    goff = uoff + POOL
    gw = POOL // 4
    dcols = D // 8
    assert IN == goff + 2 * D and w_ffn_in.shape[2] * 8 == 2 * FF

    xi, yi, ci = _place()
    dev = 4 * xi + 2 * yi + ci
    cidx = jnp.reshape(ci, (1,)).astype(jnp.int32)
    x2, tgt = x[0], loss_target[0]

    def place_cols(a):
        return lax.dynamic_update_slice(jnp.zeros(a.shape[:-1] + (D,), F32), a, (0,) * (a.ndim - 1) + (dev * dcols,))

    small = _all_reduce_small(_pack([place_cols(meta_tokens), place_cols(b_gate[0])]), "small_inputs_gather")
    meta_full, bgate_full = _unpack(small, [(N_META, D), (2, D)])
    meta_pad = jnp.pad(meta_full, ((META_ROW0, 0), (0, 0)))

    wgrp_rows = w_pool_grp[0].reshape(4 * (gw // 8), gw)
    shards = [w_in[0].T, w_attn_up[0].T, wgrp_rows, w_pool_up[0].T, w_out[0], w_ffn_in[0].T, w_ffn_down[0]]
    winT, wattT, wgrp_g, wpupT, wout, wffnT, wdown = _all_gather_rows([s.astype(BF16) for s in shards])
    wgrp = wgrp_g.reshape(8, 4, gw // 8, gw).transpose(1, 0, 2, 3).reshape(4, gw, gw)

    ln_in_g2, ln_in_b2 = ln_in_g.reshape(1, D), ln_in_b.reshape(1, D)
    tab = _rope_table(S)

    h0, h0b = _ln_in_fwd(x2, meta_pad, ln_in_g2, ln_in_b2)
    proj = _mm(h0b, winT, kind="nt", out_dtype=F32, tm=1408, tn=512, name="proj")
    att = _attn_fwd(proj, tab, attn_sinks, S, ATTN, KVW)
    ps = _pool_fwd(proj, wgrp, pool_scale, S, uoff, POOL)
    a_out = _mm(att, wattT, kind="nt", out_dtype=F32, tm=1024, tn=1024, name="attn_up")
    p_out = _mm(ps, wpupT, kind="nt", out_dtype=F32, tm=1024, tn=1024, name="pool_up")
    mixed = _gate_mix(proj, bgate_full, a_out, p_out, S, D, goff)
    y1 = _mm(mixed, wout, kind="nn", out_dtype=F32, tm=1024, tn=1024, name="out_proj")
    h1, h1b = _ln1_fwd(h0, y1, ln1_g, ln1_b)
    f = _mm(h1b, wffnT, kind="nt", out_dtype=F32, tm=1024, tn=512, name="ffn_in")
    act = _swiglu_fwd(f, S, FF)
    y2 = _mm(act, wdown, kind="nn", out_dtype=F32, tm=512, tn=1024, tk=5504, name="ffn_down")

    dz2, dz2b, dg2, db2, loss_part = _ln2_loss_bwd(h1, y2, tgt, ln2_g, ln2_b)
    gwdown = _mm(act, dz2b, kind="tn", out_dtype=BF16, tm=256, tn=2048, name="d_ffn_down")
    dact = _mm(dz2b, wdown, kind="nt", out_dtype=F32, tm=2048, tn=256, name="d_act")
    df = _swiglu_bwd(f, dact, S, FF)
    gwffnT = _mm(df, h1b, kind="tn", out_dtype=BF16, tm=512, tn=1024, name="d_ffn_in")
    dh1 = _mm(df, wffnT, kind="nn", out_dtype=F32, tm=2048, tn=1024, tk=512, name="d_h1")
    dz1, dz1b, dg1, db1 = _ln1_bwd(h0, y1, ln1_g, dh1, dz2)
    gwout = _mm(mixed, dz1b, kind="tn", out_dtype=BF16, tm=512, tn=1024, name="d_out_proj")
    dmixed = _mm(dz1b, wout, kind="nt", out_dtype=F32, tm=1024, tn=1024, name="d_mixed")

    dproj = _zero_meta_block(Tp, IN)
    dap, dproj, dbgate = _gate_bwd(proj, bgate_full, a_out, p_out, dmixed, dproj, S, D, goff)
    gwattT = _mm(dap, att, kind="tn", out_dtype=BF16, tm=512, tn=1024, name="d_attn_up", a_lead=0)
    datt = _mm(dap, wattT, kind="nn", out_dtype=BF16, tm=1024, tn=1024, name="d_att", a_lead=0)
    gwpupT = _mm(dap, ps, kind="tn", out_dtype=BF16, tm=512, tn=1024, name="d_pool_up", a_lead=1)
    dps = _mm(dap, wpupT, kind="nn", out_dtype=F32, tm=1024, tn=1024, name="d_ps", a_lead=1)
    dpl, gwgrp, dscale = _pool_bwd_mix(proj, wgrp, pool_scale, dps, S, uoff, POOL)
    dproj = _pool_bwd_window(dpl, dproj, S, uoff, POOL)
    dproj, dk, dv, dsink = _attn_bwd(proj, tab, attn_sinks, datt, dproj, S, ATTN, KVW)
    dproj = _put_dkv(dk, dv, dproj, ATTN)
    gwinT = _mm(dproj, h0b, kind="tn", out_dtype=BF16, tm=512, tn=1024, name="d_w_in")
    dh0 = _mm(dproj, winT, kind="nn", out_dtype=F32, tm=1408, tn=1024, tk=2560, name="d_h0")
    dxin, dg_in, db_in = _ln_in_bwd(x2, meta_pad, ln_in_g2, dh0, dz1)
    grad_x = dxin[:S][None]
    dmeta = dxin[S + META_ROW0:]

    small_shapes = [(D,), (D,), (1, D), (1, D), (1, D), (1, D), (1, POOL), (1, NQ), (), (N_META, D), (2, D)]
    red = _all_reduce_small(_pack([dg_in, db_in, dg1, db1, dg2, db2, dscale, dsink[:, :, 0], loss_part,
                                   dmeta, dbgate]), "small_grads_all_reduce")
    (g_ln_in_g, g_ln_in_b, g_ln1_g, g_ln1_b, g_ln2_g, g_ln2_b, g_scale, g_sinks, loss_sum, g_meta_full,
     g_bgate_full) = _unpack(red, small_shapes)
    loss = 0.5 * loss_sum
    g_meta = lax.dynamic_slice(g_meta_full, (0, dev * dcols), (N_META, dcols))
    g_bgate = lax.dynamic_slice(g_bgate_full, (0, dev * dcols), (2, dcols))[None]

    gwgrp_rows = gwgrp.reshape(4, 8, gw // 8, gw).transpose(1, 0, 2, 3).reshape(8 * 4 * (gw // 8), gw).astype(BF16)
    full = [gwinT, gwattT, gwgrp_rows, gwpupT, gwout, gwffnT, gwdown]
    names = ["w_in", "w_attn_up", "w_pool_grp", "w_pool_up", "w_out", "w_ffn_in", "w_ffn_down"]
    others = _rs_pair_exchange(full)
    psums = [_pair_sum(g, o, cidx, "pair_sum_" + nm) for g, o, nm in zip(full, others, names)]
    parts = _rs_chip_exchange(psums)
    gsh = [_sum4(p, "chip_sum_" + nm) for p, nm in zip(parts, names)]
    g_w_in, g_w_attn_up = gsh[0].T[None], gsh[1].T[None]
    g_w_pool_grp = gsh[2].reshape(w_pool_grp.shape)
    g_w_pool_up, g_w_out = gsh[3].T[None], gsh[4][None]
    g_w_ffn_in, g_w_ffn_down = gsh[5].T[None], gsh[6][None]

    weights = [meta_tokens, ln_in_g, ln_in_b, w_in, b_gate, attn_sinks, w_attn_up, w_pool_grp, pool_scale,
               w_pool_up, w_out, ln1_g, ln1_b, w_ffn_in, w_ffn_down, ln2_g, ln2_b]
    grads = [g_meta, g_ln_in_g, g_ln_in_b, g_w_in, g_bgate, g_sinks, g_w_attn_up, g_w_pool_grp, g_scale,
             g_w_pool_up, g_w_out, g_ln1_g, g_ln1_b, g_w_ffn_in, g_w_ffn_down, g_ln2_g, g_ln2_b]
    ms = [m_meta_tokens, m_ln_in_g, m_ln_in_b, m_w_in, m_b_gate, m_attn_sinks, m_w_attn_up, m_w_pool_grp,
          m_pool_scale, m_w_pool_up, m_w_out, m_ln1_g, m_ln1_b, m_w_ffn_in, m_w_ffn_down, m_ln2_g, m_ln2_b]
    vs = [v_meta_tokens, v_ln_in_g, v_ln_in_b, v_w_in, v_b_gate, v_attn_sinks, v_w_attn_up, v_w_pool_grp,
          v_pool_scale, v_w_pool_up, v_w_out, v_ln1_g, v_ln1_b, v_w_ffn_in, v_w_ffn_down, v_ln2_g, v_ln2_b]
    wnames = ["meta_tokens", "ln_in_g", "ln_in_b", "w_in", "b_gate", "attn_sinks", "w_attn_up", "w_pool_grp",
              "pool_scale", "w_pool_up", "w_out", "ln1_g", "ln1_b", "w_ffn_in", "w_ffn_down", "ln2_g", "ln2_b"]
    deltas, new_ms, new_vs = [], [], []
    for w, g, m, v, nm in zip(weights, grads, ms, vs, wnames):
        d, mn, vn = _adamw(w, g.reshape(w.shape), m, v, "adamw_" + nm)
        deltas.append(d)
        new_ms.append(mn)
        new_vs.append(vn)
    grads = [g.reshape(w.shape) for g, w in zip(grads, weights)]
    return (loss, grad_x, *grads, *deltas, *new_ms, *new_vs)
```

```python
import jax
import jax.numpy as jnp
from jax import lax
from jax.experimental import pallas as pl
from jax.experimental.pallas import tpu as pltpu

F32 = jnp.float32
BF16 = jnp.bfloat16
MESH = pl.DeviceIdType.MESH

N_META = 16
HEAD_DIM = 64
Q_PER_KV = 8
WINDOW = 128
BLOCK = 128
ATTN_SCALE = HEAD_DIM ** -0.5
ROPE_DIM = HEAD_DIM // 4
ROPE_THETA = 500000.0
NEG_INF = -1e30
POOL_WINDOWS = (2, 4, 8, 16)
LN_EPS = 1e-5
DN_ALPHA = 2.0 ** 0.25
ADAM_LR = 0.001
ADAM_B1 = 0.9
ADAM_B2 = 0.999
ADAM_EPS = 1e-08
ADAM_WD = 0.01
ADAM_STEP = 10

LANES = 128
META_ROW0 = BLOCK - N_META
VMEM_LIMIT = 56 * 1024 * 1024

ANY = pl.BlockSpec(memory_space=pl.ANY)
HBM = pl.BlockSpec(memory_space=pltpu.HBM)
SEM = pl.BlockSpec(memory_space=pltpu.SEMAPHORE)
EFFECT = pltpu.SideEffectType.DATAFLOW_SIDE_EFFECTING


def _params(sem=None, **kw):
    return pltpu.CompilerParams(dimension_semantics=sem, vmem_limit_bytes=VMEM_LIMIT, **kw)


class _Order:
    def __init__(self):
        self.last = None


def _call(order, body, operands, *, name, in_specs, out_specs, out_shape, grid=(), scratch=(), sem=None,
          aliases=None, prefetch=()):
    n_in, npf = len(operands), len(prefetch)
    tok = order.last
    if tok is not None and any(tok is op for op in operands):
        tok = None

    def wrapped(*refs):
        refs = list(refs)
        if tok is not None:
            del refs[npf + n_in]
        body(*refs)

    specs = list(in_specs) + ([ANY] if tok is not None else [])
    ops = list(operands) + ([tok] if tok is not None else [])
    if npf:
        out = pl.pallas_call(
            wrapped, name=name, out_shape=out_shape, compiler_params=_params(sem),
            grid_spec=pltpu.PrefetchScalarGridSpec(num_scalar_prefetch=npf, grid=grid, in_specs=specs,
                                                   out_specs=out_specs, scratch_shapes=list(scratch)),
        )(*prefetch, *ops)
    else:
        out = pl.pallas_call(
            wrapped, name=name, grid=grid, in_specs=specs, out_specs=out_specs, out_shape=out_shape,
            scratch_shapes=list(scratch), input_output_aliases=aliases or {}, compiler_params=_params(sem),
        )(*ops)
    order.last = out[0] if isinstance(out, (list, tuple)) else out
    return out


def _pick(dim, pref, mult=LANES):
    best = None
    t = mult
    while t <= min(dim, pref):
        if dim % t == 0:
            best = t
        t += mult
    return dim if best is None else best


_DIMS = {"nn": (((1,), (0,)), ((), ())), "nt": (((1,), (1,)), ((), ())), "tn": (((0,), (0,)), ((), ()))}


def _mm(order, a, b, *, kind, out_dtype, tm, tn, tk=None, name, a_lead=None):
    a2 = a.shape[-2:]
    if kind == "tn":
        K, M = a2
    else:
        M, K = a2
    N = b.shape[0] if kind == "nt" else b.shape[1]
    tm = _pick(M, tm)
    tn = _pick(N, tn)
    tk = K if tk is None else _pick(K, tk)
    nm, nn_, nk = M // tm, N // tn, K // tk
    a_bytes = M * K * a.dtype.itemsize
    b_bytes = N * K * b.dtype.itemsize
    i_outer = (a_bytes + nm * b_bytes <= b_bytes + nn_ * a_bytes) if nk == 1 else True

    def ij(g0, g1):
        return (g0, g1) if i_outer else (g1, g0)

    def a_map(g0, g1, k):
        i, _ = ij(g0, g1)
        idx = (k, i) if kind == "tn" else (i, k)
        return idx if a_lead is None else (a_lead,) + idx

    def b_map(g0, g1, k):
        _, j = ij(g0, g1)
        return (j, k) if kind == "nt" else (k, j)

    def o_map(g0, g1, k):
        return ij(g0, g1)

    a_blk = (tk, tm) if kind == "tn" else (tm, tk)
    if a_lead is not None:
        a_blk = (None,) + a_blk
    b_blk = (tn, tk) if kind == "nt" else (tk, tn)

    def body(a_ref, b_ref, o_ref, *acc):
        p = lax.dot_general(a_ref[...], b_ref[...], _DIMS[kind], preferred_element_type=F32)
        if nk == 1:
            o_ref[...] = p.astype(o_ref.dtype)
        else:
            k = pl.program_id(2)

            @pl.when(k == 0)
            def _():
                acc[0][...] = p

            @pl.when(k > 0)
            def _():
                acc[0][...] += p

            @pl.when(k == nk - 1)
            def _():
                o_ref[...] = acc[0][...].astype(o_ref.dtype)

    grid = (nm, nn_, nk) if i_outer else (nn_, nm, nk)
    return _call(
        order, body, [a, b], name=name, grid=grid,
        in_specs=[pl.BlockSpec(a_blk, a_map), pl.BlockSpec(b_blk, b_map)],
        out_specs=pl.BlockSpec((tm, tn), o_map),
        out_shape=jax.ShapeDtypeStruct((M, N), out_dtype),
        scratch=[] if nk == 1 else [pltpu.VMEM((tm, tn), F32)],
        sem=("parallel", "parallel", "arbitrary"))


def _ln_stats(z):
    mu = jnp.mean(z, axis=-1, keepdims=True)
    zc = z - mu
    var = jnp.mean(zc * zc, axis=-1, keepdims=True)
    rstd = lax.rsqrt(var + LN_EPS)
    return zc * rstd, rstd


def _ln_bwd(dy, xhat, rstd, g):
    dxh = dy * g
    m1 = jnp.mean(dxh, axis=-1, keepdims=True)
    m2 = jnp.mean(dxh * xhat, axis=-1, keepdims=True)
    return rstd * (dxh - m1 - xhat * m2)


def _ln_in_fwd(order, x, meta_pad, g, b):
    S, D = x.shape
    nb = S // BLOCK

    def body(x_ref, mp_ref, g_ref, b_ref, h_ref, hb_ref):
        is_meta = pl.program_id(0) == nb
        xin = jnp.where(is_meta, mp_ref[...], x_ref[...])
        xhat, _ = _ln_stats(xin)
        y = xhat * g_ref[...] + b_ref[...]
        h_ref[...] = y
        hb_ref[...] = y.astype(BF16)

    row = pl.BlockSpec((BLOCK, D), lambda i: (i, 0))
    vec = pl.BlockSpec((1, D), lambda i: (0, 0))
    return _call(
        order, body, [x, meta_pad, g, b], name="ln_in_fwd", grid=(nb + 1,),
        in_specs=[pl.BlockSpec((BLOCK, D), lambda i: (jnp.minimum(i, nb - 1), 0)),
                  pl.BlockSpec((BLOCK, D), lambda i: (0, 0)), vec, vec],
        out_specs=[row, row],
        out_shape=[jax.ShapeDtypeStruct((S + BLOCK, D), F32), jax.ShapeDtypeStruct((S + BLOCK, D), BF16)],
        sem=("parallel",))


def _ln_in_bwd(order, x, meta_pad, g, dh0, dz1):
    S, D = x.shape
    nb = S // BLOCK

    def body(x_ref, mp_ref, g_ref, dh_ref, dz_ref, dx_ref, dg_ref, db_ref):
        i = pl.program_id(0)
        is_meta = i == nb
        xin = jnp.where(is_meta, mp_ref[...], x_ref[...])
        xhat, rstd = _ln_stats(xin)
        dy = dh_ref[...] + jnp.where(is_meta, 0.0, DN_ALPHA) * dz_ref[...]
        dx_ref[...] = _ln_bwd(dy, xhat, rstd, g_ref[...])

        @pl.when(i == 0)
        def _():
            dg_ref[...] = jnp.zeros_like(dg_ref)
            db_ref[...] = jnp.zeros_like(db_ref)

        dg_ref[...] += jnp.sum(dy * xhat, axis=0, keepdims=True)
        db_ref[...] += jnp.sum(dy, axis=0, keepdims=True)

    row = pl.BlockSpec((BLOCK, D), lambda i: (i, 0))
    rowx = pl.BlockSpec((BLOCK, D), lambda i: (jnp.minimum(i, nb - 1), 0))
    vec = pl.BlockSpec((1, D), lambda i: (0, 0))
    return _call(
        order, body, [x, meta_pad, g, dh0, dz1], name="ln_in_bwd", grid=(nb + 1,),
        in_specs=[rowx, pl.BlockSpec((BLOCK, D), lambda i: (0, 0)), vec, row, rowx],
        out_specs=[row, vec, vec],
        out_shape=[jax.ShapeDtypeStruct((S + BLOCK, D), F32), jax.ShapeDtypeStruct((1, D), F32),
                   jax.ShapeDtypeStruct((1, D), F32)],
        sem=("arbitrary",))


def _ln1_fwd(order, h0, y1, g, b):
    S, D = y1.shape
    tm = _pick(S, BLOCK, 8)

    def body(h_ref, y_ref, g_ref, b_ref, o_ref, ob_ref):
        xhat, _ = _ln_stats(DN_ALPHA * h_ref[...] + y_ref[...])
        y = xhat * g_ref[...] + b_ref[...]
        o_ref[...] = y
        ob_ref[...] = y.astype(BF16)

    row = pl.BlockSpec((tm, D), lambda i: (i, 0))
    vec = pl.BlockSpec((1, D), lambda i: (0, 0))
    return _call(
        order, body, [h0, y1, g, b], name="ln1_fwd", grid=(S // tm,), in_specs=[row, row, vec, vec],
        out_specs=[row, row],
        out_shape=[jax.ShapeDtypeStruct((S, D), F32), jax.ShapeDtypeStruct((S, D), BF16)],
        sem=("parallel",))


def _ln1_bwd(order, h0, y1, g, dh1, dz2):
    S, D = y1.shape
    tm = _pick(S, BLOCK, 8)

    def body(h_ref, y_ref, g_ref, dh_ref, dz2_ref, dz_ref, dzb_ref, dg_ref, db_ref):
        i = pl.program_id(0)
        xhat, rstd = _ln_stats(DN_ALPHA * h_ref[...] + y_ref[...])
        dy = dh_ref[...] + DN_ALPHA * dz2_ref[...]
        dz = _ln_bwd(dy, xhat, rstd, g_ref[...])
        dz_ref[...] = dz
        dzb_ref[...] = dz.astype(BF16)

        @pl.when(i == 0)
        def _():
            dg_ref[...] = jnp.zeros_like(dg_ref)
            db_ref[...] = jnp.zeros_like(db_ref)

        dg_ref[...] += jnp.sum(dy * xhat, axis=0, keepdims=True)
        db_ref[...] += jnp.sum(dy, axis=0, keepdims=True)

    row = pl.BlockSpec((tm, D), lambda i: (i, 0))
    vec = pl.BlockSpec((1, D), lambda i: (0, 0))
    return _call(
        order, body, [h0, y1, g, dh1, dz2], name="ln1_bwd", grid=(S // tm,),
        in_specs=[row, row, vec, row, row], out_specs=[row, row, vec, vec],
        out_shape=[jax.ShapeDtypeStruct((S, D), F32), jax.ShapeDtypeStruct((S, D), BF16),
                   jax.ShapeDtypeStruct((1, D), F32), jax.ShapeDtypeStruct((1, D), F32)],
        sem=("arbitrary",))


def _ln2_loss_bwd(order, h1, y2, target, g, b):
    S, D = y2.shape
    tm = _pick(S, BLOCK, 8)

    def body(h_ref, y_ref, t_ref, g_ref, b_ref, dz_ref, dzb_ref, dg_ref, db_ref, loss_ref):
        i = pl.program_id(0)
        xhat, rstd = _ln_stats(DN_ALPHA * h_ref[...] + y_ref[...])
        diff = xhat * g_ref[...] + b_ref[...] - t_ref[...]
        dy = diff / D
        dz = _ln_bwd(dy, xhat, rstd, g_ref[...])
        dz_ref[...] = dz
        dzb_ref[...] = dz.astype(BF16)

        @pl.when(i == 0)
        def _():
            dg_ref[...] = jnp.zeros_like(dg_ref)
            db_ref[...] = jnp.zeros_like(db_ref)
            loss_ref[...] = jnp.zeros_like(loss_ref)

        dg_ref[...] += jnp.sum(dy * xhat, axis=0, keepdims=True)
        db_ref[...] += jnp.sum(dy, axis=0, keepdims=True)
        loss_ref[...] += jnp.sum(jnp.mean(diff * diff, axis=-1, keepdims=True), axis=0, keepdims=True)

    row = pl.BlockSpec((tm, D), lambda i: (i, 0))
    vec = pl.BlockSpec((1, D), lambda i: (0, 0))
    one = pl.BlockSpec((1, 1), lambda i: (0, 0))
    return _call(
        order, body, [h1, y2, target, g, b], name="ln2_loss_bwd", grid=(S // tm,),
        in_specs=[row, row, row, vec, vec], out_specs=[row, row, vec, vec, one],
        out_shape=[jax.ShapeDtypeStruct((S, D), F32), jax.ShapeDtypeStruct((S, D), BF16),
                   jax.ShapeDtypeStruct((1, D), F32), jax.ShapeDtypeStruct((1, D), F32),
                   jax.ShapeDtypeStruct((1, 1), F32)],
        sem=("arbitrary",))


def _rope_table(S):
    r = jnp.arange(S + BLOCK)
    pos = jnp.where(r < S, r + N_META, jnp.maximum(r - (S + META_ROW0), 0))
    half = ROPE_DIM // 2
    lane = jnp.arange(LANES) % HEAD_DIM
    inv_freq = ROPE_THETA ** (-(lane % half).astype(F32) * 2.0 / ROPE_DIM)
    ang = pos.astype(F32)[:, None] * inv_freq[None, :]
    cos, sin = jnp.cos(ang), jnp.sin(ang)
    c = jnp.where(lane < ROPE_DIM, cos, 1.0)
    sa = jnp.where(lane < half, -sin, 0.0)
    sb = jnp.where((lane >= half) & (lane < ROPE_DIM), sin, 0.0)
    return jnp.concatenate([c, sa, sb], axis=1).astype(F32)


def _rope(x, tab):
    h = ROPE_DIM // 2
    return (x * tab[:, :LANES] + pltpu.roll(x, LANES - h, 1) * tab[:, LANES:2 * LANES]
            + pltpu.roll(x, h, 1) * tab[:, 2 * LANES:])


def _rope_t(dy, tab):
    h = ROPE_DIM // 2
    return (dy * tab[:, :LANES] + pltpu.roll(dy * tab[:, LANES:2 * LANES], h, 1)
            + pltpu.roll(dy * tab[:, 2 * LANES:], LANES - h, 1))


def _attn_tiles(g, n, S, sink_ref, q_ref, k_ref, v_ref, tab_ref):
    NQG = Q_PER_KV // 2
    R = NQG * BLOCK
    halfsel = (g % 2).astype(F32)
    prev = jnp.maximum(n - 1, 0)
    qrow = pl.ds(pl.multiple_of(n * BLOCK, BLOCK), BLOCK)
    prow = pl.ds(pl.multiple_of(prev * BLOCK, BLOCK), BLOCK)
    mrow = pl.ds(S, BLOCK)

    tq = tab_ref[qrow, :]
    qf = q_ref[...]
    q4 = jnp.concatenate([_rope(qf[:, LANES * p:LANES * (p + 1)], tq) for p in range(NQG)], axis=0).astype(BF16)

    tk = jnp.concatenate([tab_ref[mrow, :], tab_ref[prow, :], tq], axis=0)
    kr = _rope(jnp.concatenate([k_ref[mrow, :], k_ref[prow, :], k_ref[qrow, :]], axis=0), tk)
    vr = jnp.concatenate([v_ref[mrow, :], v_ref[prow, :], v_ref[qrow, :]], axis=0)

    lane = lax.broadcasted_iota(jnp.int32, kr.shape, 1)
    own = jnp.where(lane < HEAD_DIM, 1.0 - halfsel, halfsel)

    def lo_hi(t):
        mine = t * own
        other = pltpu.roll(mine, HEAD_DIM, 1)
        lo = mine * (1.0 - halfsel) + other * halfsel
        hi = other * (1.0 - halfsel) + mine * halfsel
        return lo.astype(BF16), hi.astype(BF16)

    klo, khi = lo_hi(kr)
    vlo, vhi = lo_hi(vr)

    row = lax.broadcasted_iota(jnp.int32, (R, 3 * BLOCK), 0) & (BLOCK - 1)
    col = lax.broadcasted_iota(jnp.int32, (R, 3 * BLOCK), 1)
    jj = col & (BLOCK - 1)
    no_prev = jnp.where(n >= 1, 0, 2 * BLOCK)
    mask = (((col < BLOCK) & (col >= META_ROW0))
            | ((col >= BLOCK) & (col < 2 * BLOCK) & (jj > row + no_prev))
            | ((col >= 2 * BLOCK) & (jj <= row)))

    def soft(kk, parity):
        sk = jnp.concatenate(
            [jnp.full((BLOCK, 1), sink_ref[0, Q_PER_KV * g + 2 * p + parity], F32) for p in range(NQG)], axis=0)
        s = lax.dot_general(q4, kk, _DIMS["nt"], preferred_element_type=F32) * ATTN_SCALE
        s = jnp.where(mask, s, NEG_INF)
        m = jnp.maximum(jnp.max(s, axis=1, keepdims=True), sk)
        p = jnp.exp(s - m)
        es = jnp.exp(sk - m)
        inv = 1.0 / (jnp.sum(p, axis=1, keepdims=True) + es)
        return p * inv, es * inv

    pe, sink_e = soft(klo, 0)
    po, sink_o = soft(khi, 1)
    return q4, tk, (klo, khi), (vlo, vhi), (pe, po), (sink_e, sink_o), own


def _attn_specs(S, ATTN, KVW):
    Tp = S + BLOCK
    koff, voff = ATTN // LANES, (ATTN + KVW) // LANES
    gw = Q_PER_KV * HEAD_DIM
    return [pl.BlockSpec(memory_space=pltpu.SMEM),
            pl.BlockSpec((BLOCK, gw), lambda g, n: (n, g)),
            pl.BlockSpec((Tp, LANES), lambda g, n: (0, koff + g // 2)),
            pl.BlockSpec((Tp, LANES), lambda g, n: (0, voff + g // 2)),
            pl.BlockSpec((Tp, 3 * LANES), lambda g, n: (0, 0))]


def _attn_fwd(order, proj, tab, sinks, S, ATTN, KVW):
    G = KVW // HEAD_DIM
    nb = S // BLOCK
    gw = Q_PER_KV * HEAD_DIM

    def body(sink_ref, q_ref, k_ref, v_ref, tab_ref, o_ref):
        g, n = pl.program_id(0), pl.program_id(1)
        _, _, _, (vlo, vhi), (pe, po), _, _ = _attn_tiles(g, n, S, sink_ref, q_ref, k_ref, v_ref, tab_ref)
        o4 = (jnp.dot(pe.astype(BF16), vlo, preferred_element_type=F32)
              + jnp.dot(po.astype(BF16), vhi, preferred_element_type=F32))
        o_ref[...] = jnp.concatenate(
            [o4[BLOCK * p:BLOCK * (p + 1)] for p in range(Q_PER_KV // 2)], axis=1).astype(BF16)

    return _call(
        order, body, [sinks, proj, proj, proj, tab], name="attn_fwd", grid=(G, nb),
        in_specs=_attn_specs(S, ATTN, KVW),
        out_specs=pl.BlockSpec((BLOCK, gw), lambda g, n: (n, g)),
        out_shape=jax.ShapeDtypeStruct((S, ATTN), BF16),
        sem=("parallel", "arbitrary"))


def _attn_bwd(order, proj, tab, sinks, da, dproj, S, ATTN, KVW):
    G = KVW // HEAD_DIM
    nb = S // BLOCK
    Tp = S + BLOCK
    NQG = Q_PER_KV // 2
    gw = Q_PER_KV * HEAD_DIM

    def body(sink_ref, q_ref, k_ref, v_ref, tab_ref, da_ref, dproj_in, dq_ref, dk_ref, dv_ref, ds_ref):
        del dproj_in
        g, n = pl.program_id(0), pl.program_id(1)
        q4, tk, (klo, khi), (vlo, vhi), (pe, po), (sink_e, sink_o), own = _attn_tiles(
            g, n, S, sink_ref, q_ref, k_ref, v_ref, tab_ref)
        dof = da_ref[...]
        do4 = jnp.concatenate([dof[:, LANES * p:LANES * (p + 1)] for p in range(NQG)], axis=0)

        def grads(p, vv):
            dp = lax.dot_general(do4, vv, _DIMS["nt"], preferred_element_type=F32)
            delta = jnp.sum(p * dp, axis=1, keepdims=True)
            return (p * (dp - delta) * ATTN_SCALE).astype(BF16), delta

        dse, delta_e = grads(pe, vlo)
        dso, delta_o = grads(po, vhi)

        dq4 = (jnp.dot(dse, klo, preferred_element_type=F32) + jnp.dot(dso, khi, preferred_element_type=F32))
        tq = tk[2 * BLOCK:]
        dq_ref[...] = jnp.concatenate(
            [_rope_t(dq4[BLOCK * p:BLOCK * (p + 1)], tq) for p in range(NQG)], axis=1).astype(BF16)

        lane = lax.broadcasted_iota(jnp.int32, (3 * BLOCK, LANES), 1)

        def fold(lo_part, hi_part):
            t = jnp.where(lane < HEAD_DIM, lo_part, hi_part)
            return t + pltpu.roll(t, HEAD_DIM, 1)

        dk = _rope_t(fold(lax.dot_general(dse, q4, _DIMS["tn"], preferred_element_type=F32),
                          lax.dot_general(dso, q4, _DIMS["tn"], preferred_element_type=F32)), tk) * own
        dv = fold(lax.dot_general(pe.astype(BF16), do4, _DIMS["tn"], preferred_element_type=F32),
                  lax.dot_general(po.astype(BF16), do4, _DIMS["tn"], preferred_element_type=F32)) * own

        @pl.when((n == 0) & (g % 2 == 0))
        def _():
            dk_ref[...] = jnp.zeros_like(dk_ref)
            dv_ref[...] = jnp.zeros_like(dv_ref)

        @pl.when(n == 0)
        def _():
            ds_ref[...] = jnp.zeros_like(ds_ref)

        prev = jnp.maximum(n - 1, 0)
        qrow = pl.ds(pl.multiple_of(n * BLOCK, BLOCK), BLOCK)
        prow = pl.ds(pl.multiple_of(prev * BLOCK, BLOCK), BLOCK)
        mrow = pl.ds(S, BLOCK)
        for ref, val in ((dk_ref, dk), (dv_ref, dv)):
            ref[mrow, :] += val[:BLOCK]
            ref[prow, :] += val[BLOCK:2 * BLOCK]
            ref[qrow, :] += val[2 * BLOCK:]

        srow = lax.broadcasted_iota(jnp.int32, (Q_PER_KV, LANES), 0)
        acc = jnp.zeros((Q_PER_KV, LANES), F32)
        for p in range(NQG):
            for parity, (sk, dl) in enumerate(((sink_e, delta_e), (sink_o, delta_o))):
                val = -jnp.sum(sk[BLOCK * p:BLOCK * (p + 1)] * dl[BLOCK * p:BLOCK * (p + 1)])
                acc = jnp.where(srow == 2 * p + parity, val, acc)
        ds_ref[0] += acc

    in_specs = _attn_specs(S, ATTN, KVW) + [pl.BlockSpec((BLOCK, gw), lambda g, n: (n, g)), ANY]
    slab = pl.BlockSpec((Tp, LANES), lambda g, n: (0, g // 2))
    return _call(
        order, body, [sinks, proj, proj, proj, tab, da, dproj], name="attn_bwd", grid=(G, nb), in_specs=in_specs,
        out_specs=[pl.BlockSpec((BLOCK, gw), lambda g, n: (n, g)), slab, slab,
                   pl.BlockSpec((1, Q_PER_KV, LANES), lambda g, n: (g, 0, 0))],
        out_shape=[jax.ShapeDtypeStruct(dproj.shape, BF16), jax.ShapeDtypeStruct((Tp, KVW), F32),
                   jax.ShapeDtypeStruct((Tp, KVW), F32), jax.ShapeDtypeStruct((G, Q_PER_KV, LANES), F32)],
        aliases={6: 0}, sem=("arbitrary", "arbitrary"))


def _zero_meta_block(order, Tp, IN):
    tc = _pick(IN, 4096)

    def body(o_ref):
        o_ref[...] = jnp.zeros_like(o_ref)

    return _call(
        order, body, [], name="dproj_zero_meta", grid=(IN // tc,), in_specs=[],
        out_specs=pl.BlockSpec((BLOCK, tc), lambda j: (Tp // BLOCK - 1, j)),
        out_shape=jax.ShapeDtypeStruct((Tp, IN), BF16), sem=("parallel",))


def _put_dkv(order, dk, dv, dproj, ATTN):
    Tp, KVW = dk.shape
    nkb = KVW // LANES
    koff = ATTN // LANES

    def body(dk_ref, dv_ref, dproj_in, o_ref):
        del dproj_in
        t = pl.program_id(0)
        o_ref[...] = jnp.where(t < nkb, dk_ref[...], dv_ref[...]).astype(BF16)

    src = pl.BlockSpec((Tp, LANES), lambda t: (0, t % nkb))
    return _call(
        order, body, [dk, dv, dproj], name="dproj_put_dkv", grid=(2 * nkb,), in_specs=[src, src, ANY],
        out_specs=pl.BlockSpec((Tp, LANES), lambda t: (0, koff + t)),
        out_shape=jax.ShapeDtypeStruct(dproj.shape, BF16), aliases={2: 0}, sem=("parallel",))


HALO = 16


def _window_sums(x, up):
    n = x.shape[0]
    out = []
    s = x
    for k in (1, 2, 4, 8):
        s = s + pltpu.roll(s, (n - k) if up else k, 0)
        out.append(s)
    return out


def _pool_specs(S, ub, gw, tm):
    meta_halo = (S + BLOCK - HALO) // HALO

    def main(g):
        return pl.BlockSpec((tm, gw), lambda i: (i, ub + g))

    def halo(g):
        return pl.BlockSpec((HALO, gw), lambda i: (jnp.where(i == 0, meta_halo, i * (tm // HALO) - 1), ub + g))

    return [main(g) for g in range(4)] + [halo(g) for g in range(4)]


def _pooled(main_refs, halo_refs, g):
    x = jnp.concatenate([halo_refs[g][...], main_refs[g][...]], axis=0)
    s = _window_sums(x, up=False)[g]
    return (s[HALO:] * (1.0 / POOL_WINDOWS[g]) - x[HALO:]).astype(BF16)


def _pool_fwd(order, proj, wgrp, scale, S, uoff, POOL):
    gw = POOL // 4
    tm = BLOCK

    def body(*refs):
        main, halo = refs[:4], refs[4:8]
        w_ref, sc_ref, o_ref = refs[8:]
        for g in range(4):
            mixed = jnp.dot(_pooled(main, halo, g), w_ref[g], preferred_element_type=F32)
            o_ref[:, gw * g:gw * (g + 1)] = (mixed * sc_ref[:, gw * g:gw * (g + 1)]).astype(BF16)

    return _call(
        order, body, [proj] * 8 + [wgrp, scale], name="pool_fwd", grid=(S // tm,),
        in_specs=_pool_specs(S, uoff // gw, gw, tm) + [
            pl.BlockSpec((4, gw, gw), lambda i: (0, 0, 0)), pl.BlockSpec((1, POOL), lambda i: (0, 0))],
        out_specs=pl.BlockSpec((tm, POOL), lambda i: (i, 0)),
        out_shape=jax.ShapeDtypeStruct((S, POOL), BF16), sem=("parallel",))


def _pool_bwd_mix(order, proj, wgrp, scale, dps, S, uoff, POOL):
    gw = POOL // 4
    tm = BLOCK

    def body(*refs):
        main, halo = refs[:4], refs[4:8]
        w_ref, sc_ref, dps_ref, dpl_ref, dw_ref, dsc_ref = refs[8:]
        i = pl.program_id(0)

        @pl.when(i == 0)
        def _():
            dw_ref[...] = jnp.zeros_like(dw_ref)
            dsc_ref[...] = jnp.zeros_like(dsc_ref)

        for g in range(4):
            cols = slice(gw * g, gw * (g + 1))
            pooled = _pooled(main, halo, g)
            mixed = jnp.dot(pooled, w_ref[g], preferred_element_type=F32)
            dps_g = dps_ref[:, cols]
            dsc_ref[:, cols] += jnp.sum(dps_g * mixed, axis=0, keepdims=True)
            dms = (dps_g * sc_ref[:, cols]).astype(BF16)
            dw_ref[g] += lax.dot_general(pooled, dms, _DIMS["tn"], preferred_element_type=F32)
            dpl_ref[:, cols] = lax.dot_general(dms, w_ref[g], _DIMS["nt"], preferred_element_type=F32)

    row = pl.BlockSpec((tm, POOL), lambda i: (i, 0))
    return _call(
        order, body, [proj] * 8 + [wgrp, scale, dps], name="pool_bwd_mix", grid=(S // tm,),
        in_specs=_pool_specs(S, uoff // gw, gw, tm) + [
            pl.BlockSpec((4, gw, gw), lambda i: (0, 0, 0)), pl.BlockSpec((1, POOL), lambda i: (0, 0)), row],
        out_specs=[row, pl.BlockSpec((4, gw, gw), lambda i: (0, 0, 0)), pl.BlockSpec((1, POOL), lambda i: (0, 0))],
        out_shape=[jax.ShapeDtypeStruct((S, POOL), F32), jax.ShapeDtypeStruct((4, gw, gw), F32),
                   jax.ShapeDtypeStruct((1, POOL), F32)],
        sem=("arbitrary",))


def _pool_bwd_window(order, dpl, dproj, S, uoff, POOL):
    gw = POOL // 4
    nb = S // BLOCK
    ub = uoff // gw

    def body(main_ref, halo_ref, dproj_in, o_ref):
        del dproj_in
        b, g = pl.program_id(0), pl.program_id(1)
        main = jnp.where(b < nb, main_ref[...], 0.0)
        halo = jnp.where(b == nb - 1, 0.0, halo_ref[...])
        sums = _window_sums(jnp.concatenate([main, halo], axis=0), up=True)
        du = jnp.zeros((BLOCK, gw), F32)
        for k, w in enumerate(POOL_WINDOWS):
            du = jnp.where(g == k, sums[k][:BLOCK] * (1.0 / w), du)
        du = du - main
        row = lax.broadcasted_iota(jnp.int32, du.shape, 0)
        first_valid = jnp.where(b == nb, META_ROW0, 0)
        o_ref[...] = jnp.where(row >= first_valid, du, 0.0).astype(BF16)

    return _call(
        order, body, [dpl, dpl, dproj], name="pool_bwd_window", grid=(nb + 1, 4),
        in_specs=[pl.BlockSpec((BLOCK, gw), lambda b, g: (jnp.minimum(b, nb - 1), g)),
                  pl.BlockSpec((HALO, gw), lambda b, g: (
                      jnp.where(b == nb, 0, jnp.minimum((b + 1) * (BLOCK // HALO), S // HALO - 1)), g)),
                  ANY],
        out_specs=pl.BlockSpec((BLOCK, gw), lambda b, g: (b, ub + g)),
        out_shape=jax.ShapeDtypeStruct(dproj.shape, BF16), aliases={2: 0}, sem=("parallel", "parallel"))


def _sigmoid(x):
    return 1.0 / (1.0 + jnp.exp(-x))


def _gate_tiles(S, D, goff):
    tc = 512
    while goff % tc or D % tc:
        tc //= 2
    return _pick(S, 512, 8), tc


def _gate_mix(order, proj, bgate, a_out, p_out, S, D, goff):
    tm, tc = _gate_tiles(S, D, goff)
    g0b, nd = goff // tc, D // tc

    def body(l0_ref, l1_ref, b_ref, a_ref, p_ref, o_ref):
        g0 = _sigmoid(l0_ref[...] + b_ref[0:1, :])
        g1 = _sigmoid(l1_ref[...] + b_ref[1:2, :])
        o_ref[...] = (g0 * a_ref[...] + g1 * p_ref[...]).astype(BF16)

    tile = pl.BlockSpec((tm, tc), lambda i, j: (i, j))
    return _call(
        order, body, [proj, proj, bgate, a_out, p_out], name="gate_mix", grid=(S // tm, nd),
        in_specs=[pl.BlockSpec((tm, tc), lambda i, j: (i, g0b + j)),
                  pl.BlockSpec((tm, tc), lambda i, j: (i, g0b + nd + j)),
                  pl.BlockSpec((2, tc), lambda i, j: (0, j)), tile, tile],
        out_specs=tile, out_shape=jax.ShapeDtypeStruct((S, D), BF16), sem=("parallel", "parallel"))


def _gate_bwd(order, proj, bgate, a_out, p_out, dmixed, dproj, S, D, goff):
    tm, tc = _gate_tiles(S, D, goff)
    g0b, nd = goff // tc, D // tc

    def body(l0_ref, l1_ref, b_ref, a_ref, p_ref, dm_ref, dproj_in, dap_ref, dl_ref, db_ref):
        del dproj_in
        i, br = pl.program_id(1), pl.program_id(2)
        first = br == 0
        logit = jnp.where(first, l0_ref[...], l1_ref[...]) + jnp.where(first, b_ref[0:1, :], b_ref[1:2, :])
        val = jnp.where(first, a_ref[...], p_ref[...])
        gate = _sigmoid(logit)
        dm = dm_ref[...]
        dap_ref[...] = (dm * gate).astype(BF16)
        dl = dm * val * gate * (1.0 - gate)
        dl_ref[...] = dl.astype(BF16)

        @pl.when((i == 0) & first)
        def _():
            db_ref[...] = jnp.zeros_like(db_ref)

        db_ref[br] += jnp.sum(dl, axis=0, keepdims=True)

    tile = pl.BlockSpec((tm, tc), lambda j, i, br: (i, j))
    return _call(
        order, body, [proj, proj, bgate, a_out, p_out, dmixed, dproj], name="gate_bwd", grid=(nd, S // tm, 2),
        in_specs=[pl.BlockSpec((tm, tc), lambda j, i, br: (i, g0b + j)),
                  pl.BlockSpec((tm, tc), lambda j, i, br: (i, g0b + nd + j)),
                  pl.BlockSpec((2, tc), lambda j, i, br: (0, j)), tile, tile, tile, ANY],
        out_specs=[pl.BlockSpec((None, tm, tc), lambda j, i, br: (br, i, j)),
                   pl.BlockSpec((tm, tc), lambda j, i, br: (i, g0b + br * nd + j)),
                   pl.BlockSpec((2, 1, tc), lambda j, i, br: (0, 0, j))],
        out_shape=[jax.ShapeDtypeStruct((2, S, D), BF16), jax.ShapeDtypeStruct(dproj.shape, BF16),
                   jax.ShapeDtypeStruct((2, 1, D), F32)],
        aliases={6: 1}, sem=("parallel", "arbitrary", "arbitrary"))


def _swiglu_fwd(order, f, S, FF):
    tc = _pick(FF, 5504)
    nj = FF // tc

    def body(g_ref, u_ref, o_ref):
        gt = g_ref[...]
        o_ref[...] = (gt * _sigmoid(gt) * u_ref[...]).astype(BF16)

    return _call(
        order, body, [f, f], name="swiglu_fwd", grid=(S // BLOCK, nj),
        in_specs=[pl.BlockSpec((BLOCK, tc), lambda i, j: (i, j)), pl.BlockSpec((BLOCK, tc), lambda i, j: (i, nj + j))],
        out_specs=pl.BlockSpec((BLOCK, tc), lambda i, j: (i, j)),
        out_shape=jax.ShapeDtypeStruct((S, FF), BF16), sem=("parallel", "parallel"))


def _swiglu_bwd(order, f, dact, S, FF):
    tc = _pick(FF, 5504)
    nj = FF // tc

    def body(g_ref, u_ref, d_ref, o_ref):
        br = pl.program_id(2)
        gt, d = g_ref[...], d_ref[...]
        s = _sigmoid(gt)
        dgate = d * u_ref[...] * s * (1.0 + gt * (1.0 - s))
        dup = d * gt * s
        o_ref[...] = jnp.where(br == 0, dgate, dup).astype(BF16)

    return _call(
        order, body, [f, f, dact], name="swiglu_bwd", grid=(S // BLOCK, nj, 2),
        in_specs=[pl.BlockSpec((BLOCK, tc), lambda i, j, br: (i, j)),
                  pl.BlockSpec((BLOCK, tc), lambda i, j, br: (i, nj + j)),
                  pl.BlockSpec((BLOCK, tc), lambda i, j, br: (i, j))],
        out_specs=pl.BlockSpec((BLOCK, tc), lambda i, j, br: (i, br * nj + j)),
        out_shape=jax.ShapeDtypeStruct((S, 2 * FF), BF16), sem=("parallel", "parallel", "arbitrary"))


def _place():
    return lax.axis_index("x"), lax.axis_index("y"), lax.axis_index("c")


def _xfer_start(order, name, bufs, copies):
    nb = len(bufs)
    n = len(copies([None] * nb, None))
    is_new = [isinstance(b, jax.ShapeDtypeStruct) for b in bufs]
    old = [b for b, fresh in zip(bufs, is_new) if not fresh]
    no = len(old)
    tok = [] if any(order.last is b for b in old) else [order.last]
    first_out = no + len(tok)

    def body(*refs):
        send, recv = refs[first_out:first_out + n], refs[first_out + n:first_out + 2 * n]
        token = refs[-1]
        given, made = iter(refs[:no]), iter(refs[first_out + 2 * n + no:-1])
        logical = [next(made) if fresh else next(given) for fresh in is_new]
        for i, (src, dst, dev) in enumerate(copies(logical, _place())):
            pltpu.make_async_remote_copy(src_ref=src, dst_ref=dst, send_sem=send[i], recv_sem=recv[i],
                                         device_id=dev, device_id_type=MESH).start()
        token[...] = jnp.zeros_like(token)

    fresh_shapes = [b for b, fresh in zip(bufs, is_new) if fresh]
    out = pl.pallas_call(
        body, name=name,
        out_shape=tuple([pltpu.SemaphoreType.DMA(())] * (2 * n)
                        + [pltpu.HBM(b.shape, b.dtype) for b in old + fresh_shapes]
                        + [jax.ShapeDtypeStruct((8, LANES), F32)]),
        in_specs=[HBM] * no + [ANY] * len(tok),
        out_specs=tuple([SEM] * (2 * n) + [HBM] * nb + [pl.BlockSpec(memory_space=pltpu.VMEM)]),
        input_output_aliases={i: 2 * n + i for i in range(no)},
        compiler_params=pltpu.CompilerParams(has_side_effects=EFFECT),
    )(*[pltpu.with_memory_space_constraint(b, pltpu.HBM) for b in old], *tok)
    order.last = out[-1]
    thru, made = iter(out[2 * n:2 * n + no]), iter(out[2 * n + no:2 * n + nb])
    return list(out[:2 * n]), [next(made) if fresh else next(thru) for fresh in is_new]


def _xfer_wait(order, name, sems, bufs, copies):
    nb = len(bufs)
    n = len(sems) // 2
    tok = order.last

    def body(*refs):
        send, recv = refs[nb:nb + n], refs[nb + n:nb + 2 * n]
        token = refs[-1]
        for i, (src, dst, dev) in enumerate(copies(refs[:nb], _place())):
            cp = pltpu.make_async_remote_copy(src_ref=src, dst_ref=dst, send_sem=send[i], recv_sem=recv[i],
                                              device_id=dev, device_id_type=MESH)
            cp.wait_send()
            cp.wait_recv()
        token[...] = jnp.zeros_like(token)

    out = pl.pallas_call(
        body, name=name,
        out_shape=tuple([pltpu.HBM(b.shape, b.dtype) for b in bufs] + [jax.ShapeDtypeStruct((8, LANES), F32)]),
        in_specs=[HBM] * nb + [SEM] * (2 * n) + [ANY],
        out_specs=tuple([HBM] * nb + [pl.BlockSpec(memory_space=pltpu.VMEM)]),
        input_output_aliases={i: i for i in range(nb)},
        compiler_params=pltpu.CompilerParams(has_side_effects=EFFECT),
    )(*bufs, *sems, tok)
    order.last = out[-1]
    return list(out[:nb])


class _Xfer:
    def __init__(self, name, bufs, copies):
        self.name, self.bufs, self.copies = name, list(bufs), copies
        self.sems = None

    def start(self, order):
        self.sems, self.bufs = _xfer_start(order, self.name + "_start", self.bufs, self.copies)

    def wait(self, order):
        self.bufs = _xfer_wait(order, self.name + "_wait", self.sems, self.bufs, self.copies)
        return self.bufs


def _block_rows(ref, r, d):
    return ref.at[pl.ds(d * r, r)]


def _gather_send(fulls):
    def copies(refs, place):
        out = []
        for w, full in enumerate(fulls):
            r = full.shape[0] // 8
            if place is None:
                out += [None] * 4
                continue
            x, y, c = place
            mine = _block_rows(refs[w], r, 4 * x + 2 * y + c)
            out.append((mine, mine, (x, y, 1 - c)))
            for px, py in ((1 - x, y), (x, 1 - y), (1 - x, 1 - y)):
                out.append((mine, mine, (px, py, c)))
        return out
    return copies


def _gather_forward(fulls):
    def copies(refs, place):
        out = []
        for w, full in enumerate(fulls):
            r = full.shape[0] // 8
            if place is None:
                out += [None] * 3
                continue
            x, y, c = place
            for px, py in ((1 - x, y), (x, 1 - y), (1 - x, 1 - y)):
                blk = _block_rows(refs[w], r, 4 * px + 2 * py + c)
                out.append((blk, blk, (x, y, 1 - c)))
        return out
    return copies


def _pair_send(nw):
    def copies(refs, place):
        out = []
        for w in range(nw):
            if place is None:
                out += [None] * 4
                continue
            x, y, c = place
            grad, other = refs[2 * w], refs[2 * w + 1]
            r = other.shape[1]
            for k in range(4):
                out.append((_block_rows(grad, r, 2 * k + 1 - c), other.at[k], (x, y, 1 - c)))
        return out
    return copies


def _chip_send(nw):
    def copies(refs, place):
        out = []
        for w in range(nw):
            if place is None:
                out += [None] * 3
                continue
            x, y, c = place
            psum, parts = refs[2 * w], refs[2 * w + 1]
            for px, py in ((1 - x, y), (x, 1 - y), (1 - x, 1 - y)):
                out.append((psum.at[2 * px + py], parts.at[2 * x + y], (px, py, c)))
        return out
    return copies


def _dev_index():
    x, y, c = _place()
    return 4 * x + 2 * y + c


def _place_own(order, shard, name):
    r, cols = shard.shape
    tr = _pick(r, max(16, (2 << 20) // (4 * cols)), 16)
    nr = r // tr

    def body(s_ref, o_ref):
        o_ref[...] = s_ref[...].astype(BF16)

    return _call(
        order, body, [shard], name=name, grid=(nr,),
        in_specs=[pl.BlockSpec((tr, cols), lambda i: (i, 0))],
        out_specs=pl.BlockSpec((tr, cols), lambda i: (_dev_index() * nr + i, 0)),
        out_shape=jax.ShapeDtypeStruct((8 * r, cols), BF16), sem=("parallel",))


def _pair_sum(order, grad, other, name):
    r, cols = other.shape[1:]
    tr = _pick(r, max(16, (4 << 20) // (2 * cols)), 16)
    nr = r // tr

    def body(g_ref, a_ref, o_ref):
        o_ref[...] = (g_ref[...].astype(F32) + a_ref[...].astype(F32)).astype(BF16)

    blk = pl.BlockSpec((None, tr, cols), lambda k, i: (k, i, 0))
    return _call(
        order, body, [grad, other], name=name, grid=(4, nr),
        in_specs=[pl.BlockSpec((tr, cols), lambda k, i: ((2 * k + lax.axis_index("c")) * nr + i, 0)), blk],
        out_specs=blk, out_shape=jax.ShapeDtypeStruct(other.shape, BF16), sem=("parallel", "parallel"))


def _chip_sum(order, psum, parts, name):
    _, r, cols = parts.shape
    tr = _pick(r, max(16, (1 << 20) // (2 * cols)), 16)

    def my_chip():
        return 2 * lax.axis_index("x") + lax.axis_index("y")

    def body(own_ref, p0, p1, p2, p3, o_ref):
        own = own_ref[...].astype(F32)
        acc = None
        for k, p in enumerate((p0, p1, p2, p3)):
            term = jnp.where(my_chip() == k, own, p[...].astype(F32))
            acc = term if acc is None else acc + term
        o_ref[...] = acc

    def slot(k):
        return pl.BlockSpec((None, tr, cols), lambda i: (jnp.where(my_chip() == k, (k + 1) % 4, k), i, 0))

    return _call(
        order, body, [psum, parts, parts, parts, parts], name=name, grid=(r // tr,),
        in_specs=[pl.BlockSpec((None, tr, cols), lambda i: (my_chip(), i, 0))] + [slot(k) for k in range(4)],
        out_specs=pl.BlockSpec((tr, cols), lambda i: (i, 0)),
        out_shape=jax.ShapeDtypeStruct((r, cols), F32), sem=("parallel",))


def _all_reduce_small(order, pack, name):
    R = pack.shape[0]

    def body(p_ref, o_ref, buf, send_sems, recv_sems):
        x, y, c = _place()
        me = 4 * x + 2 * y + c
        buf[me] = p_ref[...]
        copies = []
        for k in range(1, 8):
            px = 1 - x if k & 4 else x
            py = 1 - y if k & 2 else y
            pc = 1 - c if k & 1 else c
            cp = pltpu.make_async_remote_copy(
                src_ref=p_ref, dst_ref=buf.at[me], send_sem=send_sems.at[k - 1], recv_sem=recv_sems.at[k - 1],
                device_id=(px, py, pc), device_id_type=MESH)
            cp.start()
            copies.append(cp)
        for cp in copies:
            cp.wait_recv()
        acc = buf[0]
        for d in range(1, 8):
            acc = acc + buf[d]
        o_ref[...] = acc
        for cp in copies:
            cp.wait_send()

    vm = pl.BlockSpec(memory_space=pltpu.VMEM)
    return _call(
        order, body, [pack], name=name, in_specs=[vm], out_specs=vm,
        out_shape=jax.ShapeDtypeStruct((R, LANES), F32),
        scratch=[pltpu.VMEM((8, R, LANES), F32), pltpu.SemaphoreType.DMA((7,)), pltpu.SemaphoreType.DMA((7,))])


def _pack(parts):
    flat = []
    for p in parts:
        v = p.reshape(-1).astype(F32)
        flat.append(jnp.pad(v, (0, (-v.shape[0]) % LANES)))
    v = jnp.concatenate(flat)
    v = jnp.pad(v, (0, (-v.shape[0]) % (8 * LANES)))
    return v.reshape(-1, LANES)


def _unpack(pack, shapes):
    v = pack.reshape(-1)
    out, off = [], 0
    for s in shapes:
        n = 1
        for d in s:
            n *= d
        out.append(v[off:off + n].reshape(s))
        off += n + (-n) % LANES
    return out


def _adamw(order, w, g, m, v, name):
    shape = w.shape
    cols = shape[-1]
    w2, g2, m2, v2 = (t.reshape(-1, cols) for t in (w, g, m, v))
    R = w2.shape[0]
    tr = _pick(R, max(8, (1 << 20) // (4 * cols)), 8)

    def body(w_ref, g_ref, m_ref, v_ref, d_ref, mo_ref, vo_ref):
        gr = g_ref[...]
        mn = ADAM_B1 * m_ref[...] + (1.0 - ADAM_B1) * gr
        vn = ADAM_B2 * v_ref[...] + (1.0 - ADAM_B2) * (gr * gr)
        m_hat = mn / (1.0 - ADAM_B1 ** ADAM_STEP)
        v_hat = vn / (1.0 - ADAM_B2 ** ADAM_STEP)
        d_ref[...] = -ADAM_LR * (m_hat / (jnp.sqrt(v_hat) + ADAM_EPS) + ADAM_WD * w_ref[...])
        mo_ref[...] = mn
        vo_ref[...] = vn

    blk = pl.BlockSpec((tr, cols), lambda i: (i, 0))
    outs = _call(
        order, body, [w2, g2, m2, v2], name=name, grid=(R // tr,), in_specs=[blk] * 4, out_specs=[blk] * 3,
        out_shape=[jax.ShapeDtypeStruct((R, cols), F32)] * 3, sem=("parallel",))
    return tuple(o.reshape(shape) for o in outs)


class _GradReduce:
    def __init__(self, tag, grads, names):
        self.tag, self.grads, self.names = tag, list(grads), names
        self.pair = self.chip = self.psums = None

    def pair_start(self, order):
        bufs = []
        for g in self.grads:
            bufs += [g, jax.ShapeDtypeStruct((4, g.shape[0] // 8, g.shape[1]), g.dtype)]
        self.pair = _Xfer("pair_" + self.tag, bufs, _pair_send(len(self.grads)))
        self.pair.start(order)

    def pair_sum_chip_start(self, order):
        bufs = self.pair.wait(order)
        self.psums = [_pair_sum(order, bufs[2 * w], bufs[2 * w + 1], "pair_sum_" + nm)
                      for w, nm in enumerate(self.names)]
        cbufs = []
        for p in self.psums:
            cbufs += [p, jax.ShapeDtypeStruct(p.shape, p.dtype)]
        self.chip = _Xfer("chip_" + self.tag, cbufs, _chip_send(len(self.psums)))
        self.chip.start(order)

    def finish(self, order):
        bufs = self.chip.wait(order)
        return [_chip_sum(order, bufs[2 * w], bufs[2 * w + 1], "chip_sum_" + nm)
                for w, nm in enumerate(self.names)]


def kernel(x, meta_tokens, ln_in_g, ln_in_b, w_in, b_gate, attn_sinks, w_attn_up, w_pool_grp, pool_scale, w_pool_up, w_out, ln1_g, ln1_b, w_ffn_in, w_ffn_down, ln2_g, ln2_b, loss_target, m_meta_tokens, m_ln_in_g, m_ln_in_b, m_w_in, m_b_gate, m_attn_sinks, m_w_attn_up, m_w_pool_grp, m_pool_scale, m_w_pool_up, m_w_out, m_ln1_g, m_ln1_b, m_w_ffn_in, m_w_ffn_down, m_ln2_g, m_ln2_b, v_meta_tokens, v_ln_in_g, v_ln_in_b, v_w_in, v_b_gate, v_attn_sinks, v_w_attn_up, v_w_pool_grp, v_pool_scale, v_w_pool_up, v_w_out, v_ln1_g, v_ln1_b, v_w_ffn_in, v_w_ffn_down, v_ln2_g, v_ln2_b):
    S, D = x.shape[1], x.shape[2]
    Tp = S + BLOCK
    NQ = attn_sinks.shape[-1]
    ATTN = NQ * HEAD_DIM
    KVW = ATTN // Q_PER_KV
    POOL = pool_scale.shape[-1]
    IN = 8 * w_in.shape[2]
    FF = 8 * w_ffn_down.shape[1]
    uoff = ATTN + 2 * KVW
    goff = uoff + POOL
    gw = POOL // 4
    dcols = D // 8
    assert IN == goff + 2 * D and w_ffn_in.shape[2] * 8 == 2 * FF

    xi, yi, ci = _place()
    dev = 4 * xi + 2 * yi + ci
    x2, tgt = x[0], loss_target[0]
    order = _Order()

    def place_cols(a):
        return lax.dynamic_update_slice(jnp.zeros(a.shape[:-1] + (D,), F32), a, (0,) * (a.ndim - 1) + (dev * dcols,))

    small = _all_reduce_small(order, _pack([place_cols(meta_tokens), place_cols(b_gate[0])]), "small_inputs_gather")
    meta_full, bgate_full = _unpack(small, [(N_META, D), (2, D)])
    meta_pad = jnp.pad(meta_full, ((META_ROW0, 0), (0, 0)))

    wgrp_rows = w_pool_grp[0].reshape(4 * (gw // 8), gw)
    full_in = _place_own(order, w_in[0].T, "own_w_in")
    ag_in = _Xfer("gather_w_in", [full_in], _gather_send([full_in]))
    ag_in.start(order)
    mix_names = ["w_attn_up", "w_pool_grp", "w_pool_up", "w_out"]
    mix_shards = [w_attn_up[0].T, wgrp_rows, w_pool_up[0].T, w_out[0]]
    full_mix = [_place_own(order, s, "own_" + nm) for s, nm in zip(mix_shards, mix_names)]
    full_ffn = _place_own(order, w_ffn_in[0].T, "own_w_ffn_in")
    full_down = _place_own(order, w_ffn_down[0], "own_w_ffn_down")

    ln_in_g2, ln_in_b2 = ln_in_g.reshape(1, D), ln_in_b.reshape(1, D)
    tab = _rope_table(S)

    (full_in,) = ag_in.wait(order)
    ag_mix = _Xfer("gather_mixers", full_mix, _gather_send(full_mix))
    ag_mix.start(order)
    ag_ffn = _Xfer("gather_w_ffn_in", [full_ffn], _gather_send([full_ffn]))
    ag_ffn.start(order)
    fw_in = _Xfer("forward_w_in", [full_in], _gather_forward([full_in]))
    fw_in.start(order)

    h0, h0b = _ln_in_fwd(order, x2, meta_pad, ln_in_g2, ln_in_b2)
    (winT,) = fw_in.wait(order)
    proj = _mm(order, h0b, winT, kind="nt", out_dtype=F32, tm=1408, tn=512, name="proj")

    full_mix = ag_mix.wait(order)
    ag_down = _Xfer("gather_w_ffn_down", [full_down], _gather_send([full_down]))
    ag_down.start(order)
    fw_mix = _Xfer("forward_mixers", full_mix, _gather_forward(full_mix))
    fw_mix.start(order)
    att = _attn_fwd(order, proj, tab, attn_sinks, S, ATTN, KVW)
    wattT, wgrp_g, wpupT, wout = fw_mix.wait(order)
    wgrp = wgrp_g.reshape(8, 4, gw // 8, gw).transpose(1, 0, 2, 3).reshape(4, gw, gw)

    ps = _pool_fwd(order, proj, wgrp, pool_scale, S, uoff, POOL)
    a_out = _mm(order, att, wattT, kind="nt", out_dtype=F32, tm=1024, tn=1024, name="attn_up")
    p_out = _mm(order, ps, wpupT, kind="nt", out_dtype=F32, tm=1024, tn=1024, name="pool_up")
    mixed = _gate_mix(order, proj, bgate_full, a_out, p_out, S, D, goff)
    y1 = _mm(order, mixed, wout, kind="nn", out_dtype=F32, tm=1024, tn=1024, name="out_proj")

    (full_ffn,) = ag_ffn.wait(order)
    fw_ffn = _Xfer("forward_w_ffn_in", [full_ffn], _gather_forward([full_ffn]))
    fw_ffn.start(order)
    h1, h1b = _ln1_fwd(order, h0, y1, ln1_g, ln1_b)
    (wffnT,) = fw_ffn.wait(order)
    f = _mm(order, h1b, wffnT, kind="nt", out_dtype=F32, tm=1024, tn=512, name="ffn_in")

    (full_down,) = ag_down.wait(order)
    fw_down = _Xfer("forward_w_ffn_down", [full_down], _gather_forward([full_down]))
    fw_down.start(order)
    act = _swiglu_fwd(order, f, S, FF)
    (wdown,) = fw_down.wait(order)
    y2 = _mm(order, act, wdown, kind="nn", out_dtype=F32, tm=512, tn=1024, tk=5504, name="ffn_down")

    dz2, dz2b, dg2, db2, loss_part = _ln2_loss_bwd(order, h1, y2, tgt, ln2_g, ln2_b)
    gwdown = _mm(order, act, dz2b, kind="tn", out_dtype=BF16, tm=256, tn=2048, name="d_ffn_down")
    rs_down = _GradReduce("w_ffn_down", [gwdown], ["w_ffn_down"])
    rs_down.pair_start(order)
    dact = _mm(order, dz2b, wdown, kind="nt", out_dtype=F32, tm=2048, tn=256, name="d_act")
    rs_down.pair_sum_chip_start(order)
    df = _swiglu_bwd(order, f, dact, S, FF)
    gwffnT = _mm(order, df, h1b, kind="tn", out_dtype=BF16, tm=512, tn=1024, name="d_ffn_in")
    rs_ffn = _GradReduce("w_ffn_in", [gwffnT], ["w_ffn_in"])
    rs_ffn.pair_start(order)
    dh1 = _mm(order, df, wffnT, kind="nn", out_dtype=F32, tm=512, tn=1024, tk=5504, name="d_h1")
    rs_ffn.pair_sum_chip_start(order)
    dz1, dz1b, dg1, db1 = _ln1_bwd(order, h0, y1, ln1_g, dh1, dz2)
    gwout = _mm(order, mixed, dz1b, kind="tn", out_dtype=BF16, tm=512, tn=1024, name="d_out_proj")
    rs_out = _GradReduce("w_out", [gwout], ["w_out"])
    rs_out.pair_start(order)
    dmixed = _mm(order, dz1b, wout, kind="nt", out_dtype=F32, tm=1024, tn=1024, name="d_mixed")
    rs_out.pair_sum_chip_start(order)

    dproj = _zero_meta_block(order, Tp, IN)
    dap, dproj, dbgate = _gate_bwd(order, proj, bgate_full, a_out, p_out, dmixed, dproj, S, D, goff)
    gwattT = _mm(order, dap, att, kind="tn", out_dtype=BF16, tm=512, tn=1024, name="d_attn_up", a_lead=0)
    datt = _mm(order, dap, wattT, kind="nn", out_dtype=BF16, tm=1024, tn=1024, name="d_att", a_lead=0)
    gwpupT = _mm(order, dap, ps, kind="tn", out_dtype=BF16, tm=512, tn=1024, name="d_pool_up", a_lead=1)
    dps = _mm(order, dap, wpupT, kind="nn", out_dtype=F32, tm=1024, tn=1024, name="d_ps", a_lead=1)
    dpl, gwgrp, dscale = _pool_bwd_mix(order, proj, wgrp, pool_scale, dps, S, uoff, POOL)
    gwgrp_rows = gwgrp.reshape(4, 8, gw // 8, gw).transpose(1, 0, 2, 3).reshape(8 * 4 * (gw // 8), gw).astype(BF16)
    rs_mix = _GradReduce("mixers", [gwattT, gwgrp_rows, gwpupT], ["w_attn_up", "w_pool_grp", "w_pool_up"])
    rs_mix.pair_start(order)
    dproj = _pool_bwd_window(order, dpl, dproj, S, uoff, POOL)
    rs_mix.pair_sum_chip_start(order)
    dproj, dk, dv, dsink = _attn_bwd(order, proj, tab, attn_sinks, datt, dproj, S, ATTN, KVW)
    dproj = _put_dkv(order, dk, dv, dproj, ATTN)
    dh0 = _mm(order, dproj, winT, kind="nn", out_dtype=F32, tm=1408, tn=1024, tk=2560, name="d_h0")
    gwinT = _mm(order, dproj, h0b, kind="tn", out_dtype=BF16, tm=512, tn=1024, name="d_w_in")
    rs_in = _GradReduce("w_in", [gwinT], ["w_in"])
    rs_in.pair_start(order)
    dxin, dg_in, db_in = _ln_in_bwd(order, x2, meta_pad, ln_in_g2, dh0, dz1)
    rs_in.pair_sum_chip_start(order)
    grad_x = dxin[:S][None]
    dmeta = dxin[S + META_ROW0:]

    weights = dict(meta_tokens=meta_tokens, ln_in_g=ln_in_g, ln_in_b=ln_in_b, w_in=w_in, b_gate=b_gate,
                   attn_sinks=attn_sinks, w_attn_up=w_attn_up, w_pool_grp=w_pool_grp, pool_scale=pool_scale,
                   w_pool_up=w_pool_up, w_out=w_out, ln1_g=ln1_g, ln1_b=ln1_b, w_ffn_in=w_ffn_in,
                   w_ffn_down=w_ffn_down, ln2_g=ln2_g, ln2_b=ln2_b)
    ms = dict(meta_tokens=m_meta_tokens, ln_in_g=m_ln_in_g, ln_in_b=m_ln_in_b, w_in=m_w_in, b_gate=m_b_gate,
              attn_sinks=m_attn_sinks, w_attn_up=m_w_attn_up, w_pool_grp=m_w_pool_grp, pool_scale=m_pool_scale,
              w_pool_up=m_w_pool_up, w_out=m_w_out, ln1_g=m_ln1_g, ln1_b=m_ln1_b, w_ffn_in=m_w_ffn_in,
              w_ffn_down=m_w_ffn_down, ln2_g=m_ln2_g, ln2_b=m_ln2_b)
    vs = dict(meta_tokens=v_meta_tokens, ln_in_g=v_ln_in_g, ln_in_b=v_ln_in_b, w_in=v_w_in, b_gate=v_b_gate,
              attn_sinks=v_attn_sinks, w_attn_up=v_w_attn_up, w_pool_grp=v_w_pool_grp, pool_scale=v_pool_scale,
              w_pool_up=v_w_pool_up, w_out=v_w_out, ln1_g=v_ln1_g, ln1_b=v_ln1_b, w_ffn_in=v_w_ffn_in,
              w_ffn_down=v_w_ffn_down, ln2_g=v_ln2_g, ln2_b=v_ln2_b)
    grads, deltas, new_ms, new_vs = {}, {}, {}, {}

    def update(nm, g, transposed=False):
        if transposed:
            d, mn, vn = _adamw(order, weights[nm][0].T, g, ms[nm][0].T, vs[nm][0].T, "adamw_" + nm)
            grads[nm], deltas[nm], new_ms[nm], new_vs[nm] = (t.T[None] for t in (g, d, mn, vn))
            return
        g = g.reshape(weights[nm].shape)
        grads[nm] = g
        deltas[nm], new_ms[nm], new_vs[nm] = _adamw(order, weights[nm], g, ms[nm], vs[nm], "adamw_" + nm)

    (g_down,) = rs_down.finish(order)
    update("w_ffn_down", g_down)
    (g_ffn,) = rs_ffn.finish(order)
    update("w_ffn_in", g_ffn, transposed=True)
    (g_out,) = rs_out.finish(order)
    update("w_out", g_out)
    g_att, g_grp, g_pup = rs_mix.finish(order)
    update("w_attn_up", g_att, transposed=True)
    update("w_pool_grp", g_grp)
    update("w_pool_up", g_pup, transposed=True)

    small_shapes = [(D,), (D,), (1, D), (1, D), (1, D), (1, D), (1, POOL), (1, NQ), (), (N_META, D), (2, D)]
    red = _all_reduce_small(order, _pack([dg_in, db_in, dg1, db1, dg2, db2, dscale, dsink[:, :, 0], loss_part,
                                          dmeta, dbgate]), "small_grads_all_reduce")
    (g_ln_in_g, g_ln_in_b, g_ln1_g, g_ln1_b, g_ln2_g, g_ln2_b, g_scale, g_sinks, loss_sum, g_meta_full,
     g_bgate_full) = _unpack(red, small_shapes)
    loss = 0.5 * loss_sum
    update("meta_tokens", lax.dynamic_slice(g_meta_full, (0, dev * dcols), (N_META, dcols)))
    update("b_gate", lax.dynamic_slice(g_bgate_full, (0, dev * dcols), (2, dcols)))
    for nm, g in (("ln_in_g", g_ln_in_g), ("ln_in_b", g_ln_in_b), ("ln1_g", g_ln1_g), ("ln1_b", g_ln1_b),
                  ("ln2_g", g_ln2_g), ("ln2_b", g_ln2_b), ("pool_scale", g_scale), ("attn_sinks", g_sinks)):
        update(nm, g)

    (g_in,) = rs_in.finish(order)
    update("w_in", g_in, transposed=True)

    names = list(weights)
    return (loss, grad_x, *[grads[n] for n in names], *[deltas[n] for n in names],
            *[new_ms[n] for n in names], *[new_vs[n] for n in names])
```

```python
import jax
import jax.numpy as jnp
from jax import lax
from jax.experimental import pallas as pl
from jax.experimental.pallas import tpu as pltpu

F32 = jnp.float32
BF16 = jnp.bfloat16
MESH = pl.DeviceIdType.MESH

N_META = 16
HEAD_DIM = 64
Q_PER_KV = 8
WINDOW = 128
BLOCK = 128
ATTN_SCALE = HEAD_DIM ** -0.5
ROPE_DIM = HEAD_DIM // 4
ROPE_THETA = 500000.0
NEG_INF = -1e30
POOL_WINDOWS = (2, 4, 8, 16)
LN_EPS = 1e-5
DN_ALPHA = 2.0 ** 0.25
ADAM_LR = 0.001
ADAM_B1 = 0.9
ADAM_B2 = 0.999
ADAM_EPS = 1e-08
ADAM_WD = 0.01
ADAM_STEP = 10

LANES = 128
META_ROW0 = BLOCK - N_META
VMEM_LIMIT = 56 * 1024 * 1024

ANY = pl.BlockSpec(memory_space=pl.ANY)
HBM = pl.BlockSpec(memory_space=pltpu.HBM)
SEM = pl.BlockSpec(memory_space=pltpu.SEMAPHORE)
EFFECT = pltpu.SideEffectType.DATAFLOW_SIDE_EFFECTING


def _params(sem=None, **kw):
    return pltpu.CompilerParams(dimension_semantics=sem, vmem_limit_bytes=VMEM_LIMIT, **kw)


class _Order:
    def __init__(self):
        self.last = None


def _call(order, body, operands, *, name, in_specs, out_specs, out_shape, grid=(), scratch=(), sem=None,
          aliases=None, prefetch=()):
    n_in, npf = len(operands), len(prefetch)
    tok = order.last
    if tok is not None and any(tok is op for op in operands):
        tok = None

    def wrapped(*refs):
        refs = list(refs)
        if tok is not None:
            del refs[npf + n_in]
        body(*refs)

    specs = list(in_specs) + ([ANY] if tok is not None else [])
    ops = list(operands) + ([tok] if tok is not None else [])
    if npf:
        out = pl.pallas_call(
            wrapped, name=name, out_shape=out_shape, compiler_params=_params(sem),
            grid_spec=pltpu.PrefetchScalarGridSpec(num_scalar_prefetch=npf, grid=grid, in_specs=specs,
                                                   out_specs=out_specs, scratch_shapes=list(scratch)),
        )(*prefetch, *ops)
    else:
        out = pl.pallas_call(
            wrapped, name=name, grid=grid, in_specs=specs, out_specs=out_specs, out_shape=out_shape,
            scratch_shapes=list(scratch), input_output_aliases=aliases or {}, compiler_params=_params(sem),
        )(*ops)
    order.last = out[0] if isinstance(out, (list, tuple)) else out
    return out


def _pick(dim, pref, mult=LANES):
    best = None
    t = mult
    while t <= min(dim, pref):
        if dim % t == 0:
            best = t
        t += mult
    return dim if best is None else best


_DIMS = {"nn": (((1,), (0,)), ((), ())), "nt": (((1,), (1,)), ((), ())), "tn": (((0,), (0,)), ((), ()))}


def _mm(order, a, b, *, kind, out_dtype, tm, tn, tk=None, name, a_lead=None):
    a2 = a.shape[-2:]
    halves = a_lead == "halves"
    if halves:
        a2 = (a2[0], 2 * a2[1])
    if kind == "tn":
        K, M = a2
    else:
        M, K = a2
    N = b.shape[0] if kind == "nt" else b.shape[1]
    half_cols = a2[1] // 2
    tm = _pick(half_cols if halves and kind == "tn" else M, tm)
    tn = _pick(N, tn)
    tk = K if tk is None else _pick(half_cols if halves and kind != "tn" else K, tk)
    nm, nn_, nk = M // tm, N // tn, K // tk
    a_bytes = M * K * a.dtype.itemsize
    b_bytes = N * K * b.dtype.itemsize
    i_outer = (a_bytes + nm * b_bytes <= b_bytes + nn_ * a_bytes) if nk == 1 else True

    def ij(g0, g1):
        return (g0, g1) if i_outer else (g1, g0)

    def a_map(g0, g1, k):
        i, _ = ij(g0, g1)
        if halves:
            per = half_cols // (tm if kind == "tn" else tk)
            return (i // per, k, i % per) if kind == "tn" else (k // per, i, k % per)
        idx = (k, i) if kind == "tn" else (i, k)
        return idx if a_lead is None else (a_lead,) + idx

    def b_map(g0, g1, k):
        _, j = ij(g0, g1)
        return (j, k) if kind == "nt" else (k, j)

    def o_map(g0, g1, k):
        return ij(g0, g1)

    a_blk = (tk, tm) if kind == "tn" else (tm, tk)
    if a_lead is not None:
        a_blk = (None,) + a_blk
    b_blk = (tn, tk) if kind == "nt" else (tk, tn)

    def body(a_ref, b_ref, o_ref, *acc):
        p = lax.dot_general(a_ref[...], b_ref[...], _DIMS[kind], preferred_element_type=F32)
        if nk == 1:
            o_ref[...] = p.astype(o_ref.dtype)
        else:
            k = pl.program_id(2)

            @pl.when(k == 0)
            def _():
                acc[0][...] = p

            @pl.when(k > 0)
            def _():
                acc[0][...] += p

            @pl.when(k == nk - 1)
            def _():
                o_ref[...] = acc[0][...].astype(o_ref.dtype)

    grid = (nm, nn_, nk) if i_outer else (nn_, nm, nk)
    return _call(
        order, body, [a, b], name=name, grid=grid,
        in_specs=[pl.BlockSpec(a_blk, a_map), pl.BlockSpec(b_blk, b_map)],
        out_specs=pl.BlockSpec((tm, tn), o_map),
        out_shape=jax.ShapeDtypeStruct((M, N), out_dtype),
        scratch=[] if nk == 1 else [pltpu.VMEM((tm, tn), F32)],
        sem=("parallel", "parallel", "arbitrary"))


def _ln_stats(z):
    mu = jnp.mean(z, axis=-1, keepdims=True)
    zc = z - mu
    var = jnp.mean(zc * zc, axis=-1, keepdims=True)
    rstd = lax.rsqrt(var + LN_EPS)
    return zc * rstd, rstd


def _ln_bwd(dy, xhat, rstd, g):
    dxh = dy * g
    m1 = jnp.mean(dxh, axis=-1, keepdims=True)
    m2 = jnp.mean(dxh * xhat, axis=-1, keepdims=True)
    return rstd * (dxh - m1 - xhat * m2)


def _ln_in_fwd(order, x, meta_pad, g, b):
    S, D = x.shape
    nb = S // BLOCK

    def body(x_ref, mp_ref, g_ref, b_ref, h_ref, hb_ref):
        is_meta = pl.program_id(0) == nb
        xin = jnp.where(is_meta, mp_ref[...], x_ref[...])
        xhat, _ = _ln_stats(xin)
        y = xhat * g_ref[...] + b_ref[...]
        h_ref[...] = y
        hb_ref[...] = y.astype(BF16)

    row = pl.BlockSpec((BLOCK, D), lambda i: (i, 0))
    vec = pl.BlockSpec((1, D), lambda i: (0, 0))
    return _call(
        order, body, [x, meta_pad, g, b], name="ln_in_fwd", grid=(nb + 1,),
        in_specs=[pl.BlockSpec((BLOCK, D), lambda i: (jnp.minimum(i, nb - 1), 0)),
                  pl.BlockSpec((BLOCK, D), lambda i: (0, 0)), vec, vec],
        out_specs=[row, row],
        out_shape=[jax.ShapeDtypeStruct((S + BLOCK, D), F32), jax.ShapeDtypeStruct((S + BLOCK, D), BF16)],
        sem=("parallel",))


def _ln_in_bwd(order, x, meta_pad, g, dh0, dz1):
    S, D = x.shape
    nb = S // BLOCK

    def body(x_ref, mp_ref, g_ref, dh_ref, dz_ref, dx_ref, dg_ref, db_ref):
        i = pl.program_id(0)
        is_meta = i == nb
        xin = jnp.where(is_meta, mp_ref[...], x_ref[...])
        xhat, rstd = _ln_stats(xin)
        dy = dh_ref[...] + jnp.where(is_meta, 0.0, DN_ALPHA) * dz_ref[...]
        dx_ref[...] = _ln_bwd(dy, xhat, rstd, g_ref[...])

        @pl.when(i == 0)
        def _():
            dg_ref[...] = jnp.zeros_like(dg_ref)
            db_ref[...] = jnp.zeros_like(db_ref)

        dg_ref[...] += jnp.sum(dy * xhat, axis=0, keepdims=True)
        db_ref[...] += jnp.sum(dy, axis=0, keepdims=True)

    row = pl.BlockSpec((BLOCK, D), lambda i: (i, 0))
    rowx = pl.BlockSpec((BLOCK, D), lambda i: (jnp.minimum(i, nb - 1), 0))
    vec = pl.BlockSpec((1, D), lambda i: (0, 0))
    return _call(
        order, body, [x, meta_pad, g, dh0, dz1], name="ln_in_bwd", grid=(nb + 1,),
        in_specs=[rowx, pl.BlockSpec((BLOCK, D), lambda i: (0, 0)), vec, row, rowx],
        out_specs=[row, vec, vec],
        out_shape=[jax.ShapeDtypeStruct((S + BLOCK, D), F32), jax.ShapeDtypeStruct((1, D), F32),
                   jax.ShapeDtypeStruct((1, D), F32)],
        sem=("arbitrary",))


def _ln1_fwd(order, h0, y1, g, b):
    S, D = y1.shape
    tm = _pick(S, BLOCK, 8)

    def body(h_ref, y_ref, g_ref, b_ref, o_ref, ob_ref):
        xhat, _ = _ln_stats(DN_ALPHA * h_ref[...] + y_ref[...])
        y = xhat * g_ref[...] + b_ref[...]
        o_ref[...] = y
        ob_ref[...] = y.astype(BF16)

    row = pl.BlockSpec((tm, D), lambda i: (i, 0))
    vec = pl.BlockSpec((1, D), lambda i: (0, 0))
    return _call(
        order, body, [h0, y1, g, b], name="ln1_fwd", grid=(S // tm,), in_specs=[row, row, vec, vec],
        out_specs=[row, row],
        out_shape=[jax.ShapeDtypeStruct((S, D), F32), jax.ShapeDtypeStruct((S, D), BF16)],
        sem=("parallel",))


def _ln1_bwd(order, h0, y1, g, dh1, dz2):
    S, D = y1.shape
    tm = _pick(S, BLOCK, 8)

    def body(h_ref, y_ref, g_ref, dh_ref, dz2_ref, dz_ref, dzb_ref, dg_ref, db_ref):
        i = pl.program_id(0)
        xhat, rstd = _ln_stats(DN_ALPHA * h_ref[...] + y_ref[...])
        dy = dh_ref[...] + DN_ALPHA * dz2_ref[...]
        dz = _ln_bwd(dy, xhat, rstd, g_ref[...])
        dz_ref[...] = dz
        dzb_ref[...] = dz.astype(BF16)

        @pl.when(i == 0)
        def _():
            dg_ref[...] = jnp.zeros_like(dg_ref)
            db_ref[...] = jnp.zeros_like(db_ref)

        dg_ref[...] += jnp.sum(dy * xhat, axis=0, keepdims=True)
        db_ref[...] += jnp.sum(dy, axis=0, keepdims=True)

    row = pl.BlockSpec((tm, D), lambda i: (i, 0))
    vec = pl.BlockSpec((1, D), lambda i: (0, 0))
    return _call(
        order, body, [h0, y1, g, dh1, dz2], name="ln1_bwd", grid=(S // tm,),
        in_specs=[row, row, vec, row, row], out_specs=[row, row, vec, vec],
        out_shape=[jax.ShapeDtypeStruct((S, D), F32), jax.ShapeDtypeStruct((S, D), BF16),
                   jax.ShapeDtypeStruct((1, D), F32), jax.ShapeDtypeStruct((1, D), F32)],
        sem=("arbitrary",))


def _ln2_loss_bwd(order, h1, y2, target, g, b):
    S, D = y2.shape
    tm = _pick(S, BLOCK, 8)

    def body(h_ref, y_ref, t_ref, g_ref, b_ref, dz_ref, dzb_ref, dg_ref, db_ref, loss_ref):
        i = pl.program_id(0)
        xhat, rstd = _ln_stats(DN_ALPHA * h_ref[...] + y_ref[...])
        diff = xhat * g_ref[...] + b_ref[...] - t_ref[...]
        dy = diff / D
        dz = _ln_bwd(dy, xhat, rstd, g_ref[...])
        dz_ref[...] = dz
        dzb_ref[...] = dz.astype(BF16)

        @pl.when(i == 0)
        def _():
            dg_ref[...] = jnp.zeros_like(dg_ref)
            db_ref[...] = jnp.zeros_like(db_ref)
            loss_ref[...] = jnp.zeros_like(loss_ref)

        dg_ref[...] += jnp.sum(dy * xhat, axis=0, keepdims=True)
        db_ref[...] += jnp.sum(dy, axis=0, keepdims=True)
        loss_ref[...] += jnp.sum(jnp.mean(diff * diff, axis=-1, keepdims=True), axis=0, keepdims=True)

    row = pl.BlockSpec((tm, D), lambda i: (i, 0))
    vec = pl.BlockSpec((1, D), lambda i: (0, 0))
    one = pl.BlockSpec((1, 1), lambda i: (0, 0))
    return _call(
        order, body, [h1, y2, target, g, b], name="ln2_loss_bwd", grid=(S // tm,),
        in_specs=[row, row, row, vec, vec], out_specs=[row, row, vec, vec, one],
        out_shape=[jax.ShapeDtypeStruct((S, D), F32), jax.ShapeDtypeStruct((S, D), BF16),
                   jax.ShapeDtypeStruct((1, D), F32), jax.ShapeDtypeStruct((1, D), F32),
                   jax.ShapeDtypeStruct((1, 1), F32)],
        sem=("arbitrary",))


def _rope_table(S):
    r = jnp.arange(S + BLOCK)
    pos = jnp.where(r < S, r + N_META, jnp.maximum(r - (S + META_ROW0), 0))
    half = ROPE_DIM // 2
    lane = jnp.arange(LANES) % HEAD_DIM
    inv_freq = ROPE_THETA ** (-(lane % half).astype(F32) * 2.0 / ROPE_DIM)
    ang = pos.astype(F32)[:, None] * inv_freq[None, :]
    cos, sin = jnp.cos(ang), jnp.sin(ang)
    c = jnp.where(lane < ROPE_DIM, cos, 1.0)
    sa = jnp.where(lane < half, -sin, 0.0)
    sb = jnp.where((lane >= half) & (lane < ROPE_DIM), sin, 0.0)
    return jnp.concatenate([c, sa, sb], axis=1).astype(F32)


def _rope(x, tab):
    h = ROPE_DIM // 2
    return (x * tab[:, :LANES] + pltpu.roll(x, LANES - h, 1) * tab[:, LANES:2 * LANES]
            + pltpu.roll(x, h, 1) * tab[:, 2 * LANES:])


def _rope_t(dy, tab):
    h = ROPE_DIM // 2
    return (dy * tab[:, :LANES] + pltpu.roll(dy * tab[:, LANES:2 * LANES], h, 1)
            + pltpu.roll(dy * tab[:, 2 * LANES:], LANES - h, 1))


def _attn_tiles(g, n, S, sink_ref, q_ref, k_ref, v_ref, tab_ref):
    NQG = Q_PER_KV // 2
    R = NQG * BLOCK
    halfsel = (g % 2).astype(F32)
    prev = jnp.maximum(n - 1, 0)
    qrow = pl.ds(pl.multiple_of(n * BLOCK, BLOCK), BLOCK)
    prow = pl.ds(pl.multiple_of(prev * BLOCK, BLOCK), BLOCK)
    mrow = pl.ds(S, BLOCK)

    tq = tab_ref[qrow, :]
    qf = q_ref[...]
    q4 = jnp.concatenate([_rope(qf[:, LANES * p:LANES * (p + 1)], tq) for p in range(NQG)], axis=0).astype(BF16)

    tk = jnp.concatenate([tab_ref[mrow, :], tab_ref[prow, :], tq], axis=0)
    kr = _rope(jnp.concatenate([k_ref[mrow, :], k_ref[prow, :], k_ref[qrow, :]], axis=0), tk)
    vr = jnp.concatenate([v_ref[mrow, :], v_ref[prow, :], v_ref[qrow, :]], axis=0)

    lane = lax.broadcasted_iota(jnp.int32, kr.shape, 1)
    own = jnp.where(lane < HEAD_DIM, 1.0 - halfsel, halfsel)

    def lo_hi(t):
        mine = t * own
        other = pltpu.roll(mine, HEAD_DIM, 1)
        lo = mine * (1.0 - halfsel) + other * halfsel
        hi = other * (1.0 - halfsel) + mine * halfsel
        return lo.astype(BF16), hi.astype(BF16)

    klo, khi = lo_hi(kr)
    vlo, vhi = lo_hi(vr)

    row = lax.broadcasted_iota(jnp.int32, (R, 3 * BLOCK), 0) & (BLOCK - 1)
    col = lax.broadcasted_iota(jnp.int32, (R, 3 * BLOCK), 1)
    jj = col & (BLOCK - 1)
    no_prev = jnp.where(n >= 1, 0, 2 * BLOCK)
    mask = (((col < BLOCK) & (col >= META_ROW0))
            | ((col >= BLOCK) & (col < 2 * BLOCK) & (jj > row + no_prev))
            | ((col >= 2 * BLOCK) & (jj <= row)))

    def soft(kk, parity):
        sk = jnp.concatenate(
            [jnp.full((BLOCK, 1), sink_ref[0, Q_PER_KV * g + 2 * p + parity], F32) for p in range(NQG)], axis=0)
        s = lax.dot_general(q4, kk, _DIMS["nt"], preferred_element_type=F32) * ATTN_SCALE
        s = jnp.where(mask, s, NEG_INF)
        m = jnp.maximum(jnp.max(s, axis=1, keepdims=True), sk)
        p = jnp.exp(s - m)
        es = jnp.exp(sk - m)
        inv = 1.0 / (jnp.sum(p, axis=1, keepdims=True) + es)
        return p * inv, es * inv

    pe, sink_e = soft(klo, 0)
    po, sink_o = soft(khi, 1)
    return q4, tk, (klo, khi), (vlo, vhi), (pe, po), (sink_e, sink_o), own


def _attn_specs(S, ATTN, KVW):
    Tp = S + BLOCK
    koff, voff = ATTN // LANES, (ATTN + KVW) // LANES
    gw = Q_PER_KV * HEAD_DIM
    return [pl.BlockSpec(memory_space=pltpu.SMEM),
            pl.BlockSpec((BLOCK, gw), lambda g, n: (n, g)),
            pl.BlockSpec((Tp, LANES), lambda g, n: (0, koff + g // 2)),
            pl.BlockSpec((Tp, LANES), lambda g, n: (0, voff + g // 2)),
            pl.BlockSpec((Tp, 3 * LANES), lambda g, n: (0, 0))]


def _attn_fwd(order, proj, tab, sinks, S, ATTN, KVW):
    G = KVW // HEAD_DIM
    nb = S // BLOCK
    gw = Q_PER_KV * HEAD_DIM

    def body(sink_ref, q_ref, k_ref, v_ref, tab_ref, o_ref):
        g, n = pl.program_id(0), pl.program_id(1)
        _, _, _, (vlo, vhi), (pe, po), _, _ = _attn_tiles(g, n, S, sink_ref, q_ref, k_ref, v_ref, tab_ref)
        o4 = (jnp.dot(pe.astype(BF16), vlo, preferred_element_type=F32)
              + jnp.dot(po.astype(BF16), vhi, preferred_element_type=F32))
        o_ref[...] = jnp.concatenate(
            [o4[BLOCK * p:BLOCK * (p + 1)] for p in range(Q_PER_KV // 2)], axis=1).astype(BF16)

    return _call(
        order, body, [sinks, proj, proj, proj, tab], name="attn_fwd", grid=(G, nb),
        in_specs=_attn_specs(S, ATTN, KVW),
        out_specs=pl.BlockSpec((BLOCK, gw), lambda g, n: (n, g)),
        out_shape=jax.ShapeDtypeStruct((S, ATTN), BF16),
        sem=("parallel", "arbitrary"))


def _attn_bwd(order, proj, tab, sinks, da, dproj, S, ATTN, KVW):
    G = KVW // HEAD_DIM
    nb = S // BLOCK
    Tp = S + BLOCK
    NQG = Q_PER_KV // 2
    gw = Q_PER_KV * HEAD_DIM

    def body(sink_ref, q_ref, k_ref, v_ref, tab_ref, da_ref, dproj_in, dq_ref, dk_ref, dv_ref, ds_ref):
        del dproj_in
        g, n = pl.program_id(0), pl.program_id(1)
        q4, tk, (klo, khi), (vlo, vhi), (pe, po), (sink_e, sink_o), own = _attn_tiles(
            g, n, S, sink_ref, q_ref, k_ref, v_ref, tab_ref)
        dof = da_ref[...]
        do4 = jnp.concatenate([dof[:, LANES * p:LANES * (p + 1)] for p in range(NQG)], axis=0)

        def grads(p, vv):
            dp = lax.dot_general(do4, vv, _DIMS["nt"], preferred_element_type=F32)
            delta = jnp.sum(p * dp, axis=1, keepdims=True)
            return (p * (dp - delta) * ATTN_SCALE).astype(BF16), delta

        dse, delta_e = grads(pe, vlo)
        dso, delta_o = grads(po, vhi)

        dq4 = (jnp.dot(dse, klo, preferred_element_type=F32) + jnp.dot(dso, khi, preferred_element_type=F32))
        tq = tk[2 * BLOCK:]
        dq_ref[...] = jnp.concatenate(
            [_rope_t(dq4[BLOCK * p:BLOCK * (p + 1)], tq) for p in range(NQG)], axis=1).astype(BF16)

        lane = lax.broadcasted_iota(jnp.int32, (3 * BLOCK, LANES), 1)

        def fold(lo_part, hi_part):
            t = jnp.where(lane < HEAD_DIM, lo_part, hi_part)
            return t + pltpu.roll(t, HEAD_DIM, 1)

        dk = _rope_t(fold(lax.dot_general(dse, q4, _DIMS["tn"], preferred_element_type=F32),
                          lax.dot_general(dso, q4, _DIMS["tn"], preferred_element_type=F32)), tk) * own
        dv = fold(lax.dot_general(pe.astype(BF16), do4, _DIMS["tn"], preferred_element_type=F32),
                  lax.dot_general(po.astype(BF16), do4, _DIMS["tn"], preferred_element_type=F32)) * own

        @pl.when((n == 0) & (g % 2 == 0))
        def _():
            dk_ref[...] = jnp.zeros_like(dk_ref)
            dv_ref[...] = jnp.zeros_like(dv_ref)

        @pl.when(n == 0)
        def _():
            ds_ref[...] = jnp.zeros_like(ds_ref)

        prev = jnp.maximum(n - 1, 0)
        qrow = pl.ds(pl.multiple_of(n * BLOCK, BLOCK), BLOCK)
        prow = pl.ds(pl.multiple_of(prev * BLOCK, BLOCK), BLOCK)
        mrow = pl.ds(S, BLOCK)
        for ref, val in ((dk_ref, dk), (dv_ref, dv)):
            ref[mrow, :] += val[:BLOCK]
            ref[prow, :] += val[BLOCK:2 * BLOCK]
            ref[qrow, :] += val[2 * BLOCK:]

        srow = lax.broadcasted_iota(jnp.int32, (Q_PER_KV, LANES), 0)
        acc = jnp.zeros((Q_PER_KV, LANES), F32)
        for p in range(NQG):
            for parity, (sk, dl) in enumerate(((sink_e, delta_e), (sink_o, delta_o))):
                val = -jnp.sum(sk[BLOCK * p:BLOCK * (p + 1)] * dl[BLOCK * p:BLOCK * (p + 1)])
                acc = jnp.where(srow == 2 * p + parity, val, acc)
        ds_ref[0] += acc

    in_specs = _attn_specs(S, ATTN, KVW) + [pl.BlockSpec((BLOCK, gw), lambda g, n: (n, g)), ANY]
    slab = pl.BlockSpec((Tp, LANES), lambda g, n: (0, g // 2))
    return _call(
        order, body, [sinks, proj, proj, proj, tab, da, dproj], name="attn_bwd", grid=(G, nb), in_specs=in_specs,
        out_specs=[pl.BlockSpec((BLOCK, gw), lambda g, n: (n, g)), slab, slab,
                   pl.BlockSpec((1, Q_PER_KV, LANES), lambda g, n: (g, 0, 0))],
        out_shape=[jax.ShapeDtypeStruct(dproj.shape, BF16), jax.ShapeDtypeStruct((Tp, KVW), F32),
                   jax.ShapeDtypeStruct((Tp, KVW), F32), jax.ShapeDtypeStruct((G, Q_PER_KV, LANES), F32)],
        aliases={6: 0}, sem=("arbitrary", "arbitrary"))


def _zero_meta_block(order, Tp, IN):
    tc = _pick(IN, 4096)

    def body(o_ref):
        o_ref[...] = jnp.zeros_like(o_ref)

    return _call(
        order, body, [], name="dproj_zero_meta", grid=(IN // tc,), in_specs=[],
        out_specs=pl.BlockSpec((BLOCK, tc), lambda j: (Tp // BLOCK - 1, j)),
        out_shape=jax.ShapeDtypeStruct((Tp, IN), BF16), sem=("parallel",))


def _put_dkv(order, dk, dv, dproj, ATTN):
    Tp, KVW = dk.shape
    nkb = KVW // LANES
    koff = ATTN // LANES

    def body(dk_ref, dv_ref, dproj_in, o_ref):
        del dproj_in
        t = pl.program_id(0)
        o_ref[...] = jnp.where(t < nkb, dk_ref[...], dv_ref[...]).astype(BF16)

    src = pl.BlockSpec((Tp, LANES), lambda t: (0, t % nkb))
    return _call(
        order, body, [dk, dv, dproj], name="dproj_put_dkv", grid=(2 * nkb,), in_specs=[src, src, ANY],
        out_specs=pl.BlockSpec((Tp, LANES), lambda t: (0, koff + t)),
        out_shape=jax.ShapeDtypeStruct(dproj.shape, BF16), aliases={2: 0}, sem=("parallel",))


HALO = 16


def _window_sums(x, up):
    n = x.shape[0]
    out = []
    s = x
    for k in (1, 2, 4, 8):
        s = s + pltpu.roll(s, (n - k) if up else k, 0)
        out.append(s)
    return out


def _pool_specs(S, ub, gw, tm):
    meta_halo = (S + BLOCK - HALO) // HALO

    def main(g):
        return pl.BlockSpec((tm, gw), lambda i: (i, ub + g))

    def halo(g):
        return pl.BlockSpec((HALO, gw), lambda i: (jnp.where(i == 0, meta_halo, i * (tm // HALO) - 1), ub + g))

    return [main(g) for g in range(4)] + [halo(g) for g in range(4)]


def _pooled(main_refs, halo_refs, g):
    x = jnp.concatenate([halo_refs[g][...], main_refs[g][...]], axis=0)
    s = _window_sums(x, up=False)[g]
    return (s[HALO:] * (1.0 / POOL_WINDOWS[g]) - x[HALO:]).astype(BF16)


def _pool_fwd(order, proj, wgrp, scale, S, uoff, POOL):
    gw = POOL // 4
    tm = BLOCK

    def body(*refs):
        main, halo = refs[:4], refs[4:8]
        w_ref, sc_ref, o_ref = refs[8:]
        for g in range(4):
            mixed = jnp.dot(_pooled(main, halo, g), w_ref[g], preferred_element_type=F32)
            o_ref[:, gw * g:gw * (g + 1)] = (mixed * sc_ref[:, gw * g:gw * (g + 1)]).astype(BF16)

    return _call(
        order, body, [proj] * 8 + [wgrp, scale], name="pool_fwd", grid=(S // tm,),
        in_specs=_pool_specs(S, uoff // gw, gw, tm) + [
            pl.BlockSpec((4, gw, gw), lambda i: (0, 0, 0)), pl.BlockSpec((1, POOL), lambda i: (0, 0))],
        out_specs=pl.BlockSpec((tm, POOL), lambda i: (i, 0)),
        out_shape=jax.ShapeDtypeStruct((S, POOL), BF16), sem=("parallel",))


def _pool_bwd_mix(order, proj, wgrp, scale, dps, S, uoff, POOL):
    gw = POOL // 4
    tm = BLOCK

    def body(*refs):
        main, halo = refs[:4], refs[4:8]
        w_ref, sc_ref, dps_ref, dpl_ref, dw_ref, dsc_ref = refs[8:]
        i = pl.program_id(0)

        @pl.when(i == 0)
        def _():
            dw_ref[...] = jnp.zeros_like(dw_ref)
            dsc_ref[...] = jnp.zeros_like(dsc_ref)

        for g in range(4):
            cols = slice(gw * g, gw * (g + 1))
            pooled = _pooled(main, halo, g)
            mixed = jnp.dot(pooled, w_ref[g], preferred_element_type=F32)
            dps_g = dps_ref[:, cols]
            dsc_ref[:, cols] += jnp.sum(dps_g * mixed, axis=0, keepdims=True)
            dms = (dps_g * sc_ref[:, cols]).astype(BF16)
            dw_ref[g] += lax.dot_general(pooled, dms, _DIMS["tn"], preferred_element_type=F32)
            dpl_ref[:, cols] = lax.dot_general(dms, w_ref[g], _DIMS["nt"], preferred_element_type=F32)

    row = pl.BlockSpec((tm, POOL), lambda i: (i, 0))
    return _call(
        order, body, [proj] * 8 + [wgrp, scale, dps], name="pool_bwd_mix", grid=(S // tm,),
        in_specs=_pool_specs(S, uoff // gw, gw, tm) + [
            pl.BlockSpec((4, gw, gw), lambda i: (0, 0, 0)), pl.BlockSpec((1, POOL), lambda i: (0, 0)), row],
        out_specs=[row, pl.BlockSpec((4, gw, gw), lambda i: (0, 0, 0)), pl.BlockSpec((1, POOL), lambda i: (0, 0))],
        out_shape=[jax.ShapeDtypeStruct((S, POOL), F32), jax.ShapeDtypeStruct((4, gw, gw), F32),
                   jax.ShapeDtypeStruct((1, POOL), F32)],
        sem=("arbitrary",))


def _pool_bwd_window(order, dpl, dproj, S, uoff, POOL):
    gw = POOL // 4
    nb = S // BLOCK
    ub = uoff // gw

    def body(main_ref, halo_ref, dproj_in, o_ref):
        del dproj_in
        b, g = pl.program_id(0), pl.program_id(1)
        main = jnp.where(b < nb, main_ref[...], 0.0)
        halo = jnp.where(b == nb - 1, 0.0, halo_ref[...])
        sums = _window_sums(jnp.concatenate([main, halo], axis=0), up=True)
        du = jnp.zeros((BLOCK, gw), F32)
        for k, w in enumerate(POOL_WINDOWS):
            du = jnp.where(g == k, sums[k][:BLOCK] * (1.0 / w), du)
        du = du - main
        row = lax.broadcasted_iota(jnp.int32, du.shape, 0)
        first_valid = jnp.where(b == nb, META_ROW0, 0)
        o_ref[...] = jnp.where(row >= first_valid, du, 0.0).astype(BF16)

    return _call(
        order, body, [dpl, dpl, dproj], name="pool_bwd_window", grid=(nb + 1, 4),
        in_specs=[pl.BlockSpec((BLOCK, gw), lambda b, g: (jnp.minimum(b, nb - 1), g)),
                  pl.BlockSpec((HALO, gw), lambda b, g: (
                      jnp.where(b == nb, 0, jnp.minimum((b + 1) * (BLOCK // HALO), S // HALO - 1)), g)),
                  ANY],
        out_specs=pl.BlockSpec((BLOCK, gw), lambda b, g: (b, ub + g)),
        out_shape=jax.ShapeDtypeStruct(dproj.shape, BF16), aliases={2: 0}, sem=("parallel", "parallel"))


def _sigmoid(x):
    return 1.0 / (1.0 + jnp.exp(-x))


def _gate_tiles(S, D, goff):
    tc = 512
    while goff % tc or D % tc:
        tc //= 2
    return _pick(S, 512, 8), tc


def _gate_mix(order, proj, bgate, a_out, p_out, S, D, goff):
    tm, tc = _gate_tiles(S, D, goff)
    g0b, nd = goff // tc, D // tc

    def body(l0_ref, l1_ref, b_ref, a_ref, p_ref, o_ref):
        g0 = _sigmoid(l0_ref[...] + b_ref[0:1, :])
        g1 = _sigmoid(l1_ref[...] + b_ref[1:2, :])
        o_ref[...] = (g0 * a_ref[...] + g1 * p_ref[...]).astype(BF16)

    tile = pl.BlockSpec((tm, tc), lambda i, j: (i, j))
    return _call(
        order, body, [proj, proj, bgate, a_out, p_out], name="gate_mix", grid=(S // tm, nd),
        in_specs=[pl.BlockSpec((tm, tc), lambda i, j: (i, g0b + j)),
                  pl.BlockSpec((tm, tc), lambda i, j: (i, g0b + nd + j)),
                  pl.BlockSpec((2, tc), lambda i, j: (0, j)), tile, tile],
        out_specs=tile, out_shape=jax.ShapeDtypeStruct((S, D), BF16), sem=("parallel", "parallel"))


def _gate_bwd(order, proj, bgate, a_out, p_out, dmixed, dproj, S, D, goff):
    tm, tc = _gate_tiles(S, D, goff)
    g0b, nd = goff // tc, D // tc

    def body(l0_ref, l1_ref, b_ref, a_ref, p_ref, dm_ref, dproj_in, dap_ref, dl_ref, db_ref):
        del dproj_in
        i, br = pl.program_id(1), pl.program_id(2)
        first = br == 0
        logit = jnp.where(first, l0_ref[...], l1_ref[...]) + jnp.where(first, b_ref[0:1, :], b_ref[1:2, :])
        val = jnp.where(first, a_ref[...], p_ref[...])
        gate = _sigmoid(logit)
        dm = dm_ref[...]
        dap_ref[...] = (dm * gate).astype(BF16)
        dl = dm * val * gate * (1.0 - gate)
        dl_ref[...] = dl.astype(BF16)

        @pl.when((i == 0) & first)
        def _():
            db_ref[...] = jnp.zeros_like(db_ref)

        db_ref[br] += jnp.sum(dl, axis=0, keepdims=True)

    tile = pl.BlockSpec((tm, tc), lambda j, i, br: (i, j))
    return _call(
        order, body, [proj, proj, bgate, a_out, p_out, dmixed, dproj], name="gate_bwd", grid=(nd, S // tm, 2),
        in_specs=[pl.BlockSpec((tm, tc), lambda j, i, br: (i, g0b + j)),
                  pl.BlockSpec((tm, tc), lambda j, i, br: (i, g0b + nd + j)),
                  pl.BlockSpec((2, tc), lambda j, i, br: (0, j)), tile, tile, tile, ANY],
        out_specs=[pl.BlockSpec((None, tm, tc), lambda j, i, br: (br, i, j)),
                   pl.BlockSpec((tm, tc), lambda j, i, br: (i, g0b + br * nd + j)),
                   pl.BlockSpec((2, 1, tc), lambda j, i, br: (0, 0, j))],
        out_shape=[jax.ShapeDtypeStruct((2, S, D), BF16), jax.ShapeDtypeStruct(dproj.shape, BF16),
                   jax.ShapeDtypeStruct((2, 1, D), F32)],
        aliases={6: 1}, sem=("parallel", "arbitrary", "arbitrary"))


def _swiglu_fwd(order, f, S, FF):
    tc = _pick(FF, 5504)
    nj = FF // tc

    def body(g_ref, u_ref, o_ref):
        gt = g_ref[...]
        o_ref[...] = (gt * _sigmoid(gt) * u_ref[...]).astype(BF16)

    return _call(
        order, body, [f, f], name="swiglu_fwd", grid=(S // BLOCK, nj),
        in_specs=[pl.BlockSpec((BLOCK, tc), lambda i, j: (i, j)), pl.BlockSpec((BLOCK, tc), lambda i, j: (i, nj + j))],
        out_specs=pl.BlockSpec((BLOCK, tc), lambda i, j: (i, j)),
        out_shape=jax.ShapeDtypeStruct((S, FF), BF16), sem=("parallel", "parallel"))


def _swiglu_bwd(order, f, dact, S, FF):
    tc = _pick(FF, 5504)
    nj = FF // tc

    def body(g_ref, u_ref, d_ref, o_ref):
        gt, d = g_ref[...], d_ref[...]
        s = _sigmoid(gt)
        o_ref[0] = (d * u_ref[...] * s * (1.0 + gt * (1.0 - s))).astype(BF16)
        o_ref[1] = (d * gt * s).astype(BF16)

    return _call(
        order, body, [f, f, dact], name="swiglu_bwd", grid=(S // BLOCK, nj),
        in_specs=[pl.BlockSpec((BLOCK, tc), lambda i, j: (i, j)),
                  pl.BlockSpec((BLOCK, tc), lambda i, j: (i, nj + j)),
                  pl.BlockSpec((BLOCK, tc), lambda i, j: (i, j))],
        out_specs=pl.BlockSpec((2, BLOCK, tc), lambda i, j: (0, i, j)),
        out_shape=jax.ShapeDtypeStruct((2, S, FF), BF16), sem=("parallel", "parallel"))


def _place():
    return lax.axis_index("x"), lax.axis_index("y"), lax.axis_index("c")


def _xfer_start(order, name, bufs, copies):
    nb = len(bufs)
    n = len(copies([None] * nb, None))
    is_new = [isinstance(b, jax.ShapeDtypeStruct) for b in bufs]
    old = [b for b, fresh in zip(bufs, is_new) if not fresh]
    no = len(old)
    tok = [] if any(order.last is b for b in old) else [order.last]
    first_out = no + len(tok)

    def body(*refs):
        send, recv = refs[first_out:first_out + n], refs[first_out + n:first_out + 2 * n]
        token = refs[-1]
        given, made = iter(refs[:no]), iter(refs[first_out + 2 * n + no:-1])
        logical = [next(made) if fresh else next(given) for fresh in is_new]
        for i, (src, dst, dev) in enumerate(copies(logical, _place())):
            pltpu.make_async_remote_copy(src_ref=src, dst_ref=dst, send_sem=send[i], recv_sem=recv[i],
                                         device_id=dev, device_id_type=MESH).start()
        token[...] = jnp.zeros_like(token)

    fresh_shapes = [b for b, fresh in zip(bufs, is_new) if fresh]
    out = pl.pallas_call(
        body, name=name,
        out_shape=tuple([pltpu.SemaphoreType.DMA(())] * (2 * n)
                        + [pltpu.HBM(b.shape, b.dtype) for b in old + fresh_shapes]
                        + [jax.ShapeDtypeStruct((8, LANES), F32)]),
        in_specs=[HBM] * no + [ANY] * len(tok),
        out_specs=tuple([SEM] * (2 * n) + [HBM] * nb + [pl.BlockSpec(memory_space=pltpu.VMEM)]),
        input_output_aliases={i: 2 * n + i for i in range(no)},
        compiler_params=pltpu.CompilerParams(has_side_effects=EFFECT),
    )(*[pltpu.with_memory_space_constraint(b, pltpu.HBM) for b in old], *tok)
    order.last = out[-1]
    thru, made = iter(out[2 * n:2 * n + no]), iter(out[2 * n + no:2 * n + nb])
    return list(out[:2 * n]), [next(made) if fresh else next(thru) for fresh in is_new]


def _xfer_wait(order, name, sems, bufs, copies):
    nb = len(bufs)
    n = len(sems) // 2
    tok = order.last

    def body(*refs):
        send, recv = refs[nb:nb + n], refs[nb + n:nb + 2 * n]
        token = refs[-1]
        for i, (src, dst, dev) in enumerate(copies(refs[:nb], _place())):
            cp = pltpu.make_async_remote_copy(src_ref=src, dst_ref=dst, send_sem=send[i], recv_sem=recv[i],
                                              device_id=dev, device_id_type=MESH)
            cp.wait_send()
            cp.wait_recv()
        token[...] = jnp.zeros_like(token)

    out = pl.pallas_call(
        body, name=name,
        out_shape=tuple([pltpu.HBM(b.shape, b.dtype) for b in bufs] + [jax.ShapeDtypeStruct((8, LANES), F32)]),
        in_specs=[HBM] * nb + [SEM] * (2 * n) + [ANY],
        out_specs=tuple([HBM] * nb + [pl.BlockSpec(memory_space=pltpu.VMEM)]),
        input_output_aliases={i: i for i in range(nb)},
        compiler_params=pltpu.CompilerParams(has_side_effects=EFFECT),
    )(*bufs, *sems, tok)
    order.last = out[-1]
    return list(out[:nb])


class _Xfer:
    def __init__(self, name, bufs, copies):
        self.name, self.bufs, self.copies = name, list(bufs), copies
        self.sems = None

    def start(self, order):
        self.sems, self.bufs = _xfer_start(order, self.name + "_start", self.bufs, self.copies)

    def wait(self, order):
        self.bufs = _xfer_wait(order, self.name + "_wait", self.sems, self.bufs, self.copies)
        return self.bufs


def _block_rows(ref, r, d):
    return ref.at[pl.ds(d * r, r)]


def _gather_send(fulls):
    def copies(refs, place):
        out = []
        for w, full in enumerate(fulls):
            r = full.shape[0] // 8
            if place is None:
                out += [None] * 4
                continue
            x, y, c = place
            mine = _block_rows(refs[w], r, 4 * x + 2 * y + c)
            out.append((mine, mine, (x, y, 1 - c)))
            for px, py in ((1 - x, y), (x, 1 - y), (1 - x, 1 - y)):
                out.append((mine, mine, (px, py, c)))
        return out
    return copies


def _gather_forward(fulls):
    def copies(refs, place):
        out = []
        for w, full in enumerate(fulls):
            r = full.shape[0] // 8
            if place is None:
                out += [None] * 3
                continue
            x, y, c = place
            for px, py in ((1 - x, y), (x, 1 - y), (1 - x, 1 - y)):
                blk = _block_rows(refs[w], r, 4 * px + 2 * py + c)
                out.append((blk, blk, (x, y, 1 - c)))
        return out
    return copies


def _pair_send(nw):
    def copies(refs, place):
        out = []
        for w in range(nw):
            if place is None:
                out += [None] * 4
                continue
            x, y, c = place
            grad, other = refs[2 * w], refs[2 * w + 1]
            r = other.shape[1]
            for k in range(4):
                out.append((_block_rows(grad, r, 2 * k + 1 - c), other.at[k], (x, y, 1 - c)))
        return out
    return copies


def _chip_send(nw):
    def copies(refs, place):
        out = []
        for w in range(nw):
            if place is None:
                out += [None] * 3
                continue
            x, y, c = place
            psum, parts = refs[2 * w], refs[2 * w + 1]
            for px, py in ((1 - x, y), (x, 1 - y), (1 - x, 1 - y)):
                out.append((psum.at[2 * px + py], parts.at[2 * x + y], (px, py, c)))
        return out
    return copies


def _dev_index():
    x, y, c = _place()
    return 4 * x + 2 * y + c


def _place_own(order, shard, name):
    r, cols = shard.shape
    tr = _pick(r, max(16, (2 << 20) // (4 * cols)), 16)
    nr = r // tr

    def body(s_ref, o_ref):
        o_ref[...] = s_ref[...].astype(BF16)

    return _call(
        order, body, [shard], name=name, grid=(nr,),
        in_specs=[pl.BlockSpec((tr, cols), lambda i: (i, 0))],
        out_specs=pl.BlockSpec((tr, cols), lambda i: (_dev_index() * nr + i, 0)),
        out_shape=jax.ShapeDtypeStruct((8 * r, cols), BF16), sem=("parallel",))


def _pair_sum(order, grad, other, name):
    r, cols = other.shape[1:]
    tr = _pick(r, max(16, (4 << 20) // (2 * cols)), 16)
    nr = r // tr

    def body(g_ref, a_ref, o_ref):
        o_ref[...] = (g_ref[...].astype(F32) + a_ref[...].astype(F32)).astype(BF16)

    blk = pl.BlockSpec((None, tr, cols), lambda k, i: (k, i, 0))
    return _call(
        order, body, [grad, other], name=name, grid=(4, nr),
        in_specs=[pl.BlockSpec((tr, cols), lambda k, i: ((2 * k + lax.axis_index("c")) * nr + i, 0)), blk],
        out_specs=blk, out_shape=jax.ShapeDtypeStruct(other.shape, BF16), sem=("parallel", "parallel"))


def _chip_sum(order, psum, parts, name):
    _, r, cols = parts.shape
    tr = _pick(r, max(16, (1 << 20) // (2 * cols)), 16)

    def my_chip():
        return 2 * lax.axis_index("x") + lax.axis_index("y")

    def body(own_ref, p0, p1, p2, p3, o_ref):
        own = own_ref[...].astype(F32)
        acc = None
        for k, p in enumerate((p0, p1, p2, p3)):
            term = jnp.where(my_chip() == k, own, p[...].astype(F32))
            acc = term if acc is None else acc + term
        o_ref[...] = acc

    def slot(k):
        return pl.BlockSpec((None, tr, cols), lambda i: (jnp.where(my_chip() == k, (k + 1) % 4, k), i, 0))

    return _call(
        order, body, [psum, parts, parts, parts, parts], name=name, grid=(r // tr,),
        in_specs=[pl.BlockSpec((None, tr, cols), lambda i: (my_chip(), i, 0))] + [slot(k) for k in range(4)],
        out_specs=pl.BlockSpec((tr, cols), lambda i: (i, 0)),
        out_shape=jax.ShapeDtypeStruct((r, cols), F32), sem=("parallel",))


def _all_reduce_small(order, pack, name):
    R = pack.shape[0]

    def body(p_ref, o_ref, buf, send_sems, recv_sems):
        x, y, c = _place()
        me = 4 * x + 2 * y + c
        buf[me] = p_ref[...]
        copies = []
        for k in range(1, 8):
            px = 1 - x if k & 4 else x
            py = 1 - y if k & 2 else y
            pc = 1 - c if k & 1 else c
            cp = pltpu.make_async_remote_copy(
                src_ref=p_ref, dst_ref=buf.at[me], send_sem=send_sems.at[k - 1], recv_sem=recv_sems.at[k - 1],
                device_id=(px, py, pc), device_id_type=MESH)
            cp.start()
            copies.append(cp)
        for cp in copies:
            cp.wait_recv()
        acc = buf[0]
        for d in range(1, 8):
            acc = acc + buf[d]
        o_ref[...] = acc
        for cp in copies:
            cp.wait_send()

    vm = pl.BlockSpec(memory_space=pltpu.VMEM)
    return _call(
        order, body, [pack], name=name, in_specs=[vm], out_specs=vm,
        out_shape=jax.ShapeDtypeStruct((R, LANES), F32),
        scratch=[pltpu.VMEM((8, R, LANES), F32), pltpu.SemaphoreType.DMA((7,)), pltpu.SemaphoreType.DMA((7,))])


def _pack(parts):
    flat = []
    for p in parts:
        v = p.reshape(-1).astype(F32)
        flat.append(jnp.pad(v, (0, (-v.shape[0]) % LANES)))
    v = jnp.concatenate(flat)
    v = jnp.pad(v, (0, (-v.shape[0]) % (8 * LANES)))
    return v.reshape(-1, LANES)


def _unpack(pack, shapes):
    v = pack.reshape(-1)
    out, off = [], 0
    for s in shapes:
        n = 1
        for d in s:
            n *= d
        out.append(v[off:off + n].reshape(s))
        off += n + (-n) % LANES
    return out


def _adamw(order, w, g, m, v, name):
    shape = w.shape
    cols = shape[-1]
    w2, g2, m2, v2 = (t.reshape(-1, cols) for t in (w, g, m, v))
    R = w2.shape[0]
    tr = _pick(R, max(8, (1 << 20) // (4 * cols)), 8)

    def body(w_ref, g_ref, m_ref, v_ref, d_ref, mo_ref, vo_ref):
        d_ref[...], mo_ref[...], vo_ref[...] = _adam_math(w_ref[...], g_ref[...], m_ref[...], v_ref[...])

    blk = pl.BlockSpec((tr, cols), lambda i: (i, 0))
    outs = _call(
        order, body, [w2, g2, m2, v2], name=name, grid=(R // tr,), in_specs=[blk] * 4, out_specs=[blk] * 3,
        out_shape=[jax.ShapeDtypeStruct((R, cols), F32)] * 3, sem=("parallel",))
    return tuple(o.reshape(shape) for o in outs)


def _adam_math(w, g, m, v):
    mn = ADAM_B1 * m + (1.0 - ADAM_B1) * g
    vn = ADAM_B2 * v + (1.0 - ADAM_B2) * (g * g)
    m_hat = mn / (1.0 - ADAM_B1 ** ADAM_STEP)
    v_hat = vn / (1.0 - ADAM_B2 ** ADAM_STEP)
    return -ADAM_LR * (m_hat / (jnp.sqrt(v_hat) + ADAM_EPS) + ADAM_WD * w), mn, vn


def _chip_sum_adamw(order, w, psum, parts, m, v, name):
    _, r, cols = parts.shape
    tr = _pick(r, max(16, (6 << 20) // (38 * cols)), 16)

    def my_chip():
        return 2 * lax.axis_index("x") + lax.axis_index("y")

    def body(w_ref, own_ref, p0, p1, p2, p3, m_ref, v_ref, g_ref, d_ref, mo_ref, vo_ref):
        own = own_ref[...].astype(F32)
        g = None
        for k, p in enumerate((p0, p1, p2, p3)):
            term = jnp.where(my_chip() == k, own, p[...].astype(F32))
            g = term if g is None else g + term
        g_ref[...] = g
        d_ref[...], mo_ref[...], vo_ref[...] = _adam_math(w_ref[...], g, m_ref[...], v_ref[...])

    def slot(k):
        return pl.BlockSpec((None, tr, cols), lambda i: (jnp.where(my_chip() == k, (k + 1) % 4, k), i, 0))

    blk = pl.BlockSpec((tr, cols), lambda i: (i, 0))
    return _call(
        order, body, [w, psum, parts, parts, parts, parts, m, v], name=name, grid=(r // tr,),
        in_specs=[blk, pl.BlockSpec((None, tr, cols), lambda i: (my_chip(), i, 0))]
        + [slot(k) for k in range(4)] + [blk, blk],
        out_specs=[blk] * 4, out_shape=[jax.ShapeDtypeStruct((r, cols), F32)] * 4, sem=("parallel",))


class _GradReduce:
    def __init__(self, tag, grads, names):
        self.tag, self.grads, self.names = tag, list(grads), names
        self.pair = self.chip = self.psums = None

    def pair_start(self, order):
        bufs = []
        for g in self.grads:
            bufs += [g, jax.ShapeDtypeStruct((4, g.shape[0] // 8, g.shape[1]), g.dtype)]
        self.pair = _Xfer("pair_" + self.tag, bufs, _pair_send(len(self.grads)))
        self.pair.start(order)

    def pair_sum_chip_start(self, order):
        bufs = self.pair.wait(order)
        self.psums = [_pair_sum(order, bufs[2 * w], bufs[2 * w + 1], "pair_sum_" + nm)
                      for w, nm in enumerate(self.names)]
        cbufs = []
        for p in self.psums:
            cbufs += [p, jax.ShapeDtypeStruct(p.shape, p.dtype)]
        self.chip = _Xfer("chip_" + self.tag, cbufs, _chip_send(len(self.psums)))
        self.chip.start(order)

    def finish(self, order):
        bufs = self.chip.wait(order)
        return [(bufs[2 * w], bufs[2 * w + 1]) for w in range(len(self.names))]


def kernel(x, meta_tokens, ln_in_g, ln_in_b, w_in, b_gate, attn_sinks, w_attn_up, w_pool_grp, pool_scale, w_pool_up, w_out, ln1_g, ln1_b, w_ffn_in, w_ffn_down, ln2_g, ln2_b, loss_target, m_meta_tokens, m_ln_in_g, m_ln_in_b, m_w_in, m_b_gate, m_attn_sinks, m_w_attn_up, m_w_pool_grp, m_pool_scale, m_w_pool_up, m_w_out, m_ln1_g, m_ln1_b, m_w_ffn_in, m_w_ffn_down, m_ln2_g, m_ln2_b, v_meta_tokens, v_ln_in_g, v_ln_in_b, v_w_in, v_b_gate, v_attn_sinks, v_w_attn_up, v_w_pool_grp, v_pool_scale, v_w_pool_up, v_w_out, v_ln1_g, v_ln1_b, v_w_ffn_in, v_w_ffn_down, v_ln2_g, v_ln2_b):
    S, D = x.shape[1], x.shape[2]
    Tp = S + BLOCK
    NQ = attn_sinks.shape[-1]
    ATTN = NQ * HEAD_DIM
    KVW = ATTN // Q_PER_KV
    POOL = pool_scale.shape[-1]
    IN = 8 * w_in.shape[2]
    FF = 8 * w_ffn_down.shape[1]
    uoff = ATTN + 2 * KVW
    goff = uoff + POOL
    gw = POOL // 4
    dcols = D // 8
    assert IN == goff + 2 * D and w_ffn_in.shape[2] * 8 == 2 * FF

    xi, yi, ci = _place()
    dev = 4 * xi + 2 * yi + ci
    x2, tgt = x[0], loss_target[0]
    order = _Order()

    def place_cols(a):
        return lax.dynamic_update_slice(jnp.zeros(a.shape[:-1] + (D,), F32), a, (0,) * (a.ndim - 1) + (dev * dcols,))

    small = _all_reduce_small(order, _pack([place_cols(meta_tokens), place_cols(b_gate[0])]), "small_inputs_gather")
    meta_full, bgate_full = _unpack(small, [(N_META, D), (2, D)])
    meta_pad = jnp.pad(meta_full, ((META_ROW0, 0), (0, 0)))

    wgrp_rows = w_pool_grp[0].reshape(4 * (gw // 8), gw)
    full_in = _place_own(order, w_in[0].T, "own_w_in")
    ag_in = _Xfer("gather_w_in", [full_in], _gather_send([full_in]))
    ag_in.start(order)
    mix_names = ["w_attn_up", "w_pool_grp", "w_pool_up", "w_out"]
    mix_shards = [w_attn_up[0].T, wgrp_rows, w_pool_up[0].T, w_out[0]]
    full_mix = [_place_own(order, s, "own_" + nm) for s, nm in zip(mix_shards, mix_names)]
    full_ffn = _place_own(order, w_ffn_in[0].T, "own_w_ffn_in")

    ln_in_g2, ln_in_b2 = ln_in_g.reshape(1, D), ln_in_b.reshape(1, D)
    tab = _rope_table(S)

    h0, h0b = _ln_in_fwd(order, x2, meta_pad, ln_in_g2, ln_in_b2)
    (full_in,) = ag_in.wait(order)
    fw_in = _Xfer("forward_w_in", [full_in], _gather_forward([full_in]))
    fw_in.start(order)
    ag_mix = _Xfer("gather_mixers", full_mix, _gather_send(full_mix))
    ag_mix.start(order)
    ag_ffn = _Xfer("gather_w_ffn_in", [full_ffn], _gather_send([full_ffn]))
    ag_ffn.start(order)
    full_down = _place_own(order, w_ffn_down[0], "own_w_ffn_down")
    (winT,) = fw_in.wait(order)
    proj = _mm(order, h0b, winT, kind="nt", out_dtype=F32, tm=1408, tn=512, name="proj")

    full_mix = ag_mix.wait(order)
    ag_down = _Xfer("gather_w_ffn_down", [full_down], _gather_send([full_down]))
    ag_down.start(order)
    fw_mix = _Xfer("forward_mixers", full_mix, _gather_forward(full_mix))
    fw_mix.start(order)
    att = _attn_fwd(order, proj, tab, attn_sinks, S, ATTN, KVW)
    wattT, wgrp_g, wpupT, wout = fw_mix.wait(order)
    wgrp = wgrp_g.reshape(8, 4, gw // 8, gw).transpose(1, 0, 2, 3).reshape(4, gw, gw)

    ps = _pool_fwd(order, proj, wgrp, pool_scale, S, uoff, POOL)
    a_out = _mm(order, att, wattT, kind="nt", out_dtype=F32, tm=1024, tn=1024, name="attn_up")
    p_out = _mm(order, ps, wpupT, kind="nt", out_dtype=F32, tm=1024, tn=1024, name="pool_up")
    mixed = _gate_mix(order, proj, bgate_full, a_out, p_out, S, D, goff)
    y1 = _mm(order, mixed, wout, kind="nn", out_dtype=F32, tm=1024, tn=1024, name="out_proj")

    (full_ffn,) = ag_ffn.wait(order)
    fw_ffn = _Xfer("forward_w_ffn_in", [full_ffn], _gather_forward([full_ffn]))
    fw_ffn.start(order)
    h1, h1b = _ln1_fwd(order, h0, y1, ln1_g, ln1_b)
    (wffnT,) = fw_ffn.wait(order)
    f = _mm(order, h1b, wffnT, kind="nt", out_dtype=F32, tm=1024, tn=512, name="ffn_in")

    (full_down,) = ag_down.wait(order)
    fw_down = _Xfer("forward_w_ffn_down", [full_down], _gather_forward([full_down]))
    fw_down.start(order)
    act = _swiglu_fwd(order, f, S, FF)
    (wdown,) = fw_down.wait(order)
    y2 = _mm(order, act, wdown, kind="nn", out_dtype=F32, tm=512, tn=1024, tk=5504, name="ffn_down")

    dz2, dz2b, dg2, db2, loss_part = _ln2_loss_bwd(order, h1, y2, tgt, ln2_g, ln2_b)
    gwdown = _mm(order, act, dz2b, kind="tn", out_dtype=BF16, tm=256, tn=2048, name="d_ffn_down")
    rs_down = _GradReduce("w_ffn_down", [gwdown], ["w_ffn_down"])
    rs_down.pair_start(order)
    dact = _mm(order, dz2b, wdown, kind="nt", out_dtype=F32, tm=2048, tn=256, name="d_act")
    rs_down.pair_sum_chip_start(order)
    df = _swiglu_bwd(order, f, dact, S, FF)
    gwffnT = _mm(order, df, h1b, kind="tn", out_dtype=BF16, tm=256, tn=2048, name="d_ffn_in", a_lead="halves")
    rs_ffn = _GradReduce("w_ffn_in", [gwffnT], ["w_ffn_in"])
    rs_ffn.pair_start(order)
    dh1 = _mm(order, df, wffnT, kind="nn", out_dtype=F32, tm=512, tn=1024, tk=5504, name="d_h1", a_lead="halves")
    rs_ffn.pair_sum_chip_start(order)
    dz1, dz1b, dg1, db1 = _ln1_bwd(order, h0, y1, ln1_g, dh1, dz2)
    gwout = _mm(order, mixed, dz1b, kind="tn", out_dtype=BF16, tm=512, tn=1024, name="d_out_proj")
    rs_out = _GradReduce("w_out", [gwout], ["w_out"])
    rs_out.pair_start(order)
    dmixed = _mm(order, dz1b, wout, kind="nt", out_dtype=F32, tm=1024, tn=1024, name="d_mixed")
    rs_out.pair_sum_chip_start(order)

    dproj = _zero_meta_block(order, Tp, IN)
    dap, dproj, dbgate = _gate_bwd(order, proj, bgate_full, a_out, p_out, dmixed, dproj, S, D, goff)
    gwattT = _mm(order, dap, att, kind="tn", out_dtype=BF16, tm=512, tn=1024, name="d_attn_up", a_lead=0)
    datt = _mm(order, dap, wattT, kind="nn", out_dtype=BF16, tm=1024, tn=1024, name="d_att", a_lead=0)
    gwpupT = _mm(order, dap, ps, kind="tn", out_dtype=BF16, tm=512, tn=1024, name="d_pool_up", a_lead=1)
    dps = _mm(order, dap, wpupT, kind="nn", out_dtype=F32, tm=1024, tn=1024, name="d_ps", a_lead=1)
    dpl, gwgrp, dscale = _pool_bwd_mix(order, proj, wgrp, pool_scale, dps, S, uoff, POOL)
    gwgrp_rows = gwgrp.reshape(4, 8, gw // 8, gw).transpose(1, 0, 2, 3).reshape(8 * 4 * (gw // 8), gw).astype(BF16)
    rs_mix = _GradReduce("mixers", [gwattT, gwgrp_rows, gwpupT], ["w_attn_up", "w_pool_grp", "w_pool_up"])
    rs_mix.pair_start(order)
    dproj = _pool_bwd_window(order, dpl, dproj, S, uoff, POOL)
    rs_mix.pair_sum_chip_start(order)
    dproj, dk, dv, dsink = _attn_bwd(order, proj, tab, attn_sinks, datt, dproj, S, ATTN, KVW)
    dproj = _put_dkv(order, dk, dv, dproj, ATTN)
    dh0 = _mm(order, dproj, winT, kind="nn", out_dtype=F32, tm=1408, tn=1024, tk=2560, name="d_h0")
    gwinT = _mm(order, dproj, h0b, kind="tn", out_dtype=BF16, tm=512, tn=1024, name="d_w_in")
    rs_in = _GradReduce("w_in", [gwinT], ["w_in"])
    rs_in.pair_start(order)
    dxin, dg_in, db_in = _ln_in_bwd(order, x2, meta_pad, ln_in_g2, dh0, dz1)
    grad_x = dxin[:S][None]
    dmeta = dxin[S + META_ROW0:]

    small_shapes = [(D,), (D,), (1, D), (1, D), (1, D), (1, D), (1, POOL), (1, NQ), (), (N_META, D), (2, D)]
    red = _all_reduce_small(order, _pack([dg_in, db_in, dg1, db1, dg2, db2, dscale, dsink[:, :, 0], loss_part,
                                          dmeta, dbgate]), "small_grads_all_reduce")
    rs_in.pair_sum_chip_start(order)

    weights = dict(meta_tokens=meta_tokens, ln_in_g=ln_in_g, ln_in_b=ln_in_b, w_in=w_in, b_gate=b_gate,
                   attn_sinks=attn_sinks, w_attn_up=w_attn_up, w_pool_grp=w_pool_grp, pool_scale=pool_scale,
                   w_pool_up=w_pool_up, w_out=w_out, ln1_g=ln1_g, ln1_b=ln1_b, w_ffn_in=w_ffn_in,
                   w_ffn_down=w_ffn_down, ln2_g=ln2_g, ln2_b=ln2_b)
    ms = dict(meta_tokens=m_meta_tokens, ln_in_g=m_ln_in_g, ln_in_b=m_ln_in_b, w_in=m_w_in, b_gate=m_b_gate,
              attn_sinks=m_attn_sinks, w_attn_up=m_w_attn_up, w_pool_grp=m_w_pool_grp, pool_scale=m_pool_scale,
              w_pool_up=m_w_pool_up, w_out=m_w_out, ln1_g=m_ln1_g, ln1_b=m_ln1_b, w_ffn_in=m_w_ffn_in,
              w_ffn_down=m_w_ffn_down, ln2_g=m_ln2_g, ln2_b=m_ln2_b)
    vs = dict(meta_tokens=v_meta_tokens, ln_in_g=v_ln_in_g, ln_in_b=v_ln_in_b, w_in=v_w_in, b_gate=v_b_gate,
              attn_sinks=v_attn_sinks, w_attn_up=v_w_attn_up, w_pool_grp=v_w_pool_grp, pool_scale=v_pool_scale,
              w_pool_up=v_w_pool_up, w_out=v_w_out, ln1_g=v_ln1_g, ln1_b=v_ln1_b, w_ffn_in=v_w_ffn_in,
              w_ffn_down=v_w_ffn_down, ln2_g=v_ln2_g, ln2_b=v_ln2_b)
    grads, deltas, new_ms, new_vs = {}, {}, {}, {}

    def update(nm, g):
        g = g.reshape(weights[nm].shape)
        grads[nm] = g
        deltas[nm], new_ms[nm], new_vs[nm] = _adamw(order, weights[nm], g, ms[nm], vs[nm], "adamw_" + nm)

    def update_reduced(nm, bufs, transposed=False):
        psum, parts = bufs
        if transposed:
            to2d, back = (lambda t: t[0].T), (lambda t: t.T[None])
        else:
            to2d, back = (lambda t: t.reshape(parts.shape[1:])), (lambda t: t.reshape(weights[nm].shape))
        outs = _chip_sum_adamw(order, to2d(weights[nm]), psum, parts, to2d(ms[nm]), to2d(vs[nm]), "adamw_" + nm)
        grads[nm], deltas[nm], new_ms[nm], new_vs[nm] = (back(t) for t in outs)

    update_reduced("w_ffn_down", rs_down.finish(order)[0])
    update_reduced("w_ffn_in", rs_ffn.finish(order)[0], transposed=True)
    update_reduced("w_out", rs_out.finish(order)[0])
    b_att, b_grp, b_pup = rs_mix.finish(order)
    update_reduced("w_attn_up", b_att, transposed=True)
    update_reduced("w_pool_grp", b_grp)
    update_reduced("w_pool_up", b_pup, transposed=True)

    (g_ln_in_g, g_ln_in_b, g_ln1_g, g_ln1_b, g_ln2_g, g_ln2_b, g_scale, g_sinks, loss_sum, g_meta_full,
     g_bgate_full) = _unpack(red, small_shapes)
    loss = 0.5 * loss_sum
    update("meta_tokens", lax.dynamic_slice(g_meta_full, (0, dev * dcols), (N_META, dcols)))
    update("b_gate", lax.dynamic_slice(g_bgate_full, (0, dev * dcols), (2, dcols)))
    for nm, g in (("ln_in_g", g_ln_in_g), ("ln_in_b", g_ln_in_b), ("ln1_g", g_ln1_g), ("ln1_b", g_ln1_b),
                  ("ln2_g", g_ln2_g), ("ln2_b", g_ln2_b), ("pool_scale", g_scale), ("attn_sinks", g_sinks)):
        update(nm, g)

    update_reduced("w_in", rs_in.finish(order)[0], transposed=True)

    names = list(weights)
    return (loss, grad_x, *[grads[n] for n in names], *[deltas[n] for n in names],
            *[new_ms[n] for n in names], *[new_vs[n] for n in names])
```

```python
import jax
import jax.numpy as jnp
from jax import lax
from jax.experimental import pallas as pl
from jax.experimental.pallas import tpu as pltpu

F32 = jnp.float32
BF16 = jnp.bfloat16
MESH = pl.DeviceIdType.MESH

N_META = 16
HEAD_DIM = 64
Q_PER_KV = 8
WINDOW = 128
BLOCK = 128
ATTN_SCALE = HEAD_DIM ** -0.5
ROPE_DIM = HEAD_DIM // 4
ROPE_THETA = 500000.0
NEG_INF = -1e30
POOL_WINDOWS = (2, 4, 8, 16)
LN_EPS = 1e-5
DN_ALPHA = 2.0 ** 0.25
ADAM_LR = 0.001
ADAM_B1 = 0.9
ADAM_B2 = 0.999
ADAM_EPS = 1e-08
ADAM_WD = 0.01
ADAM_STEP = 10

LANES = 128
META_ROW0 = BLOCK - N_META
VMEM_LIMIT = 56 * 1024 * 1024

ANY = pl.BlockSpec(memory_space=pl.ANY)
HBM = pl.BlockSpec(memory_space=pltpu.HBM)
SEM = pl.BlockSpec(memory_space=pltpu.SEMAPHORE)
EFFECT = pltpu.SideEffectType.DATAFLOW_SIDE_EFFECTING


def _params(sem=None, **kw):
    return pltpu.CompilerParams(dimension_semantics=sem, vmem_limit_bytes=VMEM_LIMIT, **kw)


class _Order:
    def __init__(self):
        self.last = None


def _call(order, body, operands, *, name, in_specs, out_specs, out_shape, grid=(), scratch=(), sem=None,
          aliases=None, prefetch=()):
    n_in, npf = len(operands), len(prefetch)
    tok = order.last
    if tok is not None and any(tok is op for op in operands):
        tok = None

    def wrapped(*refs):
        refs = list(refs)
        if tok is not None:
            del refs[npf + n_in]
        body(*refs)

    specs = list(in_specs) + ([ANY] if tok is not None else [])
    ops = list(operands) + ([tok] if tok is not None else [])
    if npf:
        out = pl.pallas_call(
            wrapped, name=name, out_shape=out_shape, compiler_params=_params(sem),
            grid_spec=pltpu.PrefetchScalarGridSpec(num_scalar_prefetch=npf, grid=grid, in_specs=specs,
                                                   out_specs=out_specs, scratch_shapes=list(scratch)),
        )(*prefetch, *ops)
    else:
        out = pl.pallas_call(
            wrapped, name=name, grid=grid, in_specs=specs, out_specs=out_specs, out_shape=out_shape,
            scratch_shapes=list(scratch), input_output_aliases=aliases or {}, compiler_params=_params(sem),
        )(*ops)
    order.last = out[0] if isinstance(out, (list, tuple)) else out
    return out


def _pick(dim, pref, mult=LANES):
    best = None
    t = mult
    while t <= min(dim, pref):
        if dim % t == 0:
            best = t
        t += mult
    return dim if best is None else best


_DIMS = {"nn": (((1,), (0,)), ((), ())), "nt": (((1,), (1,)), ((), ())), "tn": (((0,), (0,)), ((), ()))}


def _mm(order, a, b, *, kind, out_dtype, tm, tn, tk=None, name, a_lead=None):
    a2 = a.shape[-2:]
    halves = a_lead == "halves"
    if halves:
        a2 = (a2[0], 2 * a2[1])
    if kind == "tn":
        K, M = a2
    else:
        M, K = a2
    N = b.shape[0] if kind == "nt" else b.shape[1]
    half_cols = a2[1] // 2
    tm = _pick(half_cols if halves and kind == "tn" else M, tm)
    tn = _pick(N, tn)
    tk = K if tk is None else _pick(half_cols if halves and kind != "tn" else K, tk)
    nm, nn_, nk = M // tm, N // tn, K // tk
    a_bytes = M * K * a.dtype.itemsize
    b_bytes = N * K * b.dtype.itemsize
    i_outer = (a_bytes + nm * b_bytes <= b_bytes + nn_ * a_bytes) if nk == 1 else True

    def ij(g0, g1):
        return (g0, g1) if i_outer else (g1, g0)

    def a_map(g0, g1, k):
        i, _ = ij(g0, g1)
        if halves:
            per = half_cols // (tm if kind == "tn" else tk)
            return (i // per, k, i % per) if kind == "tn" else (k // per, i, k % per)
        idx = (k, i) if kind == "tn" else (i, k)
        return idx if a_lead is None else (a_lead,) + idx

    def b_map(g0, g1, k):
        _, j = ij(g0, g1)
        return (j, k) if kind == "nt" else (k, j)

    def o_map(g0, g1, k):
        return ij(g0, g1)

    a_blk = (tk, tm) if kind == "tn" else (tm, tk)
    if a_lead is not None:
        a_blk = (None,) + a_blk
    b_blk = (tn, tk) if kind == "nt" else (tk, tn)

    def body(a_ref, b_ref, o_ref, *acc):
        p = lax.dot_general(a_ref[...], b_ref[...], _DIMS[kind], preferred_element_type=F32)
        if nk == 1:
            o_ref[...] = p.astype(o_ref.dtype)
        else:
            k = pl.program_id(2)

            @pl.when(k == 0)
            def _():
                acc[0][...] = p

            @pl.when(k > 0)
            def _():
                acc[0][...] += p

            @pl.when(k == nk - 1)
            def _():
                o_ref[...] = acc[0][...].astype(o_ref.dtype)

    grid = (nm, nn_, nk) if i_outer else (nn_, nm, nk)
    return _call(
        order, body, [a, b], name=name, grid=grid,
        in_specs=[pl.BlockSpec(a_blk, a_map), pl.BlockSpec(b_blk, b_map)],
        out_specs=pl.BlockSpec((tm, tn), o_map),
        out_shape=jax.ShapeDtypeStruct((M, N), out_dtype),
        scratch=[] if nk == 1 else [pltpu.VMEM((tm, tn), F32)],
        sem=("parallel", "parallel", "arbitrary"))


def _ln_stats(z):
    mu = jnp.mean(z, axis=-1, keepdims=True)
    zc = z - mu
    var = jnp.mean(zc * zc, axis=-1, keepdims=True)
    rstd = lax.rsqrt(var + LN_EPS)
    return zc * rstd, rstd


def _ln_bwd(dy, xhat, rstd, g):
    dxh = dy * g
    m1 = jnp.mean(dxh, axis=-1, keepdims=True)
    m2 = jnp.mean(dxh * xhat, axis=-1, keepdims=True)
    return rstd * (dxh - m1 - xhat * m2)


def _ln_in_fwd(order, x, meta_pad, g, b):
    S, D = x.shape
    nb = S // BLOCK

    def body(x_ref, mp_ref, g_ref, b_ref, h_ref, hb_ref):
        is_meta = pl.program_id(0) == nb
        xin = jnp.where(is_meta, mp_ref[...], x_ref[...])
        xhat, _ = _ln_stats(xin)
        y = xhat * g_ref[...] + b_ref[...]
        h_ref[...] = y
        hb_ref[...] = y.astype(BF16)

    row = pl.BlockSpec((BLOCK, D), lambda i: (i, 0))
    vec = pl.BlockSpec((1, D), lambda i: (0, 0))
    return _call(
        order, body, [x, meta_pad, g, b], name="ln_in_fwd", grid=(nb + 1,),
        in_specs=[pl.BlockSpec((BLOCK, D), lambda i: (jnp.minimum(i, nb - 1), 0)),
                  pl.BlockSpec((BLOCK, D), lambda i: (0, 0)), vec, vec],
        out_specs=[row, row],
        out_shape=[jax.ShapeDtypeStruct((S + BLOCK, D), F32), jax.ShapeDtypeStruct((S + BLOCK, D), BF16)],
        sem=("parallel",))


def _ln_in_bwd(order, x, meta_pad, g, dh0, dz1):
    S, D = x.shape
    nb = S // BLOCK

    def body(x_ref, mp_ref, g_ref, dh_ref, dz_ref, dx_ref, dg_ref, db_ref):
        i = pl.program_id(0)
        is_meta = i == nb
        xin = jnp.where(is_meta, mp_ref[...], x_ref[...])
        xhat, rstd = _ln_stats(xin)
        dy = dh_ref[...] + jnp.where(is_meta, 0.0, DN_ALPHA) * dz_ref[...]
        dx_ref[...] = _ln_bwd(dy, xhat, rstd, g_ref[...])

        @pl.when(i == 0)
        def _():
            dg_ref[...] = jnp.zeros_like(dg_ref)
            db_ref[...] = jnp.zeros_like(db_ref)

        dg_ref[...] += jnp.sum(dy * xhat, axis=0, keepdims=True)
        db_ref[...] += jnp.sum(dy, axis=0, keepdims=True)

    row = pl.BlockSpec((BLOCK, D), lambda i: (i, 0))
    rowx = pl.BlockSpec((BLOCK, D), lambda i: (jnp.minimum(i, nb - 1), 0))
    vec = pl.BlockSpec((1, D), lambda i: (0, 0))
    return _call(
        order, body, [x, meta_pad, g, dh0, dz1], name="ln_in_bwd", grid=(nb + 1,),
        in_specs=[rowx, pl.BlockSpec((BLOCK, D), lambda i: (0, 0)), vec, row, rowx],
        out_specs=[row, vec, vec],
        out_shape=[jax.ShapeDtypeStruct((S + BLOCK, D), F32), jax.ShapeDtypeStruct((1, D), F32),
                   jax.ShapeDtypeStruct((1, D), F32)],
        sem=("arbitrary",))


def _ln1_fwd(order, h0, y1, g, b):
    S, D = y1.shape
    tm = _pick(S, BLOCK, 8)

    def body(h_ref, y_ref, g_ref, b_ref, o_ref, ob_ref):
        xhat, _ = _ln_stats(DN_ALPHA * h_ref[...] + y_ref[...])
        y = xhat * g_ref[...] + b_ref[...]
        o_ref[...] = y
        ob_ref[...] = y.astype(BF16)

    row = pl.BlockSpec((tm, D), lambda i: (i, 0))
    vec = pl.BlockSpec((1, D), lambda i: (0, 0))
    return _call(
        order, body, [h0, y1, g, b], name="ln1_fwd", grid=(S // tm,), in_specs=[row, row, vec, vec],
        out_specs=[row, row],
        out_shape=[jax.ShapeDtypeStruct((S, D), F32), jax.ShapeDtypeStruct((S, D), BF16)],
        sem=("parallel",))


def _ln1_bwd(order, h0, y1, g, dh1, dz2):
    S, D = y1.shape
    tm = _pick(S, BLOCK, 8)

    def body(h_ref, y_ref, g_ref, dh_ref, dz2_ref, dz_ref, dzb_ref, dg_ref, db_ref):
        i = pl.program_id(0)
        xhat, rstd = _ln_stats(DN_ALPHA * h_ref[...] + y_ref[...])
        dy = dh_ref[...] + DN_ALPHA * dz2_ref[...]
        dz = _ln_bwd(dy, xhat, rstd, g_ref[...])
        dz_ref[...] = dz
        dzb_ref[...] = dz.astype(BF16)

        @pl.when(i == 0)
        def _():
            dg_ref[...] = jnp.zeros_like(dg_ref)
            db_ref[...] = jnp.zeros_like(db_ref)

        dg_ref[...] += jnp.sum(dy * xhat, axis=0, keepdims=True)
        db_ref[...] += jnp.sum(dy, axis=0, keepdims=True)

    row = pl.BlockSpec((tm, D), lambda i: (i, 0))
    vec = pl.BlockSpec((1, D), lambda i: (0, 0))
    return _call(
        order, body, [h0, y1, g, dh1, dz2], name="ln1_bwd", grid=(S // tm,),
        in_specs=[row, row, vec, row, row], out_specs=[row, row, vec, vec],
        out_shape=[jax.ShapeDtypeStruct((S, D), F32), jax.ShapeDtypeStruct((S, D), BF16),
                   jax.ShapeDtypeStruct((1, D), F32), jax.ShapeDtypeStruct((1, D), F32)],
        sem=("arbitrary",))


def _ln2_loss_bwd(order, h1, y2, target, g, b):
    S, D = y2.shape
    tm = _pick(S, BLOCK, 8)

    def body(h_ref, y_ref, t_ref, g_ref, b_ref, dz_ref, dzb_ref, dg_ref, db_ref, loss_ref):
        i = pl.program_id(0)
        xhat, rstd = _ln_stats(DN_ALPHA * h_ref[...] + y_ref[...])
        diff = xhat * g_ref[...] + b_ref[...] - t_ref[...]
        dy = diff / D
        dz = _ln_bwd(dy, xhat, rstd, g_ref[...])
        dz_ref[...] = dz
        dzb_ref[...] = dz.astype(BF16)

        @pl.when(i == 0)
        def _():
            dg_ref[...] = jnp.zeros_like(dg_ref)
            db_ref[...] = jnp.zeros_like(db_ref)
            loss_ref[...] = jnp.zeros_like(loss_ref)

        dg_ref[...] += jnp.sum(dy * xhat, axis=0, keepdims=True)
        db_ref[...] += jnp.sum(dy, axis=0, keepdims=True)
        loss_ref[...] += jnp.sum(jnp.mean(diff * diff, axis=-1, keepdims=True), axis=0, keepdims=True)

    row = pl.BlockSpec((tm, D), lambda i: (i, 0))
    vec = pl.BlockSpec((1, D), lambda i: (0, 0))
    one = pl.BlockSpec((1, 1), lambda i: (0, 0))
    return _call(
        order, body, [h1, y2, target, g, b], name="ln2_loss_bwd", grid=(S // tm,),
        in_specs=[row, row, row, vec, vec], out_specs=[row, row, vec, vec, one],
        out_shape=[jax.ShapeDtypeStruct((S, D), F32), jax.ShapeDtypeStruct((S, D), BF16),
                   jax.ShapeDtypeStruct((1, D), F32), jax.ShapeDtypeStruct((1, D), F32),
                   jax.ShapeDtypeStruct((1, 1), F32)],
        sem=("arbitrary",))


def _rope_table(S):
    r = jnp.arange(S + BLOCK)
    pos = jnp.where(r < S, r + N_META, jnp.maximum(r - (S + META_ROW0), 0))
    half = ROPE_DIM // 2
    lane = jnp.arange(LANES) % HEAD_DIM
    inv_freq = ROPE_THETA ** (-(lane % half).astype(F32) * 2.0 / ROPE_DIM)
    ang = pos.astype(F32)[:, None] * inv_freq[None, :]
    cos, sin = jnp.cos(ang), jnp.sin(ang)
    c = jnp.where(lane < ROPE_DIM, cos, 1.0)
    sa = jnp.where(lane < half, -sin, 0.0)
    sb = jnp.where((lane >= half) & (lane < ROPE_DIM), sin, 0.0)
    return jnp.concatenate([c, sa, sb], axis=1).astype(F32)


def _rope(x, tab):
    h = ROPE_DIM // 2
    return (x * tab[:, :LANES] + pltpu.roll(x, LANES - h, 1) * tab[:, LANES:2 * LANES]
            + pltpu.roll(x, h, 1) * tab[:, 2 * LANES:])


def _rope_t(dy, tab):
    h = ROPE_DIM // 2
    return (dy * tab[:, :LANES] + pltpu.roll(dy * tab[:, LANES:2 * LANES], h, 1)
            + pltpu.roll(dy * tab[:, 2 * LANES:], LANES - h, 1))


def _attn_tiles(g, n, S, sink_ref, q_ref, k_ref, v_ref, tab_ref):
    NQG = Q_PER_KV // 2
    R = NQG * BLOCK
    halfsel = (g % 2).astype(F32)
    prev = jnp.maximum(n - 1, 0)
    qrow = pl.ds(pl.multiple_of(n * BLOCK, BLOCK), BLOCK)
    prow = pl.ds(pl.multiple_of(prev * BLOCK, BLOCK), BLOCK)
    mrow = pl.ds(S, BLOCK)

    tq = tab_ref[qrow, :]
    qf = q_ref[...]
    q4 = jnp.concatenate([_rope(qf[:, LANES * p:LANES * (p + 1)], tq) for p in range(NQG)], axis=0).astype(BF16)

    tk = jnp.concatenate([tab_ref[mrow, :], tab_ref[prow, :], tq], axis=0)
    kr = _rope(jnp.concatenate([k_ref[mrow, :], k_ref[prow, :], k_ref[qrow, :]], axis=0), tk)
    vr = jnp.concatenate([v_ref[mrow, :], v_ref[prow, :], v_ref[qrow, :]], axis=0)

    lane = lax.broadcasted_iota(jnp.int32, kr.shape, 1)
    own = jnp.where(lane < HEAD_DIM, 1.0 - halfsel, halfsel)

    def lo_hi(t):
        mine = t * own
        other = pltpu.roll(mine, HEAD_DIM, 1)
        lo = mine * (1.0 - halfsel) + other * halfsel
        hi = other * (1.0 - halfsel) + mine * halfsel
        return lo.astype(BF16), hi.astype(BF16)

    klo, khi = lo_hi(kr)
    vlo, vhi = lo_hi(vr)

    row = lax.broadcasted_iota(jnp.int32, (R, 3 * BLOCK), 0) & (BLOCK - 1)
    col = lax.broadcasted_iota(jnp.int32, (R, 3 * BLOCK), 1)
    jj = col & (BLOCK - 1)
    no_prev = jnp.where(n >= 1, 0, 2 * BLOCK)
    mask = (((col < BLOCK) & (col >= META_ROW0))
            | ((col >= BLOCK) & (col < 2 * BLOCK) & (jj > row + no_prev))
            | ((col >= 2 * BLOCK) & (jj <= row)))

    def soft(kk, parity):
        sk = jnp.concatenate(
            [jnp.full((BLOCK, 1), sink_ref[0, Q_PER_KV * g + 2 * p + parity], F32) for p in range(NQG)], axis=0)
        s = lax.dot_general(q4, kk, _DIMS["nt"], preferred_element_type=F32) * ATTN_SCALE
        s = jnp.where(mask, s, NEG_INF)
        m = jnp.maximum(jnp.max(s, axis=1, keepdims=True), sk)
        p = jnp.exp(s - m)
        es = jnp.exp(sk - m)
        inv = 1.0 / (jnp.sum(p, axis=1, keepdims=True) + es)
        return p * inv, es * inv

    pe, sink_e = soft(klo, 0)
    po, sink_o = soft(khi, 1)
    return q4, tk, (klo, khi), (vlo, vhi), (pe, po), (sink_e, sink_o), own


def _attn_specs(S, ATTN, KVW):
    Tp = S + BLOCK
    koff, voff = ATTN // LANES, (ATTN + KVW) // LANES
    gw = Q_PER_KV * HEAD_DIM
    return [pl.BlockSpec(memory_space=pltpu.SMEM),
            pl.BlockSpec((BLOCK, gw), lambda g, n: (n, g)),
            pl.BlockSpec((Tp, LANES), lambda g, n: (0, koff + g // 2)),
            pl.BlockSpec((Tp, LANES), lambda g, n: (0, voff + g // 2)),
            pl.BlockSpec((Tp, 3 * LANES), lambda g, n: (0, 0))]


def _attn_fwd(order, proj, tab, sinks, S, ATTN, KVW):
    G = KVW // HEAD_DIM
    nb = S // BLOCK
    gw = Q_PER_KV * HEAD_DIM

    def body(sink_ref, q_ref, k_ref, v_ref, tab_ref, o_ref):
        g, n = pl.program_id(0), pl.program_id(1)
        _, _, _, (vlo, vhi), (pe, po), _, _ = _attn_tiles(g, n, S, sink_ref, q_ref, k_ref, v_ref, tab_ref)
        o4 = (jnp.dot(pe.astype(BF16), vlo, preferred_element_type=F32)
              + jnp.dot(po.astype(BF16), vhi, preferred_element_type=F32))
        o_ref[...] = jnp.concatenate(
            [o4[BLOCK * p:BLOCK * (p + 1)] for p in range(Q_PER_KV // 2)], axis=1).astype(BF16)

    return _call(
        order, body, [sinks, proj, proj, proj, tab], name="attn_fwd", grid=(G, nb),
        in_specs=_attn_specs(S, ATTN, KVW),
        out_specs=pl.BlockSpec((BLOCK, gw), lambda g, n: (n, g)),
        out_shape=jax.ShapeDtypeStruct((S, ATTN), BF16),
        sem=("parallel", "arbitrary"))


def _attn_bwd(order, proj, tab, sinks, da, dproj, S, ATTN, KVW):
    G = KVW // HEAD_DIM
    nb = S // BLOCK
    Tp = S + BLOCK
    NQG = Q_PER_KV // 2
    gw = Q_PER_KV * HEAD_DIM

    def body(sink_ref, q_ref, k_ref, v_ref, tab_ref, da_ref, dproj_in, dq_ref, dk_ref, dv_ref, ds_ref):
        del dproj_in
        g, n = pl.program_id(0), pl.program_id(1)
        q4, tk, (klo, khi), (vlo, vhi), (pe, po), (sink_e, sink_o), own = _attn_tiles(
            g, n, S, sink_ref, q_ref, k_ref, v_ref, tab_ref)
        dof = da_ref[...]
        do4 = jnp.concatenate([dof[:, LANES * p:LANES * (p + 1)] for p in range(NQG)], axis=0)

        def grads(p, vv):
            dp = lax.dot_general(do4, vv, _DIMS["nt"], preferred_element_type=F32)
            delta = jnp.sum(p * dp, axis=1, keepdims=True)
            return (p * (dp - delta) * ATTN_SCALE).astype(BF16), delta

        dse, delta_e = grads(pe, vlo)
        dso, delta_o = grads(po, vhi)

        dq4 = (jnp.dot(dse, klo, preferred_element_type=F32) + jnp.dot(dso, khi, preferred_element_type=F32))
        tq = tk[2 * BLOCK:]
        dq_ref[...] = jnp.concatenate(
            [_rope_t(dq4[BLOCK * p:BLOCK * (p + 1)], tq) for p in range(NQG)], axis=1).astype(BF16)

        lane = lax.broadcasted_iota(jnp.int32, (3 * BLOCK, LANES), 1)

        def fold(lo_part, hi_part):
            t = jnp.where(lane < HEAD_DIM, lo_part, hi_part)
            return t + pltpu.roll(t, HEAD_DIM, 1)

        dk = _rope_t(fold(lax.dot_general(dse, q4, _DIMS["tn"], preferred_element_type=F32),
                          lax.dot_general(dso, q4, _DIMS["tn"], preferred_element_type=F32)), tk) * own
        dv = fold(lax.dot_general(pe.astype(BF16), do4, _DIMS["tn"], preferred_element_type=F32),
                  lax.dot_general(po.astype(BF16), do4, _DIMS["tn"], preferred_element_type=F32)) * own

        @pl.when((n == 0) & (g % 2 == 0))
        def _():
            dk_ref[...] = jnp.zeros_like(dk_ref)
            dv_ref[...] = jnp.zeros_like(dv_ref)

        @pl.when(n == 0)
        def _():
            ds_ref[...] = jnp.zeros_like(ds_ref)

        prev = jnp.maximum(n - 1, 0)
        qrow = pl.ds(pl.multiple_of(n * BLOCK, BLOCK), BLOCK)
        prow = pl.ds(pl.multiple_of(prev * BLOCK, BLOCK), BLOCK)
        mrow = pl.ds(S, BLOCK)
        for ref, val in ((dk_ref, dk), (dv_ref, dv)):
            ref[mrow, :] += val[:BLOCK]
            ref[prow, :] += val[BLOCK:2 * BLOCK]
            ref[qrow, :] += val[2 * BLOCK:]

        srow = lax.broadcasted_iota(jnp.int32, (Q_PER_KV, LANES), 0)
        acc = jnp.zeros((Q_PER_KV, LANES), F32)
        for p in range(NQG):
            for parity, (sk, dl) in enumerate(((sink_e, delta_e), (sink_o, delta_o))):
                val = -jnp.sum(sk[BLOCK * p:BLOCK * (p + 1)] * dl[BLOCK * p:BLOCK * (p + 1)])
                acc = jnp.where(srow == 2 * p + parity, val, acc)
        ds_ref[0] += acc

    in_specs = _attn_specs(S, ATTN, KVW) + [pl.BlockSpec((BLOCK, gw), lambda g, n: (n, g)), ANY]
    slab = pl.BlockSpec((Tp, LANES), lambda g, n: (0, g // 2))
    return _call(
        order, body, [sinks, proj, proj, proj, tab, da, dproj], name="attn_bwd", grid=(G, nb), in_specs=in_specs,
        out_specs=[pl.BlockSpec((BLOCK, gw), lambda g, n: (n, g)), slab, slab,
                   pl.BlockSpec((1, Q_PER_KV, LANES), lambda g, n: (g, 0, 0))],
        out_shape=[jax.ShapeDtypeStruct(dproj.shape, BF16), jax.ShapeDtypeStruct((Tp, KVW), F32),
                   jax.ShapeDtypeStruct((Tp, KVW), F32), jax.ShapeDtypeStruct((G, Q_PER_KV, LANES), F32)],
        aliases={6: 0}, sem=("arbitrary", "arbitrary"))


def _zero_meta_block(order, Tp, IN):
    tc = _pick(IN, 4096)

    def body(o_ref):
        o_ref[...] = jnp.zeros_like(o_ref)

    return _call(
        order, body, [], name="dproj_zero_meta", grid=(IN // tc,), in_specs=[],
        out_specs=pl.BlockSpec((BLOCK, tc), lambda j: (Tp // BLOCK - 1, j)),
        out_shape=jax.ShapeDtypeStruct((Tp, IN), BF16), sem=("parallel",))


def _put_dkv(order, dk, dv, dproj, ATTN):
    Tp, KVW = dk.shape
    nkb = KVW // LANES
    koff = ATTN // LANES

    def body(dk_ref, dv_ref, dproj_in, o_ref):
        del dproj_in
        t = pl.program_id(0)
        o_ref[...] = jnp.where(t < nkb, dk_ref[...], dv_ref[...]).astype(BF16)

    src = pl.BlockSpec((Tp, LANES), lambda t: (0, t % nkb))
    return _call(
        order, body, [dk, dv, dproj], name="dproj_put_dkv", grid=(2 * nkb,), in_specs=[src, src, ANY],
        out_specs=pl.BlockSpec((Tp, LANES), lambda t: (0, koff + t)),
        out_shape=jax.ShapeDtypeStruct(dproj.shape, BF16), aliases={2: 0}, sem=("parallel",))


HALO = 16


def _window_sums(x, up):
    n = x.shape[0]
    out = []
    s = x
    for k in (1, 2, 4, 8):
        s = s + pltpu.roll(s, (n - k) if up else k, 0)
        out.append(s)
    return out


def _pool_specs(S, ub, gw, tm):
    meta_halo = (S + BLOCK - HALO) // HALO

    def main(g):
        return pl.BlockSpec((tm, gw), lambda i: (i, ub + g))

    def halo(g):
        return pl.BlockSpec((HALO, gw), lambda i: (jnp.where(i == 0, meta_halo, i * (tm // HALO) - 1), ub + g))

    return [main(g) for g in range(4)] + [halo(g) for g in range(4)]


def _pooled(main_refs, halo_refs, g):
    x = jnp.concatenate([halo_refs[g][...], main_refs[g][...]], axis=0)
    s = _window_sums(x, up=False)[g]
    return (s[HALO:] * (1.0 / POOL_WINDOWS[g]) - x[HALO:]).astype(BF16)


def _pool_fwd(order, proj, wgrp, scale, S, uoff, POOL):
    gw = POOL // 4
    tm = BLOCK

    def body(*refs):
        main, halo = refs[:4], refs[4:8]
        w_ref, sc_ref, o_ref = refs[8:]
        for g in range(4):
            mixed = jnp.dot(_pooled(main, halo, g), w_ref[g], preferred_element_type=F32)
            o_ref[:, gw * g:gw * (g + 1)] = (mixed * sc_ref[:, gw * g:gw * (g + 1)]).astype(BF16)

    return _call(
        order, body, [proj] * 8 + [wgrp, scale], name="pool_fwd", grid=(S // tm,),
        in_specs=_pool_specs(S, uoff // gw, gw, tm) + [
            pl.BlockSpec((4, gw, gw), lambda i: (0, 0, 0)), pl.BlockSpec((1, POOL), lambda i: (0, 0))],
        out_specs=pl.BlockSpec((tm, POOL), lambda i: (i, 0)),
        out_shape=jax.ShapeDtypeStruct((S, POOL), BF16), sem=("parallel",))


def _pool_bwd_mix(order, proj, wgrp, scale, dps, S, uoff, POOL):
    gw = POOL // 4
    tm = BLOCK

    def body(*refs):
        main, halo = refs[:4], refs[4:8]
        w_ref, sc_ref, dps_ref, dpl_ref, dw_ref, dsc_ref = refs[8:]
        i = pl.program_id(0)

        @pl.when(i == 0)
        def _():
            dw_ref[...] = jnp.zeros_like(dw_ref)
            dsc_ref[...] = jnp.zeros_like(dsc_ref)

        for g in range(4):
            cols = slice(gw * g, gw * (g + 1))
            pooled = _pooled(main, halo, g)
            mixed = jnp.dot(pooled, w_ref[g], preferred_element_type=F32)
            dps_g = dps_ref[:, cols]
            dsc_ref[:, cols] += jnp.sum(dps_g * mixed, axis=0, keepdims=True)
            dms = (dps_g * sc_ref[:, cols]).astype(BF16)
            dw_ref[g] += lax.dot_general(pooled, dms, _DIMS["tn"], preferred_element_type=F32)
            dpl_ref[:, cols] = lax.dot_general(dms, w_ref[g], _DIMS["nt"], preferred_element_type=F32)

    row = pl.BlockSpec((tm, POOL), lambda i: (i, 0))
    return _call(
        order, body, [proj] * 8 + [wgrp, scale, dps], name="pool_bwd_mix", grid=(S // tm,),
        in_specs=_pool_specs(S, uoff // gw, gw, tm) + [
            pl.BlockSpec((4, gw, gw), lambda i: (0, 0, 0)), pl.BlockSpec((1, POOL), lambda i: (0, 0)), row],
        out_specs=[row, pl.BlockSpec((4, gw, gw), lambda i: (0, 0, 0)), pl.BlockSpec((1, POOL), lambda i: (0, 0))],
        out_shape=[jax.ShapeDtypeStruct((S, POOL), F32), jax.ShapeDtypeStruct((4, gw, gw), F32),
                   jax.ShapeDtypeStruct((1, POOL), F32)],
        sem=("arbitrary",))


def _pool_bwd_window(order, dpl, dproj, S, uoff, POOL):
    gw = POOL // 4
    nb = S // BLOCK
    ub = uoff // gw

    def body(main_ref, halo_ref, dproj_in, o_ref):
        del dproj_in
        b, g = pl.program_id(0), pl.program_id(1)
        main = jnp.where(b < nb, main_ref[...], 0.0)
        halo = jnp.where(b == nb - 1, 0.0, halo_ref[...])
        sums = _window_sums(jnp.concatenate([main, halo], axis=0), up=True)
        du = jnp.zeros((BLOCK, gw), F32)
        for k, w in enumerate(POOL_WINDOWS):
            du = jnp.where(g == k, sums[k][:BLOCK] * (1.0 / w), du)
        du = du - main
        row = lax.broadcasted_iota(jnp.int32, du.shape, 0)
        first_valid = jnp.where(b == nb, META_ROW0, 0)
        o_ref[...] = jnp.where(row >= first_valid, du, 0.0).astype(BF16)

    return _call(
        order, body, [dpl, dpl, dproj], name="pool_bwd_window", grid=(nb + 1, 4),
        in_specs=[pl.BlockSpec((BLOCK, gw), lambda b, g: (jnp.minimum(b, nb - 1), g)),
                  pl.BlockSpec((HALO, gw), lambda b, g: (
                      jnp.where(b == nb, 0, jnp.minimum((b + 1) * (BLOCK // HALO), S // HALO - 1)), g)),
                  ANY],
        out_specs=pl.BlockSpec((BLOCK, gw), lambda b, g: (b, ub + g)),
        out_shape=jax.ShapeDtypeStruct(dproj.shape, BF16), aliases={2: 0}, sem=("parallel", "parallel"))


def _sigmoid(x):
    return 1.0 / (1.0 + jnp.exp(-x))


def _gate_tiles(S, D, goff):
    tc = 512
    while goff % tc or D % tc:
        tc //= 2
    return _pick(S, 512, 8), tc


def _gate_mix(order, proj, bgate, a_out, p_out, S, D, goff):
    tm, tc = _gate_tiles(S, D, goff)
    g0b, nd = goff // tc, D // tc

    def body(l0_ref, l1_ref, b_ref, a_ref, p_ref, o_ref):
        g0 = _sigmoid(l0_ref[...] + b_ref[0:1, :])
        g1 = _sigmoid(l1_ref[...] + b_ref[1:2, :])
        o_ref[...] = (g0 * a_ref[...] + g1 * p_ref[...]).astype(BF16)

    tile = pl.BlockSpec((tm, tc), lambda i, j: (i, j))
    return _call(
        order, body, [proj, proj, bgate, a_out, p_out], name="gate_mix", grid=(S // tm, nd),
        in_specs=[pl.BlockSpec((tm, tc), lambda i, j: (i, g0b + j)),
                  pl.BlockSpec((tm, tc), lambda i, j: (i, g0b + nd + j)),
                  pl.BlockSpec((2, tc), lambda i, j: (0, j)), tile, tile],
        out_specs=tile, out_shape=jax.ShapeDtypeStruct((S, D), BF16), sem=("parallel", "parallel"))


def _gate_bwd(order, proj, bgate, a_out, p_out, dmixed, dproj, S, D, goff):
    tm, tc = _gate_tiles(S, D, goff)
    g0b, nd = goff // tc, D // tc

    def body(l0_ref, l1_ref, b_ref, a_ref, p_ref, dm_ref, dproj_in, dap_ref, dl_ref, db_ref):
        del dproj_in
        i, br = pl.program_id(1), pl.program_id(2)
        first = br == 0
        logit = jnp.where(first, l0_ref[...], l1_ref[...]) + jnp.where(first, b_ref[0:1, :], b_ref[1:2, :])
        val = jnp.where(first, a_ref[...], p_ref[...])
        gate = _sigmoid(logit)
        dm = dm_ref[...]
        dap_ref[...] = (dm * gate).astype(BF16)
        dl = dm * val * gate * (1.0 - gate)
        dl_ref[...] = dl.astype(BF16)

        @pl.when((i == 0) & first)
        def _():
            db_ref[...] = jnp.zeros_like(db_ref)

        db_ref[br] += jnp.sum(dl, axis=0, keepdims=True)

    tile = pl.BlockSpec((tm, tc), lambda j, i, br: (i, j))
    return _call(
        order, body, [proj, proj, bgate, a_out, p_out, dmixed, dproj], name="gate_bwd", grid=(nd, S // tm, 2),
        in_specs=[pl.BlockSpec((tm, tc), lambda j, i, br: (i, g0b + j)),
                  pl.BlockSpec((tm, tc), lambda j, i, br: (i, g0b + nd + j)),
                  pl.BlockSpec((2, tc), lambda j, i, br: (0, j)), tile, tile, tile, ANY],
        out_specs=[pl.BlockSpec((None, tm, tc), lambda j, i, br: (br, i, j)),
                   pl.BlockSpec((tm, tc), lambda j, i, br: (i, g0b + br * nd + j)),
                   pl.BlockSpec((2, 1, tc), lambda j, i, br: (0, 0, j))],
        out_shape=[jax.ShapeDtypeStruct((2, S, D), BF16), jax.ShapeDtypeStruct(dproj.shape, BF16),
                   jax.ShapeDtypeStruct((2, 1, D), F32)],
        aliases={6: 1}, sem=("parallel", "arbitrary", "arbitrary"))


def _ffn_in_swiglu(order, h, wT, FF):
    S, D = h.shape
    tm, tn = _pick(S, 1024), _pick(FF, 256)
    nj = FF // tn

    def body(h_ref, wg_ref, wu_ref, f_ref, act_ref):
        hb = h_ref[...]
        gt = lax.dot_general(hb, wg_ref[...], _DIMS["nt"], preferred_element_type=F32)
        up = lax.dot_general(hb, wu_ref[...], _DIMS["nt"], preferred_element_type=F32)
        f_ref[0] = gt
        f_ref[1] = up
        act_ref[...] = (gt * _sigmoid(gt) * up).astype(BF16)

    return _call(
        order, body, [h, wT, wT], name="ffn_in", grid=(S // tm, nj),
        in_specs=[pl.BlockSpec((tm, D), lambda i, j: (i, 0)), pl.BlockSpec((tn, D), lambda i, j: (j, 0)),
                  pl.BlockSpec((tn, D), lambda i, j: (nj + j, 0))],
        out_specs=[pl.BlockSpec((2, tm, tn), lambda i, j: (0, i, j)), pl.BlockSpec((tm, tn), lambda i, j: (i, j))],
        out_shape=[jax.ShapeDtypeStruct((2, S, FF), F32), jax.ShapeDtypeStruct((S, FF), BF16)],
        sem=("parallel", "parallel"))


def _d_act_swiglu(order, dy, wdown, f):
    S, D = dy.shape
    FF = wdown.shape[0]
    tm, tn = _pick(S, 1024), _pick(FF, 256)

    def body(dy_ref, w_ref, f_ref, o_ref):
        d = lax.dot_general(dy_ref[...], w_ref[...], _DIMS["nt"], preferred_element_type=F32)
        gt, up = f_ref[0], f_ref[1]
        s = _sigmoid(gt)
        o_ref[0] = (d * up * s * (1.0 + gt * (1.0 - s))).astype(BF16)
        o_ref[1] = (d * gt * s).astype(BF16)

    pair = pl.BlockSpec((2, tm, tn), lambda i, j: (0, i, j))
    return _call(
        order, body, [dy, wdown, f], name="d_act", grid=(S // tm, FF // tn),
        in_specs=[pl.BlockSpec((tm, D), lambda i, j: (i, 0)), pl.BlockSpec((tn, D), lambda i, j: (j, 0)), pair],
        out_specs=pair, out_shape=jax.ShapeDtypeStruct((2, S, FF), BF16), sem=("parallel", "parallel"))


def _place():
    return lax.axis_index("x"), lax.axis_index("y"), lax.axis_index("c")


def _xfer_start(order, name, bufs, copies):
    nb = len(bufs)
    n = len(copies([None] * nb, None))
    is_new = [isinstance(b, jax.ShapeDtypeStruct) for b in bufs]
    old = [b for b, fresh in zip(bufs, is_new) if not fresh]
    no = len(old)
    tok = [] if any(order.last is b for b in old) else [order.last]
    first_out = no + len(tok)

    def body(*refs):
        send, recv = refs[first_out:first_out + n], refs[first_out + n:first_out + 2 * n]
        token = refs[-1]
        given, made = iter(refs[:no]), iter(refs[first_out + 2 * n + no:-1])
        logical = [next(made) if fresh else next(given) for fresh in is_new]
        for i, (src, dst, dev) in enumerate(copies(logical, _place())):
            pltpu.make_async_remote_copy(src_ref=src, dst_ref=dst, send_sem=send[i], recv_sem=recv[i],
                                         device_id=dev, device_id_type=MESH).start()
        token[...] = jnp.zeros_like(token)

    fresh_shapes = [b for b, fresh in zip(bufs, is_new) if fresh]
    out = pl.pallas_call(
        body, name=name,
        out_shape=tuple([pltpu.SemaphoreType.DMA(())] * (2 * n)
                        + [pltpu.HBM(b.shape, b.dtype) for b in old + fresh_shapes]
                        + [jax.ShapeDtypeStruct((8, LANES), F32)]),
        in_specs=[HBM] * no + [ANY] * len(tok),
        out_specs=tuple([SEM] * (2 * n) + [HBM] * nb + [pl.BlockSpec(memory_space=pltpu.VMEM)]),
        input_output_aliases={i: 2 * n + i for i in range(no)},
        compiler_params=pltpu.CompilerParams(has_side_effects=EFFECT),
    )(*[pltpu.with_memory_space_constraint(b, pltpu.HBM) for b in old], *tok)
    order.last = out[-1]
    thru, made = iter(out[2 * n:2 * n + no]), iter(out[2 * n + no:2 * n + nb])
    return list(out[:2 * n]), [next(made) if fresh else next(thru) for fresh in is_new]


def _xfer_wait(order, name, sems, bufs, copies):
    nb = len(bufs)
    n = len(sems) // 2
    tok = order.last

    def body(*refs):
        send, recv = refs[nb:nb + n], refs[nb + n:nb + 2 * n]
        token = refs[-1]
        for i, (src, dst, dev) in enumerate(copies(refs[:nb], _place())):
            cp = pltpu.make_async_remote_copy(src_ref=src, dst_ref=dst, send_sem=send[i], recv_sem=recv[i],
                                              device_id=dev, device_id_type=MESH)
            cp.wait_send()
            cp.wait_recv()
        token[...] = jnp.zeros_like(token)

    out = pl.pallas_call(
        body, name=name,
        out_shape=tuple([pltpu.HBM(b.shape, b.dtype) for b in bufs] + [jax.ShapeDtypeStruct((8, LANES), F32)]),
        in_specs=[HBM] * nb + [SEM] * (2 * n) + [ANY],
        out_specs=tuple([HBM] * nb + [pl.BlockSpec(memory_space=pltpu.VMEM)]),
        input_output_aliases={i: i for i in range(nb)},
        compiler_params=pltpu.CompilerParams(has_side_effects=EFFECT),
    )(*bufs, *sems, tok)
    order.last = out[-1]
    return list(out[:nb])


class _Xfer:
    def __init__(self, name, bufs, copies):
        self.name, self.bufs, self.copies = name, list(bufs), copies
        self.sems = None

    def start(self, order):
        self.sems, self.bufs = _xfer_start(order, self.name + "_start", self.bufs, self.copies)

    def wait(self, order):
        self.bufs = _xfer_wait(order, self.name + "_wait", self.sems, self.bufs, self.copies)
        return self.bufs


def _block_rows(ref, r, d):
    return ref.at[pl.ds(d * r, r)]


def _gather_send(fulls):
    def copies(refs, place):
        out = []
        for w, full in enumerate(fulls):
            r = full.shape[0] // 8
            if place is None:
                out += [None] * 4
                continue
            x, y, c = place
            mine = _block_rows(refs[w], r, 4 * x + 2 * y + c)
            out.append((mine, mine, (x, y, 1 - c)))
            for px, py in ((1 - x, y), (x, 1 - y), (1 - x, 1 - y)):
                out.append((mine, mine, (px, py, c)))
        return out
    return copies


def _gather_forward(fulls):
    def copies(refs, place):
        out = []
        for w, full in enumerate(fulls):
            r = full.shape[0] // 8
            if place is None:
                out += [None] * 3
                continue
            x, y, c = place
            for px, py in ((1 - x, y), (x, 1 - y), (1 - x, 1 - y)):
                blk = _block_rows(refs[w], r, 4 * px + 2 * py + c)
                out.append((blk, blk, (x, y, 1 - c)))
        return out
    return copies


def _pair_send(nw):
    def copies(refs, place):
        out = []
        for w in range(nw):
            if place is None:
                out += [None] * 4
                continue
            x, y, c = place
            grad, other = refs[2 * w], refs[2 * w + 1]
            r = other.shape[1]
            for k in range(4):
                out.append((_block_rows(grad, r, 2 * k + 1 - c), other.at[k], (x, y, 1 - c)))
        return out
    return copies


def _chip_send(nw):
    def copies(refs, place):
        out = []
        for w in range(nw):
            if place is None:
                out += [None] * 3
                continue
            x, y, c = place
            psum, parts = refs[2 * w], refs[2 * w + 1]
            for px, py in ((1 - x, y), (x, 1 - y), (1 - x, 1 - y)):
                out.append((psum.at[2 * px + py], parts.at[2 * x + y], (px, py, c)))
        return out
    return copies


def _dev_index():
    x, y, c = _place()
    return 4 * x + 2 * y + c


def _place_own(order, shard, name):
    r, cols = shard.shape
    tr = _pick(r, max(16, (2 << 20) // (4 * cols)), 16)
    nr = r // tr

    def body(s_ref, o_ref):
        o_ref[...] = s_ref[...].astype(BF16)

    return _call(
        order, body, [shard], name=name, grid=(nr,),
        in_specs=[pl.BlockSpec((tr, cols), lambda i: (i, 0))],
        out_specs=pl.BlockSpec((tr, cols), lambda i: (_dev_index() * nr + i, 0)),
        out_shape=jax.ShapeDtypeStruct((8 * r, cols), BF16), sem=("parallel",))


def _pair_sum(order, grad, other, name):
    r, cols = other.shape[1:]
    tr = _pick(r, max(16, (4 << 20) // (2 * cols)), 16)
    nr = r // tr

    def body(g_ref, a_ref, o_ref):
        o_ref[...] = (g_ref[...].astype(F32) + a_ref[...].astype(F32)).astype(BF16)

    blk = pl.BlockSpec((None, tr, cols), lambda k, i: (k, i, 0))
    return _call(
        order, body, [grad, other], name=name, grid=(4, nr),
        in_specs=[pl.BlockSpec((tr, cols), lambda k, i: ((2 * k + lax.axis_index("c")) * nr + i, 0)), blk],
        out_specs=blk, out_shape=jax.ShapeDtypeStruct(other.shape, BF16), sem=("parallel", "parallel"))


def _chip_sum(order, psum, parts, name):
    _, r, cols = parts.shape
    tr = _pick(r, max(16, (1 << 20) // (2 * cols)), 16)

    def my_chip():
        return 2 * lax.axis_index("x") + lax.axis_index("y")

    def body(own_ref, p0, p1, p2, p3, o_ref):
        own = own_ref[...].astype(F32)
        acc = None
        for k, p in enumerate((p0, p1, p2, p3)):
            term = jnp.where(my_chip() == k, own, p[...].astype(F32))
            acc = term if acc is None else acc + term
        o_ref[...] = acc

    def slot(k):
        return pl.BlockSpec((None, tr, cols), lambda i: (jnp.where(my_chip() == k, (k + 1) % 4, k), i, 0))

    return _call(
        order, body, [psum, parts, parts, parts, parts], name=name, grid=(r // tr,),
        in_specs=[pl.BlockSpec((None, tr, cols), lambda i: (my_chip(), i, 0))] + [slot(k) for k in range(4)],
        out_specs=pl.BlockSpec((tr, cols), lambda i: (i, 0)),
        out_shape=jax.ShapeDtypeStruct((r, cols), F32), sem=("parallel",))


def _all_reduce_small(order, pack, name):
    R = pack.shape[0]

    def body(p_ref, o_ref, buf, send_sems, recv_sems):
        x, y, c = _place()
        me = 4 * x + 2 * y + c
        buf[me] = p_ref[...]
        copies = []
        for k in range(1, 8):
            px = 1 - x if k & 4 else x
            py = 1 - y if k & 2 else y
            pc = 1 - c if k & 1 else c
            cp = pltpu.make_async_remote_copy(
                src_ref=p_ref, dst_ref=buf.at[me], send_sem=send_sems.at[k - 1], recv_sem=recv_sems.at[k - 1],
                device_id=(px, py, pc), device_id_type=MESH)
            cp.start()
            copies.append(cp)
        for cp in copies:
            cp.wait_recv()
        acc = buf[0]
        for d in range(1, 8):
            acc = acc + buf[d]
        o_ref[...] = acc
        for cp in copies:
            cp.wait_send()

    vm = pl.BlockSpec(memory_space=pltpu.VMEM)
    return _call(
        order, body, [pack], name=name, in_specs=[vm], out_specs=vm,
        out_shape=jax.ShapeDtypeStruct((R, LANES), F32),
        scratch=[pltpu.VMEM((8, R, LANES), F32), pltpu.SemaphoreType.DMA((7,)), pltpu.SemaphoreType.DMA((7,))])


def _pack(parts):
    flat = []
    for p in parts:
        v = p.reshape(-1).astype(F32)
        flat.append(jnp.pad(v, (0, (-v.shape[0]) % LANES)))
    v = jnp.concatenate(flat)
    v = jnp.pad(v, (0, (-v.shape[0]) % (8 * LANES)))
    return v.reshape(-1, LANES)


def _unpack(pack, shapes):
    v = pack.reshape(-1)
    out, off = [], 0
    for s in shapes:
        n = 1
        for d in s:
            n *= d
        out.append(v[off:off + n].reshape(s))
        off += n + (-n) % LANES
    return out


def _adamw(order, w, g, m, v, name):
    shape = w.shape
    cols = shape[-1]
    w2, g2, m2, v2 = (t.reshape(-1, cols) for t in (w, g, m, v))
    R = w2.shape[0]
    tr = _pick(R, max(8, (1 << 20) // (4 * cols)), 8)

    def body(w_ref, g_ref, m_ref, v_ref, d_ref, mo_ref, vo_ref):
        d_ref[...], mo_ref[...], vo_ref[...] = _adam_math(w_ref[...], g_ref[...], m_ref[...], v_ref[...])

    blk = pl.BlockSpec((tr, cols), lambda i: (i, 0))
    outs = _call(
        order, body, [w2, g2, m2, v2], name=name, grid=(R // tr,), in_specs=[blk] * 4, out_specs=[blk] * 3,
        out_shape=[jax.ShapeDtypeStruct((R, cols), F32)] * 3, sem=("parallel",))
    return tuple(o.reshape(shape) for o in outs)


def _adam_math(w, g, m, v):
    mn = ADAM_B1 * m + (1.0 - ADAM_B1) * g
    vn = ADAM_B2 * v + (1.0 - ADAM_B2) * (g * g)
    m_hat = mn / (1.0 - ADAM_B1 ** ADAM_STEP)
    v_hat = vn / (1.0 - ADAM_B2 ** ADAM_STEP)
    return -ADAM_LR * (m_hat / (jnp.sqrt(v_hat) + ADAM_EPS) + ADAM_WD * w), mn, vn


def _chip_sum_adamw(order, w, psum, parts, m, v, name):
    _, r, cols = parts.shape
    tr = _pick(r, max(16, (6 << 20) // (38 * cols)), 16)

    def my_chip():
        return 2 * lax.axis_index("x") + lax.axis_index("y")

    def body(w_ref, own_ref, p0, p1, p2, p3, m_ref, v_ref, g_ref, d_ref, mo_ref, vo_ref):
        own = own_ref[...].astype(F32)
        g = None
        for k, p in enumerate((p0, p1, p2, p3)):
            term = jnp.where(my_chip() == k, own, p[...].astype(F32))
            g = term if g is None else g + term
        g_ref[...] = g
        d_ref[...], mo_ref[...], vo_ref[...] = _adam_math(w_ref[...], g, m_ref[...], v_ref[...])

    def slot(k):
        return pl.BlockSpec((None, tr, cols), lambda i: (jnp.where(my_chip() == k, (k + 1) % 4, k), i, 0))

    blk = pl.BlockSpec((tr, cols), lambda i: (i, 0))
    return _call(
        order, body, [w, psum, parts, parts, parts, parts, m, v], name=name, grid=(r // tr,),
        in_specs=[blk, pl.BlockSpec((None, tr, cols), lambda i: (my_chip(), i, 0))]
        + [slot(k) for k in range(4)] + [blk, blk],
        out_specs=[blk] * 4, out_shape=[jax.ShapeDtypeStruct((r, cols), F32)] * 4, sem=("parallel",))


class _GradReduce:
    def __init__(self, tag, grads, names):
        self.tag, self.grads, self.names = tag, list(grads), names
        self.pair = self.chip = self.psums = None

    def pair_start(self, order):
        bufs = []
        for g in self.grads:
            bufs += [g, jax.ShapeDtypeStruct((4, g.shape[0] // 8, g.shape[1]), g.dtype)]
        self.pair = _Xfer("pair_" + self.tag, bufs, _pair_send(len(self.grads)))
        self.pair.start(order)

    def pair_sum_chip_start(self, order):
        bufs = self.pair.wait(order)
        self.psums = [_pair_sum(order, bufs[2 * w], bufs[2 * w + 1], "pair_sum_" + nm)
                      for w, nm in enumerate(self.names)]
        cbufs = []
        for p in self.psums:
            cbufs += [p, jax.ShapeDtypeStruct(p.shape, p.dtype)]
        self.chip = _Xfer("chip_" + self.tag, cbufs, _chip_send(len(self.psums)))
        self.chip.start(order)

    def finish(self, order):
        bufs = self.chip.wait(order)
        return [(bufs[2 * w], bufs[2 * w + 1]) for w in range(len(self.names))]


def kernel(x, meta_tokens, ln_in_g, ln_in_b, w_in, b_gate, attn_sinks, w_attn_up, w_pool_grp, pool_scale, w_pool_up, w_out, ln1_g, ln1_b, w_ffn_in, w_ffn_down, ln2_g, ln2_b, loss_target, m_meta_tokens, m_ln_in_g, m_ln_in_b, m_w_in, m_b_gate, m_attn_sinks, m_w_attn_up, m_w_pool_grp, m_pool_scale, m_w_pool_up, m_w_out, m_ln1_g, m_ln1_b, m_w_ffn_in, m_w_ffn_down, m_ln2_g, m_ln2_b, v_meta_tokens, v_ln_in_g, v_ln_in_b, v_w_in, v_b_gate, v_attn_sinks, v_w_attn_up, v_w_pool_grp, v_pool_scale, v_w_pool_up, v_w_out, v_ln1_g, v_ln1_b, v_w_ffn_in, v_w_ffn_down, v_ln2_g, v_ln2_b):
    S, D = x.shape[1], x.shape[2]
    Tp = S + BLOCK
    NQ = attn_sinks.shape[-1]
    ATTN = NQ * HEAD_DIM
    KVW = ATTN // Q_PER_KV
    POOL = pool_scale.shape[-1]
    IN = 8 * w_in.shape[2]
    FF = 8 * w_ffn_down.shape[1]
    uoff = ATTN + 2 * KVW
    goff = uoff + POOL
    gw = POOL // 4
    dcols = D // 8
    assert IN == goff + 2 * D and w_ffn_in.shape[2] * 8 == 2 * FF

    xi, yi, ci = _place()
    dev = 4 * xi + 2 * yi + ci
    x2, tgt = x[0], loss_target[0]
    order = _Order()

    def place_cols(a):
        return lax.dynamic_update_slice(jnp.zeros(a.shape[:-1] + (D,), F32), a, (0,) * (a.ndim - 1) + (dev * dcols,))

    small = _all_reduce_small(order, _pack([place_cols(meta_tokens), place_cols(b_gate[0])]), "small_inputs_gather")
    meta_full, bgate_full = _unpack(small, [(N_META, D), (2, D)])
    meta_pad = jnp.pad(meta_full, ((META_ROW0, 0), (0, 0)))

    wgrp_rows = w_pool_grp[0].reshape(4 * (gw // 8), gw)
    full_in = _place_own(order, w_in[0].T, "own_w_in")
    ag_in = _Xfer("gather_w_in", [full_in], _gather_send([full_in]))
    ag_in.start(order)
    mix_names = ["w_attn_up", "w_pool_grp", "w_pool_up", "w_out"]
    mix_shards = [w_attn_up[0].T, wgrp_rows, w_pool_up[0].T, w_out[0]]
    full_mix = [_place_own(order, s, "own_" + nm) for s, nm in zip(mix_shards, mix_names)]
    full_ffn = _place_own(order, w_ffn_in[0].T, "own_w_ffn_in")

    ln_in_g2, ln_in_b2 = ln_in_g.reshape(1, D), ln_in_b.reshape(1, D)
    tab = _rope_table(S)

    h0, h0b = _ln_in_fwd(order, x2, meta_pad, ln_in_g2, ln_in_b2)
    (full_in,) = ag_in.wait(order)
    fw_in = _Xfer("forward_w_in", [full_in], _gather_forward([full_in]))
    fw_in.start(order)
    ag_mix = _Xfer("gather_mixers", full_mix, _gather_send(full_mix))
    ag_mix.start(order)
    ag_ffn = _Xfer("gather_w_ffn_in", [full_ffn], _gather_send([full_ffn]))
    ag_ffn.start(order)
    full_down = _place_own(order, w_ffn_down[0], "own_w_ffn_down")
    (winT,) = fw_in.wait(order)
    proj = _mm(order, h0b, winT, kind="nt", out_dtype=F32, tm=1408, tn=512, name="proj")

    full_mix = ag_mix.wait(order)
    ag_down = _Xfer("gather_w_ffn_down", [full_down], _gather_send([full_down]))
    ag_down.start(order)
    fw_mix = _Xfer("forward_mixers", full_mix, _gather_forward(full_mix))
    fw_mix.start(order)
    att = _attn_fwd(order, proj, tab, attn_sinks, S, ATTN, KVW)
    wattT, wgrp_g, wpupT, wout = fw_mix.wait(order)
    wgrp = wgrp_g.reshape(8, 4, gw // 8, gw).transpose(1, 0, 2, 3).reshape(4, gw, gw)

    ps = _pool_fwd(order, proj, wgrp, pool_scale, S, uoff, POOL)
    a_out = _mm(order, att, wattT, kind="nt", out_dtype=F32, tm=1024, tn=1024, name="attn_up")
    p_out = _mm(order, ps, wpupT, kind="nt", out_dtype=F32, tm=1024, tn=1024, name="pool_up")
    mixed = _gate_mix(order, proj, bgate_full, a_out, p_out, S, D, goff)
    y1 = _mm(order, mixed, wout, kind="nn", out_dtype=F32, tm=1024, tn=1024, name="out_proj")

    (full_ffn,) = ag_ffn.wait(order)
    fw_ffn = _Xfer("forward_w_ffn_in", [full_ffn], _gather_forward([full_ffn]))
    fw_ffn.start(order)
    h1, h1b = _ln1_fwd(order, h0, y1, ln1_g, ln1_b)
    (wffnT,) = fw_ffn.wait(order)
    f, act = _ffn_in_swiglu(order, h1b, wffnT, FF)

    (full_down,) = ag_down.wait(order)
    fw_down = _Xfer("forward_w_ffn_down", [full_down], _gather_forward([full_down]))
    fw_down.start(order)
    (wdown,) = fw_down.wait(order)
    y2 = _mm(order, act, wdown, kind="nn", out_dtype=F32, tm=512, tn=1024, tk=5504, name="ffn_down")

    dz2, dz2b, dg2, db2, loss_part = _ln2_loss_bwd(order, h1, y2, tgt, ln2_g, ln2_b)
    gwdown = _mm(order, act, dz2b, kind="tn", out_dtype=BF16, tm=256, tn=2048, name="d_ffn_down")
    rs_down = _GradReduce("w_ffn_down", [gwdown], ["w_ffn_down"])
    rs_down.pair_start(order)
    df = _d_act_swiglu(order, dz2b, wdown, f)
    rs_down.pair_sum_chip_start(order)
    gwffnT = _mm(order, df, h1b, kind="tn", out_dtype=BF16, tm=256, tn=2048, name="d_ffn_in", a_lead="halves")
    rs_ffn = _GradReduce("w_ffn_in", [gwffnT], ["w_ffn_in"])
    rs_ffn.pair_start(order)
    dh1 = _mm(order, df, wffnT, kind="nn", out_dtype=F32, tm=512, tn=1024, tk=5504, name="d_h1", a_lead="halves")
    rs_ffn.pair_sum_chip_start(order)
    dz1, dz1b, dg1, db1 = _ln1_bwd(order, h0, y1, ln1_g, dh1, dz2)
    gwout = _mm(order, mixed, dz1b, kind="tn", out_dtype=BF16, tm=512, tn=1024, name="d_out_proj")
    rs_out = _GradReduce("w_out", [gwout], ["w_out"])
    rs_out.pair_start(order)
    dmixed = _mm(order, dz1b, wout, kind="nt", out_dtype=F32, tm=1024, tn=1024, name="d_mixed")
    rs_out.pair_sum_chip_start(order)

    dproj = _zero_meta_block(order, Tp, IN)
    dap, dproj, dbgate = _gate_bwd(order, proj, bgate_full, a_out, p_out, dmixed, dproj, S, D, goff)
    gwattT = _mm(order, dap, att, kind="tn", out_dtype=BF16, tm=512, tn=1024, name="d_attn_up", a_lead=0)
    datt = _mm(order, dap, wattT, kind="nn", out_dtype=BF16, tm=1024, tn=1024, name="d_att", a_lead=0)
    gwpupT = _mm(order, dap, ps, kind="tn", out_dtype=BF16, tm=512, tn=1024, name="d_pool_up", a_lead=1)
    dps = _mm(order, dap, wpupT, kind="nn", out_dtype=F32, tm=1024, tn=1024, name="d_ps", a_lead=1)
    dpl, gwgrp, dscale = _pool_bwd_mix(order, proj, wgrp, pool_scale, dps, S, uoff, POOL)
    gwgrp_rows = gwgrp.reshape(4, 8, gw // 8, gw).transpose(1, 0, 2, 3).reshape(8 * 4 * (gw // 8), gw).astype(BF16)
    rs_mix = _GradReduce("mixers", [gwattT, gwgrp_rows, gwpupT], ["w_attn_up", "w_pool_grp", "w_pool_up"])
    rs_mix.pair_start(order)
    dproj = _pool_bwd_window(order, dpl, dproj, S, uoff, POOL)
    rs_mix.pair_sum_chip_start(order)
    dproj, dk, dv, dsink = _attn_bwd(order, proj, tab, attn_sinks, datt, dproj, S, ATTN, KVW)
    dproj = _put_dkv(order, dk, dv, dproj, ATTN)

    weights = dict(meta_tokens=meta_tokens, ln_in_g=ln_in_g, ln_in_b=ln_in_b, w_in=w_in, b_gate=b_gate,
                   attn_sinks=attn_sinks, w_attn_up=w_attn_up, w_pool_grp=w_pool_grp, pool_scale=pool_scale,
                   w_pool_up=w_pool_up, w_out=w_out, ln1_g=ln1_g, ln1_b=ln1_b, w_ffn_in=w_ffn_in,
                   w_ffn_down=w_ffn_down, ln2_g=ln2_g, ln2_b=ln2_b)
    ms = dict(meta_tokens=m_meta_tokens, ln_in_g=m_ln_in_g, ln_in_b=m_ln_in_b, w_in=m_w_in, b_gate=m_b_gate,
              attn_sinks=m_attn_sinks, w_attn_up=m_w_attn_up, w_pool_grp=m_w_pool_grp, pool_scale=m_pool_scale,
              w_pool_up=m_w_pool_up, w_out=m_w_out, ln1_g=m_ln1_g, ln1_b=m_ln1_b, w_ffn_in=m_w_ffn_in,
              w_ffn_down=m_w_ffn_down, ln2_g=m_ln2_g, ln2_b=m_ln2_b)
    vs = dict(meta_tokens=v_meta_tokens, ln_in_g=v_ln_in_g, ln_in_b=v_ln_in_b, w_in=v_w_in, b_gate=v_b_gate,
              attn_sinks=v_attn_sinks, w_attn_up=v_w_attn_up, w_pool_grp=v_w_pool_grp, pool_scale=v_pool_scale,
              w_pool_up=v_w_pool_up, w_out=v_w_out, ln1_g=v_ln1_g, ln1_b=v_ln1_b, w_ffn_in=v_w_ffn_in,
              w_ffn_down=v_w_ffn_down, ln2_g=v_ln2_g, ln2_b=v_ln2_b)
    grads, deltas, new_ms, new_vs = {}, {}, {}, {}

    def update(nm, g):
        g = g.reshape(weights[nm].shape)
        grads[nm] = g
        deltas[nm], new_ms[nm], new_vs[nm] = _adamw(order, weights[nm], g, ms[nm], vs[nm], "adamw_" + nm)

    def update_reduced(nm, bufs, transposed=False):
        psum, parts = bufs
        if transposed:
            to2d, back = (lambda t: t[0].T), (lambda t: t.T[None])
        else:
            to2d, back = (lambda t: t.reshape(parts.shape[1:])), (lambda t: t.reshape(weights[nm].shape))
        outs = _chip_sum_adamw(order, to2d(weights[nm]), psum, parts, to2d(ms[nm]), to2d(vs[nm]), "adamw_" + nm)
        grads[nm], deltas[nm], new_ms[nm], new_vs[nm] = (back(t) for t in outs)

    gwinT = _mm(order, dproj, h0b, kind="tn", out_dtype=BF16, tm=512, tn=1024, name="d_w_in")
    rs_in = _GradReduce("w_in", [gwinT], ["w_in"])
    rs_in.pair_start(order)
    update_reduced("w_ffn_down", rs_down.finish(order)[0])
    rs_in.pair_sum_chip_start(order)
    dh0 = _mm(order, dproj, winT, kind="nn", out_dtype=F32, tm=1408, tn=1024, tk=2560, name="d_h0")
    dxin, dg_in, db_in = _ln_in_bwd(order, x2, meta_pad, ln_in_g2, dh0, dz1)
    grad_x = dxin[:S][None]
    dmeta = dxin[S + META_ROW0:]

    small_shapes = [(D,), (D,), (1, D), (1, D), (1, D), (1, D), (1, POOL), (1, NQ), (), (N_META, D), (2, D)]
    red = _all_reduce_small(order, _pack([dg_in, db_in, dg1, db1, dg2, db2, dscale, dsink[:, :, 0], loss_part,
                                          dmeta, dbgate]), "small_grads_all_reduce")

    update_reduced("w_ffn_in", rs_ffn.finish(order)[0], transposed=True)
    update_reduced("w_out", rs_out.finish(order)[0])
    b_att, b_grp, b_pup = rs_mix.finish(order)
    update_reduced("w_attn_up", b_att, transposed=True)
    update_reduced("w_pool_grp", b_grp)
    update_reduced("w_pool_up", b_pup, transposed=True)

    (g_ln_in_g, g_ln_in_b, g_ln1_g, g_ln1_b, g_ln2_g, g_ln2_b, g_scale, g_sinks, loss_sum, g_meta_full,
     g_bgate_full) = _unpack(red, small_shapes)
    loss = 0.5 * loss_sum
    update("meta_tokens", lax.dynamic_slice(g_meta_full, (0, dev * dcols), (N_META, dcols)))
    update("b_gate", lax.dynamic_slice(g_bgate_full, (0, dev * dcols), (2, dcols)))
    for nm, g in (("ln_in_g", g_ln_in_g), ("ln_in_b", g_ln_in_b), ("ln1_g", g_ln1_g), ("ln1_b", g_ln1_b),
                  ("ln2_g", g_ln2_g), ("ln2_b", g_ln2_b), ("pool_scale", g_scale), ("attn_sinks", g_sinks)):
        update(nm, g)

    update_reduced("w_in", rs_in.finish(order)[0], transposed=True)

    names = list(weights)
    return (loss, grad_x, *[grads[n] for n in names], *[deltas[n] for n in names],
            *[new_ms[n] for n in names], *[new_vs[n] for n in names])
```

```python
import jax
import jax.numpy as jnp
from jax import lax
from jax.experimental import pallas as pl
from jax.experimental.pallas import tpu as pltpu

F32 = jnp.float32
BF16 = jnp.bfloat16
MESH = pl.DeviceIdType.MESH

N_META = 16
HEAD_DIM = 64
Q_PER_KV = 8
WINDOW = 128
BLOCK = 128
ATTN_SCALE = HEAD_DIM ** -0.5
ROPE_DIM = HEAD_DIM // 4
ROPE_THETA = 500000.0
NEG_INF = -1e30
POOL_WINDOWS = (2, 4, 8, 16)
LN_EPS = 1e-5
DN_ALPHA = 2.0 ** 0.25
ADAM_LR = 0.001
ADAM_B1 = 0.9
ADAM_B2 = 0.999
ADAM_EPS = 1e-08
ADAM_WD = 0.01
ADAM_STEP = 10

LANES = 128
META_ROW0 = BLOCK - N_META
VMEM_LIMIT = 56 * 1024 * 1024

ANY = pl.BlockSpec(memory_space=pl.ANY)
HBM = pl.BlockSpec(memory_space=pltpu.HBM)
SEM = pl.BlockSpec(memory_space=pltpu.SEMAPHORE)
EFFECT = pltpu.SideEffectType.DATAFLOW_SIDE_EFFECTING


def _params(sem=None, **kw):
    return pltpu.CompilerParams(dimension_semantics=sem, vmem_limit_bytes=VMEM_LIMIT, **kw)


class _Order:
    def __init__(self):
        self.last = None


def _call(order, body, operands, *, name, in_specs, out_specs, out_shape, grid=(), scratch=(), sem=None,
          aliases=None, prefetch=()):
    n_in, npf = len(operands), len(prefetch)
    tok = order.last
    if tok is not None and any(tok is op for op in operands):
        tok = None

    def wrapped(*refs):
        refs = list(refs)
        if tok is not None:
            del refs[npf + n_in]
        body(*refs)

    specs = list(in_specs) + ([ANY] if tok is not None else [])
    ops = list(operands) + ([tok] if tok is not None else [])
    if npf:
        out = pl.pallas_call(
            wrapped, name=name, out_shape=out_shape, compiler_params=_params(sem),
            grid_spec=pltpu.PrefetchScalarGridSpec(num_scalar_prefetch=npf, grid=grid, in_specs=specs,
                                                   out_specs=out_specs, scratch_shapes=list(scratch)),
        )(*prefetch, *ops)
    else:
        out = pl.pallas_call(
            wrapped, name=name, grid=grid, in_specs=specs, out_specs=out_specs, out_shape=out_shape,
            scratch_shapes=list(scratch), input_output_aliases=aliases or {}, compiler_params=_params(sem),
        )(*ops)
    order.last = out[0] if isinstance(out, (list, tuple)) else out
    return out


def _pick(dim, pref, mult=LANES):
    best = None
    t = mult
    while t <= min(dim, pref):
        if dim % t == 0:
            best = t
        t += mult
    return dim if best is None else best


_DIMS = {"nn": (((1,), (0,)), ((), ())), "nt": (((1,), (1,)), ((), ())), "tn": (((0,), (0,)), ((), ()))}


def _mm(order, a, b, *, kind, out_dtype, tm, tn, tk=None, name, a_lead=None):
    a2 = a.shape[-2:]
    halves = a_lead == "halves"
    if halves:
        a2 = (a2[0], 2 * a2[1])
    if kind == "tn":
        K, M = a2
    else:
        M, K = a2
    N = b.shape[0] if kind == "nt" else b.shape[1]
    half_cols = a2[1] // 2
    tm = _pick(half_cols if halves and kind == "tn" else M, tm)
    tn = _pick(N, tn)
    tk = K if tk is None else _pick(half_cols if halves and kind != "tn" else K, tk)
    nm, nn_, nk = M // tm, N // tn, K // tk
    a_bytes = M * K * a.dtype.itemsize
    b_bytes = N * K * b.dtype.itemsize
    i_outer = (a_bytes + nm * b_bytes <= b_bytes + nn_ * a_bytes) if nk == 1 else True

    def ij(g0, g1):
        return (g0, g1) if i_outer else (g1, g0)

    def a_map(g0, g1, k):
        i, _ = ij(g0, g1)
        if halves:
            per = half_cols // (tm if kind == "tn" else tk)
            return (i // per, k, i % per) if kind == "tn" else (k // per, i, k % per)
        idx = (k, i) if kind == "tn" else (i, k)
        return idx if a_lead is None else (a_lead,) + idx

    def b_map(g0, g1, k):
        _, j = ij(g0, g1)
        return (j, k) if kind == "nt" else (k, j)

    def o_map(g0, g1, k):
        return ij(g0, g1)

    a_blk = (tk, tm) if kind == "tn" else (tm, tk)
    if a_lead is not None:
        a_blk = (None,) + a_blk
    b_blk = (tn, tk) if kind == "nt" else (tk, tn)

    def body(a_ref, b_ref, o_ref, *acc):
        p = lax.dot_general(a_ref[...], b_ref[...], _DIMS[kind], preferred_element_type=F32)
        if nk == 1:
            o_ref[...] = p.astype(o_ref.dtype)
        else:
            k = pl.program_id(2)

            @pl.when(k == 0)
            def _():
                acc[0][...] = p

            @pl.when(k > 0)
            def _():
                acc[0][...] += p

            @pl.when(k == nk - 1)
            def _():
                o_ref[...] = acc[0][...].astype(o_ref.dtype)

    grid = (nm, nn_, nk) if i_outer else (nn_, nm, nk)
    return _call(
        order, body, [a, b], name=name, grid=grid,
        in_specs=[pl.BlockSpec(a_blk, a_map), pl.BlockSpec(b_blk, b_map)],
        out_specs=pl.BlockSpec((tm, tn), o_map),
        out_shape=jax.ShapeDtypeStruct((M, N), out_dtype),
        scratch=[] if nk == 1 else [pltpu.VMEM((tm, tn), F32)],
        sem=("parallel", "parallel", "arbitrary"))


def _ln_stats(z):
    mu = jnp.mean(z, axis=-1, keepdims=True)
    zc = z - mu
    var = jnp.mean(zc * zc, axis=-1, keepdims=True)
    rstd = lax.rsqrt(var + LN_EPS)
    return zc * rstd, rstd


def _ln_bwd(dy, xhat, rstd, g):
    dxh = dy * g
    m1 = jnp.mean(dxh, axis=-1, keepdims=True)
    m2 = jnp.mean(dxh * xhat, axis=-1, keepdims=True)
    return rstd * (dxh - m1 - xhat * m2)


def _ln_in_fwd(order, x, meta_pad, g, b):
    S, D = x.shape
    nb = S // BLOCK

    def body(x_ref, mp_ref, g_ref, b_ref, h_ref, hb_ref):
        is_meta = pl.program_id(0) == nb
        xin = jnp.where(is_meta, mp_ref[...], x_ref[...])
        xhat, _ = _ln_stats(xin)
        y = xhat * g_ref[...] + b_ref[...]
        h_ref[...] = y
        hb_ref[...] = y.astype(BF16)

    row = pl.BlockSpec((BLOCK, D), lambda i: (i, 0))
    vec = pl.BlockSpec((1, D), lambda i: (0, 0))
    return _call(
        order, body, [x, meta_pad, g, b], name="ln_in_fwd", grid=(nb + 1,),
        in_specs=[pl.BlockSpec((BLOCK, D), lambda i: (jnp.minimum(i, nb - 1), 0)),
                  pl.BlockSpec((BLOCK, D), lambda i: (0, 0)), vec, vec],
        out_specs=[row, row],
        out_shape=[jax.ShapeDtypeStruct((S + BLOCK, D), F32), jax.ShapeDtypeStruct((S + BLOCK, D), BF16)],
        sem=("parallel",))


def _ln_in_bwd(order, x, meta_pad, g, dh0, dz1):
    S, D = x.shape
    nb = S // BLOCK

    def body(x_ref, mp_ref, g_ref, dh_ref, dz_ref, dx_ref, dg_ref, db_ref):
        i = pl.program_id(0)
        is_meta = i == nb
        xin = jnp.where(is_meta, mp_ref[...], x_ref[...])
        xhat, rstd = _ln_stats(xin)
        dy = dh_ref[...] + jnp.where(is_meta, 0.0, DN_ALPHA) * dz_ref[...]
        dx_ref[...] = _ln_bwd(dy, xhat, rstd, g_ref[...])

        @pl.when(i == 0)
        def _():
            dg_ref[...] = jnp.zeros_like(dg_ref)
            db_ref[...] = jnp.zeros_like(db_ref)

        dg_ref[...] += jnp.sum(dy * xhat, axis=0, keepdims=True)
        db_ref[...] += jnp.sum(dy, axis=0, keepdims=True)

    row = pl.BlockSpec((BLOCK, D), lambda i: (i, 0))
    rowx = pl.BlockSpec((BLOCK, D), lambda i: (jnp.minimum(i, nb - 1), 0))
    vec = pl.BlockSpec((1, D), lambda i: (0, 0))
    return _call(
        order, body, [x, meta_pad, g, dh0, dz1], name="ln_in_bwd", grid=(nb + 1,),
        in_specs=[rowx, pl.BlockSpec((BLOCK, D), lambda i: (0, 0)), vec, row, rowx],
        out_specs=[row, vec, vec],
        out_shape=[jax.ShapeDtypeStruct((S + BLOCK, D), F32), jax.ShapeDtypeStruct((1, D), F32),
                   jax.ShapeDtypeStruct((1, D), F32)],
        sem=("arbitrary",))


def _ln1_fwd(order, h0, y1, g, b):
    S, D = y1.shape
    tm = _pick(S, BLOCK, 8)

    def body(h_ref, y_ref, g_ref, b_ref, o_ref, ob_ref):
        xhat, _ = _ln_stats(DN_ALPHA * h_ref[...] + y_ref[...])
        y = xhat * g_ref[...] + b_ref[...]
        o_ref[...] = y
        ob_ref[...] = y.astype(BF16)

    row = pl.BlockSpec((tm, D), lambda i: (i, 0))
    vec = pl.BlockSpec((1, D), lambda i: (0, 0))
    return _call(
        order, body, [h0, y1, g, b], name="ln1_fwd", grid=(S // tm,), in_specs=[row, row, vec, vec],
        out_specs=[row, row],
        out_shape=[jax.ShapeDtypeStruct((S, D), F32), jax.ShapeDtypeStruct((S, D), BF16)],
        sem=("parallel",))


def _ln1_bwd(order, h0, y1, g, dh1, dz2):
    S, D = y1.shape
    tm = _pick(S, BLOCK, 8)

    def body(h_ref, y_ref, g_ref, dh_ref, dz2_ref, dz_ref, dzb_ref, dg_ref, db_ref):
        i = pl.program_id(0)
        xhat, rstd = _ln_stats(DN_ALPHA * h_ref[...] + y_ref[...])
        dy = dh_ref[...] + DN_ALPHA * dz2_ref[...]
        dz = _ln_bwd(dy, xhat, rstd, g_ref[...])
        dz_ref[...] = dz
        dzb_ref[...] = dz.astype(BF16)

        @pl.when(i == 0)
        def _():
            dg_ref[...] = jnp.zeros_like(dg_ref)
            db_ref[...] = jnp.zeros_like(db_ref)

        dg_ref[...] += jnp.sum(dy * xhat, axis=0, keepdims=True)
        db_ref[...] += jnp.sum(dy, axis=0, keepdims=True)

    row = pl.BlockSpec((tm, D), lambda i: (i, 0))
    vec = pl.BlockSpec((1, D), lambda i: (0, 0))
    return _call(
        order, body, [h0, y1, g, dh1, dz2], name="ln1_bwd", grid=(S // tm,),
        in_specs=[row, row, vec, row, row], out_specs=[row, row, vec, vec],
        out_shape=[jax.ShapeDtypeStruct((S, D), F32), jax.ShapeDtypeStruct((S, D), BF16),
                   jax.ShapeDtypeStruct((1, D), F32), jax.ShapeDtypeStruct((1, D), F32)],
        sem=("arbitrary",))


def _ln2_loss_bwd(order, h1, y2, target, g, b):
    S, D = y2.shape
    tm = _pick(S, BLOCK, 8)

    def body(h_ref, y_ref, t_ref, g_ref, b_ref, dz_ref, dzb_ref, dg_ref, db_ref, loss_ref):
        i = pl.program_id(0)
        xhat, rstd = _ln_stats(DN_ALPHA * h_ref[...] + y_ref[...])
        diff = xhat * g_ref[...] + b_ref[...] - t_ref[...]
        dy = diff / D
        dz = _ln_bwd(dy, xhat, rstd, g_ref[...])
        dz_ref[...] = dz
        dzb_ref[...] = dz.astype(BF16)

        @pl.when(i == 0)
        def _():
            dg_ref[...] = jnp.zeros_like(dg_ref)
            db_ref[...] = jnp.zeros_like(db_ref)
            loss_ref[...] = jnp.zeros_like(loss_ref)

        dg_ref[...] += jnp.sum(dy * xhat, axis=0, keepdims=True)
        db_ref[...] += jnp.sum(dy, axis=0, keepdims=True)
        loss_ref[...] += jnp.sum(jnp.mean(diff * diff, axis=-1, keepdims=True), axis=0, keepdims=True)

    row = pl.BlockSpec((tm, D), lambda i: (i, 0))
    vec = pl.BlockSpec((1, D), lambda i: (0, 0))
    one = pl.BlockSpec((1, 1), lambda i: (0, 0))
    return _call(
        order, body, [h1, y2, target, g, b], name="ln2_loss_bwd", grid=(S // tm,),
        in_specs=[row, row, row, vec, vec], out_specs=[row, row, vec, vec, one],
        out_shape=[jax.ShapeDtypeStruct((S, D), F32), jax.ShapeDtypeStruct((S, D), BF16),
                   jax.ShapeDtypeStruct((1, D), F32), jax.ShapeDtypeStruct((1, D), F32),
                   jax.ShapeDtypeStruct((1, 1), F32)],
        sem=("arbitrary",))


def _rope_table(S):
    r = jnp.arange(S + BLOCK)
    pos = jnp.where(r < S, r + N_META, jnp.maximum(r - (S + META_ROW0), 0))
    half = ROPE_DIM // 2
    lane = jnp.arange(LANES) % HEAD_DIM
    inv_freq = ROPE_THETA ** (-(lane % half).astype(F32) * 2.0 / ROPE_DIM)
    ang = pos.astype(F32)[:, None] * inv_freq[None, :]
    cos, sin = jnp.cos(ang), jnp.sin(ang)
    c = jnp.where(lane < ROPE_DIM, cos, 1.0)
    sa = jnp.where(lane < half, -sin, 0.0)
    sb = jnp.where((lane >= half) & (lane < ROPE_DIM), sin, 0.0)
    return jnp.concatenate([c, sa, sb], axis=1).astype(F32)


def _rope(x, tab):
    h = ROPE_DIM // 2
    return (x * tab[:, :LANES] + pltpu.roll(x, LANES - h, 1) * tab[:, LANES:2 * LANES]
            + pltpu.roll(x, h, 1) * tab[:, 2 * LANES:])


def _rope_t(dy, tab):
    h = ROPE_DIM // 2
    return (dy * tab[:, :LANES] + pltpu.roll(dy * tab[:, LANES:2 * LANES], h, 1)
            + pltpu.roll(dy * tab[:, 2 * LANES:], LANES - h, 1))


def _attn_tiles(g, n, S, sink_ref, q_ref, k_ref, v_ref, tab_ref):
    NQG = Q_PER_KV // 2
    R = NQG * BLOCK
    halfsel = (g % 2).astype(F32)
    prev = jnp.maximum(n - 1, 0)
    qrow = pl.ds(pl.multiple_of(n * BLOCK, BLOCK), BLOCK)
    prow = pl.ds(pl.multiple_of(prev * BLOCK, BLOCK), BLOCK)
    mrow = pl.ds(S, BLOCK)

    tq = tab_ref[qrow, :]
    qf = q_ref[...]
    q4 = jnp.concatenate([_rope(qf[:, LANES * p:LANES * (p + 1)], tq) for p in range(NQG)], axis=0).astype(BF16)

    tk = jnp.concatenate([tab_ref[mrow, :], tab_ref[prow, :], tq], axis=0)
    kr = _rope(jnp.concatenate([k_ref[mrow, :], k_ref[prow, :], k_ref[qrow, :]], axis=0), tk)
    vr = jnp.concatenate([v_ref[mrow, :], v_ref[prow, :], v_ref[qrow, :]], axis=0)

    lane = lax.broadcasted_iota(jnp.int32, kr.shape, 1)
    own = jnp.where(lane < HEAD_DIM, 1.0 - halfsel, halfsel)

    def lo_hi(t):
        mine = t * own
        other = pltpu.roll(mine, HEAD_DIM, 1)
        lo = mine * (1.0 - halfsel) + other * halfsel
        hi = other * (1.0 - halfsel) + mine * halfsel
        return lo.astype(BF16), hi.astype(BF16)

    klo, khi = lo_hi(kr)
    vlo, vhi = lo_hi(vr)

    row = lax.broadcasted_iota(jnp.int32, (R, 3 * BLOCK), 0) & (BLOCK - 1)
    col = lax.broadcasted_iota(jnp.int32, (R, 3 * BLOCK), 1)
    jj = col & (BLOCK - 1)
    no_prev = jnp.where(n >= 1, 0, 2 * BLOCK)
    mask = (((col < BLOCK) & (col >= META_ROW0))
            | ((col >= BLOCK) & (col < 2 * BLOCK) & (jj > row + no_prev))
            | ((col >= 2 * BLOCK) & (jj <= row)))

    def soft(kk, parity):
        sk = jnp.concatenate(
            [jnp.full((BLOCK, 1), sink_ref[0, Q_PER_KV * g + 2 * p + parity], F32) for p in range(NQG)], axis=0)
        s = lax.dot_general(q4, kk, _DIMS["nt"], preferred_element_type=F32) * ATTN_SCALE
        s = jnp.where(mask, s, NEG_INF)
        m = jnp.maximum(jnp.max(s, axis=1, keepdims=True), sk)
        p = jnp.exp(s - m)
        es = jnp.exp(sk - m)
        inv = 1.0 / (jnp.sum(p, axis=1, keepdims=True) + es)
        return p * inv, es * inv

    pe, sink_e = soft(klo, 0)
    po, sink_o = soft(khi, 1)
    return q4, tk, (klo, khi), (vlo, vhi), (pe, po), (sink_e, sink_o), own


def _attn_specs(S, ATTN, KVW):
    Tp = S + BLOCK
    koff, voff = ATTN // LANES, (ATTN + KVW) // LANES
    gw = Q_PER_KV * HEAD_DIM
    return [pl.BlockSpec(memory_space=pltpu.SMEM),
            pl.BlockSpec((BLOCK, gw), lambda g, n: (n, g)),
            pl.BlockSpec((Tp, LANES), lambda g, n: (0, koff + g // 2)),
            pl.BlockSpec((Tp, LANES), lambda g, n: (0, voff + g // 2)),
            pl.BlockSpec((Tp, 3 * LANES), lambda g, n: (0, 0))]


def _attn_fwd(order, proj, tab, sinks, S, ATTN, KVW):
    G = KVW // HEAD_DIM
    nb = S // BLOCK
    gw = Q_PER_KV * HEAD_DIM

    def body(sink_ref, q_ref, k_ref, v_ref, tab_ref, o_ref):
        g, n = pl.program_id(0), pl.program_id(1)
        _, _, _, (vlo, vhi), (pe, po), _, _ = _attn_tiles(g, n, S, sink_ref, q_ref, k_ref, v_ref, tab_ref)
        o4 = (jnp.dot(pe.astype(BF16), vlo, preferred_element_type=F32)
              + jnp.dot(po.astype(BF16), vhi, preferred_element_type=F32))
        o_ref[...] = jnp.concatenate(
            [o4[BLOCK * p:BLOCK * (p + 1)] for p in range(Q_PER_KV // 2)], axis=1).astype(BF16)

    return _call(
        order, body, [sinks, proj, proj, proj, tab], name="attn_fwd", grid=(G, nb),
        in_specs=_attn_specs(S, ATTN, KVW),
        out_specs=pl.BlockSpec((BLOCK, gw), lambda g, n: (n, g)),
        out_shape=jax.ShapeDtypeStruct((S, ATTN), BF16),
        sem=("parallel", "arbitrary"))


def _attn_bwd(order, proj, tab, sinks, da, dproj, S, ATTN, KVW):
    G = KVW // HEAD_DIM
    nb = S // BLOCK
    Tp = S + BLOCK
    NQG = Q_PER_KV // 2
    gw = Q_PER_KV * HEAD_DIM

    def body(sink_ref, q_ref, k_ref, v_ref, tab_ref, da_ref, dproj_in, dq_ref, dk_ref, dv_ref, ds_ref):
        del dproj_in
        g, n = pl.program_id(0), pl.program_id(1)
        q4, tk, (klo, khi), (vlo, vhi), (pe, po), (sink_e, sink_o), own = _attn_tiles(
            g, n, S, sink_ref, q_ref, k_ref, v_ref, tab_ref)
        dof = da_ref[...]
        do4 = jnp.concatenate([dof[:, LANES * p:LANES * (p + 1)] for p in range(NQG)], axis=0)

        def grads(p, vv):
            dp = lax.dot_general(do4, vv, _DIMS["nt"], preferred_element_type=F32)
            delta = jnp.sum(p * dp, axis=1, keepdims=True)
            return (p * (dp - delta) * ATTN_SCALE).astype(BF16), delta

        dse, delta_e = grads(pe, vlo)
        dso, delta_o = grads(po, vhi)

        dq4 = (jnp.dot(dse, klo, preferred_element_type=F32) + jnp.dot(dso, khi, preferred_element_type=F32))
        tq = tk[2 * BLOCK:]
        dq_ref[...] = jnp.concatenate(
            [_rope_t(dq4[BLOCK * p:BLOCK * (p + 1)], tq) for p in range(NQG)], axis=1).astype(BF16)

        lane = lax.broadcasted_iota(jnp.int32, (3 * BLOCK, LANES), 1)

        def fold(lo_part, hi_part):
            t = jnp.where(lane < HEAD_DIM, lo_part, hi_part)
            return t + pltpu.roll(t, HEAD_DIM, 1)

        dk = _rope_t(fold(lax.dot_general(dse, q4, _DIMS["tn"], preferred_element_type=F32),
                          lax.dot_general(dso, q4, _DIMS["tn"], preferred_element_type=F32)), tk) * own
        dv = fold(lax.dot_general(pe.astype(BF16), do4, _DIMS["tn"], preferred_element_type=F32),
                  lax.dot_general(po.astype(BF16), do4, _DIMS["tn"], preferred_element_type=F32)) * own

        @pl.when((n == 0) & (g % 2 == 0))
        def _():
            dk_ref[...] = jnp.zeros_like(dk_ref)
            dv_ref[...] = jnp.zeros_like(dv_ref)

        @pl.when(n == 0)
        def _():
            ds_ref[...] = jnp.zeros_like(ds_ref)

        prev = jnp.maximum(n - 1, 0)
        qrow = pl.ds(pl.multiple_of(n * BLOCK, BLOCK), BLOCK)
        prow = pl.ds(pl.multiple_of(prev * BLOCK, BLOCK), BLOCK)
        mrow = pl.ds(S, BLOCK)
        for ref, val in ((dk_ref, dk), (dv_ref, dv)):
            ref[mrow, :] += val[:BLOCK]
            ref[prow, :] += val[BLOCK:2 * BLOCK]
            ref[qrow, :] += val[2 * BLOCK:]

        srow = lax.broadcasted_iota(jnp.int32, (Q_PER_KV, LANES), 0)
        acc = jnp.zeros((Q_PER_KV, LANES), F32)
        for p in range(NQG):
            for parity, (sk, dl) in enumerate(((sink_e, delta_e), (sink_o, delta_o))):
                val = -jnp.sum(sk[BLOCK * p:BLOCK * (p + 1)] * dl[BLOCK * p:BLOCK * (p + 1)])
                acc = jnp.where(srow == 2 * p + parity, val, acc)
        ds_ref[0] += acc

    in_specs = _attn_specs(S, ATTN, KVW) + [pl.BlockSpec((BLOCK, gw), lambda g, n: (n, g)), ANY]
    slab = pl.BlockSpec((Tp, LANES), lambda g, n: (0, g // 2))
    return _call(
        order, body, [sinks, proj, proj, proj, tab, da, dproj], name="attn_bwd", grid=(G, nb), in_specs=in_specs,
        out_specs=[pl.BlockSpec((BLOCK, gw), lambda g, n: (n, g)), slab, slab,
                   pl.BlockSpec((1, Q_PER_KV, LANES), lambda g, n: (g, 0, 0))],
        out_shape=[jax.ShapeDtypeStruct(dproj.shape, BF16), jax.ShapeDtypeStruct((Tp, KVW), F32),
                   jax.ShapeDtypeStruct((Tp, KVW), F32), jax.ShapeDtypeStruct((G, Q_PER_KV, LANES), F32)],
        aliases={6: 0}, sem=("arbitrary", "arbitrary"))


def _zero_meta_block(order, Tp, IN):
    tc = _pick(IN, 4096)

    def body(o_ref):
        o_ref[...] = jnp.zeros_like(o_ref)

    return _call(
        order, body, [], name="dproj_zero_meta", grid=(IN // tc,), in_specs=[],
        out_specs=pl.BlockSpec((BLOCK, tc), lambda j: (Tp // BLOCK - 1, j)),
        out_shape=jax.ShapeDtypeStruct((Tp, IN), BF16), sem=("parallel",))


def _put_dkv(order, dk, dv, dproj, ATTN):
    Tp, KVW = dk.shape
    nkb = KVW // LANES
    koff = ATTN // LANES

    def body(dk_ref, dv_ref, dproj_in, o_ref):
        del dproj_in
        t = pl.program_id(0)
        o_ref[...] = jnp.where(t < nkb, dk_ref[...], dv_ref[...]).astype(BF16)

    src = pl.BlockSpec((Tp, LANES), lambda t: (0, t % nkb))
    return _call(
        order, body, [dk, dv, dproj], name="dproj_put_dkv", grid=(2 * nkb,), in_specs=[src, src, ANY],
        out_specs=pl.BlockSpec((Tp, LANES), lambda t: (0, koff + t)),
        out_shape=jax.ShapeDtypeStruct(dproj.shape, BF16), aliases={2: 0}, sem=("parallel",))


HALO = 16


def _window_sums(x, up):
    n = x.shape[0]
    out = []
    s = x
    for k in (1, 2, 4, 8):
        s = s + pltpu.roll(s, (n - k) if up else k, 0)
        out.append(s)
    return out


def _pool_specs(S, ub, gw, tm):
    meta_halo = (S + BLOCK - HALO) // HALO

    def main(g):
        return pl.BlockSpec((tm, gw), lambda i: (i, ub + g))

    def halo(g):
        return pl.BlockSpec((HALO, gw), lambda i: (jnp.where(i == 0, meta_halo, i * (tm // HALO) - 1), ub + g))

    return [main(g) for g in range(4)] + [halo(g) for g in range(4)]


def _pooled(main_refs, halo_refs, g):
    x = jnp.concatenate([halo_refs[g][...], main_refs[g][...]], axis=0)
    s = _window_sums(x, up=False)[g]
    return (s[HALO:] * (1.0 / POOL_WINDOWS[g]) - x[HALO:]).astype(BF16)


def _pool_fwd(order, proj, wgrp, scale, S, uoff, POOL):
    gw = POOL // 4
    tm = BLOCK

    def body(*refs):
        main, halo = refs[:4], refs[4:8]
        w_ref, sc_ref, o_ref = refs[8:]
        for g in range(4):
            mixed = jnp.dot(_pooled(main, halo, g), w_ref[g], preferred_element_type=F32)
            o_ref[:, gw * g:gw * (g + 1)] = (mixed * sc_ref[:, gw * g:gw * (g + 1)]).astype(BF16)

    return _call(
        order, body, [proj] * 8 + [wgrp, scale], name="pool_fwd", grid=(S // tm,),
        in_specs=_pool_specs(S, uoff // gw, gw, tm) + [
            pl.BlockSpec((4, gw, gw), lambda i: (0, 0, 0)), pl.BlockSpec((1, POOL), lambda i: (0, 0))],
        out_specs=pl.BlockSpec((tm, POOL), lambda i: (i, 0)),
        out_shape=jax.ShapeDtypeStruct((S, POOL), BF16), sem=("parallel",))


def _pool_bwd_mix(order, proj, wgrp, scale, dps, S, uoff, POOL):
    gw = POOL // 4
    tm = BLOCK

    def body(*refs):
        main, halo = refs[:4], refs[4:8]
        w_ref, sc_ref, dps_ref, dpl_ref, dw_ref, dsc_ref = refs[8:]
        i = pl.program_id(0)

        @pl.when(i == 0)
        def _():
            dw_ref[...] = jnp.zeros_like(dw_ref)
            dsc_ref[...] = jnp.zeros_like(dsc_ref)

        for g in range(4):
            cols = slice(gw * g, gw * (g + 1))
            pooled = _pooled(main, halo, g)
            mixed = jnp.dot(pooled, w_ref[g], preferred_element_type=F32)
            dps_g = dps_ref[:, cols]
            dsc_ref[:, cols] += jnp.sum(dps_g * mixed, axis=0, keepdims=True)
            dms = (dps_g * sc_ref[:, cols]).astype(BF16)
            dw_ref[g] += lax.dot_general(pooled, dms, _DIMS["tn"], preferred_element_type=F32)
            dpl_ref[:, cols] = lax.dot_general(dms, w_ref[g], _DIMS["nt"], preferred_element_type=F32)

    row = pl.BlockSpec((tm, POOL), lambda i: (i, 0))
    return _call(
        order, body, [proj] * 8 + [wgrp, scale, dps], name="pool_bwd_mix", grid=(S // tm,),
        in_specs=_pool_specs(S, uoff // gw, gw, tm) + [
            pl.BlockSpec((4, gw, gw), lambda i: (0, 0, 0)), pl.BlockSpec((1, POOL), lambda i: (0, 0)), row],
        out_specs=[row, pl.BlockSpec((4, gw, gw), lambda i: (0, 0, 0)), pl.BlockSpec((1, POOL), lambda i: (0, 0))],
        out_shape=[jax.ShapeDtypeStruct((S, POOL), F32), jax.ShapeDtypeStruct((4, gw, gw), F32),
                   jax.ShapeDtypeStruct((1, POOL), F32)],
        sem=("arbitrary",))


def _pool_bwd_window(order, dpl, dproj, S, uoff, POOL):
    gw = POOL // 4
    nb = S // BLOCK
    ub = uoff // gw

    def body(main_ref, halo_ref, dproj_in, o_ref):
        del dproj_in
        b, g = pl.program_id(0), pl.program_id(1)
        main = jnp.where(b < nb, main_ref[...], 0.0)
        halo = jnp.where(b == nb - 1, 0.0, halo_ref[...])
        sums = _window_sums(jnp.concatenate([main, halo], axis=0), up=True)
        du = jnp.zeros((BLOCK, gw), F32)
        for k, w in enumerate(POOL_WINDOWS):
            du = jnp.where(g == k, sums[k][:BLOCK] * (1.0 / w), du)
        du = du - main
        row = lax.broadcasted_iota(jnp.int32, du.shape, 0)
        first_valid = jnp.where(b == nb, META_ROW0, 0)
        o_ref[...] = jnp.where(row >= first_valid, du, 0.0).astype(BF16)

    return _call(
        order, body, [dpl, dpl, dproj], name="pool_bwd_window", grid=(nb + 1, 4),
        in_specs=[pl.BlockSpec((BLOCK, gw), lambda b, g: (jnp.minimum(b, nb - 1), g)),
                  pl.BlockSpec((HALO, gw), lambda b, g: (
                      jnp.where(b == nb, 0, jnp.minimum((b + 1) * (BLOCK // HALO), S // HALO - 1)), g)),
                  ANY],
        out_specs=pl.BlockSpec((BLOCK, gw), lambda b, g: (b, ub + g)),
        out_shape=jax.ShapeDtypeStruct(dproj.shape, BF16), aliases={2: 0}, sem=("parallel", "parallel"))


def _sigmoid(x):
    return 1.0 / (1.0 + jnp.exp(-x))


def _gate_tiles(S, D, goff):
    tc = 512
    while goff % tc or D % tc:
        tc //= 2
    return _pick(S, 512, 8), tc


def _gate_mix(order, proj, bgate, a_out, p_out, S, D, goff):
    tm, tc = _gate_tiles(S, D, goff)
    g0b, nd = goff // tc, D // tc

    def body(l0_ref, l1_ref, b_ref, a_ref, p_ref, o_ref):
        g0 = _sigmoid(l0_ref[...] + b_ref[0:1, :])
        g1 = _sigmoid(l1_ref[...] + b_ref[1:2, :])
        o_ref[...] = (g0 * a_ref[...] + g1 * p_ref[...]).astype(BF16)

    tile = pl.BlockSpec((tm, tc), lambda i, j: (i, j))
    return _call(
        order, body, [proj, proj, bgate, a_out, p_out], name="gate_mix", grid=(S // tm, nd),
        in_specs=[pl.BlockSpec((tm, tc), lambda i, j: (i, g0b + j)),
                  pl.BlockSpec((tm, tc), lambda i, j: (i, g0b + nd + j)),
                  pl.BlockSpec((2, tc), lambda i, j: (0, j)), tile, tile],
        out_specs=tile, out_shape=jax.ShapeDtypeStruct((S, D), BF16), sem=("parallel", "parallel"))


def _gate_bwd(order, proj, bgate, a_out, p_out, dmixed, dproj, S, D, goff):
    tm, tc = _gate_tiles(S, D, goff)
    g0b, nd, ni = goff // tc, D // tc, S // tm
    nsteps = nd * ni

    def body(l0_ref, l1_ref, b_ref, a_ref, p_ref, dm_ref, dproj_in, dap_ref, dproj_ref, db_ref, buf, sems):
        del dproj_in
        j, i = pl.program_id(0), pl.program_id(1)
        step = j * ni + i
        slot = step % 2

        def put(sl, br):
            col = pl.multiple_of((g0b + br * nd + j) * tc, tc)
            return pltpu.make_async_copy(
                buf.at[sl, br], dproj_ref.at[pl.ds(pl.multiple_of(i * tm, tm), tm), pl.ds(col, tc)], sems.at[sl, br])

        @pl.when(step >= 2)
        def _():
            put(slot, 0).wait()
            put(slot, 1).wait()

        @pl.when(i == 0)
        def _():
            db_ref[...] = jnp.zeros_like(db_ref)

        dm = dm_ref[...]
        for br, (l_ref, val_ref) in enumerate(((l0_ref, a_ref), (l1_ref, p_ref))):
            gate = _sigmoid(l_ref[...] + b_ref[br:br + 1, :])
            dap_ref[br] = (dm * gate).astype(BF16)
            dl = dm * val_ref[...] * gate * (1.0 - gate)
            buf[slot, br] = dl.astype(BF16)
            db_ref[br] += jnp.sum(dl, axis=0, keepdims=True)
            put(slot, br).start()

        @pl.when(step == nsteps - 1)
        def _():
            for sl in ((slot, 1 - slot) if nsteps > 1 else (slot,)):
                put(sl, 0).wait()
                put(sl, 1).wait()

    tile = pl.BlockSpec((tm, tc), lambda j, i: (i, j))
    return _call(
        order, body, [proj, proj, bgate, a_out, p_out, dmixed, dproj], name="gate_bwd", grid=(nd, ni),
        in_specs=[pl.BlockSpec((tm, tc), lambda j, i: (i, g0b + j)),
                  pl.BlockSpec((tm, tc), lambda j, i: (i, g0b + nd + j)),
                  pl.BlockSpec((2, tc), lambda j, i: (0, j)), tile, tile, tile, ANY],
        out_specs=[pl.BlockSpec((2, tm, tc), lambda j, i: (0, i, j)), ANY,
                   pl.BlockSpec((2, 1, tc), lambda j, i: (0, 0, j))],
        out_shape=[jax.ShapeDtypeStruct((2, S, D), BF16), jax.ShapeDtypeStruct(dproj.shape, BF16),
                   jax.ShapeDtypeStruct((2, 1, D), F32)],
        scratch=[pltpu.VMEM((2, 2, tm, tc), BF16), pltpu.SemaphoreType.DMA((2, 2))],
        aliases={6: 1}, sem=("arbitrary", "arbitrary"))


def _ffn_in_swiglu(order, h, wT, FF):
    S, D = h.shape
    tm, tn = _pick(S, 1024), _pick(FF, 256)
    nj = FF // tn

    def body(h_ref, wg_ref, wu_ref, f_ref, act_ref):
        hb = h_ref[...]
        gt = lax.dot_general(hb, wg_ref[...], _DIMS["nt"], preferred_element_type=F32)
        up = lax.dot_general(hb, wu_ref[...], _DIMS["nt"], preferred_element_type=F32)
        f_ref[0] = gt
        f_ref[1] = up
        act_ref[...] = (gt * _sigmoid(gt) * up).astype(BF16)

    return _call(
        order, body, [h, wT, wT], name="ffn_in", grid=(S // tm, nj),
        in_specs=[pl.BlockSpec((tm, D), lambda i, j: (i, 0)), pl.BlockSpec((tn, D), lambda i, j: (j, 0)),
                  pl.BlockSpec((tn, D), lambda i, j: (nj + j, 0))],
        out_specs=[pl.BlockSpec((2, tm, tn), lambda i, j: (0, i, j)), pl.BlockSpec((tm, tn), lambda i, j: (i, j))],
        out_shape=[jax.ShapeDtypeStruct((2, S, FF), F32), jax.ShapeDtypeStruct((S, FF), BF16)],
        sem=("parallel", "parallel"))


def _d_act_swiglu(order, dy, wdown, f):
    S, D = dy.shape
    FF = wdown.shape[0]
    tm, tn = _pick(S, 1024), _pick(FF, 256)

    def body(dy_ref, w_ref, f_ref, o_ref):
        d = lax.dot_general(dy_ref[...], w_ref[...], _DIMS["nt"], preferred_element_type=F32)
        gt, up = f_ref[0], f_ref[1]
        s = _sigmoid(gt)
        o_ref[0] = (d * up * s * (1.0 + gt * (1.0 - s))).astype(BF16)
        o_ref[1] = (d * gt * s).astype(BF16)

    pair = pl.BlockSpec((2, tm, tn), lambda i, j: (0, i, j))
    return _call(
        order, body, [dy, wdown, f], name="d_act", grid=(S // tm, FF // tn),
        in_specs=[pl.BlockSpec((tm, D), lambda i, j: (i, 0)), pl.BlockSpec((tn, D), lambda i, j: (j, 0)), pair],
        out_specs=pair, out_shape=jax.ShapeDtypeStruct((2, S, FF), BF16), sem=("parallel", "parallel"))


def _place():
    return lax.axis_index("x"), lax.axis_index("y"), lax.axis_index("c")


def _xfer_start(order, name, bufs, copies):
    nb = len(bufs)
    n = len(copies([None] * nb, None))
    is_new = [isinstance(b, jax.ShapeDtypeStruct) for b in bufs]
    old = [b for b, fresh in zip(bufs, is_new) if not fresh]
    no = len(old)
    tok = [] if any(order.last is b for b in old) else [order.last]
    first_out = no + len(tok)

    def body(*refs):
        send, recv = refs[first_out:first_out + n], refs[first_out + n:first_out + 2 * n]
        token = refs[-1]
        given, made = iter(refs[:no]), iter(refs[first_out + 2 * n + no:-1])
        logical = [next(made) if fresh else next(given) for fresh in is_new]
        for i, (src, dst, dev) in enumerate(copies(logical, _place())):
            pltpu.make_async_remote_copy(src_ref=src, dst_ref=dst, send_sem=send[i], recv_sem=recv[i],
                                         device_id=dev, device_id_type=MESH).start()
        token[...] = jnp.zeros_like(token)

    fresh_shapes = [b for b, fresh in zip(bufs, is_new) if fresh]
    out = pl.pallas_call(
        body, name=name,
        out_shape=tuple([pltpu.SemaphoreType.DMA(())] * (2 * n)
                        + [pltpu.HBM(b.shape, b.dtype) for b in old + fresh_shapes]
                        + [jax.ShapeDtypeStruct((8, LANES), F32)]),
        in_specs=[HBM] * no + [ANY] * len(tok),
        out_specs=tuple([SEM] * (2 * n) + [HBM] * nb + [pl.BlockSpec(memory_space=pltpu.VMEM)]),
        input_output_aliases={i: 2 * n + i for i in range(no)},
        compiler_params=pltpu.CompilerParams(has_side_effects=EFFECT),
    )(*[pltpu.with_memory_space_constraint(b, pltpu.HBM) for b in old], *tok)
    order.last = out[-1]
    thru, made = iter(out[2 * n:2 * n + no]), iter(out[2 * n + no:2 * n + nb])
    return list(out[:2 * n]), [next(made) if fresh else next(thru) for fresh in is_new]


def _xfer_wait(order, name, sems, bufs, copies):
    nb = len(bufs)
    n = len(sems) // 2
    tok = order.last

    def body(*refs):
        send, recv = refs[nb:nb + n], refs[nb + n:nb + 2 * n]
        token = refs[-1]
        for i, (src, dst, dev) in enumerate(copies(refs[:nb], _place())):
            cp = pltpu.make_async_remote_copy(src_ref=src, dst_ref=dst, send_sem=send[i], recv_sem=recv[i],
                                              device_id=dev, device_id_type=MESH)
            cp.wait_send()
            cp.wait_recv()
        token[...] = jnp.zeros_like(token)

    out = pl.pallas_call(
        body, name=name,
        out_shape=tuple([pltpu.HBM(b.shape, b.dtype) for b in bufs] + [jax.ShapeDtypeStruct((8, LANES), F32)]),
        in_specs=[HBM] * nb + [SEM] * (2 * n) + [ANY],
        out_specs=tuple([HBM] * nb + [pl.BlockSpec(memory_space=pltpu.VMEM)]),
        input_output_aliases={i: i for i in range(nb)},
        compiler_params=pltpu.CompilerParams(has_side_effects=EFFECT),
    )(*bufs, *sems, tok)
    order.last = out[-1]
    return list(out[:nb])


class _Xfer:
    def __init__(self, name, bufs, copies):
        self.name, self.bufs, self.copies = name, list(bufs), copies
        self.sems = None

    def start(self, order):
        self.sems, self.bufs = _xfer_start(order, self.name + "_start", self.bufs, self.copies)

    def wait(self, order):
        self.bufs = _xfer_wait(order, self.name + "_wait", self.sems, self.bufs, self.copies)
        return self.bufs


def _block_rows(ref, r, d):
    return ref.at[pl.ds(d * r, r)]


def _gather_send(fulls):
    def copies(refs, place):
        out = []
        for w, full in enumerate(fulls):
            r = full.shape[0] // 8
            if place is None:
                out += [None] * 4
                continue
            x, y, c = place
            mine = _block_rows(refs[w], r, 4 * x + 2 * y + c)
            out.append((mine, mine, (x, y, 1 - c)))
            for px, py in ((1 - x, y), (x, 1 - y), (1 - x, 1 - y)):
                out.append((mine, mine, (px, py, c)))
        return out
    return copies


def _gather_forward(fulls):
    def copies(refs, place):
        out = []
        for w, full in enumerate(fulls):
            r = full.shape[0] // 8
            if place is None:
                out += [None] * 3
                continue
            x, y, c = place
            for px, py in ((1 - x, y), (x, 1 - y), (1 - x, 1 - y)):
                blk = _block_rows(refs[w], r, 4 * px + 2 * py + c)
                out.append((blk, blk, (x, y, 1 - c)))
        return out
    return copies


def _pair_send(nw):
    def copies(refs, place):
        out = []
        for w in range(nw):
            if place is None:
                out += [None] * 4
                continue
            x, y, c = place
            grad, other = refs[2 * w], refs[2 * w + 1]
            r = other.shape[1]
            for k in range(4):
                out.append((_block_rows(grad, r, 2 * k + 1 - c), other.at[k], (x, y, 1 - c)))
        return out
    return copies


def _chip_send(nw):
    def copies(refs, place):
        out = []
        for w in range(nw):
            if place is None:
                out += [None] * 3
                continue
            x, y, c = place
            psum, parts = refs[2 * w], refs[2 * w + 1]
            for px, py in ((1 - x, y), (x, 1 - y), (1 - x, 1 - y)):
                out.append((psum.at[2 * px + py], parts.at[2 * x + y], (px, py, c)))
        return out
    return copies


def _dev_index():
    x, y, c = _place()
    return 4 * x + 2 * y + c


def _place_own(order, shard, name):
    r, cols = shard.shape
    tr = _pick(r, max(16, (12 << 20) // (4 * cols)), 16)
    nr = r // tr

    def body(s_ref, o_ref):
        o_ref[...] = s_ref[...].astype(BF16)

    return _call(
        order, body, [shard], name=name, grid=(nr,),
        in_specs=[pl.BlockSpec((tr, cols), lambda i: (i, 0))],
        out_specs=pl.BlockSpec((tr, cols), lambda i: (_dev_index() * nr + i, 0)),
        out_shape=jax.ShapeDtypeStruct((8 * r, cols), BF16), sem=("parallel",))


def _pair_sum(order, grad, other, name):
    r, cols = other.shape[1:]
    tr = _pick(r, max(16, (7 << 20) // (2 * cols)), 16)
    nr = r // tr

    def body(g_ref, a_ref, o_ref):
        o_ref[...] = (g_ref[...].astype(F32) + a_ref[...].astype(F32)).astype(BF16)

    blk = pl.BlockSpec((None, tr, cols), lambda k, i: (k, i, 0))
    return _call(
        order, body, [grad, other], name=name, grid=(4, nr),
        in_specs=[pl.BlockSpec((tr, cols), lambda k, i: ((2 * k + lax.axis_index("c")) * nr + i, 0)), blk],
        out_specs=blk, out_shape=jax.ShapeDtypeStruct(other.shape, BF16), sem=("parallel", "parallel"))


def _chip_sum(order, psum, parts, name):
    _, r, cols = parts.shape
    tr = _pick(r, max(16, (1 << 20) // (2 * cols)), 16)

    def my_chip():
        return 2 * lax.axis_index("x") + lax.axis_index("y")

    def body(own_ref, p0, p1, p2, p3, o_ref):
        own = own_ref[...].astype(F32)
        acc = None
        for k, p in enumerate((p0, p1, p2, p3)):
            term = jnp.where(my_chip() == k, own, p[...].astype(F32))
            acc = term if acc is None else acc + term
        o_ref[...] = acc

    def slot(k):
        return pl.BlockSpec((None, tr, cols), lambda i: (jnp.where(my_chip() == k, (k + 1) % 4, k), i, 0))

    return _call(
        order, body, [psum, parts, parts, parts, parts], name=name, grid=(r // tr,),
        in_specs=[pl.BlockSpec((None, tr, cols), lambda i: (my_chip(), i, 0))] + [slot(k) for k in range(4)],
        out_specs=pl.BlockSpec((tr, cols), lambda i: (i, 0)),
        out_shape=jax.ShapeDtypeStruct((r, cols), F32), sem=("parallel",))


def _all_reduce_small(order, pack, name):
    R = pack.shape[0]

    def body(p_ref, o_ref, buf, send_sems, recv_sems):
        x, y, c = _place()
        me = 4 * x + 2 * y + c
        buf[me] = p_ref[...]
        copies = []
        for k in range(1, 8):
            px = 1 - x if k & 4 else x
            py = 1 - y if k & 2 else y
            pc = 1 - c if k & 1 else c
            cp = pltpu.make_async_remote_copy(
                src_ref=p_ref, dst_ref=buf.at[me], send_sem=send_sems.at[k - 1], recv_sem=recv_sems.at[k - 1],
                device_id=(px, py, pc), device_id_type=MESH)
            cp.start()
            copies.append(cp)
        for cp in copies:
            cp.wait_recv()
        acc = buf[0]
        for d in range(1, 8):
            acc = acc + buf[d]
        o_ref[...] = acc
        for cp in copies:
            cp.wait_send()

    vm = pl.BlockSpec(memory_space=pltpu.VMEM)
    return _call(
        order, body, [pack], name=name, in_specs=[vm], out_specs=vm,
        out_shape=jax.ShapeDtypeStruct((R, LANES), F32),
        scratch=[pltpu.VMEM((8, R, LANES), F32), pltpu.SemaphoreType.DMA((7,)), pltpu.SemaphoreType.DMA((7,))])


def _pack(parts):
    flat = []
    for p in parts:
        v = p.reshape(-1).astype(F32)
        flat.append(jnp.pad(v, (0, (-v.shape[0]) % LANES)))
    v = jnp.concatenate(flat)
    v = jnp.pad(v, (0, (-v.shape[0]) % (8 * LANES)))
    return v.reshape(-1, LANES)


def _unpack(pack, shapes):
    v = pack.reshape(-1)
    out, off = [], 0
    for s in shapes:
        n = 1
        for d in s:
            n *= d
        out.append(v[off:off + n].reshape(s))
        off += n + (-n) % LANES
    return out


def _adamw(order, w, g, m, v, name):
    shape = w.shape
    cols = shape[-1]
    w2, g2, m2, v2 = (t.reshape(-1, cols) for t in (w, g, m, v))
    R = w2.shape[0]
    tr = _pick(R, max(8, (1 << 20) // (4 * cols)), 8)

    def body(w_ref, g_ref, m_ref, v_ref, d_ref, mo_ref, vo_ref):
        d_ref[...], mo_ref[...], vo_ref[...] = _adam_math(w_ref[...], g_ref[...], m_ref[...], v_ref[...])

    blk = pl.BlockSpec((tr, cols), lambda i: (i, 0))
    outs = _call(
        order, body, [w2, g2, m2, v2], name=name, grid=(R // tr,), in_specs=[blk] * 4, out_specs=[blk] * 3,
        out_shape=[jax.ShapeDtypeStruct((R, cols), F32)] * 3, sem=("parallel",))
    return tuple(o.reshape(shape) for o in outs)


def _adam_math(w, g, m, v):
    mn = ADAM_B1 * m + (1.0 - ADAM_B1) * g
    vn = ADAM_B2 * v + (1.0 - ADAM_B2) * (g * g)
    m_hat = mn / (1.0 - ADAM_B1 ** ADAM_STEP)
    v_hat = vn / (1.0 - ADAM_B2 ** ADAM_STEP)
    return -ADAM_LR * (m_hat / (jnp.sqrt(v_hat) + ADAM_EPS) + ADAM_WD * w), mn, vn


def _chip_sum_adamw(order, w, psum, parts, m, v, name):
    _, r, cols = parts.shape
    tr = _pick(r, max(16, (6 << 20) // (38 * cols)), 16)

    def my_chip():
        return 2 * lax.axis_index("x") + lax.axis_index("y")

    def body(w_ref, own_ref, p0, p1, p2, p3, m_ref, v_ref, g_ref, d_ref, mo_ref, vo_ref):
        own = own_ref[...].astype(F32)
        g = None
        for k, p in enumerate((p0, p1, p2, p3)):
            term = jnp.where(my_chip() == k, own, p[...].astype(F32))
            g = term if g is None else g + term
        g_ref[...] = g
        d_ref[...], mo_ref[...], vo_ref[...] = _adam_math(w_ref[...], g, m_ref[...], v_ref[...])

    def slot(k):
        return pl.BlockSpec((None, tr, cols), lambda i: (jnp.where(my_chip() == k, (k + 1) % 4, k), i, 0))

    blk = pl.BlockSpec((tr, cols), lambda i: (i, 0))
    return _call(
        order, body, [w, psum, parts, parts, parts, parts, m, v], name=name, grid=(r // tr,),
        in_specs=[blk, pl.BlockSpec((None, tr, cols), lambda i: (my_chip(), i, 0))]
        + [slot(k) for k in range(4)] + [blk, blk],
        out_specs=[blk] * 4, out_shape=[jax.ShapeDtypeStruct((r, cols), F32)] * 4, sem=("parallel",))


class _GradReduce:
    def __init__(self, tag, grads, names):
        self.tag, self.grads, self.names = tag, list(grads), names
        self.pair = self.chip = self.psums = None

    def pair_start(self, order):
        bufs = []
        for g in self.grads:
            bufs += [g, jax.ShapeDtypeStruct((4, g.shape[0] // 8, g.shape[1]), g.dtype)]
        self.pair = _Xfer("pair_" + self.tag, bufs, _pair_send(len(self.grads)))
        self.pair.start(order)

    def pair_sum_chip_start(self, order):
        bufs = self.pair.wait(order)
        self.psums = [_pair_sum(order, bufs[2 * w], bufs[2 * w + 1], "pair_sum_" + nm)
                      for w, nm in enumerate(self.names)]
        cbufs = []
        for p in self.psums:
            cbufs += [p, jax.ShapeDtypeStruct(p.shape, p.dtype)]
        self.chip = _Xfer("chip_" + self.tag, cbufs, _chip_send(len(self.psums)))
        self.chip.start(order)

    def finish(self, order):
        bufs = self.chip.wait(order)
        return [(bufs[2 * w], bufs[2 * w + 1]) for w in range(len(self.names))]


def kernel(x, meta_tokens, ln_in_g, ln_in_b, w_in, b_gate, attn_sinks, w_attn_up, w_pool_grp, pool_scale, w_pool_up, w_out, ln1_g, ln1_b, w_ffn_in, w_ffn_down, ln2_g, ln2_b, loss_target, m_meta_tokens, m_ln_in_g, m_ln_in_b, m_w_in, m_b_gate, m_attn_sinks, m_w_attn_up, m_w_pool_grp, m_pool_scale, m_w_pool_up, m_w_out, m_ln1_g, m_ln1_b, m_w_ffn_in, m_w_ffn_down, m_ln2_g, m_ln2_b, v_meta_tokens, v_ln_in_g, v_ln_in_b, v_w_in, v_b_gate, v_attn_sinks, v_w_attn_up, v_w_pool_grp, v_pool_scale, v_w_pool_up, v_w_out, v_ln1_g, v_ln1_b, v_w_ffn_in, v_w_ffn_down, v_ln2_g, v_ln2_b):
    S, D = x.shape[1], x.shape[2]
    Tp = S + BLOCK
    NQ = attn_sinks.shape[-1]
    ATTN = NQ * HEAD_DIM
    KVW = ATTN // Q_PER_KV
    POOL = pool_scale.shape[-1]
    IN = 8 * w_in.shape[2]
    FF = 8 * w_ffn_down.shape[1]
    uoff = ATTN + 2 * KVW
    goff = uoff + POOL
    gw = POOL // 4
    dcols = D // 8
    assert IN == goff + 2 * D and w_ffn_in.shape[2] * 8 == 2 * FF

    xi, yi, ci = _place()
    dev = 4 * xi + 2 * yi + ci
    x2, tgt = x[0], loss_target[0]
    order = _Order()

    def place_cols(a):
        return lax.dynamic_update_slice(jnp.zeros(a.shape[:-1] + (D,), F32), a, (0,) * (a.ndim - 1) + (dev * dcols,))

    small = _all_reduce_small(order, _pack([place_cols(meta_tokens), place_cols(b_gate[0])]), "small_inputs_gather")
    meta_full, bgate_full = _unpack(small, [(N_META, D), (2, D)])
    meta_pad = jnp.pad(meta_full, ((META_ROW0, 0), (0, 0)))

    wgrp_rows = w_pool_grp[0].reshape(4 * (gw // 8), gw)
    full_in = _place_own(order, w_in[0].T, "own_w_in")
    ag_in = _Xfer("gather_w_in", [full_in], _gather_send([full_in]))
    ag_in.start(order)
    mix_names = ["w_attn_up", "w_pool_grp", "w_pool_up", "w_out"]
    mix_shards = [w_attn_up[0].T, wgrp_rows, w_pool_up[0].T, w_out[0]]
    full_mix = [_place_own(order, s, "own_" + nm) for s, nm in zip(mix_shards, mix_names)]
    full_ffn = _place_own(order, w_ffn_in[0].T, "own_w_ffn_in")

    ln_in_g2, ln_in_b2 = ln_in_g.reshape(1, D), ln_in_b.reshape(1, D)
    tab = _rope_table(S)

    h0, h0b = _ln_in_fwd(order, x2, meta_pad, ln_in_g2, ln_in_b2)
    (full_in,) = ag_in.wait(order)
    fw_in = _Xfer("forward_w_in", [full_in], _gather_forward([full_in]))
    fw_in.start(order)
    ag_mix = _Xfer("gather_mixers", full_mix, _gather_send(full_mix))
    ag_mix.start(order)
    ag_ffn = _Xfer("gather_w_ffn_in", [full_ffn], _gather_send([full_ffn]))
    ag_ffn.start(order)
    full_down = _place_own(order, w_ffn_down[0], "own_w_ffn_down")
    (winT,) = fw_in.wait(order)
    proj = _mm(order, h0b, winT, kind="nt", out_dtype=F32, tm=1408, tn=512, name="proj")

    full_mix = ag_mix.wait(order)
    ag_down = _Xfer("gather_w_ffn_down", [full_down], _gather_send([full_down]))
    ag_down.start(order)
    fw_mix = _Xfer("forward_mixers", full_mix, _gather_forward(full_mix))
    fw_mix.start(order)
    att = _attn_fwd(order, proj, tab, attn_sinks, S, ATTN, KVW)
    wattT, wgrp_g, wpupT, wout = fw_mix.wait(order)
    wgrp = wgrp_g.reshape(8, 4, gw // 8, gw).transpose(1, 0, 2, 3).reshape(4, gw, gw)

    ps = _pool_fwd(order, proj, wgrp, pool_scale, S, uoff, POOL)
    a_out = _mm(order, att, wattT, kind="nt", out_dtype=F32, tm=1024, tn=1024, name="attn_up")
    p_out = _mm(order, ps, wpupT, kind="nt", out_dtype=F32, tm=1024, tn=1024, name="pool_up")
    mixed = _gate_mix(order, proj, bgate_full, a_out, p_out, S, D, goff)
    y1 = _mm(order, mixed, wout, kind="nn", out_dtype=F32, tm=1024, tn=1024, name="out_proj")

    (full_ffn,) = ag_ffn.wait(order)
    fw_ffn = _Xfer("forward_w_ffn_in", [full_ffn], _gather_forward([full_ffn]))
    fw_ffn.start(order)
    h1, h1b = _ln1_fwd(order, h0, y1, ln1_g, ln1_b)
    (wffnT,) = fw_ffn.wait(order)
    f, act = _ffn_in_swiglu(order, h1b, wffnT, FF)

    (full_down,) = ag_down.wait(order)
    fw_down = _Xfer("forward_w_ffn_down", [full_down], _gather_forward([full_down]))
    fw_down.start(order)
    (wdown,) = fw_down.wait(order)
    y2 = _mm(order, act, wdown, kind="nn", out_dtype=F32, tm=512, tn=1024, tk=5504, name="ffn_down")

    dz2, dz2b, dg2, db2, loss_part = _ln2_loss_bwd(order, h1, y2, tgt, ln2_g, ln2_b)
    gwdown = _mm(order, act, dz2b, kind="tn", out_dtype=BF16, tm=256, tn=2048, name="d_ffn_down")
    rs_down = _GradReduce("w_ffn_down", [gwdown], ["w_ffn_down"])
    rs_down.pair_start(order)
    df = _d_act_swiglu(order, dz2b, wdown, f)
    rs_down.pair_sum_chip_start(order)
    gwffnT = _mm(order, df, h1b, kind="tn", out_dtype=BF16, tm=256, tn=2048, name="d_ffn_in", a_lead="halves")
    rs_ffn = _GradReduce("w_ffn_in", [gwffnT], ["w_ffn_in"])
    rs_ffn.pair_start(order)
    dh1 = _mm(order, df, wffnT, kind="nn", out_dtype=F32, tm=512, tn=1024, tk=5504, name="d_h1", a_lead="halves")
    rs_ffn.pair_sum_chip_start(order)
    dz1, dz1b, dg1, db1 = _ln1_bwd(order, h0, y1, ln1_g, dh1, dz2)
    gwout = _mm(order, mixed, dz1b, kind="tn", out_dtype=BF16, tm=512, tn=1024, name="d_out_proj")
    rs_out = _GradReduce("w_out", [gwout], ["w_out"])
    rs_out.pair_start(order)
    dmixed = _mm(order, dz1b, wout, kind="nt", out_dtype=F32, tm=1024, tn=1024, name="d_mixed")
    rs_out.pair_sum_chip_start(order)

    dproj = _zero_meta_block(order, Tp, IN)
    dap, dproj, dbgate = _gate_bwd(order, proj, bgate_full, a_out, p_out, dmixed, dproj, S, D, goff)
    gwattT = _mm(order, dap, att, kind="tn", out_dtype=BF16, tm=512, tn=1024, name="d_attn_up", a_lead=0)
    datt = _mm(order, dap, wattT, kind="nn", out_dtype=BF16, tm=1024, tn=1024, name="d_att", a_lead=0)
    gwpupT = _mm(order, dap, ps, kind="tn", out_dtype=BF16, tm=512, tn=1024, name="d_pool_up", a_lead=1)
    dps = _mm(order, dap, wpupT, kind="nn", out_dtype=F32, tm=1024, tn=1024, name="d_ps", a_lead=1)
    dpl, gwgrp, dscale = _pool_bwd_mix(order, proj, wgrp, pool_scale, dps, S, uoff, POOL)
    gwgrp_rows = gwgrp.reshape(4, 8, gw // 8, gw).transpose(1, 0, 2, 3).reshape(8 * 4 * (gw // 8), gw).astype(BF16)
    rs_mix = _GradReduce("mixers", [gwattT, gwgrp_rows, gwpupT], ["w_attn_up", "w_pool_grp", "w_pool_up"])
    rs_mix.pair_start(order)
    dproj = _pool_bwd_window(order, dpl, dproj, S, uoff, POOL)
    rs_mix.pair_sum_chip_start(order)
    dproj, dk, dv, dsink = _attn_bwd(order, proj, tab, attn_sinks, datt, dproj, S, ATTN, KVW)
    dproj = _put_dkv(order, dk, dv, dproj, ATTN)

    weights = dict(meta_tokens=meta_tokens, ln_in_g=ln_in_g, ln_in_b=ln_in_b, w_in=w_in, b_gate=b_gate,
                   attn_sinks=attn_sinks, w_attn_up=w_attn_up, w_pool_grp=w_pool_grp, pool_scale=pool_scale,
                   w_pool_up=w_pool_up, w_out=w_out, ln1_g=ln1_g, ln1_b=ln1_b, w_ffn_in=w_ffn_in,
                   w_ffn_down=w_ffn_down, ln2_g=ln2_g, ln2_b=ln2_b)
    ms = dict(meta_tokens=m_meta_tokens, ln_in_g=m_ln_in_g, ln_in_b=m_ln_in_b, w_in=m_w_in, b_gate=m_b_gate,
              attn_sinks=m_attn_sinks, w_attn_up=m_w_attn_up, w_pool_grp=m_w_pool_grp, pool_scale=m_pool_scale,
              w_pool_up=m_w_pool_up, w_out=m_w_out, ln1_g=m_ln1_g, ln1_b=m_ln1_b, w_ffn_in=m_w_ffn_in,
              w_ffn_down=m_w_ffn_down, ln2_g=m_ln2_g, ln2_b=m_ln2_b)
    vs = dict(meta_tokens=v_meta_tokens, ln_in_g=v_ln_in_g, ln_in_b=v_ln_in_b, w_in=v_w_in, b_gate=v_b_gate,
              attn_sinks=v_attn_sinks, w_attn_up=v_w_attn_up, w_pool_grp=v_w_pool_grp, pool_scale=v_pool_scale,
              w_pool_up=v_w_pool_up, w_out=v_w_out, ln1_g=v_ln1_g, ln1_b=v_ln1_b, w_ffn_in=v_w_ffn_in,
              w_ffn_down=v_w_ffn_down, ln2_g=v_ln2_g, ln2_b=v_ln2_b)
    grads, deltas, new_ms, new_vs = {}, {}, {}, {}

    def update(nm, g):
        g = g.reshape(weights[nm].shape)
        grads[nm] = g
        deltas[nm], new_ms[nm], new_vs[nm] = _adamw(order, weights[nm], g, ms[nm], vs[nm], "adamw_" + nm)

    def update_reduced(nm, bufs, transposed=False):
        psum, parts = bufs
        if transposed:
            to2d, back = (lambda t: t[0].T), (lambda t: t.T[None])
        else:
            to2d, back = (lambda t: t.reshape(parts.shape[1:])), (lambda t: t.reshape(weights[nm].shape))
        outs = _chip_sum_adamw(order, to2d(weights[nm]), psum, parts, to2d(ms[nm]), to2d(vs[nm]), "adamw_" + nm)
        grads[nm], deltas[nm], new_ms[nm], new_vs[nm] = (back(t) for t in outs)

    gwinT = _mm(order, dproj, h0b, kind="tn", out_dtype=BF16, tm=512, tn=1024, name="d_w_in")
    rs_in = _GradReduce("w_in", [gwinT], ["w_in"])
    rs_in.pair_start(order)
    update_reduced("w_ffn_down", rs_down.finish(order)[0])
    rs_in.pair_sum_chip_start(order)
    dh0 = _mm(order, dproj, winT, kind="nn", out_dtype=F32, tm=1408, tn=1024, tk=2560, name="d_h0")
    dxin, dg_in, db_in = _ln_in_bwd(order, x2, meta_pad, ln_in_g2, dh0, dz1)
    grad_x = dxin[:S][None]
    dmeta = dxin[S + META_ROW0:]

    small_shapes = [(D,), (D,), (1, D), (1, D), (1, D), (1, D), (1, POOL), (1, NQ), (), (N_META, D), (2, D)]
    red = _all_reduce_small(order, _pack([dg_in, db_in, dg1, db1, dg2, db2, dscale, dsink[:, :, 0], loss_part,
                                          dmeta, dbgate]), "small_grads_all_reduce")

    update_reduced("w_ffn_in", rs_ffn.finish(order)[0], transposed=True)
    update_reduced("w_out", rs_out.finish(order)[0])
    b_att, b_grp, b_pup = rs_mix.finish(order)
    update("w_attn_up", _chip_sum(order, *b_att, "chip_sum_w_attn_up").T)
    update_reduced("w_pool_grp", b_grp)
    update("w_pool_up", _chip_sum(order, *b_pup, "chip_sum_w_pool_up").T)

    (g_ln_in_g, g_ln_in_b, g_ln1_g, g_ln1_b, g_ln2_g, g_ln2_b, g_scale, g_sinks, loss_sum, g_meta_full,
     g_bgate_full) = _unpack(red, small_shapes)
    loss = 0.5 * loss_sum
    update("meta_tokens", lax.dynamic_slice(g_meta_full, (0, dev * dcols), (N_META, dcols)))
    update("b_gate", lax.dynamic_slice(g_bgate_full, (0, dev * dcols), (2, dcols)))
    for nm, g in (("ln_in_g", g_ln_in_g), ("ln_in_b", g_ln_in_b), ("ln1_g", g_ln1_g), ("ln1_b", g_ln1_b),
                  ("ln2_g", g_ln2_g), ("ln2_b", g_ln2_b), ("pool_scale", g_scale), ("attn_sinks", g_sinks)):
        update(nm, g)

    update_reduced("w_in", rs_in.finish(order)[0], transposed=True)

    names = list(weights)
    return (loss, grad_x, *[grads[n] for n in names], *[deltas[n] for n in names],
            *[new_ms[n] for n in names], *[new_vs[n] for n in names])
```

```python
import jax
import jax.numpy as jnp
from jax import lax
from jax.experimental import pallas as pl
from jax.experimental.pallas import tpu as pltpu

F32 = jnp.float32
BF16 = jnp.bfloat16
MESH = pl.DeviceIdType.MESH

N_META = 16
HEAD_DIM = 64
Q_PER_KV = 8
WINDOW = 128
BLOCK = 128
ATTN_SCALE = HEAD_DIM ** -0.5
ROPE_DIM = HEAD_DIM // 4
ROPE_THETA = 500000.0
NEG_INF = -1e30
POOL_WINDOWS = (2, 4, 8, 16)
LN_EPS = 1e-5
DN_ALPHA = 2.0 ** 0.25
ADAM_LR = 0.001
ADAM_B1 = 0.9
ADAM_B2 = 0.999
ADAM_EPS = 1e-08
ADAM_WD = 0.01
ADAM_STEP = 10

LANES = 128
META_ROW0 = BLOCK - N_META
VMEM_LIMIT = 56 * 1024 * 1024

ANY = pl.BlockSpec(memory_space=pl.ANY)
HBM = pl.BlockSpec(memory_space=pltpu.HBM)
SEM = pl.BlockSpec(memory_space=pltpu.SEMAPHORE)
EFFECT = pltpu.SideEffectType.DATAFLOW_SIDE_EFFECTING


def _params(sem=None, **kw):
    return pltpu.CompilerParams(dimension_semantics=sem, vmem_limit_bytes=VMEM_LIMIT, **kw)


class _Order:
    def __init__(self):
        self.last = None


def _call(order, body, operands, *, name, in_specs, out_specs, out_shape, grid=(), scratch=(), sem=None,
          aliases=None, prefetch=()):
    n_in, npf = len(operands), len(prefetch)
    tok = order.last
    if tok is not None and any(tok is op for op in operands):
        tok = None

    def wrapped(*refs):
        refs = list(refs)
        if tok is not None:
            del refs[npf + n_in]
        body(*refs)

    specs = list(in_specs) + ([ANY] if tok is not None else [])
    ops = list(operands) + ([tok] if tok is not None else [])
    if npf:
        out = pl.pallas_call(
            wrapped, name=name, out_shape=out_shape, compiler_params=_params(sem),
            grid_spec=pltpu.PrefetchScalarGridSpec(num_scalar_prefetch=npf, grid=grid, in_specs=specs,
                                                   out_specs=out_specs, scratch_shapes=list(scratch)),
        )(*prefetch, *ops)
    else:
        out = pl.pallas_call(
            wrapped, name=name, grid=grid, in_specs=specs, out_specs=out_specs, out_shape=out_shape,
            scratch_shapes=list(scratch), input_output_aliases=aliases or {}, compiler_params=_params(sem),
        )(*ops)
    order.last = out[0] if isinstance(out, (list, tuple)) else out
    return out


def _pick(dim, pref, mult=LANES):
    best = None
    t = mult
    while t <= min(dim, pref):
        if dim % t == 0:
            best = t
        t += mult
    return dim if best is None else best


_DIMS = {"nn": (((1,), (0,)), ((), ())), "nt": (((1,), (1,)), ((), ())), "tn": (((0,), (0,)), ((), ()))}


def _mm(order, a, b, *, kind, out_dtype, tm, tn, tk=None, name, a_lead=None):
    a2 = a.shape[-2:]
    halves = a_lead == "halves"
    if halves:
        a2 = (a2[0], 2 * a2[1])
    if kind == "tn":
        K, M = a2
    else:
        M, K = a2
    N = b.shape[0] if kind == "nt" else b.shape[1]
    half_cols = a2[1] // 2
    tm = _pick(half_cols if halves and kind == "tn" else M, tm)
    tn = _pick(N, tn)
    tk = K if tk is None else _pick(half_cols if halves and kind != "tn" else K, tk)
    nm, nn_, nk = M // tm, N // tn, K // tk
    a_bytes = M * K * a.dtype.itemsize
    b_bytes = N * K * b.dtype.itemsize
    i_outer = (a_bytes + nm * b_bytes <= b_bytes + nn_ * a_bytes) if nk == 1 else True

    def ij(g0, g1):
        return (g0, g1) if i_outer else (g1, g0)

    def a_map(g0, g1, k):
        i, _ = ij(g0, g1)
        if halves:
            per = half_cols // (tm if kind == "tn" else tk)
            return (i // per, k, i % per) if kind == "tn" else (k // per, i, k % per)
        idx = (k, i) if kind == "tn" else (i, k)
        return idx if a_lead is None else (a_lead,) + idx

    def b_map(g0, g1, k):
        _, j = ij(g0, g1)
        return (j, k) if kind == "nt" else (k, j)

    def o_map(g0, g1, k):
        return ij(g0, g1)

    a_blk = (tk, tm) if kind == "tn" else (tm, tk)
    if a_lead is not None:
        a_blk = (None,) + a_blk
    b_blk = (tn, tk) if kind == "nt" else (tk, tn)

    in_place = out_dtype == F32

    def body(a_ref, b_ref, o_ref, *acc):
        p = lax.dot_general(a_ref[...], b_ref[...], _DIMS[kind], preferred_element_type=F32)
        if nk == 1:
            o_ref[...] = p.astype(o_ref.dtype)
        else:
            k = pl.program_id(2)
            acc_ref = o_ref if in_place else acc[0]

            @pl.when(k == 0)
            def _():
                acc_ref[...] = p

            @pl.when(k > 0)
            def _():
                acc_ref[...] += p

            if not in_place:
                @pl.when(k == nk - 1)
                def _():
                    o_ref[...] = acc_ref[...].astype(o_ref.dtype)

    grid = (nm, nn_, nk) if i_outer else (nn_, nm, nk)
    return _call(
        order, body, [a, b], name=name, grid=grid,
        in_specs=[pl.BlockSpec(a_blk, a_map), pl.BlockSpec(b_blk, b_map)],
        out_specs=pl.BlockSpec((tm, tn), o_map),
        out_shape=jax.ShapeDtypeStruct((M, N), out_dtype),
        scratch=[] if nk == 1 or in_place else [pltpu.VMEM((tm, tn), F32)],
        sem=("parallel", "parallel", "arbitrary"))


def _ln_stats(z):
    mu = jnp.mean(z, axis=-1, keepdims=True)
    zc = z - mu
    var = jnp.mean(zc * zc, axis=-1, keepdims=True)
    rstd = lax.rsqrt(var + LN_EPS)
    return zc * rstd, rstd


def _ln_bwd(dy, xhat, rstd, g):
    dxh = dy * g
    m1 = jnp.mean(dxh, axis=-1, keepdims=True)
    m2 = jnp.mean(dxh * xhat, axis=-1, keepdims=True)
    return rstd * (dxh - m1 - xhat * m2)


def _ln_in_fwd(order, x, meta_pad, g, b):
    S, D = x.shape
    nb = S // BLOCK

    def body(x_ref, mp_ref, g_ref, b_ref, h_ref, hb_ref):
        is_meta = pl.program_id(0) == nb
        xin = jnp.where(is_meta, mp_ref[...], x_ref[...])
        xhat, _ = _ln_stats(xin)
        y = xhat * g_ref[...] + b_ref[...]
        h_ref[...] = y
        hb_ref[...] = y.astype(BF16)

    row = pl.BlockSpec((BLOCK, D), lambda i: (i, 0))
    vec = pl.BlockSpec((1, D), lambda i: (0, 0))
    return _call(
        order, body, [x, meta_pad, g, b], name="ln_in_fwd", grid=(nb + 1,),
        in_specs=[pl.BlockSpec((BLOCK, D), lambda i: (jnp.minimum(i, nb - 1), 0)),
                  pl.BlockSpec((BLOCK, D), lambda i: (0, 0)), vec, vec],
        out_specs=[row, row],
        out_shape=[jax.ShapeDtypeStruct((S + BLOCK, D), F32), jax.ShapeDtypeStruct((S + BLOCK, D), BF16)],
        sem=("parallel",))


def _ln_in_bwd(order, x, meta_pad, g, dh0, dz1):
    S, D = x.shape
    nb = S // BLOCK

    def body(x_ref, mp_ref, g_ref, dh_ref, dz_ref, dx_ref, dg_ref, db_ref):
        i = pl.program_id(0)
        is_meta = i == nb
        xin = jnp.where(is_meta, mp_ref[...], x_ref[...])
        xhat, rstd = _ln_stats(xin)
        dy = dh_ref[...] + jnp.where(is_meta, 0.0, DN_ALPHA) * dz_ref[...]
        dx_ref[...] = _ln_bwd(dy, xhat, rstd, g_ref[...])

        @pl.when(i == 0)
        def _():
            dg_ref[...] = jnp.zeros_like(dg_ref)
            db_ref[...] = jnp.zeros_like(db_ref)

        dg_ref[...] += jnp.sum(dy * xhat, axis=0, keepdims=True)
        db_ref[...] += jnp.sum(dy, axis=0, keepdims=True)

    row = pl.BlockSpec((BLOCK, D), lambda i: (i, 0))
    rowx = pl.BlockSpec((BLOCK, D), lambda i: (jnp.minimum(i, nb - 1), 0))
    vec = pl.BlockSpec((1, D), lambda i: (0, 0))
    return _call(
        order, body, [x, meta_pad, g, dh0, dz1], name="ln_in_bwd", grid=(nb + 1,),
        in_specs=[rowx, pl.BlockSpec((BLOCK, D), lambda i: (0, 0)), vec, row, rowx],
        out_specs=[row, vec, vec],
        out_shape=[jax.ShapeDtypeStruct((S + BLOCK, D), F32), jax.ShapeDtypeStruct((1, D), F32),
                   jax.ShapeDtypeStruct((1, D), F32)],
        sem=("arbitrary",))


def _ln1_fwd(order, h0, y1, g, b):
    S, D = y1.shape
    tm = _pick(S, BLOCK, 8)

    def body(h_ref, y_ref, g_ref, b_ref, o_ref, ob_ref):
        xhat, _ = _ln_stats(DN_ALPHA * h_ref[...] + y_ref[...])
        y = xhat * g_ref[...] + b_ref[...]
        o_ref[...] = y
        ob_ref[...] = y.astype(BF16)

    row = pl.BlockSpec((tm, D), lambda i: (i, 0))
    vec = pl.BlockSpec((1, D), lambda i: (0, 0))
    return _call(
        order, body, [h0, y1, g, b], name="ln1_fwd", grid=(S // tm,), in_specs=[row, row, vec, vec],
        out_specs=[row, row],
        out_shape=[jax.ShapeDtypeStruct((S, D), F32), jax.ShapeDtypeStruct((S, D), BF16)],
        sem=("parallel",))


def _ln1_bwd(order, h0, y1, g, dh1, dz2):
    S, D = y1.shape
    tm = _pick(S, BLOCK, 8)

    def body(h_ref, y_ref, g_ref, dh_ref, dz2_ref, dz_ref, dzb_ref, dg_ref, db_ref):
        i = pl.program_id(0)
        xhat, rstd = _ln_stats(DN_ALPHA * h_ref[...] + y_ref[...])
        dy = dh_ref[...] + DN_ALPHA * dz2_ref[...]
        dz = _ln_bwd(dy, xhat, rstd, g_ref[...])
        dz_ref[...] = dz
        dzb_ref[...] = dz.astype(BF16)

        @pl.when(i == 0)
        def _():
            dg_ref[...] = jnp.zeros_like(dg_ref)
            db_ref[...] = jnp.zeros_like(db_ref)

        dg_ref[...] += jnp.sum(dy * xhat, axis=0, keepdims=True)
        db_ref[...] += jnp.sum(dy, axis=0, keepdims=True)

    row = pl.BlockSpec((tm, D), lambda i: (i, 0))
    vec = pl.BlockSpec((1, D), lambda i: (0, 0))
    return _call(
        order, body, [h0, y1, g, dh1, dz2], name="ln1_bwd", grid=(S // tm,),
        in_specs=[row, row, vec, row, row], out_specs=[row, row, vec, vec],
        out_shape=[jax.ShapeDtypeStruct((S, D), F32), jax.ShapeDtypeStruct((S, D), BF16),
                   jax.ShapeDtypeStruct((1, D), F32), jax.ShapeDtypeStruct((1, D), F32)],
        sem=("arbitrary",))


def _ln2_loss_bwd(order, h1, y2, target, g, b):
    S, D = y2.shape
    tm = _pick(S, BLOCK, 8)

    def body(h_ref, y_ref, t_ref, g_ref, b_ref, dz_ref, dzb_ref, dg_ref, db_ref, loss_ref):
        i = pl.program_id(0)
        xhat, rstd = _ln_stats(DN_ALPHA * h_ref[...] + y_ref[...])
        diff = xhat * g_ref[...] + b_ref[...] - t_ref[...]
        dy = diff / D
        dz = _ln_bwd(dy, xhat, rstd, g_ref[...])
        dz_ref[...] = dz
        dzb_ref[...] = dz.astype(BF16)

        @pl.when(i == 0)
        def _():
            dg_ref[...] = jnp.zeros_like(dg_ref)
            db_ref[...] = jnp.zeros_like(db_ref)
            loss_ref[...] = jnp.zeros_like(loss_ref)

        dg_ref[...] += jnp.sum(dy * xhat, axis=0, keepdims=True)
        db_ref[...] += jnp.sum(dy, axis=0, keepdims=True)
        loss_ref[...] += jnp.sum(jnp.mean(diff * diff, axis=-1, keepdims=True), axis=0, keepdims=True)

    row = pl.BlockSpec((tm, D), lambda i: (i, 0))
    vec = pl.BlockSpec((1, D), lambda i: (0, 0))
    one = pl.BlockSpec((1, 1), lambda i: (0, 0))
    return _call(
        order, body, [h1, y2, target, g, b], name="ln2_loss_bwd", grid=(S // tm,),
        in_specs=[row, row, row, vec, vec], out_specs=[row, row, vec, vec, one],
        out_shape=[jax.ShapeDtypeStruct((S, D), F32), jax.ShapeDtypeStruct((S, D), BF16),
                   jax.ShapeDtypeStruct((1, D), F32), jax.ShapeDtypeStruct((1, D), F32),
                   jax.ShapeDtypeStruct((1, 1), F32)],
        sem=("arbitrary",))


def _rope_table(S):
    r = jnp.arange(S + BLOCK)
    pos = jnp.where(r < S, r + N_META, jnp.maximum(r - (S + META_ROW0), 0))
    half = ROPE_DIM // 2
    lane = jnp.arange(LANES) % HEAD_DIM
    inv_freq = ROPE_THETA ** (-(lane % half).astype(F32) * 2.0 / ROPE_DIM)
    ang = pos.astype(F32)[:, None] * inv_freq[None, :]
    cos, sin = jnp.cos(ang), jnp.sin(ang)
    c = jnp.where(lane < ROPE_DIM, cos, 1.0)
    sa = jnp.where(lane < half, -sin, 0.0)
    sb = jnp.where((lane >= half) & (lane < ROPE_DIM), sin, 0.0)
    return jnp.concatenate([c, sa, sb], axis=1).astype(F32)


def _rope(x, tab):
    h = ROPE_DIM // 2
    return (x * tab[:, :LANES] + pltpu.roll(x, LANES - h, 1) * tab[:, LANES:2 * LANES]
            + pltpu.roll(x, h, 1) * tab[:, 2 * LANES:])


def _rope_t(dy, tab):
    h = ROPE_DIM // 2
    return (dy * tab[:, :LANES] + pltpu.roll(dy * tab[:, LANES:2 * LANES], h, 1)
            + pltpu.roll(dy * tab[:, 2 * LANES:], LANES - h, 1))


NKEY = N_META + 2 * BLOCK


def _attn_tiles(g, n, S, sink_ref, q_ref, k_ref, v_ref, tab_ref):
    NQG = Q_PER_KV // 2
    R = NQG * BLOCK
    halfsel = (g % 2).astype(F32)
    prev = jnp.maximum(n - 1, 0)
    qrow = pl.ds(pl.multiple_of(n * BLOCK, BLOCK), BLOCK)
    prow = pl.ds(pl.multiple_of(prev * BLOCK, BLOCK), BLOCK)
    mrow = pl.ds(S + META_ROW0, N_META)

    tq = tab_ref[qrow, :]
    qf = q_ref[...]
    q4 = jnp.concatenate([_rope(qf[:, LANES * p:LANES * (p + 1)], tq) for p in range(NQG)], axis=0).astype(BF16)

    tk = jnp.concatenate([tab_ref[mrow, :], tab_ref[prow, :], tq], axis=0)
    kr = _rope(jnp.concatenate([k_ref[mrow, :], k_ref[prow, :], k_ref[qrow, :]], axis=0), tk)
    vr = jnp.concatenate([v_ref[mrow, :], v_ref[prow, :], v_ref[qrow, :]], axis=0)

    lane = lax.broadcasted_iota(jnp.int32, kr.shape, 1)
    own = jnp.where(lane < HEAD_DIM, 1.0 - halfsel, halfsel)

    def lo_hi(t):
        mine = t * own
        other = pltpu.roll(mine, HEAD_DIM, 1)
        lo = mine * (1.0 - halfsel) + other * halfsel
        hi = other * (1.0 - halfsel) + mine * halfsel
        return lo.astype(BF16), hi.astype(BF16)

    klo, khi = lo_hi(kr)
    vlo, vhi = lo_hi(vr)

    jj = lax.broadcasted_iota(jnp.int32, (BLOCK, R), 0)
    qi = lax.broadcasted_iota(jnp.int32, (BLOCK, R), 1) & (BLOCK - 1)
    in_cur = jj <= qi
    band_ok = in_cur | (jj > qi + jnp.where(n >= 1, 0, 2 * BLOCK))

    def soft(kk, parity):
        sk = jnp.concatenate(
            [jnp.full((1, BLOCK), sink_ref[0, Q_PER_KV * g + 2 * p + parity], F32) for p in range(NQG)], axis=1)
        s = lax.dot_general(kk, q4, _DIMS["nt"], preferred_element_type=F32) * ATTN_SCALE
        band = jnp.where(in_cur, s[N_META + BLOCK:], s[N_META:N_META + BLOCK])
        s = jnp.concatenate([s[:N_META], jnp.where(band_ok, band, NEG_INF)], axis=0)
        m = jnp.maximum(jnp.max(s, axis=0, keepdims=True), sk)
        p = jnp.exp(s - m)
        es = jnp.exp(sk - m)
        inv = 1.0 / (jnp.sum(p, axis=0, keepdims=True) + es)
        return p * inv, es * inv

    pe, sink_e = soft(klo, 0)
    po, sink_o = soft(khi, 1)
    return q4, tk, (klo, khi), (vlo, vhi), (pe, po), (sink_e, sink_o), own, in_cur


def _spread(t, in_cur):
    band = t[N_META:]
    return jnp.concatenate([t[:N_META], jnp.where(in_cur, 0.0, band), jnp.where(in_cur, band, 0.0)], axis=0)


def _attn_specs(S, ATTN, KVW):
    Tp = S + BLOCK
    koff, voff = ATTN // LANES, (ATTN + KVW) // LANES
    gw = Q_PER_KV * HEAD_DIM
    return [pl.BlockSpec(memory_space=pltpu.SMEM),
            pl.BlockSpec((BLOCK, gw), lambda g, n: (n, g)),
            pl.BlockSpec((Tp, LANES), lambda g, n: (0, koff + g // 2)),
            pl.BlockSpec((Tp, LANES), lambda g, n: (0, voff + g // 2)),
            pl.BlockSpec((Tp, 3 * LANES), lambda g, n: (0, 0))]


def _attn_fwd(order, proj, tab, sinks, S, ATTN, KVW):
    G = KVW // HEAD_DIM
    nb = S // BLOCK
    gw = Q_PER_KV * HEAD_DIM

    def body(sink_ref, q_ref, k_ref, v_ref, tab_ref, o_ref):
        g, n = pl.program_id(0), pl.program_id(1)
        _, _, _, (vlo, vhi), (pe, po), _, _, in_cur = _attn_tiles(g, n, S, sink_ref, q_ref, k_ref, v_ref, tab_ref)
        o4 = (lax.dot_general(_spread(pe, in_cur).astype(BF16), vlo, _DIMS["tn"], preferred_element_type=F32)
              + lax.dot_general(_spread(po, in_cur).astype(BF16), vhi, _DIMS["tn"], preferred_element_type=F32))
        o_ref[...] = jnp.concatenate(
            [o4[BLOCK * p:BLOCK * (p + 1)] for p in range(Q_PER_KV // 2)], axis=1).astype(BF16)

    return _call(
        order, body, [sinks, proj, proj, proj, tab], name="attn_fwd", grid=(G, nb),
        in_specs=_attn_specs(S, ATTN, KVW),
        out_specs=pl.BlockSpec((BLOCK, gw), lambda g, n: (n, g)),
        out_shape=jax.ShapeDtypeStruct((S, ATTN), BF16),
        sem=("parallel", "arbitrary"))


def _attn_bwd(order, proj, tab, sinks, da, dproj, S, ATTN, KVW):
    G = KVW // HEAD_DIM
    nb = S // BLOCK
    Tp = S + BLOCK
    NQG = Q_PER_KV // 2
    gw = Q_PER_KV * HEAD_DIM

    def body(sink_ref, q_ref, k_ref, v_ref, tab_ref, da_ref, dproj_in, dq_ref, dk_ref, dv_ref, ds_ref):
        del dproj_in
        g, n = pl.program_id(0), pl.program_id(1)
        q4, tk, (klo, khi), (vlo, vhi), (pe, po), (sink_e, sink_o), own, in_cur = _attn_tiles(
            g, n, S, sink_ref, q_ref, k_ref, v_ref, tab_ref)
        dof = da_ref[...]
        do4 = jnp.concatenate([dof[:, LANES * p:LANES * (p + 1)] for p in range(NQG)], axis=0)

        def grads(p, vv):
            dp = lax.dot_general(vv, do4, _DIMS["nt"], preferred_element_type=F32)
            dp = jnp.concatenate(
                [dp[:N_META], jnp.where(in_cur, dp[N_META + BLOCK:], dp[N_META:N_META + BLOCK])], axis=0)
            delta = jnp.sum(p * dp, axis=0, keepdims=True)
            return _spread(p * (dp - delta) * ATTN_SCALE, in_cur).astype(BF16), delta

        dse, delta_e = grads(pe, vlo)
        dso, delta_o = grads(po, vhi)

        dq4 = (lax.dot_general(dse, klo, _DIMS["tn"], preferred_element_type=F32)
               + lax.dot_general(dso, khi, _DIMS["tn"], preferred_element_type=F32))
        tq = tk[N_META + BLOCK:]
        dq_ref[...] = jnp.concatenate(
            [_rope_t(dq4[BLOCK * p:BLOCK * (p + 1)], tq) for p in range(NQG)], axis=1).astype(BF16)

        lane = lax.broadcasted_iota(jnp.int32, (NKEY, LANES), 1)

        def fold(lo_part, hi_part):
            t = jnp.where(lane < HEAD_DIM, lo_part, hi_part)
            return t + pltpu.roll(t, HEAD_DIM, 1)

        dk = _rope_t(fold(jnp.dot(dse, q4, preferred_element_type=F32),
                          jnp.dot(dso, q4, preferred_element_type=F32)), tk) * own
        dv = fold(jnp.dot(_spread(pe, in_cur).astype(BF16), do4, preferred_element_type=F32),
                  jnp.dot(_spread(po, in_cur).astype(BF16), do4, preferred_element_type=F32)) * own

        @pl.when((n == 0) & (g % 2 == 0))
        def _():
            dk_ref[...] = jnp.zeros_like(dk_ref)
            dv_ref[...] = jnp.zeros_like(dv_ref)

        @pl.when(n == 0)
        def _():
            ds_ref[...] = jnp.zeros_like(ds_ref)

        prev = jnp.maximum(n - 1, 0)
        qrow = pl.ds(pl.multiple_of(n * BLOCK, BLOCK), BLOCK)
        prow = pl.ds(pl.multiple_of(prev * BLOCK, BLOCK), BLOCK)
        mrow = pl.ds(S + META_ROW0, N_META)
        for ref, val in ((dk_ref, dk), (dv_ref, dv)):
            ref[mrow, :] += val[:N_META]
            ref[prow, :] += val[N_META:N_META + BLOCK]
            ref[qrow, :] += val[N_META + BLOCK:]

        srow = lax.broadcasted_iota(jnp.int32, (Q_PER_KV, LANES), 0)
        acc = jnp.zeros((Q_PER_KV, LANES), F32)
        for p in range(NQG):
            for parity, (sk, dl) in enumerate(((sink_e, delta_e), (sink_o, delta_o))):
                val = -jnp.sum(sk[:, BLOCK * p:BLOCK * (p + 1)] * dl[:, BLOCK * p:BLOCK * (p + 1)])
                acc = jnp.where(srow == 2 * p + parity, val, acc)
        ds_ref[0] += acc

    in_specs = _attn_specs(S, ATTN, KVW) + [pl.BlockSpec((BLOCK, gw), lambda g, n: (n, g)), ANY]
    slab = pl.BlockSpec((Tp, LANES), lambda g, n: (0, g // 2))
    return _call(
        order, body, [sinks, proj, proj, proj, tab, da, dproj], name="attn_bwd", grid=(G, nb), in_specs=in_specs,
        out_specs=[pl.BlockSpec((BLOCK, gw), lambda g, n: (n, g)), slab, slab,
                   pl.BlockSpec((1, Q_PER_KV, LANES), lambda g, n: (g, 0, 0))],
        out_shape=[jax.ShapeDtypeStruct(dproj.shape, BF16), jax.ShapeDtypeStruct((Tp, KVW), F32),
                   jax.ShapeDtypeStruct((Tp, KVW), F32), jax.ShapeDtypeStruct((G, Q_PER_KV, LANES), F32)],
        aliases={6: 0}, sem=("arbitrary", "arbitrary"))


def _zero_meta_block(order, Tp, IN):
    tc = _pick(IN, 4096)

    def body(o_ref):
        o_ref[...] = jnp.zeros_like(o_ref)

    return _call(
        order, body, [], name="dproj_zero_meta", grid=(IN // tc,), in_specs=[],
        out_specs=pl.BlockSpec((BLOCK, tc), lambda j: (Tp // BLOCK - 1, j)),
        out_shape=jax.ShapeDtypeStruct((Tp, IN), BF16), sem=("parallel",))


def _put_dkv(order, dk, dv, dproj, ATTN):
    Tp, KVW = dk.shape
    nkb = KVW // LANES
    koff = ATTN // LANES

    def body(dk_ref, dv_ref, dproj_in, o_ref):
        del dproj_in
        t = pl.program_id(0)
        o_ref[...] = jnp.where(t < nkb, dk_ref[...], dv_ref[...]).astype(BF16)

    src = pl.BlockSpec((Tp, LANES), lambda t: (0, t % nkb))
    return _call(
        order, body, [dk, dv, dproj], name="dproj_put_dkv", grid=(2 * nkb,), in_specs=[src, src, ANY],
        out_specs=pl.BlockSpec((Tp, LANES), lambda t: (0, koff + t)),
        out_shape=jax.ShapeDtypeStruct(dproj.shape, BF16), aliases={2: 0}, sem=("parallel",))


HALO = 16


def _window_sums(x, up):
    n = x.shape[0]
    out = []
    s = x
    for k in (1, 2, 4, 8):
        s = s + pltpu.roll(s, (n - k) if up else k, 0)
        out.append(s)
    return out


def _pool_specs(S, ub, gw, tm):
    meta_halo = (S + BLOCK - HALO) // HALO

    def main(g):
        return pl.BlockSpec((tm, gw), lambda i: (i, ub + g))

    def halo(g):
        return pl.BlockSpec((HALO, gw), lambda i: (jnp.where(i == 0, meta_halo, i * (tm // HALO) - 1), ub + g))

    return [main(g) for g in range(4)] + [halo(g) for g in range(4)]


def _pooled(main_refs, halo_refs, g):
    x = jnp.concatenate([halo_refs[g][...], main_refs[g][...]], axis=0)
    s = _window_sums(x, up=False)[g]
    return (s[HALO:] * (1.0 / POOL_WINDOWS[g]) - x[HALO:]).astype(BF16)


def _pool_fwd(order, proj, wgrp, scale, S, uoff, POOL):
    gw = POOL // 4
    tm = BLOCK

    def body(*refs):
        main, halo = refs[:4], refs[4:8]
        w_ref, sc_ref, o_ref = refs[8:]
        for g in range(4):
            mixed = jnp.dot(_pooled(main, halo, g), w_ref[g], preferred_element_type=F32)
            o_ref[:, gw * g:gw * (g + 1)] = (mixed * sc_ref[:, gw * g:gw * (g + 1)]).astype(BF16)

    return _call(
        order, body, [proj] * 8 + [wgrp, scale], name="pool_fwd", grid=(S // tm,),
        in_specs=_pool_specs(S, uoff // gw, gw, tm) + [
            pl.BlockSpec((4, gw, gw), lambda i: (0, 0, 0)), pl.BlockSpec((1, POOL), lambda i: (0, 0))],
        out_specs=pl.BlockSpec((tm, POOL), lambda i: (i, 0)),
        out_shape=jax.ShapeDtypeStruct((S, POOL), BF16), sem=("parallel",))


def _pool_bwd_mix(order, proj, wgrp, scale, dps, S, uoff, POOL):
    gw = POOL // 4
    tm = BLOCK

    def body(*refs):
        main, halo = refs[:4], refs[4:8]
        w_ref, sc_ref, dps_ref, dpl_ref, dw_ref, dsc_ref = refs[8:]
        i = pl.program_id(0)

        @pl.when(i == 0)
        def _():
            dw_ref[...] = jnp.zeros_like(dw_ref)
            dsc_ref[...] = jnp.zeros_like(dsc_ref)

        for g in range(4):
            cols = slice(gw * g, gw * (g + 1))
            pooled = _pooled(main, halo, g)
            mixed = jnp.dot(pooled, w_ref[g], preferred_element_type=F32)
            dps_g = dps_ref[:, cols]
            dsc_ref[:, cols] += jnp.sum(dps_g * mixed, axis=0, keepdims=True)
            dms = (dps_g * sc_ref[:, cols]).astype(BF16)
            dw_ref[g] += lax.dot_general(pooled, dms, _DIMS["tn"], preferred_element_type=F32)
            dpl_ref[:, cols] = lax.dot_general(dms, w_ref[g], _DIMS["nt"], preferred_element_type=F32)

    row = pl.BlockSpec((tm, POOL), lambda i: (i, 0))
    return _call(
        order, body, [proj] * 8 + [wgrp, scale, dps], name="pool_bwd_mix", grid=(S // tm,),
        in_specs=_pool_specs(S, uoff // gw, gw, tm) + [
            pl.BlockSpec((4, gw, gw), lambda i: (0, 0, 0)), pl.BlockSpec((1, POOL), lambda i: (0, 0)), row],
        out_specs=[row, pl.BlockSpec((4, gw, gw), lambda i: (0, 0, 0)), pl.BlockSpec((1, POOL), lambda i: (0, 0))],
        out_shape=[jax.ShapeDtypeStruct((S, POOL), F32), jax.ShapeDtypeStruct((4, gw, gw), F32),
                   jax.ShapeDtypeStruct((1, POOL), F32)],
        sem=("arbitrary",))


def _pool_bwd_window(order, dpl, dproj, S, uoff, POOL):
    gw = POOL // 4
    nb = S // BLOCK
    ub = uoff // gw

    def body(main_ref, halo_ref, dproj_in, o_ref):
        del dproj_in
        b, g = pl.program_id(0), pl.program_id(1)
        main = jnp.where(b < nb, main_ref[...], 0.0)
        halo = jnp.where(b == nb - 1, 0.0, halo_ref[...])
        sums = _window_sums(jnp.concatenate([main, halo], axis=0), up=True)
        du = jnp.zeros((BLOCK, gw), F32)
        for k, w in enumerate(POOL_WINDOWS):
            du = jnp.where(g == k, sums[k][:BLOCK] * (1.0 / w), du)
        du = du - main
        row = lax.broadcasted_iota(jnp.int32, du.shape, 0)
        first_valid = jnp.where(b == nb, META_ROW0, 0)
        o_ref[...] = jnp.where(row >= first_valid, du, 0.0).astype(BF16)

    return _call(
        order, body, [dpl, dpl, dproj], name="pool_bwd_window", grid=(nb + 1, 4),
        in_specs=[pl.BlockSpec((BLOCK, gw), lambda b, g: (jnp.minimum(b, nb - 1), g)),
                  pl.BlockSpec((HALO, gw), lambda b, g: (
                      jnp.where(b == nb, 0, jnp.minimum((b + 1) * (BLOCK // HALO), S // HALO - 1)), g)),
                  ANY],
        out_specs=pl.BlockSpec((BLOCK, gw), lambda b, g: (b, ub + g)),
        out_shape=jax.ShapeDtypeStruct(dproj.shape, BF16), aliases={2: 0}, sem=("parallel", "parallel"))


def _sigmoid(x):
    return 1.0 / (1.0 + jnp.exp(-x))


def _gate_tiles(S, D, goff):
    tc = 512
    while goff % tc or D % tc:
        tc //= 2
    return _pick(S, 512, 8), tc


def _gate_mix(order, proj, bgate, a_out, p_out, S, D, goff):
    tm, tc = _gate_tiles(S, D, goff)
    g0b, nd = goff // tc, D // tc

    def body(l0_ref, l1_ref, b_ref, a_ref, p_ref, o_ref):
        g0 = _sigmoid(l0_ref[...] + b_ref[0:1, :])
        g1 = _sigmoid(l1_ref[...] + b_ref[1:2, :])
        o_ref[...] = (g0 * a_ref[...] + g1 * p_ref[...]).astype(BF16)

    tile = pl.BlockSpec((tm, tc), lambda i, j: (i, j))
    return _call(
        order, body, [proj, proj, bgate, a_out, p_out], name="gate_mix", grid=(S // tm, nd),
        in_specs=[pl.BlockSpec((tm, tc), lambda i, j: (i, g0b + j)),
                  pl.BlockSpec((tm, tc), lambda i, j: (i, g0b + nd + j)),
                  pl.BlockSpec((2, tc), lambda i, j: (0, j)), tile, tile],
        out_specs=tile, out_shape=jax.ShapeDtypeStruct((S, D), BF16), sem=("parallel", "parallel"))


def _gate_bwd(order, proj, bgate, a_out, p_out, dmixed, dproj, S, D, goff):
    tm, tc = _gate_tiles(S, D, goff)
    g0b, nd, ni = goff // tc, D // tc, S // tm
    nsteps = nd * ni

    def body(l0_ref, l1_ref, b_ref, a_ref, p_ref, dm_ref, dproj_in, dap_ref, dproj_ref, db_ref, buf, sems):
        del dproj_in
        j, i = pl.program_id(0), pl.program_id(1)
        step = j * ni + i
        slot = step % 2

        def put(sl, br):
            col = pl.multiple_of((g0b + br * nd + j) * tc, tc)
            return pltpu.make_async_copy(
                buf.at[sl, br], dproj_ref.at[pl.ds(pl.multiple_of(i * tm, tm), tm), pl.ds(col, tc)], sems.at[sl, br])

        @pl.when(step >= 2)
        def _():
            put(slot, 0).wait()
            put(slot, 1).wait()

        @pl.when(i == 0)
        def _():
            db_ref[...] = jnp.zeros_like(db_ref)

        dm = dm_ref[...]
        for br, (l_ref, val_ref) in enumerate(((l0_ref, a_ref), (l1_ref, p_ref))):
            gate = _sigmoid(l_ref[...] + b_ref[br:br + 1, :])
            dap_ref[br] = (dm * gate).astype(BF16)
            dl = dm * val_ref[...] * gate * (1.0 - gate)
            buf[slot, br] = dl.astype(BF16)
            db_ref[br] += jnp.sum(dl, axis=0, keepdims=True)
            put(slot, br).start()

        @pl.when(step == nsteps - 1)
        def _():
            for sl in ((slot, 1 - slot) if nsteps > 1 else (slot,)):
                put(sl, 0).wait()
                put(sl, 1).wait()

    tile = pl.BlockSpec((tm, tc), lambda j, i: (i, j))
    return _call(
        order, body, [proj, proj, bgate, a_out, p_out, dmixed, dproj], name="gate_bwd", grid=(nd, ni),
        in_specs=[pl.BlockSpec((tm, tc), lambda j, i: (i, g0b + j)),
                  pl.BlockSpec((tm, tc), lambda j, i: (i, g0b + nd + j)),
                  pl.BlockSpec((2, tc), lambda j, i: (0, j)), tile, tile, tile, ANY],
        out_specs=[pl.BlockSpec((2, tm, tc), lambda j, i: (0, i, j)), ANY,
                   pl.BlockSpec((2, 1, tc), lambda j, i: (0, 0, j))],
        out_shape=[jax.ShapeDtypeStruct((2, S, D), BF16), jax.ShapeDtypeStruct(dproj.shape, BF16),
                   jax.ShapeDtypeStruct((2, 1, D), F32)],
        scratch=[pltpu.VMEM((2, 2, tm, tc), BF16), pltpu.SemaphoreType.DMA((2, 2))],
        aliases={6: 1}, sem=("arbitrary", "arbitrary"))


def _ffn_in_swiglu(order, h, wT, FF):
    S, D = h.shape
    tm, tn = _pick(S, 1024), _pick(FF, 256)
    nj = FF // tn

    def body(h_ref, wg_ref, wu_ref, f_ref, act_ref):
        hb = h_ref[...]
        gt = lax.dot_general(hb, wg_ref[...], _DIMS["nt"], preferred_element_type=F32)
        up = lax.dot_general(hb, wu_ref[...], _DIMS["nt"], preferred_element_type=F32)
        f_ref[0] = gt
        f_ref[1] = up
        act_ref[...] = (gt * _sigmoid(gt) * up).astype(BF16)

    return _call(
        order, body, [h, wT, wT], name="ffn_in", grid=(S // tm, nj),
        in_specs=[pl.BlockSpec((tm, D), lambda i, j: (i, 0)), pl.BlockSpec((tn, D), lambda i, j: (j, 0)),
                  pl.BlockSpec((tn, D), lambda i, j: (nj + j, 0))],
        out_specs=[pl.BlockSpec((2, tm, tn), lambda i, j: (0, i, j)), pl.BlockSpec((tm, tn), lambda i, j: (i, j))],
        out_shape=[jax.ShapeDtypeStruct((2, S, FF), F32), jax.ShapeDtypeStruct((S, FF), BF16)],
        sem=("parallel", "parallel"))


def _d_act_swiglu(order, dy, wdown, f):
    S, D = dy.shape
    FF = wdown.shape[0]
    tm, tn = _pick(S, 1024), _pick(FF, 256)

    def body(dy_ref, w_ref, f_ref, o_ref):
        d = lax.dot_general(dy_ref[...], w_ref[...], _DIMS["nt"], preferred_element_type=F32)
        gt, up = f_ref[0], f_ref[1]
        s = _sigmoid(gt)
        o_ref[0] = (d * up * s * (1.0 + gt * (1.0 - s))).astype(BF16)
        o_ref[1] = (d * gt * s).astype(BF16)

    pair = pl.BlockSpec((2, tm, tn), lambda i, j: (0, i, j))
    return _call(
        order, body, [dy, wdown, f], name="d_act", grid=(S // tm, FF // tn),
        in_specs=[pl.BlockSpec((tm, D), lambda i, j: (i, 0)), pl.BlockSpec((tn, D), lambda i, j: (j, 0)), pair],
        out_specs=pair, out_shape=jax.ShapeDtypeStruct((2, S, FF), BF16), sem=("parallel", "parallel"))


def _place():
    return lax.axis_index("x"), lax.axis_index("y"), lax.axis_index("c")


def _xfer_start(order, name, bufs, copies):
    nb = len(bufs)
    n = len(copies([None] * nb, None))
    is_new = [isinstance(b, jax.ShapeDtypeStruct) for b in bufs]
    old = [b for b, fresh in zip(bufs, is_new) if not fresh]
    no = len(old)
    tok = [] if any(order.last is b for b in old) else [order.last]
    first_out = no + len(tok)

    def body(*refs):
        send, recv = refs[first_out:first_out + n], refs[first_out + n:first_out + 2 * n]
        token = refs[-1]
        given, made = iter(refs[:no]), iter(refs[first_out + 2 * n + no:-1])
        logical = [next(made) if fresh else next(given) for fresh in is_new]
        for i, (src, dst, dev) in enumerate(copies(logical, _place())):
            pltpu.make_async_remote_copy(src_ref=src, dst_ref=dst, send_sem=send[i], recv_sem=recv[i],
                                         device_id=dev, device_id_type=MESH).start()
        token[...] = jnp.zeros_like(token)

    fresh_shapes = [b for b, fresh in zip(bufs, is_new) if fresh]
    out = pl.pallas_call(
        body, name=name,
        out_shape=tuple([pltpu.SemaphoreType.DMA(())] * (2 * n)
                        + [pltpu.HBM(b.shape, b.dtype) for b in old + fresh_shapes]
                        + [jax.ShapeDtypeStruct((8, LANES), F32)]),
        in_specs=[HBM] * no + [ANY] * len(tok),
        out_specs=tuple([SEM] * (2 * n) + [HBM] * nb + [pl.BlockSpec(memory_space=pltpu.VMEM)]),
        input_output_aliases={i: 2 * n + i for i in range(no)},
        compiler_params=pltpu.CompilerParams(has_side_effects=EFFECT),
    )(*[pltpu.with_memory_space_constraint(b, pltpu.HBM) for b in old], *tok)
    order.last = out[-1]
    thru, made = iter(out[2 * n:2 * n + no]), iter(out[2 * n + no:2 * n + nb])
    return list(out[:2 * n]), [next(made) if fresh else next(thru) for fresh in is_new]


def _xfer_wait(order, name, sems, bufs, copies):
    nb = len(bufs)
    n = len(sems) // 2
    tok = order.last

    def body(*refs):
        send, recv = refs[nb:nb + n], refs[nb + n:nb + 2 * n]
        token = refs[-1]
        for i, (src, dst, dev) in enumerate(copies(refs[:nb], _place())):
            cp = pltpu.make_async_remote_copy(src_ref=src, dst_ref=dst, send_sem=send[i], recv_sem=recv[i],
                                              device_id=dev, device_id_type=MESH)
            cp.wait_send()
            cp.wait_recv()
        token[...] = jnp.zeros_like(token)

    out = pl.pallas_call(
        body, name=name,
        out_shape=tuple([pltpu.HBM(b.shape, b.dtype) for b in bufs] + [jax.ShapeDtypeStruct((8, LANES), F32)]),
        in_specs=[HBM] * nb + [SEM] * (2 * n) + [ANY],
        out_specs=tuple([HBM] * nb + [pl.BlockSpec(memory_space=pltpu.VMEM)]),
        input_output_aliases={i: i for i in range(nb)},
        compiler_params=pltpu.CompilerParams(has_side_effects=EFFECT),
    )(*bufs, *sems, tok)
    order.last = out[-1]
    return list(out[:nb])


class _Xfer:
    def __init__(self, name, bufs, copies):
        self.name, self.bufs, self.copies = name, list(bufs), copies
        self.sems = None

    def start(self, order):
        self.sems, self.bufs = _xfer_start(order, self.name + "_start", self.bufs, self.copies)

    def wait(self, order):
        self.bufs = _xfer_wait(order, self.name + "_wait", self.sems, self.bufs, self.copies)
        return self.bufs


def _block_rows(ref, r, d):
    return ref.at[pl.ds(d * r, r)]


def _gather_send(fulls):
    def copies(refs, place):
        out = []
        for w, full in enumerate(fulls):
            r = full.shape[0] // 8
            if place is None:
                out += [None] * 4
                continue
            x, y, c = place
            mine = _block_rows(refs[w], r, 4 * x + 2 * y + c)
            out.append((mine, mine, (x, y, 1 - c)))
            for px, py in ((1 - x, y), (x, 1 - y), (1 - x, 1 - y)):
                out.append((mine, mine, (px, py, c)))
        return out
    return copies


def _gather_forward(fulls):
    def copies(refs, place):
        out = []
        for w, full in enumerate(fulls):
            r = full.shape[0] // 8
            if place is None:
                out += [None] * 3
                continue
            x, y, c = place
            for px, py in ((1 - x, y), (x, 1 - y), (1 - x, 1 - y)):
                blk = _block_rows(refs[w], r, 4 * px + 2 * py + c)
                out.append((blk, blk, (x, y, 1 - c)))
        return out
    return copies


def _pair_send(nw):
    def copies(refs, place):
        out = []
        for w in range(nw):
            if place is None:
                out += [None] * 4
                continue
            x, y, c = place
            grad, other = refs[2 * w], refs[2 * w + 1]
            r = other.shape[1]
            for k in range(4):
                out.append((_block_rows(grad, r, 2 * k + 1 - c), other.at[k], (x, y, 1 - c)))
        return out
    return copies


def _chip_send(nw):
    def copies(refs, place):
        out = []
        for w in range(nw):
            if place is None:
                out += [None] * 3
                continue
            x, y, c = place
            psum, parts = refs[2 * w], refs[2 * w + 1]
            for px, py in ((1 - x, y), (x, 1 - y), (1 - x, 1 - y)):
                out.append((psum.at[2 * px + py], parts.at[2 * x + y], (px, py, c)))
        return out
    return copies


def _dev_index():
    x, y, c = _place()
    return 4 * x + 2 * y + c


def _place_own(order, shard, name):
    r, cols = shard.shape
    tr = _pick(r, max(16, (12 << 20) // (4 * cols)), 16)
    nr = r // tr

    def body(s_ref, o_ref):
        o_ref[...] = s_ref[...].astype(BF16)

    return _call(
        order, body, [shard], name=name, grid=(nr,),
        in_specs=[pl.BlockSpec((tr, cols), lambda i: (i, 0))],
        out_specs=pl.BlockSpec((tr, cols), lambda i: (_dev_index() * nr + i, 0)),
        out_shape=jax.ShapeDtypeStruct((8 * r, cols), BF16), sem=("parallel",))


def _pair_sum(order, grad, other, name):
    r, cols = other.shape[1:]
    tr = _pick(r, max(16, (7 << 20) // (2 * cols)), 16)
    nr = r // tr

    def body(g_ref, a_ref, o_ref):
        o_ref[...] = (g_ref[...].astype(F32) + a_ref[...].astype(F32)).astype(BF16)

    blk = pl.BlockSpec((None, tr, cols), lambda k, i: (k, i, 0))
    return _call(
        order, body, [grad, other], name=name, grid=(4, nr),
        in_specs=[pl.BlockSpec((tr, cols), lambda k, i: ((2 * k + lax.axis_index("c")) * nr + i, 0)), blk],
        out_specs=blk, out_shape=jax.ShapeDtypeStruct(other.shape, BF16), sem=("parallel", "parallel"))


def _chip_sum(order, psum, parts, name):
    _, r, cols = parts.shape
    tr = _pick(r, max(16, (1 << 20) // (2 * cols)), 16)

    def my_chip():
        return 2 * lax.axis_index("x") + lax.axis_index("y")

    def body(own_ref, p0, p1, p2, p3, o_ref):
        own = own_ref[...].astype(F32)
        acc = None
        for k, p in enumerate((p0, p1, p2, p3)):
            term = jnp.where(my_chip() == k, own, p[...].astype(F32))
            acc = term if acc is None else acc + term
        o_ref[...] = acc

    def slot(k):
        return pl.BlockSpec((None, tr, cols), lambda i: (jnp.where(my_chip() == k, (k + 1) % 4, k), i, 0))

    return _call(
        order, body, [psum, parts, parts, parts, parts], name=name, grid=(r // tr,),
        in_specs=[pl.BlockSpec((None, tr, cols), lambda i: (my_chip(), i, 0))] + [slot(k) for k in range(4)],
        out_specs=pl.BlockSpec((tr, cols), lambda i: (i, 0)),
        out_shape=jax.ShapeDtypeStruct((r, cols), F32), sem=("parallel",))


def _all_reduce_small(order, pack, name):
    R = pack.shape[0]

    def body(p_ref, o_ref, buf, send_sems, recv_sems):
        x, y, c = _place()
        me = 4 * x + 2 * y + c
        buf[me] = p_ref[...]
        copies = []
        for k in range(1, 8):
            px = 1 - x if k & 4 else x
            py = 1 - y if k & 2 else y
            pc = 1 - c if k & 1 else c
            cp = pltpu.make_async_remote_copy(
                src_ref=p_ref, dst_ref=buf.at[me], send_sem=send_sems.at[k - 1], recv_sem=recv_sems.at[k - 1],
                device_id=(px, py, pc), device_id_type=MESH)
            cp.start()
            copies.append(cp)
        for cp in copies:
            cp.wait_recv()
        acc = buf[0]
        for d in range(1, 8):
            acc = acc + buf[d]
        o_ref[...] = acc
        for cp in copies:
            cp.wait_send()

    vm = pl.BlockSpec(memory_space=pltpu.VMEM)
    return _call(
        order, body, [pack], name=name, in_specs=[vm], out_specs=vm,
        out_shape=jax.ShapeDtypeStruct((R, LANES), F32),
        scratch=[pltpu.VMEM((8, R, LANES), F32), pltpu.SemaphoreType.DMA((7,)), pltpu.SemaphoreType.DMA((7,))])


def _pack(parts):
    flat = []
    for p in parts:
        v = p.reshape(-1).astype(F32)
        flat.append(jnp.pad(v, (0, (-v.shape[0]) % LANES)))
    v = jnp.concatenate(flat)
    v = jnp.pad(v, (0, (-v.shape[0]) % (8 * LANES)))
    return v.reshape(-1, LANES)


def _unpack(pack, shapes):
    v = pack.reshape(-1)
    out, off = [], 0
    for s in shapes:
        n = 1
        for d in s:
            n *= d
        out.append(v[off:off + n].reshape(s))
        off += n + (-n) % LANES
    return out


def _adamw(order, w, g, m, v, name):
    shape = w.shape
    cols = shape[-1]
    w2, g2, m2, v2 = (t.reshape(-1, cols) for t in (w, g, m, v))
    R = w2.shape[0]
    tr = _pick(R, max(8, (1 << 20) // (4 * cols)), 8)

    def body(w_ref, g_ref, m_ref, v_ref, d_ref, mo_ref, vo_ref):
        d_ref[...], mo_ref[...], vo_ref[...] = _adam_math(w_ref[...], g_ref[...], m_ref[...], v_ref[...])

    blk = pl.BlockSpec((tr, cols), lambda i: (i, 0))
    outs = _call(
        order, body, [w2, g2, m2, v2], name=name, grid=(R // tr,), in_specs=[blk] * 4, out_specs=[blk] * 3,
        out_shape=[jax.ShapeDtypeStruct((R, cols), F32)] * 3, sem=("parallel",))
    return tuple(o.reshape(shape) for o in outs)


def _adam_math(w, g, m, v):
    mn = ADAM_B1 * m + (1.0 - ADAM_B1) * g
    vn = ADAM_B2 * v + (1.0 - ADAM_B2) * (g * g)
    m_hat = mn / (1.0 - ADAM_B1 ** ADAM_STEP)
    v_hat = vn / (1.0 - ADAM_B2 ** ADAM_STEP)
    return -ADAM_LR * (m_hat / (jnp.sqrt(v_hat) + ADAM_EPS) + ADAM_WD * w), mn, vn


def _chip_sum_adamw(order, w, psum, parts, m, v, name):
    _, r, cols = parts.shape
    tr = _pick(r, max(16, (6 << 20) // (38 * cols)), 16)

    def my_chip():
        return 2 * lax.axis_index("x") + lax.axis_index("y")

    def body(w_ref, own_ref, p0, p1, p2, p3, m_ref, v_ref, g_ref, d_ref, mo_ref, vo_ref):
        own = own_ref[...].astype(F32)
        g = None
        for k, p in enumerate((p0, p1, p2, p3)):
            term = jnp.where(my_chip() == k, own, p[...].astype(F32))
            g = term if g is None else g + term
        g_ref[...] = g
        d_ref[...], mo_ref[...], vo_ref[...] = _adam_math(w_ref[...], g, m_ref[...], v_ref[...])

    def slot(k):
        return pl.BlockSpec((None, tr, cols), lambda i: (jnp.where(my_chip() == k, (k + 1) % 4, k), i, 0))

    blk = pl.BlockSpec((tr, cols), lambda i: (i, 0))
    return _call(
        order, body, [w, psum, parts, parts, parts, parts, m, v], name=name, grid=(r // tr,),
        in_specs=[blk, pl.BlockSpec((None, tr, cols), lambda i: (my_chip(), i, 0))]
        + [slot(k) for k in range(4)] + [blk, blk],
        out_specs=[blk] * 4, out_shape=[jax.ShapeDtypeStruct((r, cols), F32)] * 4, sem=("parallel",))


class _GradReduce:
    def __init__(self, tag, grads, names):
        self.tag, self.grads, self.names = tag, list(grads), names
        self.pair = self.chip = self.psums = None

    def pair_start(self, order):
        bufs = []
        for g in self.grads:
            bufs += [g, jax.ShapeDtypeStruct((4, g.shape[0] // 8, g.shape[1]), g.dtype)]
        self.pair = _Xfer("pair_" + self.tag, bufs, _pair_send(len(self.grads)))
        self.pair.start(order)

    def pair_sum_chip_start(self, order):
        bufs = self.pair.wait(order)
        self.psums = [_pair_sum(order, bufs[2 * w], bufs[2 * w + 1], "pair_sum_" + nm)
                      for w, nm in enumerate(self.names)]
        cbufs = []
        for p in self.psums:
            cbufs += [p, jax.ShapeDtypeStruct(p.shape, p.dtype)]
        self.chip = _Xfer("chip_" + self.tag, cbufs, _chip_send(len(self.psums)))
        self.chip.start(order)

    def finish(self, order):
        bufs = self.chip.wait(order)
        return [(bufs[2 * w], bufs[2 * w + 1]) for w in range(len(self.names))]


def kernel(x, meta_tokens, ln_in_g, ln_in_b, w_in, b_gate, attn_sinks, w_attn_up, w_pool_grp, pool_scale, w_pool_up, w_out, ln1_g, ln1_b, w_ffn_in, w_ffn_down, ln2_g, ln2_b, loss_target, m_meta_tokens, m_ln_in_g, m_ln_in_b, m_w_in, m_b_gate, m_attn_sinks, m_w_attn_up, m_w_pool_grp, m_pool_scale, m_w_pool_up, m_w_out, m_ln1_g, m_ln1_b, m_w_ffn_in, m_w_ffn_down, m_ln2_g, m_ln2_b, v_meta_tokens, v_ln_in_g, v_ln_in_b, v_w_in, v_b_gate, v_attn_sinks, v_w_attn_up, v_w_pool_grp, v_pool_scale, v_w_pool_up, v_w_out, v_ln1_g, v_ln1_b, v_w_ffn_in, v_w_ffn_down, v_ln2_g, v_ln2_b):
    S, D = x.shape[1], x.shape[2]
    Tp = S + BLOCK
    NQ = attn_sinks.shape[-1]
    ATTN = NQ * HEAD_DIM
    KVW = ATTN // Q_PER_KV
    POOL = pool_scale.shape[-1]
    IN = 8 * w_in.shape[2]
    FF = 8 * w_ffn_down.shape[1]
    uoff = ATTN + 2 * KVW
    goff = uoff + POOL
    gw = POOL // 4
    dcols = D // 8
    assert IN == goff + 2 * D and w_ffn_in.shape[2] * 8 == 2 * FF

    xi, yi, ci = _place()
    dev = 4 * xi + 2 * yi + ci
    x2, tgt = x[0], loss_target[0]
    order = _Order()

    def place_cols(a):
        return lax.dynamic_update_slice(jnp.zeros(a.shape[:-1] + (D,), F32), a, (0,) * (a.ndim - 1) + (dev * dcols,))

    small = _all_reduce_small(order, _pack([place_cols(meta_tokens), place_cols(b_gate[0])]), "small_inputs_gather")
    meta_full, bgate_full = _unpack(small, [(N_META, D), (2, D)])
    meta_pad = jnp.pad(meta_full, ((META_ROW0, 0), (0, 0)))

    wgrp_rows = w_pool_grp[0].reshape(4 * (gw // 8), gw)
    full_in = _place_own(order, w_in[0].T, "own_w_in")
    ag_in = _Xfer("gather_w_in", [full_in], _gather_send([full_in]))
    ag_in.start(order)
    mix_names = ["w_attn_up", "w_pool_grp", "w_pool_up", "w_out"]
    mix_shards = [w_attn_up[0].T, wgrp_rows, w_pool_up[0].T, w_out[0]]
    full_mix = [_place_own(order, s, "own_" + nm) for s, nm in zip(mix_shards, mix_names)]
    full_ffn = _place_own(order, w_ffn_in[0].T, "own_w_ffn_in")

    ln_in_g2, ln_in_b2 = ln_in_g.reshape(1, D), ln_in_b.reshape(1, D)
    tab = _rope_table(S)

    h0, h0b = _ln_in_fwd(order, x2, meta_pad, ln_in_g2, ln_in_b2)
    (full_in,) = ag_in.wait(order)
    fw_in = _Xfer("forward_w_in", [full_in], _gather_forward([full_in]))
    fw_in.start(order)
    ag_mix = _Xfer("gather_mixers", full_mix, _gather_send(full_mix))
    ag_mix.start(order)
    ag_ffn = _Xfer("gather_w_ffn_in", [full_ffn], _gather_send([full_ffn]))
    ag_ffn.start(order)
    full_down = _place_own(order, w_ffn_down[0], "own_w_ffn_down")
    (winT,) = fw_in.wait(order)
    proj = _mm(order, h0b, winT, kind="nt", out_dtype=F32, tm=1408, tn=512, name="proj")

    full_mix = ag_mix.wait(order)
    ag_down = _Xfer("gather_w_ffn_down", [full_down], _gather_send([full_down]))
    ag_down.start(order)
    fw_mix = _Xfer("forward_mixers", full_mix, _gather_forward(full_mix))
    fw_mix.start(order)
    att = _attn_fwd(order, proj, tab, attn_sinks, S, ATTN, KVW)
    wattT, wgrp_g, wpupT, wout = fw_mix.wait(order)
    wgrp = wgrp_g.reshape(8, 4, gw // 8, gw).transpose(1, 0, 2, 3).reshape(4, gw, gw)

    ps = _pool_fwd(order, proj, wgrp, pool_scale, S, uoff, POOL)
    a_out = _mm(order, att, wattT, kind="nt", out_dtype=F32, tm=1024, tn=1024, name="attn_up")
    p_out = _mm(order, ps, wpupT, kind="nt", out_dtype=F32, tm=1024, tn=1024, name="pool_up")
    mixed = _gate_mix(order, proj, bgate_full, a_out, p_out, S, D, goff)
    y1 = _mm(order, mixed, wout, kind="nn", out_dtype=F32, tm=1024, tn=1024, name="out_proj")

    (full_ffn,) = ag_ffn.wait(order)
    fw_ffn = _Xfer("forward_w_ffn_in", [full_ffn], _gather_forward([full_ffn]))
    fw_ffn.start(order)
    h1, h1b = _ln1_fwd(order, h0, y1, ln1_g, ln1_b)
    (wffnT,) = fw_ffn.wait(order)
    f, act = _ffn_in_swiglu(order, h1b, wffnT, FF)

    (full_down,) = ag_down.wait(order)
    fw_down = _Xfer("forward_w_ffn_down", [full_down], _gather_forward([full_down]))
    fw_down.start(order)
    (wdown,) = fw_down.wait(order)
    y2 = _mm(order, act, wdown, kind="nn", out_dtype=F32, tm=1024, tn=1024, tk=5504, name="ffn_down")

    dz2, dz2b, dg2, db2, loss_part = _ln2_loss_bwd(order, h1, y2, tgt, ln2_g, ln2_b)
    gwdown = _mm(order, act, dz2b, kind="tn", out_dtype=BF16, tm=256, tn=2048, name="d_ffn_down")
    rs_down = _GradReduce("w_ffn_down", [gwdown], ["w_ffn_down"])
    rs_down.pair_start(order)
    df = _d_act_swiglu(order, dz2b, wdown, f)
    rs_down.pair_sum_chip_start(order)
    gwffnT = _mm(order, df, h1b, kind="tn", out_dtype=BF16, tm=256, tn=2048, name="d_ffn_in", a_lead="halves")
    rs_ffn = _GradReduce("w_ffn_in", [gwffnT], ["w_ffn_in"])
    rs_ffn.pair_start(order)
    dh1 = _mm(order, df, wffnT, kind="nn", out_dtype=F32, tm=1024, tn=1024, tk=5504, name="d_h1", a_lead="halves")
    rs_ffn.pair_sum_chip_start(order)
    dz1, dz1b, dg1, db1 = _ln1_bwd(order, h0, y1, ln1_g, dh1, dz2)
    gwout = _mm(order, mixed, dz1b, kind="tn", out_dtype=BF16, tm=512, tn=1024, name="d_out_proj")
    rs_out = _GradReduce("w_out", [gwout], ["w_out"])
    rs_out.pair_start(order)
    dmixed = _mm(order, dz1b, wout, kind="nt", out_dtype=F32, tm=1024, tn=1024, name="d_mixed")
    rs_out.pair_sum_chip_start(order)

    dproj = _zero_meta_block(order, Tp, IN)
    dap, dproj, dbgate = _gate_bwd(order, proj, bgate_full, a_out, p_out, dmixed, dproj, S, D, goff)
    gwattT = _mm(order, dap, att, kind="tn", out_dtype=BF16, tm=512, tn=1024, name="d_attn_up", a_lead=0)
    datt = _mm(order, dap, wattT, kind="nn", out_dtype=BF16, tm=1024, tn=1024, name="d_att", a_lead=0)
    gwpupT = _mm(order, dap, ps, kind="tn", out_dtype=BF16, tm=512, tn=1024, name="d_pool_up", a_lead=1)
    dps = _mm(order, dap, wpupT, kind="nn", out_dtype=F32, tm=1024, tn=1024, name="d_ps", a_lead=1)
    dpl, gwgrp, dscale = _pool_bwd_mix(order, proj, wgrp, pool_scale, dps, S, uoff, POOL)
    gwgrp_rows = gwgrp.reshape(4, 8, gw // 8, gw).transpose(1, 0, 2, 3).reshape(8 * 4 * (gw // 8), gw).astype(BF16)
    rs_mix = _GradReduce("mixers", [gwattT, gwgrp_rows, gwpupT], ["w_attn_up", "w_pool_grp", "w_pool_up"])
    rs_mix.pair_start(order)
    dproj = _pool_bwd_window(order, dpl, dproj, S, uoff, POOL)
    rs_mix.pair_sum_chip_start(order)
    dproj, dk, dv, dsink = _attn_bwd(order, proj, tab, attn_sinks, datt, dproj, S, ATTN, KVW)
    dproj = _put_dkv(order, dk, dv, dproj, ATTN)

    weights = dict(meta_tokens=meta_tokens, ln_in_g=ln_in_g, ln_in_b=ln_in_b, w_in=w_in, b_gate=b_gate,
                   attn_sinks=attn_sinks, w_attn_up=w_attn_up, w_pool_grp=w_pool_grp, pool_scale=pool_scale,
                   w_pool_up=w_pool_up, w_out=w_out, ln1_g=ln1_g, ln1_b=ln1_b, w_ffn_in=w_ffn_in,
                   w_ffn_down=w_ffn_down, ln2_g=ln2_g, ln2_b=ln2_b)
    ms = dict(meta_tokens=m_meta_tokens, ln_in_g=m_ln_in_g, ln_in_b=m_ln_in_b, w_in=m_w_in, b_gate=m_b_gate,
              attn_sinks=m_attn_sinks, w_attn_up=m_w_attn_up, w_pool_grp=m_w_pool_grp, pool_scale=m_pool_scale,
              w_pool_up=m_w_pool_up, w_out=m_w_out, ln1_g=m_ln1_g, ln1_b=m_ln1_b, w_ffn_in=m_w_ffn_in,
              w_ffn_down=m_w_ffn_down, ln2_g=m_ln2_g, ln2_b=m_ln2_b)
    vs = dict(meta_tokens=v_meta_tokens, ln_in_g=v_ln_in_g, ln_in_b=v_ln_in_b, w_in=v_w_in, b_gate=v_b_gate,
              attn_sinks=v_attn_sinks, w_attn_up=v_w_attn_up, w_pool_grp=v_w_pool_grp, pool_scale=v_pool_scale,
              w_pool_up=v_w_pool_up, w_out=v_w_out, ln1_g=v_ln1_g, ln1_b=v_ln1_b, w_ffn_in=v_w_ffn_in,
              w_ffn_down=v_w_ffn_down, ln2_g=v_ln2_g, ln2_b=v_ln2_b)
    grads, deltas, new_ms, new_vs = {}, {}, {}, {}

    def update(nm, g):
        g = g.reshape(weights[nm].shape)
        grads[nm] = g
        deltas[nm], new_ms[nm], new_vs[nm] = _adamw(order, weights[nm], g, ms[nm], vs[nm], "adamw_" + nm)

    def update_reduced(nm, bufs, transposed=False):
        psum, parts = bufs
        if transposed:
            to2d, back = (lambda t: t[0].T), (lambda t: t.T[None])
        else:
            to2d, back = (lambda t: t.reshape(parts.shape[1:])), (lambda t: t.reshape(weights[nm].shape))
        outs = _chip_sum_adamw(order, to2d(weights[nm]), psum, parts, to2d(ms[nm]), to2d(vs[nm]), "adamw_" + nm)
        grads[nm], deltas[nm], new_ms[nm], new_vs[nm] = (back(t) for t in outs)

    gwinT = _mm(order, dproj, h0b, kind="tn", out_dtype=BF16, tm=512, tn=1024, name="d_w_in")
    rs_in = _GradReduce("w_in", [gwinT], ["w_in"])
    rs_in.pair_start(order)
    update_reduced("w_ffn_down", rs_down.finish(order)[0])
    rs_in.pair_sum_chip_start(order)
    dh0 = _mm(order, dproj, winT, kind="nn", out_dtype=F32, tm=1408, tn=1024, tk=2560, name="d_h0")
    dxin, dg_in, db_in = _ln_in_bwd(order, x2, meta_pad, ln_in_g2, dh0, dz1)
    grad_x = dxin[:S][None]
    dmeta = dxin[S + META_ROW0:]

    small_shapes = [(D,), (D,), (1, D), (1, D), (1, D), (1, D), (1, POOL), (1, NQ), (), (N_META, D), (2, D)]
    red = _all_reduce_small(order, _pack([dg_in, db_in, dg1, db1, dg2, db2, dscale, dsink[:, :, 0], loss_part,
                                          dmeta, dbgate]), "small_grads_all_reduce")

    update_reduced("w_ffn_in", rs_ffn.finish(order)[0], transposed=True)
    update_reduced("w_out", rs_out.finish(order)[0])
    b_att, b_grp, b_pup = rs_mix.finish(order)
    update("w_attn_up", _chip_sum(order, *b_att, "chip_sum_w_attn_up").T)
    update_reduced("w_pool_grp", b_grp)
    update("w_pool_up", _chip_sum(order, *b_pup, "chip_sum_w_pool_up").T)

    (g_ln_in_g, g_ln_in_b, g_ln1_g, g_ln1_b, g_ln2_g, g_ln2_b, g_scale, g_sinks, loss_sum, g_meta_full,
     g_bgate_full) = _unpack(red, small_shapes)
    loss = 0.5 * loss_sum
    update("meta_tokens", lax.dynamic_slice(g_meta_full, (0, dev * dcols), (N_META, dcols)))
    update("b_gate", lax.dynamic_slice(g_bgate_full, (0, dev * dcols), (2, dcols)))
    for nm, g in (("ln_in_g", g_ln_in_g), ("ln_in_b", g_ln_in_b), ("ln1_g", g_ln1_g), ("ln1_b", g_ln1_b),
                  ("ln2_g", g_ln2_g), ("ln2_b", g_ln2_b), ("pool_scale", g_scale), ("attn_sinks", g_sinks)):
        update(nm, g)

    update_reduced("w_in", rs_in.finish(order)[0], transposed=True)

    names = list(weights)
    return (loss, grad_x, *[grads[n] for n in names], *[deltas[n] for n in names],
            *[new_ms[n] for n in names], *[new_vs[n] for n in names])
```

```python
import jax
import jax.numpy as jnp
from jax import lax
from jax.experimental import pallas as pl
from jax.experimental.pallas import tpu as pltpu

F32 = jnp.float32
BF16 = jnp.bfloat16
MESH = pl.DeviceIdType.MESH

N_META = 16
HEAD_DIM = 64
Q_PER_KV = 8
WINDOW = 128
BLOCK = 128
ATTN_SCALE = HEAD_DIM ** -0.5
ROPE_DIM = HEAD_DIM // 4
ROPE_THETA = 500000.0
NEG_INF = -1e30
POOL_WINDOWS = (2, 4, 8, 16)
LN_EPS = 1e-5
DN_ALPHA = 2.0 ** 0.25
ADAM_LR = 0.001
ADAM_B1 = 0.9
ADAM_B2 = 0.999
ADAM_EPS = 1e-08
ADAM_WD = 0.01
ADAM_STEP = 10

LANES = 128
META_ROW0 = BLOCK - N_META
VMEM_LIMIT = 56 * 1024 * 1024

ANY = pl.BlockSpec(memory_space=pl.ANY)
HBM = pl.BlockSpec(memory_space=pltpu.HBM)
SEM = pl.BlockSpec(memory_space=pltpu.SEMAPHORE)
EFFECT = pltpu.SideEffectType.DATAFLOW_SIDE_EFFECTING


def _params(sem=None, **kw):
    return pltpu.CompilerParams(dimension_semantics=sem, vmem_limit_bytes=VMEM_LIMIT, **kw)


class _Order:
    def __init__(self):
        self.last = None


def _call(order, body, operands, *, name, in_specs, out_specs, out_shape, grid=(), scratch=(), sem=None,
          aliases=None, prefetch=()):
    n_in, npf = len(operands), len(prefetch)
    tok = order.last
    if tok is not None and any(tok is op for op in operands):
        tok = None

    def wrapped(*refs):
        refs = list(refs)
        if tok is not None:
            del refs[npf + n_in]
        body(*refs)

    specs = list(in_specs) + ([ANY] if tok is not None else [])
    ops = list(operands) + ([tok] if tok is not None else [])
    if npf:
        out = pl.pallas_call(
            wrapped, name=name, out_shape=out_shape, compiler_params=_params(sem),
            grid_spec=pltpu.PrefetchScalarGridSpec(num_scalar_prefetch=npf, grid=grid, in_specs=specs,
                                                   out_specs=out_specs, scratch_shapes=list(scratch)),
        )(*prefetch, *ops)
    else:
        out = pl.pallas_call(
            wrapped, name=name, grid=grid, in_specs=specs, out_specs=out_specs, out_shape=out_shape,
            scratch_shapes=list(scratch), input_output_aliases=aliases or {}, compiler_params=_params(sem),
        )(*ops)
    order.last = out[0] if isinstance(out, (list, tuple)) else out
    return out


def _pick(dim, pref, mult=LANES):
    best = None
    t = mult
    while t <= min(dim, pref):
        if dim % t == 0:
            best = t
        t += mult
    return dim if best is None else best


_DIMS = {"nn": (((1,), (0,)), ((), ())), "nt": (((1,), (1,)), ((), ())), "tn": (((0,), (0,)), ((), ()))}


def _mm(order, a, b, *, kind, out_dtype, tm, tn, tk=None, name, a_lead=None):
    a2 = a.shape[-2:]
    halves = a_lead == "halves"
    if halves:
        a2 = (a2[0], 2 * a2[1])
    if kind == "tn":
        K, M = a2
    else:
        M, K = a2
    N = b.shape[0] if kind == "nt" else b.shape[1]
    half_cols = a2[1] // 2
    tm = _pick(half_cols if halves and kind == "tn" else M, tm)
    tn = _pick(N, tn)
    tk = K if tk is None else _pick(half_cols if halves and kind != "tn" else K, tk)
    nm, nn_, nk = M // tm, N // tn, K // tk
    a_bytes = M * K * a.dtype.itemsize
    b_bytes = N * K * b.dtype.itemsize
    i_outer = (a_bytes + nm * b_bytes <= b_bytes + nn_ * a_bytes) if nk == 1 else True

    def ij(g0, g1):
        return (g0, g1) if i_outer else (g1, g0)

    def a_map(g0, g1, k):
        i, _ = ij(g0, g1)
        if halves:
            per = half_cols // (tm if kind == "tn" else tk)
            return (i // per, k, i % per) if kind == "tn" else (k // per, i, k % per)
        idx = (k, i) if kind == "tn" else (i, k)
        return idx if a_lead is None else (a_lead,) + idx

    def b_map(g0, g1, k):
        _, j = ij(g0, g1)
        return (j, k) if kind == "nt" else (k, j)

    def o_map(g0, g1, k):
        return ij(g0, g1)

    a_blk = (tk, tm) if kind == "tn" else (tm, tk)
    if a_lead is not None:
        a_blk = (None,) + a_blk
    b_blk = (tn, tk) if kind == "nt" else (tk, tn)

    in_place = out_dtype == F32

    def body(a_ref, b_ref, o_ref, *acc):
        p = lax.dot_general(a_ref[...], b_ref[...], _DIMS[kind], preferred_element_type=F32)
        if nk == 1:
            o_ref[...] = p.astype(o_ref.dtype)
        else:
            k = pl.program_id(2)
            acc_ref = o_ref if in_place else acc[0]

            @pl.when(k == 0)
            def _():
                acc_ref[...] = p

            @pl.when(k > 0)
            def _():
                acc_ref[...] += p

            if not in_place:
                @pl.when(k == nk - 1)
                def _():
                    o_ref[...] = acc_ref[...].astype(o_ref.dtype)

    grid = (nm, nn_, nk) if i_outer else (nn_, nm, nk)
    return _call(
        order, body, [a, b], name=name, grid=grid,
        in_specs=[pl.BlockSpec(a_blk, a_map), pl.BlockSpec(b_blk, b_map)],
        out_specs=pl.BlockSpec((tm, tn), o_map),
        out_shape=jax.ShapeDtypeStruct((M, N), out_dtype),
        scratch=[] if nk == 1 or in_place else [pltpu.VMEM((tm, tn), F32)],
        sem=("parallel", "parallel", "arbitrary"))


def _x_half(far):
    x = lax.axis_index("x")
    return 1 - x if far else x


def _mm_nt_half(order, a, bT, *, far, tm, tn, name, into=None):
    M, K = a.shape
    N = bT.shape[0]
    tm, tn = _pick(M, tm), _pick(N // 2, tn)
    nh = N // 2 // tn

    def body(a_ref, b_ref, *rest):
        rest[-1][...] = lax.dot_general(a_ref[...], b_ref[...], _DIMS["nt"], preferred_element_type=F32)

    return _call(
        order, body, [a, bT] + ([] if into is None else [into]), name=name, grid=(M // tm, nh),
        in_specs=[pl.BlockSpec((tm, K), lambda i, j: (i, 0)),
                  pl.BlockSpec((tn, K), lambda i, j: (_x_half(far) * nh + j, 0))] + ([] if into is None else [ANY]),
        out_specs=pl.BlockSpec((tm, tn), lambda i, j: (i, _x_half(far) * nh + j)),
        out_shape=jax.ShapeDtypeStruct((M, N), F32), aliases=None if into is None else {2: 0},
        sem=("parallel", "parallel"))


def _ln_stats(z):
    mu = jnp.mean(z, axis=-1, keepdims=True)
    zc = z - mu
    var = jnp.mean(zc * zc, axis=-1, keepdims=True)
    rstd = lax.rsqrt(var + LN_EPS)
    return zc * rstd, rstd


def _ln_bwd(dy, xhat, rstd, g):
    dxh = dy * g
    m1 = jnp.mean(dxh, axis=-1, keepdims=True)
    m2 = jnp.mean(dxh * xhat, axis=-1, keepdims=True)
    return rstd * (dxh - m1 - xhat * m2)


def _ln_in_fwd(order, x, meta_pad, g, b):
    S, D = x.shape
    nb = S // BLOCK

    def body(x_ref, mp_ref, g_ref, b_ref, h_ref, hb_ref):
        is_meta = pl.program_id(0) == nb
        xin = jnp.where(is_meta, mp_ref[...], x_ref[...])
        xhat, _ = _ln_stats(xin)
        y = xhat * g_ref[...] + b_ref[...]
        h_ref[...] = y
        hb_ref[...] = y.astype(BF16)

    row = pl.BlockSpec((BLOCK, D), lambda i: (i, 0))
    vec = pl.BlockSpec((1, D), lambda i: (0, 0))
    return _call(
        order, body, [x, meta_pad, g, b], name="ln_in_fwd", grid=(nb + 1,),
        in_specs=[pl.BlockSpec((BLOCK, D), lambda i: (jnp.minimum(i, nb - 1), 0)),
                  pl.BlockSpec((BLOCK, D), lambda i: (0, 0)), vec, vec],
        out_specs=[row, row],
        out_shape=[jax.ShapeDtypeStruct((S + BLOCK, D), F32), jax.ShapeDtypeStruct((S + BLOCK, D), BF16)],
        sem=("parallel",))


def _ln_in_bwd(order, x, meta_pad, g, dh0, dz1):
    S, D = x.shape
    nb = S // BLOCK

    def body(x_ref, mp_ref, g_ref, dh_ref, dz_ref, dx_ref, dg_ref, db_ref):
        i = pl.program_id(0)
        is_meta = i == nb
        xin = jnp.where(is_meta, mp_ref[...], x_ref[...])
        xhat, rstd = _ln_stats(xin)
        dy = dh_ref[...] + jnp.where(is_meta, 0.0, DN_ALPHA) * dz_ref[...]
        dx_ref[...] = _ln_bwd(dy, xhat, rstd, g_ref[...])

        @pl.when(i == 0)
        def _():
            dg_ref[...] = jnp.zeros_like(dg_ref)
            db_ref[...] = jnp.zeros_like(db_ref)

        dg_ref[...] += jnp.sum(dy * xhat, axis=0, keepdims=True)
        db_ref[...] += jnp.sum(dy, axis=0, keepdims=True)

    row = pl.BlockSpec((BLOCK, D), lambda i: (i, 0))
    rowx = pl.BlockSpec((BLOCK, D), lambda i: (jnp.minimum(i, nb - 1), 0))
    vec = pl.BlockSpec((1, D), lambda i: (0, 0))
    return _call(
        order, body, [x, meta_pad, g, dh0, dz1], name="ln_in_bwd", grid=(nb + 1,),
        in_specs=[rowx, pl.BlockSpec((BLOCK, D), lambda i: (0, 0)), vec, row, rowx],
        out_specs=[row, vec, vec],
        out_shape=[jax.ShapeDtypeStruct((S + BLOCK, D), F32), jax.ShapeDtypeStruct((1, D), F32),
                   jax.ShapeDtypeStruct((1, D), F32)],
        sem=("arbitrary",))


def _ln1_fwd(order, h0, y1, g, b):
    S, D = y1.shape
    tm = _pick(S, BLOCK, 8)

    def body(h_ref, y_ref, g_ref, b_ref, o_ref, ob_ref):
        xhat, _ = _ln_stats(DN_ALPHA * h_ref[...] + y_ref[...])
        y = xhat * g_ref[...] + b_ref[...]
        o_ref[...] = y
        ob_ref[...] = y.astype(BF16)

    row = pl.BlockSpec((tm, D), lambda i: (i, 0))
    vec = pl.BlockSpec((1, D), lambda i: (0, 0))
    return _call(
        order, body, [h0, y1, g, b], name="ln1_fwd", grid=(S // tm,), in_specs=[row, row, vec, vec],
        out_specs=[row, row],
        out_shape=[jax.ShapeDtypeStruct((S, D), F32), jax.ShapeDtypeStruct((S, D), BF16)],
        sem=("parallel",))


def _ln1_bwd(order, h0, y1, g, dh1, dz2):
    S, D = y1.shape
    tm = _pick(S, BLOCK, 8)

    def body(h_ref, y_ref, g_ref, dh_ref, dz2_ref, dz_ref, dzb_ref, dg_ref, db_ref):
        i = pl.program_id(0)
        xhat, rstd = _ln_stats(DN_ALPHA * h_ref[...] + y_ref[...])
        dy = dh_ref[...] + DN_ALPHA * dz2_ref[...]
        dz = _ln_bwd(dy, xhat, rstd, g_ref[...])
        dz_ref[...] = dz
        dzb_ref[...] = dz.astype(BF16)

        @pl.when(i == 0)
        def _():
            dg_ref[...] = jnp.zeros_like(dg_ref)
            db_ref[...] = jnp.zeros_like(db_ref)

        dg_ref[...] += jnp.sum(dy * xhat, axis=0, keepdims=True)
        db_ref[...] += jnp.sum(dy, axis=0, keepdims=True)

    row = pl.BlockSpec((tm, D), lambda i: (i, 0))
    vec = pl.BlockSpec((1, D), lambda i: (0, 0))
    return _call(
        order, body, [h0, y1, g, dh1, dz2], name="ln1_bwd", grid=(S // tm,),
        in_specs=[row, row, vec, row, row], out_specs=[row, row, vec, vec],
        out_shape=[jax.ShapeDtypeStruct((S, D), F32), jax.ShapeDtypeStruct((S, D), BF16),
                   jax.ShapeDtypeStruct((1, D), F32), jax.ShapeDtypeStruct((1, D), F32)],
        sem=("arbitrary",))


def _ln2_loss_bwd(order, h1, y2, target, g, b):
    S, D = y2.shape
    tm = _pick(S, BLOCK, 8)

    def body(h_ref, y_ref, t_ref, g_ref, b_ref, dz_ref, dzb_ref, dg_ref, db_ref, loss_ref):
        i = pl.program_id(0)
        xhat, rstd = _ln_stats(DN_ALPHA * h_ref[...] + y_ref[...])
        diff = xhat * g_ref[...] + b_ref[...] - t_ref[...]
        dy = diff / D
        dz = _ln_bwd(dy, xhat, rstd, g_ref[...])
        dz_ref[...] = dz
        dzb_ref[...] = dz.astype(BF16)

        @pl.when(i == 0)
        def _():
            dg_ref[...] = jnp.zeros_like(dg_ref)
            db_ref[...] = jnp.zeros_like(db_ref)
            loss_ref[...] = jnp.zeros_like(loss_ref)

        dg_ref[...] += jnp.sum(dy * xhat, axis=0, keepdims=True)
        db_ref[...] += jnp.sum(dy, axis=0, keepdims=True)
        loss_ref[...] += jnp.sum(jnp.mean(diff * diff, axis=-1, keepdims=True), axis=0, keepdims=True)

    row = pl.BlockSpec((tm, D), lambda i: (i, 0))
    vec = pl.BlockSpec((1, D), lambda i: (0, 0))
    one = pl.BlockSpec((1, 1), lambda i: (0, 0))
    return _call(
        order, body, [h1, y2, target, g, b], name="ln2_loss_bwd", grid=(S // tm,),
        in_specs=[row, row, row, vec, vec], out_specs=[row, row, vec, vec, one],
        out_shape=[jax.ShapeDtypeStruct((S, D), F32), jax.ShapeDtypeStruct((S, D), BF16),
                   jax.ShapeDtypeStruct((1, D), F32), jax.ShapeDtypeStruct((1, D), F32),
                   jax.ShapeDtypeStruct((1, 1), F32)],
        sem=("arbitrary",))


def _rope_table(S):
    r = jnp.arange(S + BLOCK)
    pos = jnp.where(r < S, r + N_META, jnp.maximum(r - (S + META_ROW0), 0))
    half = ROPE_DIM // 2
    lane = jnp.arange(LANES) % HEAD_DIM
    inv_freq = ROPE_THETA ** (-(lane % half).astype(F32) * 2.0 / ROPE_DIM)
    ang = pos.astype(F32)[:, None] * inv_freq[None, :]
    cos, sin = jnp.cos(ang), jnp.sin(ang)
    c = jnp.where(lane < ROPE_DIM, cos, 1.0)
    sa = jnp.where(lane < half, -sin, 0.0)
    sb = jnp.where((lane >= half) & (lane < ROPE_DIM), sin, 0.0)
    return jnp.concatenate([c, sa, sb], axis=1).astype(F32)


def _rope(x, tab):
    h = ROPE_DIM // 2
    return (x * tab[:, :LANES] + pltpu.roll(x, LANES - h, 1) * tab[:, LANES:2 * LANES]
            + pltpu.roll(x, h, 1) * tab[:, 2 * LANES:])


def _rope_t(dy, tab):
    h = ROPE_DIM // 2
    return (dy * tab[:, :LANES] + pltpu.roll(dy * tab[:, LANES:2 * LANES], h, 1)
            + pltpu.roll(dy * tab[:, 2 * LANES:], LANES - h, 1))


NKEY = N_META + 2 * BLOCK


def _attn_tiles(g, n, S, sink_ref, q_ref, k_ref, v_ref, tab_ref):
    NQG = Q_PER_KV // 2
    R = NQG * BLOCK
    halfsel = (g % 2).astype(F32)
    prev = jnp.maximum(n - 1, 0)
    qrow = pl.ds(pl.multiple_of(n * BLOCK, BLOCK), BLOCK)
    prow = pl.ds(pl.multiple_of(prev * BLOCK, BLOCK), BLOCK)
    mrow = pl.ds(S + META_ROW0, N_META)

    tq = tab_ref[qrow, :]
    qf = q_ref[...]
    q4 = jnp.concatenate([_rope(qf[:, LANES * p:LANES * (p + 1)], tq) for p in range(NQG)], axis=0).astype(BF16)

    tk = jnp.concatenate([tab_ref[mrow, :], tab_ref[prow, :], tq], axis=0)
    kr = _rope(jnp.concatenate([k_ref[mrow, :], k_ref[prow, :], k_ref[qrow, :]], axis=0), tk)
    vr = jnp.concatenate([v_ref[mrow, :], v_ref[prow, :], v_ref[qrow, :]], axis=0)

    lane = lax.broadcasted_iota(jnp.int32, kr.shape, 1)
    own = jnp.where(lane < HEAD_DIM, 1.0 - halfsel, halfsel)

    def lo_hi(t):
        mine = t * own
        other = pltpu.roll(mine, HEAD_DIM, 1)
        lo = mine * (1.0 - halfsel) + other * halfsel
        hi = other * (1.0 - halfsel) + mine * halfsel
        return lo.astype(BF16), hi.astype(BF16)

    klo, khi = lo_hi(kr)
    vlo, vhi = lo_hi(vr)

    jj = lax.broadcasted_iota(jnp.int32, (BLOCK, R), 0)
    qi = lax.broadcasted_iota(jnp.int32, (BLOCK, R), 1) & (BLOCK - 1)
    in_cur = jj <= qi
    band_ok = in_cur | (jj > qi + jnp.where(n >= 1, 0, 2 * BLOCK))

    def soft(kk, parity):
        sk = jnp.concatenate(
            [jnp.full((1, BLOCK), sink_ref[0, Q_PER_KV * g + 2 * p + parity], F32) for p in range(NQG)], axis=1)
        s = lax.dot_general(kk, q4, _DIMS["nt"], preferred_element_type=F32) * ATTN_SCALE
        band = jnp.where(in_cur, s[N_META + BLOCK:], s[N_META:N_META + BLOCK])
        s = jnp.concatenate([s[:N_META], jnp.where(band_ok, band, NEG_INF)], axis=0)
        m = jnp.maximum(jnp.max(s, axis=0, keepdims=True), sk)
        p = jnp.exp(s - m)
        es = jnp.exp(sk - m)
        inv = 1.0 / (jnp.sum(p, axis=0, keepdims=True) + es)
        return p * inv, es * inv

    pe, sink_e = soft(klo, 0)
    po, sink_o = soft(khi, 1)
    return q4, tk, (klo, khi), (vlo, vhi), (pe, po), (sink_e, sink_o), own, in_cur


def _spread(t, in_cur):
    band = t[N_META:]
    return jnp.concatenate([t[:N_META], jnp.where(in_cur, 0.0, band), jnp.where(in_cur, band, 0.0)], axis=0)


def _attn_specs(S, ATTN, KVW):
    Tp = S + BLOCK
    koff, voff = ATTN // LANES, (ATTN + KVW) // LANES
    gw = Q_PER_KV * HEAD_DIM
    return [pl.BlockSpec(memory_space=pltpu.SMEM),
            pl.BlockSpec((BLOCK, gw), lambda g, n: (n, g)),
            pl.BlockSpec((Tp, LANES), lambda g, n: (0, koff + g // 2)),
            pl.BlockSpec((Tp, LANES), lambda g, n: (0, voff + g // 2)),
            pl.BlockSpec((Tp, 3 * LANES), lambda g, n: (0, 0))]


def _attn_fwd(order, proj, tab, sinks, S, ATTN, KVW):
    G = KVW // HEAD_DIM
    nb = S // BLOCK
    gw = Q_PER_KV * HEAD_DIM

    def body(sink_ref, q_ref, k_ref, v_ref, tab_ref, o_ref):
        g, n = pl.program_id(0), pl.program_id(1)
        _, _, _, (vlo, vhi), (pe, po), _, _, in_cur = _attn_tiles(g, n, S, sink_ref, q_ref, k_ref, v_ref, tab_ref)
        o4 = (lax.dot_general(_spread(pe, in_cur).astype(BF16), vlo, _DIMS["tn"], preferred_element_type=F32)
              + lax.dot_general(_spread(po, in_cur).astype(BF16), vhi, _DIMS["tn"], preferred_element_type=F32))
        o_ref[...] = jnp.concatenate(
            [o4[BLOCK * p:BLOCK * (p + 1)] for p in range(Q_PER_KV // 2)], axis=1).astype(BF16)

    return _call(
        order, body, [sinks, proj, proj, proj, tab], name="attn_fwd", grid=(G, nb),
        in_specs=_attn_specs(S, ATTN, KVW),
        out_specs=pl.BlockSpec((BLOCK, gw), lambda g, n: (n, g)),
        out_shape=jax.ShapeDtypeStruct((S, ATTN), BF16),
        sem=("parallel", "arbitrary"))


def _attn_bwd(order, proj, tab, sinks, da, dproj, S, ATTN, KVW):
    G = KVW // HEAD_DIM
    nb = S // BLOCK
    Tp = S + BLOCK
    NQG = Q_PER_KV // 2
    gw = Q_PER_KV * HEAD_DIM

    def body(sink_ref, q_ref, k_ref, v_ref, tab_ref, da_ref, dproj_in, dq_ref, dk_ref, dv_ref, ds_ref):
        del dproj_in
        g, n = pl.program_id(0), pl.program_id(1)
        q4, tk, (klo, khi), (vlo, vhi), (pe, po), (sink_e, sink_o), own, in_cur = _attn_tiles(
            g, n, S, sink_ref, q_ref, k_ref, v_ref, tab_ref)
        dof = da_ref[...]
        do4 = jnp.concatenate([dof[:, LANES * p:LANES * (p + 1)] for p in range(NQG)], axis=0)

        def grads(p, vv):
            dp = lax.dot_general(vv, do4, _DIMS["nt"], preferred_element_type=F32)
            dp = jnp.concatenate(
                [dp[:N_META], jnp.where(in_cur, dp[N_META + BLOCK:], dp[N_META:N_META + BLOCK])], axis=0)
            delta = jnp.sum(p * dp, axis=0, keepdims=True)
            return _spread(p * (dp - delta) * ATTN_SCALE, in_cur).astype(BF16), delta

        dse, delta_e = grads(pe, vlo)
        dso, delta_o = grads(po, vhi)

        dq4 = (lax.dot_general(dse, klo, _DIMS["tn"], preferred_element_type=F32)
               + lax.dot_general(dso, khi, _DIMS["tn"], preferred_element_type=F32))
        tq = tk[N_META + BLOCK:]
        dq_ref[...] = jnp.concatenate(
            [_rope_t(dq4[BLOCK * p:BLOCK * (p + 1)], tq) for p in range(NQG)], axis=1).astype(BF16)

        lane = lax.broadcasted_iota(jnp.int32, (NKEY, LANES), 1)

        def fold(lo_part, hi_part):
            t = jnp.where(lane < HEAD_DIM, lo_part, hi_part)
            return t + pltpu.roll(t, HEAD_DIM, 1)

        dk = _rope_t(fold(jnp.dot(dse, q4, preferred_element_type=F32),
                          jnp.dot(dso, q4, preferred_element_type=F32)), tk) * own
        dv = fold(jnp.dot(_spread(pe, in_cur).astype(BF16), do4, preferred_element_type=F32),
                  jnp.dot(_spread(po, in_cur).astype(BF16), do4, preferred_element_type=F32)) * own

        @pl.when((n == 0) & (g % 2 == 0))
        def _():
            dk_ref[...] = jnp.zeros_like(dk_ref)
            dv_ref[...] = jnp.zeros_like(dv_ref)

        @pl.when(n == 0)
        def _():
            ds_ref[...] = jnp.zeros_like(ds_ref)

        prev = jnp.maximum(n - 1, 0)
        qrow = pl.ds(pl.multiple_of(n * BLOCK, BLOCK), BLOCK)
        prow = pl.ds(pl.multiple_of(prev * BLOCK, BLOCK), BLOCK)
        mrow = pl.ds(S + META_ROW0, N_META)
        for ref, val in ((dk_ref, dk), (dv_ref, dv)):
            ref[mrow, :] += val[:N_META]
            ref[prow, :] += val[N_META:N_META + BLOCK]
            ref[qrow, :] += val[N_META + BLOCK:]

        srow = lax.broadcasted_iota(jnp.int32, (Q_PER_KV, LANES), 0)
        acc = jnp.zeros((Q_PER_KV, LANES), F32)
        for p in range(NQG):
            for parity, (sk, dl) in enumerate(((sink_e, delta_e), (sink_o, delta_o))):
                val = -jnp.sum(sk[:, BLOCK * p:BLOCK * (p + 1)] * dl[:, BLOCK * p:BLOCK * (p + 1)])
                acc = jnp.where(srow == 2 * p + parity, val, acc)
        ds_ref[0] += acc

    in_specs = _attn_specs(S, ATTN, KVW) + [pl.BlockSpec((BLOCK, gw), lambda g, n: (n, g)), ANY]
    slab = pl.BlockSpec((Tp, LANES), lambda g, n: (0, g // 2))
    return _call(
        order, body, [sinks, proj, proj, proj, tab, da, dproj], name="attn_bwd", grid=(G, nb), in_specs=in_specs,
        out_specs=[pl.BlockSpec((BLOCK, gw), lambda g, n: (n, g)), slab, slab,
                   pl.BlockSpec((1, Q_PER_KV, LANES), lambda g, n: (g, 0, 0))],
        out_shape=[jax.ShapeDtypeStruct(dproj.shape, BF16), jax.ShapeDtypeStruct((Tp, KVW), F32),
                   jax.ShapeDtypeStruct((Tp, KVW), F32), jax.ShapeDtypeStruct((G, Q_PER_KV, LANES), F32)],
        aliases={6: 0}, sem=("arbitrary", "arbitrary"))


def _zero_meta_block(order, Tp, IN):
    tc = _pick(IN, 4096)

    def body(o_ref):
        o_ref[...] = jnp.zeros_like(o_ref)

    return _call(
        order, body, [], name="dproj_zero_meta", grid=(IN // tc,), in_specs=[],
        out_specs=pl.BlockSpec((BLOCK, tc), lambda j: (Tp // BLOCK - 1, j)),
        out_shape=jax.ShapeDtypeStruct((Tp, IN), BF16), sem=("parallel",))


def _put_dkv(order, dk, dv, dproj, ATTN):
    Tp, KVW = dk.shape
    nkb = KVW // LANES
    koff = ATTN // LANES

    def body(dk_ref, dv_ref, dproj_in, o_ref):
        del dproj_in
        t = pl.program_id(0)
        o_ref[...] = jnp.where(t < nkb, dk_ref[...], dv_ref[...]).astype(BF16)

    src = pl.BlockSpec((Tp, LANES), lambda t: (0, t % nkb))
    return _call(
        order, body, [dk, dv, dproj], name="dproj_put_dkv", grid=(2 * nkb,), in_specs=[src, src, ANY],
        out_specs=pl.BlockSpec((Tp, LANES), lambda t: (0, koff + t)),
        out_shape=jax.ShapeDtypeStruct(dproj.shape, BF16), aliases={2: 0}, sem=("parallel",))


HALO = 16


def _window_sums(x, up):
    n = x.shape[0]
    out = []
    s = x
    for k in (1, 2, 4, 8):
        s = s + pltpu.roll(s, (n - k) if up else k, 0)
        out.append(s)
    return out


def _pool_specs(S, ub, gw, tm):
    meta_halo = (S + BLOCK - HALO) // HALO

    def main(g):
        return pl.BlockSpec((tm, gw), lambda i: (i, ub + g))

    def halo(g):
        return pl.BlockSpec((HALO, gw), lambda i: (jnp.where(i == 0, meta_halo, i * (tm // HALO) - 1), ub + g))

    return [main(g) for g in range(4)] + [halo(g) for g in range(4)]


def _pooled(main_refs, halo_refs, g):
    x = jnp.concatenate([halo_refs[g][...], main_refs[g][...]], axis=0)
    s = _window_sums(x, up=False)[g]
    return (s[HALO:] * (1.0 / POOL_WINDOWS[g]) - x[HALO:]).astype(BF16)


def _pool_fwd(order, proj, wgrp, scale, S, uoff, POOL):
    gw = POOL // 4
    tm = BLOCK

    def body(*refs):
        main, halo = refs[:4], refs[4:8]
        w_ref, sc_ref, o_ref = refs[8:]
        for g in range(4):
            mixed = jnp.dot(_pooled(main, halo, g), w_ref[g], preferred_element_type=F32)
            o_ref[:, gw * g:gw * (g + 1)] = (mixed * sc_ref[:, gw * g:gw * (g + 1)]).astype(BF16)

    return _call(
        order, body, [proj] * 8 + [wgrp, scale], name="pool_fwd", grid=(S // tm,),
        in_specs=_pool_specs(S, uoff // gw, gw, tm) + [
            pl.BlockSpec((4, gw, gw), lambda i: (0, 0, 0)), pl.BlockSpec((1, POOL), lambda i: (0, 0))],
        out_specs=pl.BlockSpec((tm, POOL), lambda i: (i, 0)),
        out_shape=jax.ShapeDtypeStruct((S, POOL), BF16), sem=("parallel",))


def _pool_bwd_mix(order, proj, wgrp, scale, dps, S, uoff, POOL):
    gw = POOL // 4
    tm = BLOCK

    def body(*refs):
        main, halo = refs[:4], refs[4:8]
        w_ref, sc_ref, dps_ref, dpl_ref, dw_ref, dsc_ref = refs[8:]
        i = pl.program_id(0)

        @pl.when(i == 0)
        def _():
            dw_ref[...] = jnp.zeros_like(dw_ref)
            dsc_ref[...] = jnp.zeros_like(dsc_ref)

        for g in range(4):
            cols = slice(gw * g, gw * (g + 1))
            pooled = _pooled(main, halo, g)
            mixed = jnp.dot(pooled, w_ref[g], preferred_element_type=F32)
            dps_g = dps_ref[:, cols]
            dsc_ref[:, cols] += jnp.sum(dps_g * mixed, axis=0, keepdims=True)
            dms = (dps_g * sc_ref[:, cols]).astype(BF16)
            dw_ref[g] += lax.dot_general(pooled, dms, _DIMS["tn"], preferred_element_type=F32)
            dpl_ref[:, cols] = lax.dot_general(dms, w_ref[g], _DIMS["nt"], preferred_element_type=F32)

    row = pl.BlockSpec((tm, POOL), lambda i: (i, 0))
    return _call(
        order, body, [proj] * 8 + [wgrp, scale, dps], name="pool_bwd_mix", grid=(S // tm,),
        in_specs=_pool_specs(S, uoff // gw, gw, tm) + [
            pl.BlockSpec((4, gw, gw), lambda i: (0, 0, 0)), pl.BlockSpec((1, POOL), lambda i: (0, 0)), row],
        out_specs=[row, pl.BlockSpec((4, gw, gw), lambda i: (0, 0, 0)), pl.BlockSpec((1, POOL), lambda i: (0, 0))],
        out_shape=[jax.ShapeDtypeStruct((S, POOL), F32), jax.ShapeDtypeStruct((4, gw, gw), F32),
                   jax.ShapeDtypeStruct((1, POOL), F32)],
        sem=("arbitrary",))


def _pool_bwd_window(order, dpl, dproj, S, uoff, POOL):
    gw = POOL // 4
    nb = S // BLOCK
    ub = uoff // gw

    def body(main_ref, halo_ref, dproj_in, o_ref):
        del dproj_in
        b, g = pl.program_id(0), pl.program_id(1)
        main = jnp.where(b < nb, main_ref[...], 0.0)
        halo = jnp.where(b == nb - 1, 0.0, halo_ref[...])
        sums = _window_sums(jnp.concatenate([main, halo], axis=0), up=True)
        du = jnp.zeros((BLOCK, gw), F32)
        for k, w in enumerate(POOL_WINDOWS):
            du = jnp.where(g == k, sums[k][:BLOCK] * (1.0 / w), du)
        du = du - main
        row = lax.broadcasted_iota(jnp.int32, du.shape, 0)
        first_valid = jnp.where(b == nb, META_ROW0, 0)
        o_ref[...] = jnp.where(row >= first_valid, du, 0.0).astype(BF16)

    return _call(
        order, body, [dpl, dpl, dproj], name="pool_bwd_window", grid=(nb + 1, 4),
        in_specs=[pl.BlockSpec((BLOCK, gw), lambda b, g: (jnp.minimum(b, nb - 1), g)),
                  pl.BlockSpec((HALO, gw), lambda b, g: (
                      jnp.where(b == nb, 0, jnp.minimum((b + 1) * (BLOCK // HALO), S // HALO - 1)), g)),
                  ANY],
        out_specs=pl.BlockSpec((BLOCK, gw), lambda b, g: (b, ub + g)),
        out_shape=jax.ShapeDtypeStruct(dproj.shape, BF16), aliases={2: 0}, sem=("parallel", "parallel"))


def _sigmoid(x):
    return 1.0 / (1.0 + jnp.exp(-x))


def _gate_tiles(S, D, goff):
    tc = 512
    while goff % tc or D % tc:
        tc //= 2
    return _pick(S, 512, 8), tc


def _gate_mix(order, proj, bgate, a_out, p_out, S, D, goff):
    tm, tc = _gate_tiles(S, D, goff)
    g0b, nd = goff // tc, D // tc

    def body(l0_ref, l1_ref, b_ref, a_ref, p_ref, o_ref):
        g0 = _sigmoid(l0_ref[...] + b_ref[0:1, :])
        g1 = _sigmoid(l1_ref[...] + b_ref[1:2, :])
        o_ref[...] = (g0 * a_ref[...] + g1 * p_ref[...]).astype(BF16)

    tile = pl.BlockSpec((tm, tc), lambda i, j: (i, j))
    return _call(
        order, body, [proj, proj, bgate, a_out, p_out], name="gate_mix", grid=(S // tm, nd),
        in_specs=[pl.BlockSpec((tm, tc), lambda i, j: (i, g0b + j)),
                  pl.BlockSpec((tm, tc), lambda i, j: (i, g0b + nd + j)),
                  pl.BlockSpec((2, tc), lambda i, j: (0, j)), tile, tile],
        out_specs=tile, out_shape=jax.ShapeDtypeStruct((S, D), BF16), sem=("parallel", "parallel"))


def _gate_bwd(order, proj, bgate, a_out, p_out, dmixed, dproj, S, D, goff):
    tm, tc = _gate_tiles(S, D, goff)
    g0b, nd, ni = goff // tc, D // tc, S // tm
    nsteps = nd * ni

    def body(l0_ref, l1_ref, b_ref, a_ref, p_ref, dm_ref, dproj_in, dap_ref, dproj_ref, db_ref, buf, sems):
        del dproj_in
        j, i = pl.program_id(0), pl.program_id(1)
        step = j * ni + i
        slot = step % 2

        def put(sl, br):
            col = pl.multiple_of((g0b + br * nd + j) * tc, tc)
            return pltpu.make_async_copy(
                buf.at[sl, br], dproj_ref.at[pl.ds(pl.multiple_of(i * tm, tm), tm), pl.ds(col, tc)], sems.at[sl, br])

        @pl.when(step >= 2)
        def _():
            put(slot, 0).wait()
            put(slot, 1).wait()

        @pl.when(i == 0)
        def _():
            db_ref[...] = jnp.zeros_like(db_ref)

        dm = dm_ref[...]
        for br, (l_ref, val_ref) in enumerate(((l0_ref, a_ref), (l1_ref, p_ref))):
            gate = _sigmoid(l_ref[...] + b_ref[br:br + 1, :])
            dap_ref[br] = (dm * gate).astype(BF16)
            dl = dm * val_ref[...] * gate * (1.0 - gate)
            buf[slot, br] = dl.astype(BF16)
            db_ref[br] += jnp.sum(dl, axis=0, keepdims=True)
            put(slot, br).start()

        @pl.when(step == nsteps - 1)
        def _():
            for sl in ((slot, 1 - slot) if nsteps > 1 else (slot,)):
                put(sl, 0).wait()
                put(sl, 1).wait()

    tile = pl.BlockSpec((tm, tc), lambda j, i: (i, j))
    return _call(
        order, body, [proj, proj, bgate, a_out, p_out, dmixed, dproj], name="gate_bwd", grid=(nd, ni),
        in_specs=[pl.BlockSpec((tm, tc), lambda j, i: (i, g0b + j)),
                  pl.BlockSpec((tm, tc), lambda j, i: (i, g0b + nd + j)),
                  pl.BlockSpec((2, tc), lambda j, i: (0, j)), tile, tile, tile, ANY],
        out_specs=[pl.BlockSpec((2, tm, tc), lambda j, i: (0, i, j)), ANY,
                   pl.BlockSpec((2, 1, tc), lambda j, i: (0, 0, j))],
        out_shape=[jax.ShapeDtypeStruct((2, S, D), BF16), jax.ShapeDtypeStruct(dproj.shape, BF16),
                   jax.ShapeDtypeStruct((2, 1, D), F32)],
        scratch=[pltpu.VMEM((2, 2, tm, tc), BF16), pltpu.SemaphoreType.DMA((2, 2))],
        aliases={6: 1}, sem=("arbitrary", "arbitrary"))


def _ffn_in_near(order, h, wT, FF):
    S, D = h.shape
    tm, tn = _pick(S, 1024), _pick(FF, 512)
    nj = FF // tn

    def body(h_ref, w_ref, f_ref):
        f_ref[...] = lax.dot_general(h_ref[...], w_ref[...], _DIMS["nt"], preferred_element_type=F32)

    return _call(
        order, body, [h, wT], name="ffn_in_near", grid=(S // tm, nj),
        in_specs=[pl.BlockSpec((tm, D), lambda i, j: (i, 0)),
                  pl.BlockSpec((tn, D), lambda i, j: (_x_half(False) * nj + j, 0))],
        out_specs=pl.BlockSpec((tm, tn), lambda i, j: (i, j)),
        out_shape=jax.ShapeDtypeStruct((S, FF), F32), sem=("parallel", "parallel"))


def _gate_up(near, far):
    near_is_gate = lax.axis_index("x") == 0
    return jnp.where(near_is_gate, near, far), jnp.where(near_is_gate, far, near)


def _ffn_in_far(order, h, wT, near):
    S, D = h.shape
    FF = near.shape[1]
    tm, tn = _pick(S, 1024), _pick(FF, 512)
    nj = FF // tn

    def body(h_ref, w_ref, near_ref, f_ref, act_ref):
        far = lax.dot_general(h_ref[...], w_ref[...], _DIMS["nt"], preferred_element_type=F32)
        f_ref[...] = far
        gt, up = _gate_up(near_ref[...], far)
        act_ref[...] = (gt * _sigmoid(gt) * up).astype(BF16)

    tile = pl.BlockSpec((tm, tn), lambda i, j: (i, j))
    return _call(
        order, body, [h, wT, near], name="ffn_in_far", grid=(S // tm, nj),
        in_specs=[pl.BlockSpec((tm, D), lambda i, j: (i, 0)),
                  pl.BlockSpec((tn, D), lambda i, j: (_x_half(True) * nj + j, 0)), tile],
        out_specs=[tile, tile],
        out_shape=[jax.ShapeDtypeStruct((S, FF), F32), jax.ShapeDtypeStruct((S, FF), BF16)],
        sem=("parallel", "parallel"))


def _d_act_swiglu(order, dy, wdown, near, far):
    S, D = dy.shape
    FF = wdown.shape[0]
    tm, tn = _pick(S, 1024), _pick(FF, 256)

    def body(dy_ref, w_ref, near_ref, far_ref, o_ref):
        d = lax.dot_general(dy_ref[...], w_ref[...], _DIMS["nt"], preferred_element_type=F32)
        gt, up = _gate_up(near_ref[...], far_ref[...])
        s = _sigmoid(gt)
        o_ref[0] = (d * up * s * (1.0 + gt * (1.0 - s))).astype(BF16)
        o_ref[1] = (d * gt * s).astype(BF16)

    tile = pl.BlockSpec((tm, tn), lambda i, j: (i, j))
    return _call(
        order, body, [dy, wdown, near, far], name="d_act", grid=(S // tm, FF // tn),
        in_specs=[pl.BlockSpec((tm, D), lambda i, j: (i, 0)), pl.BlockSpec((tn, D), lambda i, j: (j, 0)), tile, tile],
        out_specs=pl.BlockSpec((2, tm, tn), lambda i, j: (0, i, j)),
        out_shape=jax.ShapeDtypeStruct((2, S, FF), BF16), sem=("parallel", "parallel"))


def _place():
    return lax.axis_index("x"), lax.axis_index("y"), lax.axis_index("c")


def _xfer_start(order, name, bufs, copies):
    nb = len(bufs)
    n = len(copies([None] * nb, None))
    is_new = [isinstance(b, jax.ShapeDtypeStruct) for b in bufs]
    old = [b for b, fresh in zip(bufs, is_new) if not fresh]
    no = len(old)
    tok = [] if any(order.last is b for b in old) else [order.last]
    first_out = no + len(tok)

    def body(*refs):
        send, recv = refs[first_out:first_out + n], refs[first_out + n:first_out + 2 * n]
        token = refs[-1]
        given, made = iter(refs[:no]), iter(refs[first_out + 2 * n + no:-1])
        logical = [next(made) if fresh else next(given) for fresh in is_new]
        for i, (src, dst, dev) in enumerate(copies(logical, _place())):
            pltpu.make_async_remote_copy(src_ref=src, dst_ref=dst, send_sem=send[i], recv_sem=recv[i],
                                         device_id=dev, device_id_type=MESH).start()
        token[...] = jnp.zeros_like(token)

    fresh_shapes = [b for b, fresh in zip(bufs, is_new) if fresh]
    out = pl.pallas_call(
        body, name=name,
        out_shape=tuple([pltpu.SemaphoreType.DMA(())] * (2 * n)
                        + [pltpu.HBM(b.shape, b.dtype) for b in old + fresh_shapes]
                        + [jax.ShapeDtypeStruct((8, LANES), F32)]),
        in_specs=[HBM] * no + [ANY] * len(tok),
        out_specs=tuple([SEM] * (2 * n) + [HBM] * nb + [pl.BlockSpec(memory_space=pltpu.VMEM)]),
        input_output_aliases={i: 2 * n + i for i in range(no)},
        compiler_params=pltpu.CompilerParams(has_side_effects=EFFECT),
    )(*[pltpu.with_memory_space_constraint(b, pltpu.HBM) for b in old], *tok)
    order.last = out[-1]
    thru, made = iter(out[2 * n:2 * n + no]), iter(out[2 * n + no:2 * n + nb])
    return list(out[:2 * n]), [next(made) if fresh else next(thru) for fresh in is_new]


def _xfer_wait(order, name, sems, bufs, copies):
    nb = len(bufs)
    n = len(sems) // 2
    tok = order.last

    def body(*refs):
        send, recv = refs[nb:nb + n], refs[nb + n:nb + 2 * n]
        token = refs[-1]
        for i, (src, dst, dev) in enumerate(copies(refs[:nb], _place())):
            cp = pltpu.make_async_remote_copy(src_ref=src, dst_ref=dst, send_sem=send[i], recv_sem=recv[i],
                                              device_id=dev, device_id_type=MESH)
            cp.wait_send()
            cp.wait_recv()
        token[...] = jnp.zeros_like(token)

    out = pl.pallas_call(
        body, name=name,
        out_shape=tuple([pltpu.HBM(b.shape, b.dtype) for b in bufs] + [jax.ShapeDtypeStruct((8, LANES), F32)]),
        in_specs=[HBM] * nb + [SEM] * (2 * n) + [ANY],
        out_specs=tuple([HBM] * nb + [pl.BlockSpec(memory_space=pltpu.VMEM)]),
        input_output_aliases={i: i for i in range(nb)},
        compiler_params=pltpu.CompilerParams(has_side_effects=EFFECT),
    )(*bufs, *sems, tok)
    order.last = out[-1]
    return list(out[:nb])


class _Xfer:
    def __init__(self, name, bufs, copies):
        self.name, self.bufs, self.copies = name, list(bufs), copies
        self.sems = None

    def start(self, order):
        self.sems, self.bufs = _xfer_start(order, self.name + "_start", self.bufs, self.copies)

    def wait(self, order, bufs=None):
        self.bufs = _xfer_wait(order, self.name + "_wait", self.sems, bufs or self.bufs, self.copies)
        return self.bufs


def _block_rows(ref, r, d):
    return ref.at[pl.ds(d * r, r)]


NEAR = ("xn", "yn")
ALL_CHIPS = ("xn", "yn", "diag")


def _chip_of(which, x, y):
    return {"xn": (1 - x, y), "yn": (x, 1 - y), "diag": (1 - x, 1 - y)}[which]


def _gather_send(fulls, chips=ALL_CHIPS, sibling=True):
    def copies(refs, place):
        out = []
        for w, full in enumerate(fulls):
            r = full.shape[0] // 8
            if place is None:
                out += [None] * (len(chips) + int(sibling))
                continue
            x, y, c = place
            mine = _block_rows(refs[w], r, 4 * x + 2 * y + c)
            if sibling:
                out.append((mine, mine, (x, y, 1 - c)))
            for which in chips:
                out.append((mine, mine, (*_chip_of(which, x, y), c)))
        return out
    return copies


def _gather_forward(fulls, chips=ALL_CHIPS):
    def copies(refs, place):
        out = []
        for w, full in enumerate(fulls):
            r = full.shape[0] // 8
            if place is None:
                out += [None] * len(chips)
                continue
            x, y, c = place
            for which in chips:
                px, py = _chip_of(which, x, y)
                blk = _block_rows(refs[w], r, 4 * px + 2 * py + c)
                out.append((blk, blk, (x, y, 1 - c)))
        return out
    return copies


def _pair_send(nw):
    def copies(refs, place):
        out = []
        for w in range(nw):
            if place is None:
                out += [None] * 4
                continue
            x, y, c = place
            grad, other = refs[2 * w], refs[2 * w + 1]
            r = other.shape[1]
            for k in range(4):
                out.append((_block_rows(grad, r, 2 * k + 1 - c), other.at[k], (x, y, 1 - c)))
        return out
    return copies


def _chip_send(nw):
    def copies(refs, place):
        out = []
        for w in range(nw):
            if place is None:
                out += [None] * 3
                continue
            x, y, c = place
            psum, parts = refs[2 * w], refs[2 * w + 1]
            for px, py in ((1 - x, y), (x, 1 - y), (1 - x, 1 - y)):
                out.append((psum.at[2 * px + py], parts.at[2 * x + y], (px, py, c)))
        return out
    return copies


def _dev_index():
    x, y, c = _place()
    return 4 * x + 2 * y + c


def _place_own(order, shard, name):
    r, cols = shard.shape
    tr = _pick(r, max(16, (12 << 20) // (4 * cols)), 16)
    nr = r // tr

    def body(s_ref, o_ref):
        o_ref[...] = s_ref[...].astype(BF16)

    return _call(
        order, body, [shard], name=name, grid=(nr,),
        in_specs=[pl.BlockSpec((tr, cols), lambda i: (i, 0))],
        out_specs=pl.BlockSpec((tr, cols), lambda i: (_dev_index() * nr + i, 0)),
        out_shape=jax.ShapeDtypeStruct((8 * r, cols), BF16), sem=("parallel",))


def _pair_sum(order, grad, other, name):
    r, cols = other.shape[1:]
    tr = _pick(r, max(16, (7 << 20) // (2 * cols)), 16)
    nr = r // tr

    def body(g_ref, a_ref, o_ref):
        o_ref[...] = (g_ref[...].astype(F32) + a_ref[...].astype(F32)).astype(BF16)

    blk = pl.BlockSpec((None, tr, cols), lambda k, i: (k, i, 0))
    return _call(
        order, body, [grad, other], name=name, grid=(4, nr),
        in_specs=[pl.BlockSpec((tr, cols), lambda k, i: ((2 * k + lax.axis_index("c")) * nr + i, 0)), blk],
        out_specs=blk, out_shape=jax.ShapeDtypeStruct(other.shape, BF16), sem=("parallel", "parallel"))


def _chip_sum(order, psum, parts, name):
    _, r, cols = parts.shape
    tr = _pick(r, max(16, (1 << 20) // (2 * cols)), 16)

    def my_chip():
        return 2 * lax.axis_index("x") + lax.axis_index("y")

    def body(own_ref, p0, p1, p2, p3, o_ref):
        own = own_ref[...].astype(F32)
        acc = None
        for k, p in enumerate((p0, p1, p2, p3)):
            term = jnp.where(my_chip() == k, own, p[...].astype(F32))
            acc = term if acc is None else acc + term
        o_ref[...] = acc

    def slot(k):
        return pl.BlockSpec((None, tr, cols), lambda i: (jnp.where(my_chip() == k, (k + 1) % 4, k), i, 0))

    return _call(
        order, body, [psum, parts, parts, parts, parts], name=name, grid=(r // tr,),
        in_specs=[pl.BlockSpec((None, tr, cols), lambda i: (my_chip(), i, 0))] + [slot(k) for k in range(4)],
        out_specs=pl.BlockSpec((tr, cols), lambda i: (i, 0)),
        out_shape=jax.ShapeDtypeStruct((r, cols), F32), sem=("parallel",))


def _all_reduce_small(order, pack, name):
    R = pack.shape[0]

    def body(p_ref, o_ref, buf, send_sems, recv_sems):
        x, y, c = _place()
        me = 4 * x + 2 * y + c
        buf[me] = p_ref[...]
        copies = []
        for k in range(1, 8):
            px = 1 - x if k & 4 else x
            py = 1 - y if k & 2 else y
            pc = 1 - c if k & 1 else c
            cp = pltpu.make_async_remote_copy(
                src_ref=p_ref, dst_ref=buf.at[me], send_sem=send_sems.at[k - 1], recv_sem=recv_sems.at[k - 1],
                device_id=(px, py, pc), device_id_type=MESH)
            cp.start()
            copies.append(cp)
        for cp in copies:
            cp.wait_recv()
        acc = buf[0]
        for d in range(1, 8):
            acc = acc + buf[d]
        o_ref[...] = acc
        for cp in copies:
            cp.wait_send()

    vm = pl.BlockSpec(memory_space=pltpu.VMEM)
    return _call(
        order, body, [pack], name=name, in_specs=[vm], out_specs=vm,
        out_shape=jax.ShapeDtypeStruct((R, LANES), F32),
        scratch=[pltpu.VMEM((8, R, LANES), F32), pltpu.SemaphoreType.DMA((7,)), pltpu.SemaphoreType.DMA((7,))])


def _pack(parts):
    flat = []
    for p in parts:
        v = p.reshape(-1).astype(F32)
        flat.append(jnp.pad(v, (0, (-v.shape[0]) % LANES)))
    v = jnp.concatenate(flat)
    v = jnp.pad(v, (0, (-v.shape[0]) % (8 * LANES)))
    return v.reshape(-1, LANES)


def _unpack(pack, shapes):
    v = pack.reshape(-1)
    out, off = [], 0
    for s in shapes:
        n = 1
        for d in s:
            n *= d
        out.append(v[off:off + n].reshape(s))
        off += n + (-n) % LANES
    return out


def _adamw(order, w, g, m, v, name):
    shape = w.shape
    cols = shape[-1]
    w2, g2, m2, v2 = (t.reshape(-1, cols) for t in (w, g, m, v))
    R = w2.shape[0]
    tr = _pick(R, max(8, (1 << 20) // (4 * cols)), 8)

    def body(w_ref, g_ref, m_ref, v_ref, d_ref, mo_ref, vo_ref):
        d_ref[...], mo_ref[...], vo_ref[...] = _adam_math(w_ref[...], g_ref[...], m_ref[...], v_ref[...])

    blk = pl.BlockSpec((tr, cols), lambda i: (i, 0))
    outs = _call(
        order, body, [w2, g2, m2, v2], name=name, grid=(R // tr,), in_specs=[blk] * 4, out_specs=[blk] * 3,
        out_shape=[jax.ShapeDtypeStruct((R, cols), F32)] * 3, sem=("parallel",))
    return tuple(o.reshape(shape) for o in outs)


def _adam_math(w, g, m, v):
    mn = ADAM_B1 * m + (1.0 - ADAM_B1) * g
    vn = ADAM_B2 * v + (1.0 - ADAM_B2) * (g * g)
    m_hat = mn / (1.0 - ADAM_B1 ** ADAM_STEP)
    v_hat = vn / (1.0 - ADAM_B2 ** ADAM_STEP)
    return -ADAM_LR * (m_hat / (jnp.sqrt(v_hat) + ADAM_EPS) + ADAM_WD * w), mn, vn


def _chip_sum_adamw(order, w, psum, parts, m, v, name):
    _, r, cols = parts.shape
    tr = _pick(r, max(16, (6 << 20) // (38 * cols)), 16)

    def my_chip():
        return 2 * lax.axis_index("x") + lax.axis_index("y")

    def body(w_ref, own_ref, p0, p1, p2, p3, m_ref, v_ref, g_ref, d_ref, mo_ref, vo_ref):
        own = own_ref[...].astype(F32)
        g = None
        for k, p in enumerate((p0, p1, p2, p3)):
            term = jnp.where(my_chip() == k, own, p[...].astype(F32))
            g = term if g is None else g + term
        g_ref[...] = g
        d_ref[...], mo_ref[...], vo_ref[...] = _adam_math(w_ref[...], g, m_ref[...], v_ref[...])

    def slot(k):
        return pl.BlockSpec((None, tr, cols), lambda i: (jnp.where(my_chip() == k, (k + 1) % 4, k), i, 0))

    blk = pl.BlockSpec((tr, cols), lambda i: (i, 0))
    return _call(
        order, body, [w, psum, parts, parts, parts, parts, m, v], name=name, grid=(r // tr,),
        in_specs=[blk, pl.BlockSpec((None, tr, cols), lambda i: (my_chip(), i, 0))]
        + [slot(k) for k in range(4)] + [blk, blk],
        out_specs=[blk] * 4, out_shape=[jax.ShapeDtypeStruct((r, cols), F32)] * 4, sem=("parallel",))


class _GradReduce:
    def __init__(self, tag, grads, names):
        self.tag, self.grads, self.names = tag, list(grads), names
        self.pair = self.chip = self.psums = None

    def pair_start(self, order):
        bufs = []
        for g in self.grads:
            bufs += [g, jax.ShapeDtypeStruct((4, g.shape[0] // 8, g.shape[1]), g.dtype)]
        self.pair = _Xfer("pair_" + self.tag, bufs, _pair_send(len(self.grads)))
        self.pair.start(order)

    def pair_sum_chip_start(self, order):
        bufs = self.pair.wait(order)
        self.psums = [_pair_sum(order, bufs[2 * w], bufs[2 * w + 1], "pair_sum_" + nm)
                      for w, nm in enumerate(self.names)]
        cbufs = []
        for p in self.psums:
            cbufs += [p, jax.ShapeDtypeStruct(p.shape, p.dtype)]
        self.chip = _Xfer("chip_" + self.tag, cbufs, _chip_send(len(self.psums)))
        self.chip.start(order)

    def finish(self, order):
        bufs = self.chip.wait(order)
        return [(bufs[2 * w], bufs[2 * w + 1]) for w in range(len(self.names))]


def kernel(x, meta_tokens, ln_in_g, ln_in_b, w_in, b_gate, attn_sinks, w_attn_up, w_pool_grp, pool_scale, w_pool_up, w_out, ln1_g, ln1_b, w_ffn_in, w_ffn_down, ln2_g, ln2_b, loss_target, m_meta_tokens, m_ln_in_g, m_ln_in_b, m_w_in, m_b_gate, m_attn_sinks, m_w_attn_up, m_w_pool_grp, m_pool_scale, m_w_pool_up, m_w_out, m_ln1_g, m_ln1_b, m_w_ffn_in, m_w_ffn_down, m_ln2_g, m_ln2_b, v_meta_tokens, v_ln_in_g, v_ln_in_b, v_w_in, v_b_gate, v_attn_sinks, v_w_attn_up, v_w_pool_grp, v_pool_scale, v_w_pool_up, v_w_out, v_ln1_g, v_ln1_b, v_w_ffn_in, v_w_ffn_down, v_ln2_g, v_ln2_b):
    S, D = x.shape[1], x.shape[2]
    Tp = S + BLOCK
    NQ = attn_sinks.shape[-1]
    ATTN = NQ * HEAD_DIM
    KVW = ATTN // Q_PER_KV
    POOL = pool_scale.shape[-1]
    IN = 8 * w_in.shape[2]
    FF = 8 * w_ffn_down.shape[1]
    uoff = ATTN + 2 * KVW
    goff = uoff + POOL
    gw = POOL // 4
    dcols = D // 8
    assert IN == goff + 2 * D and w_ffn_in.shape[2] * 8 == 2 * FF

    xi, yi, ci = _place()
    dev = 4 * xi + 2 * yi + ci
    x2, tgt = x[0], loss_target[0]
    order = _Order()

    def place_cols(a):
        return lax.dynamic_update_slice(jnp.zeros(a.shape[:-1] + (D,), F32), a, (0,) * (a.ndim - 1) + (dev * dcols,))

    small = _all_reduce_small(order, _pack([place_cols(meta_tokens), place_cols(b_gate[0])]), "small_inputs_gather")
    meta_full, bgate_full = _unpack(small, [(N_META, D), (2, D)])
    meta_pad = jnp.pad(meta_full, ((META_ROW0, 0), (0, 0)))

    wgrp_rows = w_pool_grp[0].reshape(4 * (gw // 8), gw)
    full_in = _place_own(order, w_in[0].T, "own_w_in")
    g_in = _Xfer("gather_w_in_near", [full_in], _gather_send([full_in], NEAR))
    g_in.start(order)
    mix_names = ["w_attn_up", "w_pool_grp", "w_pool_up", "w_out"]
    mix_shards = [w_attn_up[0].T, wgrp_rows, w_pool_up[0].T, w_out[0]]
    full_mix = [_place_own(order, s, "own_" + nm) for s, nm in zip(mix_shards, mix_names)]
    full_ffn = _place_own(order, w_ffn_in[0].T, "own_w_ffn_in")

    ln_in_g2, ln_in_b2 = ln_in_g.reshape(1, D), ln_in_b.reshape(1, D)
    tab = _rope_table(S)

    h0, h0b = _ln_in_fwd(order, x2, meta_pad, ln_in_g2, ln_in_b2)
    bufs = g_in.wait(order)
    d_in = _Xfer("gather_w_in_diag", bufs, _gather_send(bufs, ("diag",), sibling=False))
    d_in.start(order)
    f_in = _Xfer("forward_w_in_near", d_in.bufs, _gather_forward(bufs, NEAR))
    f_in.start(order)
    full_down = _place_own(order, w_ffn_down[0], "own_w_ffn_down")
    bufs = f_in.wait(order)
    proj = _mm_nt_half(order, h0b, bufs[0], far=False, tm=1408, tn=640, name="proj_near")
    bufs = d_in.wait(order, bufs)
    fd_in = _Xfer("forward_w_in_diag", bufs, _gather_forward(bufs, ("diag",)))
    fd_in.start(order)
    ag_mix = _Xfer("gather_mixers", full_mix, _gather_send(full_mix))
    ag_mix.start(order)
    g_ffn = _Xfer("gather_w_ffn_in_near", [full_ffn], _gather_send([full_ffn], NEAR))
    g_ffn.start(order)
    (winT,) = fd_in.wait(order)
    proj = _mm_nt_half(order, h0b, winT, far=True, tm=1408, tn=640, name="proj_far", into=proj)

    att = _attn_fwd(order, proj, tab, attn_sinks, S, ATTN, KVW)
    full_mix = ag_mix.wait(order)
    fw_mix = _Xfer("forward_mixers", full_mix, _gather_forward(full_mix))
    fw_mix.start(order)
    wattT, wgrp_g, wpupT, wout = fw_mix.wait(order)
    wgrp = wgrp_g.reshape(8, 4, gw // 8, gw).transpose(1, 0, 2, 3).reshape(4, gw, gw)

    ps = _pool_fwd(order, proj, wgrp, pool_scale, S, uoff, POOL)
    a_out = _mm(order, att, wattT, kind="nt", out_dtype=F32, tm=1024, tn=1024, name="attn_up")
    p_out = _mm(order, ps, wpupT, kind="nt", out_dtype=F32, tm=1024, tn=1024, name="pool_up")
    mixed = _gate_mix(order, proj, bgate_full, a_out, p_out, S, D, goff)
    y1 = _mm(order, mixed, wout, kind="nn", out_dtype=F32, tm=1024, tn=1024, name="out_proj")

    bufs = g_ffn.wait(order)
    d_ffn = _Xfer("gather_w_ffn_in_diag", bufs, _gather_send(bufs, ("diag",), sibling=False))
    d_ffn.start(order)
    f_ffn = _Xfer("forward_w_ffn_in_near", d_ffn.bufs, _gather_forward(bufs, NEAR))
    f_ffn.start(order)
    h1, h1b = _ln1_fwd(order, h0, y1, ln1_g, ln1_b)
    bufs = f_ffn.wait(order)
    f_near = _ffn_in_near(order, h1b, bufs[0], FF)
    bufs = d_ffn.wait(order, bufs)
    fd_ffn = _Xfer("forward_w_ffn_in_diag", bufs, _gather_forward(bufs, ("diag",)))
    fd_ffn.start(order)
    ag_down = _Xfer("gather_w_ffn_down", [full_down], _gather_send([full_down]))
    ag_down.start(order)
    (wffnT,) = fd_ffn.wait(order)
    f_far, act = _ffn_in_far(order, h1b, wffnT, f_near)

    (full_down,) = ag_down.wait(order)
    fw_down = _Xfer("forward_w_ffn_down", [full_down], _gather_forward([full_down]))
    fw_down.start(order)
    (wdown,) = fw_down.wait(order)
    y2 = _mm(order, act, wdown, kind="nn", out_dtype=F32, tm=1024, tn=1024, tk=5504, name="ffn_down")

    dz2, dz2b, dg2, db2, loss_part = _ln2_loss_bwd(order, h1, y2, tgt, ln2_g, ln2_b)
    df = _d_act_swiglu(order, dz2b, wdown, f_near, f_far)
    gwdown = _mm(order, act, dz2b, kind="tn", out_dtype=BF16, tm=256, tn=2048, name="d_ffn_down")
    rs_down = _GradReduce("w_ffn_down", [gwdown], ["w_ffn_down"])
    rs_down.pair_start(order)
    gwffnT = _mm(order, df, h1b, kind="tn", out_dtype=BF16, tm=256, tn=2048, name="d_ffn_in", a_lead="halves")
    rs_down.pair_sum_chip_start(order)
    rs_ffn = _GradReduce("w_ffn_in", [gwffnT], ["w_ffn_in"])
    rs_ffn.pair_start(order)
    dh1 = _mm(order, df, wffnT, kind="nn", out_dtype=F32, tm=1024, tn=1024, tk=5504, name="d_h1", a_lead="halves")
    rs_ffn.pair_sum_chip_start(order)
    dz1, dz1b, dg1, db1 = _ln1_bwd(order, h0, y1, ln1_g, dh1, dz2)
    gwout = _mm(order, mixed, dz1b, kind="tn", out_dtype=BF16, tm=512, tn=1024, name="d_out_proj")
    rs_out = _GradReduce("w_out", [gwout], ["w_out"])
    rs_out.pair_start(order)
    dmixed = _mm(order, dz1b, wout, kind="nt", out_dtype=F32, tm=1024, tn=1024, name="d_mixed")
    rs_out.pair_sum_chip_start(order)

    dproj = _zero_meta_block(order, Tp, IN)
    dap, dproj, dbgate = _gate_bwd(order, proj, bgate_full, a_out, p_out, dmixed, dproj, S, D, goff)
    gwattT = _mm(order, dap, att, kind="tn", out_dtype=BF16, tm=512, tn=1024, name="d_attn_up", a_lead=0)
    datt = _mm(order, dap, wattT, kind="nn", out_dtype=BF16, tm=1024, tn=1024, name="d_att", a_lead=0)
    gwpupT = _mm(order, dap, ps, kind="tn", out_dtype=BF16, tm=512, tn=1024, name="d_pool_up", a_lead=1)
    dps = _mm(order, dap, wpupT, kind="nn", out_dtype=F32, tm=1024, tn=1024, name="d_ps", a_lead=1)
    dpl, gwgrp, dscale = _pool_bwd_mix(order, proj, wgrp, pool_scale, dps, S, uoff, POOL)
    gwgrp_rows = gwgrp.reshape(4, 8, gw // 8, gw).transpose(1, 0, 2, 3).reshape(8 * 4 * (gw // 8), gw).astype(BF16)
    rs_mix = _GradReduce("mixers", [gwattT, gwgrp_rows, gwpupT], ["w_attn_up", "w_pool_grp", "w_pool_up"])
    rs_mix.pair_start(order)
    dproj = _pool_bwd_window(order, dpl, dproj, S, uoff, POOL)
    rs_mix.pair_sum_chip_start(order)
    dproj, dk, dv, dsink = _attn_bwd(order, proj, tab, attn_sinks, datt, dproj, S, ATTN, KVW)
    dproj = _put_dkv(order, dk, dv, dproj, ATTN)

    weights = dict(meta_tokens=meta_tokens, ln_in_g=ln_in_g, ln_in_b=ln_in_b, w_in=w_in, b_gate=b_gate,
                   attn_sinks=attn_sinks, w_attn_up=w_attn_up, w_pool_grp=w_pool_grp, pool_scale=pool_scale,
                   w_pool_up=w_pool_up, w_out=w_out, ln1_g=ln1_g, ln1_b=ln1_b, w_ffn_in=w_ffn_in,
                   w_ffn_down=w_ffn_down, ln2_g=ln2_g, ln2_b=ln2_b)
    ms = dict(meta_tokens=m_meta_tokens, ln_in_g=m_ln_in_g, ln_in_b=m_ln_in_b, w_in=m_w_in, b_gate=m_b_gate,
              attn_sinks=m_attn_sinks, w_attn_up=m_w_attn_up, w_pool_grp=m_w_pool_grp, pool_scale=m_pool_scale,
              w_pool_up=m_w_pool_up, w_out=m_w_out, ln1_g=m_ln1_g, ln1_b=m_ln1_b, w_ffn_in=m_w_ffn_in,
              w_ffn_down=m_w_ffn_down, ln2_g=m_ln2_g, ln2_b=m_ln2_b)
    vs = dict(meta_tokens=v_meta_tokens, ln_in_g=v_ln_in_g, ln_in_b=v_ln_in_b, w_in=v_w_in, b_gate=v_b_gate,
              attn_sinks=v_attn_sinks, w_attn_up=v_w_attn_up, w_pool_grp=v_w_pool_grp, pool_scale=v_pool_scale,
              w_pool_up=v_w_pool_up, w_out=v_w_out, ln1_g=v_ln1_g, ln1_b=v_ln1_b, w_ffn_in=v_w_ffn_in,
              w_ffn_down=v_w_ffn_down, ln2_g=v_ln2_g, ln2_b=v_ln2_b)
    grads, deltas, new_ms, new_vs = {}, {}, {}, {}

    def update(nm, g):
        g = g.reshape(weights[nm].shape)
        grads[nm] = g
        deltas[nm], new_ms[nm], new_vs[nm] = _adamw(order, weights[nm], g, ms[nm], vs[nm], "adamw_" + nm)

    def update_reduced(nm, bufs, transposed=False):
        psum, parts = bufs
        if transposed:
            to2d, back = (lambda t: t[0].T), (lambda t: t.T[None])
        else:
            to2d, back = (lambda t: t.reshape(parts.shape[1:])), (lambda t: t.reshape(weights[nm].shape))
        outs = _chip_sum_adamw(order, to2d(weights[nm]), psum, parts, to2d(ms[nm]), to2d(vs[nm]), "adamw_" + nm)
        grads[nm], deltas[nm], new_ms[nm], new_vs[nm] = (back(t) for t in outs)

    gwinT = _mm(order, dproj, h0b, kind="tn", out_dtype=BF16, tm=512, tn=1024, name="d_w_in")
    rs_in = _GradReduce("w_in", [gwinT], ["w_in"])
    rs_in.pair_start(order)
    update_reduced("w_ffn_down", rs_down.finish(order)[0])
    rs_in.pair_sum_chip_start(order)
    dh0 = _mm(order, dproj, winT, kind="nn", out_dtype=F32, tm=1408, tn=1024, tk=2560, name="d_h0")
    dxin, dg_in, db_in = _ln_in_bwd(order, x2, meta_pad, ln_in_g2, dh0, dz1)
    grad_x = dxin[:S][None]
    dmeta = dxin[S + META_ROW0:]

    small_shapes = [(D,), (D,), (1, D), (1, D), (1, D), (1, D), (1, POOL), (1, NQ), (), (N_META, D), (2, D)]
    red = _all_reduce_small(order, _pack([dg_in, db_in, dg1, db1, dg2, db2, dscale, dsink[:, :, 0], loss_part,
                                          dmeta, dbgate]), "small_grads_all_reduce")

    update_reduced("w_ffn_in", rs_ffn.finish(order)[0], transposed=True)
    update_reduced("w_out", rs_out.finish(order)[0])
    b_att, b_grp, b_pup = rs_mix.finish(order)
    update("w_attn_up", _chip_sum(order, *b_att, "chip_sum_w_attn_up").T)
    update_reduced("w_pool_grp", b_grp)
    update("w_pool_up", _chip_sum(order, *b_pup, "chip_sum_w_pool_up").T)

    (g_ln_in_g, g_ln_in_b, g_ln1_g, g_ln1_b, g_ln2_g, g_ln2_b, g_scale, g_sinks, loss_sum, g_meta_full,
     g_bgate_full) = _unpack(red, small_shapes)
    loss = 0.5 * loss_sum
    update("meta_tokens", lax.dynamic_slice(g_meta_full, (0, dev * dcols), (N_META, dcols)))
    update("b_gate", lax.dynamic_slice(g_bgate_full, (0, dev * dcols), (2, dcols)))
    for nm, g in (("ln_in_g", g_ln_in_g), ("ln_in_b", g_ln_in_b), ("ln1_g", g_ln1_g), ("ln1_b", g_ln1_b),
                  ("ln2_g", g_ln2_g), ("ln2_b", g_ln2_b), ("pool_scale", g_scale), ("attn_sinks", g_sinks)):
        update(nm, g)

    update_reduced("w_in", rs_in.finish(order)[0], transposed=True)

    names = list(weights)
    return (loss, grad_x, *[grads[n] for n in names], *[deltas[n] for n in names],
            *[new_ms[n] for n in names], *[new_vs[n] for n in names])
```

```python
import jax
import jax.numpy as jnp
from jax import lax
from jax.experimental import pallas as pl
from jax.experimental.pallas import tpu as pltpu

F32 = jnp.float32
BF16 = jnp.bfloat16
MESH = pl.DeviceIdType.MESH

N_META = 16
HEAD_DIM = 64
Q_PER_KV = 8
WINDOW = 128
BLOCK = 128
ATTN_SCALE = HEAD_DIM ** -0.5
ROPE_DIM = HEAD_DIM // 4
ROPE_THETA = 500000.0
NEG_INF = -1e30
POOL_WINDOWS = (2, 4, 8, 16)
LN_EPS = 1e-5
DN_ALPHA = 2.0 ** 0.25
ADAM_LR = 0.001
ADAM_B1 = 0.9
ADAM_B2 = 0.999
ADAM_EPS = 1e-08
ADAM_WD = 0.01
ADAM_STEP = 10

LANES = 128
META_ROW0 = BLOCK - N_META
VMEM_LIMIT = 56 * 1024 * 1024

ANY = pl.BlockSpec(memory_space=pl.ANY)
HBM = pl.BlockSpec(memory_space=pltpu.HBM)
SEM = pl.BlockSpec(memory_space=pltpu.SEMAPHORE)
EFFECT = pltpu.SideEffectType.DATAFLOW_SIDE_EFFECTING


def _params(sem=None, **kw):
    return pltpu.CompilerParams(dimension_semantics=sem, vmem_limit_bytes=VMEM_LIMIT, **kw)


class _Order:
    def __init__(self):
        self.last = None


def _call(order, body, operands, *, name, in_specs, out_specs, out_shape, grid=(), scratch=(), sem=None,
          aliases=None, prefetch=()):
    n_in, npf = len(operands), len(prefetch)
    tok = order.last
    if tok is not None and any(tok is op for op in operands):
        tok = None

    def wrapped(*refs):
        refs = list(refs)
        if tok is not None:
            del refs[npf + n_in]
        body(*refs)

    specs = list(in_specs) + ([ANY] if tok is not None else [])
    ops = list(operands) + ([tok] if tok is not None else [])
    if npf:
        out = pl.pallas_call(
            wrapped, name=name, out_shape=out_shape, compiler_params=_params(sem),
            grid_spec=pltpu.PrefetchScalarGridSpec(num_scalar_prefetch=npf, grid=grid, in_specs=specs,
                                                   out_specs=out_specs, scratch_shapes=list(scratch)),
        )(*prefetch, *ops)
    else:
        out = pl.pallas_call(
            wrapped, name=name, grid=grid, in_specs=specs, out_specs=out_specs, out_shape=out_shape,
            scratch_shapes=list(scratch), input_output_aliases=aliases or {}, compiler_params=_params(sem),
        )(*ops)
    order.last = out[0] if isinstance(out, (list, tuple)) else out
    return out


def _pick(dim, pref, mult=LANES):
    best = None
    t = mult
    while t <= min(dim, pref):
        if dim % t == 0:
            best = t
        t += mult
    return dim if best is None else best


_DIMS = {"nn": (((1,), (0,)), ((), ())), "nt": (((1,), (1,)), ((), ())), "tn": (((0,), (0,)), ((), ()))}


def _mm(order, a, b, *, kind, out_dtype, tm, tn, tk=None, name, a_lead=None):
    a2 = a.shape[-2:]
    halves = a_lead == "halves"
    if halves:
        a2 = (a2[0], 2 * a2[1])
    if kind == "tn":
        K, M = a2
    else:
        M, K = a2
    N = b.shape[0] if kind == "nt" else b.shape[1]
    half_cols = a2[1] // 2
    tm = _pick(half_cols if halves and kind == "tn" else M, tm)
    tn = _pick(N, tn)
    tk = K if tk is None else _pick(half_cols if halves and kind != "tn" else K, tk)
    nm, nn_, nk = M // tm, N // tn, K // tk
    a_bytes = M * K * a.dtype.itemsize
    b_bytes = N * K * b.dtype.itemsize
    i_outer = (a_bytes + nm * b_bytes <= b_bytes + nn_ * a_bytes) if nk == 1 else True

    def ij(g0, g1):
        return (g0, g1) if i_outer else (g1, g0)

    def a_map(g0, g1, k):
        i, _ = ij(g0, g1)
        if halves:
            per = half_cols // (tm if kind == "tn" else tk)
            return (i // per, k, i % per) if kind == "tn" else (k // per, i, k % per)
        idx = (k, i) if kind == "tn" else (i, k)
        return idx if a_lead is None else (a_lead,) + idx

    def b_map(g0, g1, k):
        _, j = ij(g0, g1)
        return (j, k) if kind == "nt" else (k, j)

    def o_map(g0, g1, k):
        return ij(g0, g1)

    a_blk = (tk, tm) if kind == "tn" else (tm, tk)
    if a_lead is not None:
        a_blk = (None,) + a_blk
    b_blk = (tn, tk) if kind == "nt" else (tk, tn)

    in_place = out_dtype == F32

    def body(a_ref, b_ref, o_ref, *acc):
        p = lax.dot_general(a_ref[...], b_ref[...], _DIMS[kind], preferred_element_type=F32)
        if nk == 1:
            o_ref[...] = p.astype(o_ref.dtype)
        else:
            k = pl.program_id(2)
            acc_ref = o_ref if in_place else acc[0]

            @pl.when(k == 0)
            def _():
                acc_ref[...] = p

            @pl.when(k > 0)
            def _():
                acc_ref[...] += p

            if not in_place:
                @pl.when(k == nk - 1)
                def _():
                    o_ref[...] = acc_ref[...].astype(o_ref.dtype)

    grid = (nm, nn_, nk) if i_outer else (nn_, nm, nk)
    return _call(
        order, body, [a, b], name=name, grid=grid,
        in_specs=[pl.BlockSpec(a_blk, a_map), pl.BlockSpec(b_blk, b_map)],
        out_specs=pl.BlockSpec((tm, tn), o_map),
        out_shape=jax.ShapeDtypeStruct((M, N), out_dtype),
        scratch=[] if nk == 1 or in_place else [pltpu.VMEM((tm, tn), F32)],
        sem=("parallel", "parallel", "arbitrary"))


def _x_half(far):
    x = lax.axis_index("x")
    return 1 - x if far else x


def _mm_nt_half(order, a, bT, *, far, tm, tn, name, into=None):
    M, K = a.shape
    N = bT.shape[0]
    tm, tn = _pick(M, tm), _pick(N // 2, tn)
    nh = N // 2 // tn

    def body(a_ref, b_ref, *rest):
        rest[-1][...] = lax.dot_general(a_ref[...], b_ref[...], _DIMS["nt"], preferred_element_type=F32)

    return _call(
        order, body, [a, bT] + ([] if into is None else [into]), name=name, grid=(M // tm, nh),
        in_specs=[pl.BlockSpec((tm, K), lambda i, j: (i, 0)),
                  pl.BlockSpec((tn, K), lambda i, j: (_x_half(far) * nh + j, 0))] + ([] if into is None else [ANY]),
        out_specs=pl.BlockSpec((tm, tn), lambda i, j: (i, _x_half(far) * nh + j)),
        out_shape=jax.ShapeDtypeStruct((M, N), F32), aliases=None if into is None else {2: 0},
        sem=("parallel", "parallel"))


def _ln_stats(z):
    mu = jnp.mean(z, axis=-1, keepdims=True)
    zc = z - mu
    var = jnp.mean(zc * zc, axis=-1, keepdims=True)
    rstd = lax.rsqrt(var + LN_EPS)
    return zc * rstd, rstd


def _ln_bwd(dy, xhat, rstd, g):
    dxh = dy * g
    m1 = jnp.mean(dxh, axis=-1, keepdims=True)
    m2 = jnp.mean(dxh * xhat, axis=-1, keepdims=True)
    return rstd * (dxh - m1 - xhat * m2)


def _ln_in_fwd(order, x, meta_pad, g, b):
    S, D = x.shape
    nb = S // BLOCK

    def body(x_ref, mp_ref, g_ref, b_ref, h_ref, hb_ref):
        is_meta = pl.program_id(0) == nb
        xin = jnp.where(is_meta, mp_ref[...], x_ref[...])
        xhat, _ = _ln_stats(xin)
        y = xhat * g_ref[...] + b_ref[...]
        h_ref[...] = y
        hb_ref[...] = y.astype(BF16)

    row = pl.BlockSpec((BLOCK, D), lambda i: (i, 0))
    vec = pl.BlockSpec((1, D), lambda i: (0, 0))
    return _call(
        order, body, [x, meta_pad, g, b], name="ln_in_fwd", grid=(nb + 1,),
        in_specs=[pl.BlockSpec((BLOCK, D), lambda i: (jnp.minimum(i, nb - 1), 0)),
                  pl.BlockSpec((BLOCK, D), lambda i: (0, 0)), vec, vec],
        out_specs=[row, row],
        out_shape=[jax.ShapeDtypeStruct((S + BLOCK, D), F32), jax.ShapeDtypeStruct((S + BLOCK, D), BF16)],
        sem=("parallel",))


def _ln_in_bwd(order, x, meta_pad, g, dh0, dz1):
    S, D = x.shape
    nb = S // BLOCK

    def body(x_ref, mp_ref, g_ref, dh_ref, dz_ref, dx_ref, dg_ref, db_ref):
        i = pl.program_id(0)
        is_meta = i == nb
        xin = jnp.where(is_meta, mp_ref[...], x_ref[...])
        xhat, rstd = _ln_stats(xin)
        dy = dh_ref[...] + jnp.where(is_meta, 0.0, DN_ALPHA) * dz_ref[...]
        dx_ref[...] = _ln_bwd(dy, xhat, rstd, g_ref[...])

        @pl.when(i == 0)
        def _():
            dg_ref[...] = jnp.zeros_like(dg_ref)
            db_ref[...] = jnp.zeros_like(db_ref)

        dg_ref[...] += jnp.sum(dy * xhat, axis=0, keepdims=True)
        db_ref[...] += jnp.sum(dy, axis=0, keepdims=True)

    row = pl.BlockSpec((BLOCK, D), lambda i: (i, 0))
    rowx = pl.BlockSpec((BLOCK, D), lambda i: (jnp.minimum(i, nb - 1), 0))
    vec = pl.BlockSpec((1, D), lambda i: (0, 0))
    return _call(
        order, body, [x, meta_pad, g, dh0, dz1], name="ln_in_bwd", grid=(nb + 1,),
        in_specs=[rowx, pl.BlockSpec((BLOCK, D), lambda i: (0, 0)), vec, row, rowx],
        out_specs=[row, vec, vec],
        out_shape=[jax.ShapeDtypeStruct((S + BLOCK, D), F32), jax.ShapeDtypeStruct((1, D), F32),
                   jax.ShapeDtypeStruct((1, D), F32)],
        sem=("arbitrary",))


def _ln1_fwd(order, h0, y1, g, b):
    S, D = y1.shape
    tm = _pick(S, BLOCK, 8)

    def body(h_ref, y_ref, g_ref, b_ref, o_ref, ob_ref):
        xhat, _ = _ln_stats(DN_ALPHA * h_ref[...] + y_ref[...])
        y = xhat * g_ref[...] + b_ref[...]
        o_ref[...] = y
        ob_ref[...] = y.astype(BF16)

    row = pl.BlockSpec((tm, D), lambda i: (i, 0))
    vec = pl.BlockSpec((1, D), lambda i: (0, 0))
    return _call(
        order, body, [h0, y1, g, b], name="ln1_fwd", grid=(S // tm,), in_specs=[row, row, vec, vec],
        out_specs=[row, row],
        out_shape=[jax.ShapeDtypeStruct((S, D), F32), jax.ShapeDtypeStruct((S, D), BF16)],
        sem=("parallel",))


def _ln1_bwd(order, h0, y1, g, dh1, dz2):
    S, D = y1.shape
    tm = _pick(S, BLOCK, 8)

    def body(h_ref, y_ref, g_ref, dh_ref, dz2_ref, dz_ref, dzb_ref, dg_ref, db_ref):
        i = pl.program_id(0)
        xhat, rstd = _ln_stats(DN_ALPHA * h_ref[...] + y_ref[...])
        dy = dh_ref[...] + DN_ALPHA * dz2_ref[...]
        dz = _ln_bwd(dy, xhat, rstd, g_ref[...])
        dz_ref[...] = dz
        dzb_ref[...] = dz.astype(BF16)

        @pl.when(i == 0)
        def _():
            dg_ref[...] = jnp.zeros_like(dg_ref)
            db_ref[...] = jnp.zeros_like(db_ref)

        dg_ref[...] += jnp.sum(dy * xhat, axis=0, keepdims=True)
        db_ref[...] += jnp.sum(dy, axis=0, keepdims=True)

    row = pl.BlockSpec((tm, D), lambda i: (i, 0))
    vec = pl.BlockSpec((1, D), lambda i: (0, 0))
    return _call(
        order, body, [h0, y1, g, dh1, dz2], name="ln1_bwd", grid=(S // tm,),
        in_specs=[row, row, vec, row, row], out_specs=[row, row, vec, vec],
        out_shape=[jax.ShapeDtypeStruct((S, D), F32), jax.ShapeDtypeStruct((S, D), BF16),
                   jax.ShapeDtypeStruct((1, D), F32), jax.ShapeDtypeStruct((1, D), F32)],
        sem=("arbitrary",))


def _ln2_loss_bwd(order, h1, y2, target, g, b):
    S, D = y2.shape
    tm = _pick(S, BLOCK, 8)

    def body(h_ref, y_ref, t_ref, g_ref, b_ref, dz_ref, dzb_ref, dg_ref, db_ref, loss_ref):
        i = pl.program_id(0)
        xhat, rstd = _ln_stats(DN_ALPHA * h_ref[...] + y_ref[...])
        diff = xhat * g_ref[...] + b_ref[...] - t_ref[...]
        dy = diff / D
        dz = _ln_bwd(dy, xhat, rstd, g_ref[...])
        dz_ref[...] = dz
        dzb_ref[...] = dz.astype(BF16)

        @pl.when(i == 0)
        def _():
            dg_ref[...] = jnp.zeros_like(dg_ref)
            db_ref[...] = jnp.zeros_like(db_ref)
            loss_ref[...] = jnp.zeros_like(loss_ref)

        dg_ref[...] += jnp.sum(dy * xhat, axis=0, keepdims=True)
        db_ref[...] += jnp.sum(dy, axis=0, keepdims=True)
        loss_ref[...] += jnp.sum(jnp.mean(diff * diff, axis=-1, keepdims=True), axis=0, keepdims=True)

    row = pl.BlockSpec((tm, D), lambda i: (i, 0))
    vec = pl.BlockSpec((1, D), lambda i: (0, 0))
    one = pl.BlockSpec((1, 1), lambda i: (0, 0))
    return _call(
        order, body, [h1, y2, target, g, b], name="ln2_loss_bwd", grid=(S // tm,),
        in_specs=[row, row, row, vec, vec], out_specs=[row, row, vec, vec, one],
        out_shape=[jax.ShapeDtypeStruct((S, D), F32), jax.ShapeDtypeStruct((S, D), BF16),
                   jax.ShapeDtypeStruct((1, D), F32), jax.ShapeDtypeStruct((1, D), F32),
                   jax.ShapeDtypeStruct((1, 1), F32)],
        sem=("arbitrary",))


def _rope_table(S):
    r = jnp.arange(S + BLOCK)
    pos = jnp.where(r < S, r + N_META, jnp.maximum(r - (S + META_ROW0), 0))
    half = ROPE_DIM // 2
    lane = jnp.arange(LANES) % HEAD_DIM
    inv_freq = ROPE_THETA ** (-(lane % half).astype(F32) * 2.0 / ROPE_DIM)
    ang = pos.astype(F32)[:, None] * inv_freq[None, :]
    cos, sin = jnp.cos(ang), jnp.sin(ang)
    c = jnp.where(lane < ROPE_DIM, cos, 1.0)
    sa = jnp.where(lane < half, -sin, 0.0)
    sb = jnp.where((lane >= half) & (lane < ROPE_DIM), sin, 0.0)
    return jnp.concatenate([c, sa, sb], axis=1).astype(F32)


def _rope(x, tab):
    h = ROPE_DIM // 2
    return (x * tab[:, :LANES] + pltpu.roll(x, LANES - h, 1) * tab[:, LANES:2 * LANES]
            + pltpu.roll(x, h, 1) * tab[:, 2 * LANES:])


def _rope_t(dy, tab):
    h = ROPE_DIM // 2
    return (dy * tab[:, :LANES] + pltpu.roll(dy * tab[:, LANES:2 * LANES], h, 1)
            + pltpu.roll(dy * tab[:, 2 * LANES:], LANES - h, 1))


NKEY = N_META + 2 * BLOCK


def _attn_tiles(g, n, S, sink_ref, q_ref, k_ref, v_ref, tab_ref):
    NQG = Q_PER_KV // 2
    R = NQG * BLOCK
    halfsel = (g % 2).astype(F32)
    prev = jnp.maximum(n - 1, 0)
    qrow = pl.ds(pl.multiple_of(n * BLOCK, BLOCK), BLOCK)
    prow = pl.ds(pl.multiple_of(prev * BLOCK, BLOCK), BLOCK)
    mrow = pl.ds(S + META_ROW0, N_META)

    tq = tab_ref[qrow, :]
    qf = q_ref[...]
    q4 = jnp.concatenate([_rope(qf[:, LANES * p:LANES * (p + 1)], tq) for p in range(NQG)], axis=0).astype(BF16)

    tk = jnp.concatenate([tab_ref[mrow, :], tab_ref[prow, :], tq], axis=0)
    kr = _rope(jnp.concatenate([k_ref[mrow, :], k_ref[prow, :], k_ref[qrow, :]], axis=0), tk)
    vr = jnp.concatenate([v_ref[mrow, :], v_ref[prow, :], v_ref[qrow, :]], axis=0)

    lane = lax.broadcasted_iota(jnp.int32, kr.shape, 1)
    own = jnp.where(lane < HEAD_DIM, 1.0 - halfsel, halfsel)

    def lo_hi(t):
        mine = t * own
        other = pltpu.roll(mine, HEAD_DIM, 1)
        lo = mine * (1.0 - halfsel) + other * halfsel
        hi = other * (1.0 - halfsel) + mine * halfsel
        return lo.astype(BF16), hi.astype(BF16)

    klo, khi = lo_hi(kr)
    vlo, vhi = lo_hi(vr)

    jj = lax.broadcasted_iota(jnp.int32, (BLOCK, R), 0)
    qi = lax.broadcasted_iota(jnp.int32, (BLOCK, R), 1) & (BLOCK - 1)
    in_cur = jj <= qi
    band_ok = in_cur | (jj > qi + jnp.where(n >= 1, 0, 2 * BLOCK))

    def soft(kk, parity):
        sk = jnp.concatenate(
            [jnp.full((1, BLOCK), sink_ref[0, Q_PER_KV * g + 2 * p + parity], F32) for p in range(NQG)], axis=1)
        s = lax.dot_general(kk, q4, _DIMS["nt"], preferred_element_type=F32) * ATTN_SCALE
        band = jnp.where(in_cur, s[N_META + BLOCK:], s[N_META:N_META + BLOCK])
        s = jnp.concatenate([s[:N_META], jnp.where(band_ok, band, NEG_INF)], axis=0)
        m = jnp.maximum(jnp.max(s, axis=0, keepdims=True), sk)
        p = jnp.exp(s - m)
        es = jnp.exp(sk - m)
        inv = 1.0 / (jnp.sum(p, axis=0, keepdims=True) + es)
        return p * inv, es * inv

    pe, sink_e = soft(klo, 0)
    po, sink_o = soft(khi, 1)
    return q4, tk, (klo, khi), (vlo, vhi), (pe, po), (sink_e, sink_o), own, in_cur


def _spread(t, in_cur):
    band = t[N_META:]
    return jnp.concatenate([t[:N_META], jnp.where(in_cur, 0.0, band), jnp.where(in_cur, band, 0.0)], axis=0)


def _attn_specs(S, ATTN, KVW):
    Tp = S + BLOCK
    koff, voff = ATTN // LANES, (ATTN + KVW) // LANES
    gw = Q_PER_KV * HEAD_DIM
    return [pl.BlockSpec(memory_space=pltpu.SMEM),
            pl.BlockSpec((BLOCK, gw), lambda g, n: (n, g)),
            pl.BlockSpec((Tp, LANES), lambda g, n: (0, koff + g // 2)),
            pl.BlockSpec((Tp, LANES), lambda g, n: (0, voff + g // 2)),
            pl.BlockSpec((Tp, 3 * LANES), lambda g, n: (0, 0))]


def _attn_fwd(order, proj, tab, sinks, S, ATTN, KVW):
    G = KVW // HEAD_DIM
    nb = S // BLOCK
    gw = Q_PER_KV * HEAD_DIM

    def body(sink_ref, q_ref, k_ref, v_ref, tab_ref, o_ref):
        g, n = pl.program_id(0), pl.program_id(1)
        _, _, _, (vlo, vhi), (pe, po), _, _, in_cur = _attn_tiles(g, n, S, sink_ref, q_ref, k_ref, v_ref, tab_ref)
        o4 = (lax.dot_general(_spread(pe, in_cur).astype(BF16), vlo, _DIMS["tn"], preferred_element_type=F32)
              + lax.dot_general(_spread(po, in_cur).astype(BF16), vhi, _DIMS["tn"], preferred_element_type=F32))
        o_ref[...] = jnp.concatenate(
            [o4[BLOCK * p:BLOCK * (p + 1)] for p in range(Q_PER_KV // 2)], axis=1).astype(BF16)

    return _call(
        order, body, [sinks, proj, proj, proj, tab], name="attn_fwd", grid=(G, nb),
        in_specs=_attn_specs(S, ATTN, KVW),
        out_specs=pl.BlockSpec((BLOCK, gw), lambda g, n: (n, g)),
        out_shape=jax.ShapeDtypeStruct((S, ATTN), BF16),
        sem=("parallel", "arbitrary"))


def _attn_bwd(order, proj, tab, sinks, da, dproj, S, ATTN, KVW):
    G = KVW // HEAD_DIM
    nb = S // BLOCK
    Tp = S + BLOCK
    NQG = Q_PER_KV // 2
    gw = Q_PER_KV * HEAD_DIM

    def body(sink_ref, q_ref, k_ref, v_ref, tab_ref, da_ref, dproj_in, dq_ref, dk_ref, dv_ref, ds_ref):
        del dproj_in
        g, n = pl.program_id(0), pl.program_id(1)
        q4, tk, (klo, khi), (vlo, vhi), (pe, po), (sink_e, sink_o), own, in_cur = _attn_tiles(
            g, n, S, sink_ref, q_ref, k_ref, v_ref, tab_ref)
        dof = da_ref[...]
        do4 = jnp.concatenate([dof[:, LANES * p:LANES * (p + 1)] for p in range(NQG)], axis=0)

        def grads(p, vv):
            dp = lax.dot_general(vv, do4, _DIMS["nt"], preferred_element_type=F32)
            dp = jnp.concatenate(
                [dp[:N_META], jnp.where(in_cur, dp[N_META + BLOCK:], dp[N_META:N_META + BLOCK])], axis=0)
            delta = jnp.sum(p * dp, axis=0, keepdims=True)
            return _spread(p * (dp - delta) * ATTN_SCALE, in_cur).astype(BF16), delta

        dse, delta_e = grads(pe, vlo)
        dso, delta_o = grads(po, vhi)

        dq4 = (lax.dot_general(dse, klo, _DIMS["tn"], preferred_element_type=F32)
               + lax.dot_general(dso, khi, _DIMS["tn"], preferred_element_type=F32))
        tq = tk[N_META + BLOCK:]
        dq_ref[...] = jnp.concatenate(
            [_rope_t(dq4[BLOCK * p:BLOCK * (p + 1)], tq) for p in range(NQG)], axis=1).astype(BF16)

        lane = lax.broadcasted_iota(jnp.int32, (NKEY, LANES), 1)

        def fold(lo_part, hi_part):
            t = jnp.where(lane < HEAD_DIM, lo_part, hi_part)
            return t + pltpu.roll(t, HEAD_DIM, 1)

        dk = _rope_t(fold(jnp.dot(dse, q4, preferred_element_type=F32),
                          jnp.dot(dso, q4, preferred_element_type=F32)), tk) * own
        dv = fold(jnp.dot(_spread(pe, in_cur).astype(BF16), do4, preferred_element_type=F32),
                  jnp.dot(_spread(po, in_cur).astype(BF16), do4, preferred_element_type=F32)) * own

        @pl.when((n == 0) & (g % 2 == 0))
        def _():
            dk_ref[...] = jnp.zeros_like(dk_ref)
            dv_ref[...] = jnp.zeros_like(dv_ref)

        @pl.when(n == 0)
        def _():
            ds_ref[...] = jnp.zeros_like(ds_ref)

        prev = jnp.maximum(n - 1, 0)
        qrow = pl.ds(pl.multiple_of(n * BLOCK, BLOCK), BLOCK)
        prow = pl.ds(pl.multiple_of(prev * BLOCK, BLOCK), BLOCK)
        mrow = pl.ds(S + META_ROW0, N_META)
        for ref, val in ((dk_ref, dk), (dv_ref, dv)):
            ref[mrow, :] += val[:N_META]
            ref[prow, :] += val[N_META:N_META + BLOCK]
            ref[qrow, :] += val[N_META + BLOCK:]

        srow = lax.broadcasted_iota(jnp.int32, (Q_PER_KV, LANES), 0)
        acc = jnp.zeros((Q_PER_KV, LANES), F32)
        for p in range(NQG):
            for parity, (sk, dl) in enumerate(((sink_e, delta_e), (sink_o, delta_o))):
                val = -jnp.sum(sk[:, BLOCK * p:BLOCK * (p + 1)] * dl[:, BLOCK * p:BLOCK * (p + 1)])
                acc = jnp.where(srow == 2 * p + parity, val, acc)
        ds_ref[0] += acc

    in_specs = _attn_specs(S, ATTN, KVW) + [pl.BlockSpec((BLOCK, gw), lambda g, n: (n, g)), ANY]
    slab = pl.BlockSpec((Tp, LANES), lambda g, n: (0, g // 2))
    return _call(
        order, body, [sinks, proj, proj, proj, tab, da, dproj], name="attn_bwd", grid=(G, nb), in_specs=in_specs,
        out_specs=[pl.BlockSpec((BLOCK, gw), lambda g, n: (n, g)), slab, slab,
                   pl.BlockSpec((1, Q_PER_KV, LANES), lambda g, n: (g, 0, 0))],
        out_shape=[jax.ShapeDtypeStruct(dproj.shape, BF16), jax.ShapeDtypeStruct((Tp, KVW), F32),
                   jax.ShapeDtypeStruct((Tp, KVW), F32), jax.ShapeDtypeStruct((G, Q_PER_KV, LANES), F32)],
        aliases={6: 0}, sem=("arbitrary", "arbitrary"))


def _zero_meta_block(order, Tp, IN):
    tc = _pick(IN, 4096)

    def body(o_ref):
        o_ref[...] = jnp.zeros_like(o_ref)

    return _call(
        order, body, [], name="dproj_zero_meta", grid=(IN // tc,), in_specs=[],
        out_specs=pl.BlockSpec((BLOCK, tc), lambda j: (Tp // BLOCK - 1, j)),
        out_shape=jax.ShapeDtypeStruct((Tp, IN), BF16), sem=("parallel",))


def _put_dkv(order, dk, dv, dproj, ATTN):
    Tp, KVW = dk.shape
    nkb = KVW // LANES
    koff = ATTN // LANES

    def body(dk_ref, dv_ref, dproj_in, o_ref):
        del dproj_in
        t = pl.program_id(0)
        o_ref[...] = jnp.where(t < nkb, dk_ref[...], dv_ref[...]).astype(BF16)

    src = pl.BlockSpec((Tp, LANES), lambda t: (0, t % nkb))
    return _call(
        order, body, [dk, dv, dproj], name="dproj_put_dkv", grid=(2 * nkb,), in_specs=[src, src, ANY],
        out_specs=pl.BlockSpec((Tp, LANES), lambda t: (0, koff + t)),
        out_shape=jax.ShapeDtypeStruct(dproj.shape, BF16), aliases={2: 0}, sem=("parallel",))


HALO = 16


def _window_sums(x, up):
    n = x.shape[0]
    out = []
    s = x
    for k in (1, 2, 4, 8):
        s = s + pltpu.roll(s, (n - k) if up else k, 0)
        out.append(s)
    return out


def _pool_specs(S, ub, gw, tm):
    meta_halo = (S + BLOCK - HALO) // HALO

    def main(g):
        return pl.BlockSpec((tm, gw), lambda i: (i, ub + g))

    def halo(g):
        return pl.BlockSpec((HALO, gw), lambda i: (jnp.where(i == 0, meta_halo, i * (tm // HALO) - 1), ub + g))

    return [main(g) for g in range(4)] + [halo(g) for g in range(4)]


def _pooled(main_refs, halo_refs, g):
    x = jnp.concatenate([halo_refs[g][...], main_refs[g][...]], axis=0)
    s = _window_sums(x, up=False)[g]
    return (s[HALO:] * (1.0 / POOL_WINDOWS[g]) - x[HALO:]).astype(BF16)


def _pool_fwd(order, proj, wgrp, scale, S, uoff, POOL):
    gw = POOL // 4
    tm = BLOCK

    def body(*refs):
        main, halo = refs[:4], refs[4:8]
        w_ref, sc_ref, o_ref = refs[8:]
        for g in range(4):
            mixed = jnp.dot(_pooled(main, halo, g), w_ref[g], preferred_element_type=F32)
            o_ref[:, gw * g:gw * (g + 1)] = (mixed * sc_ref[:, gw * g:gw * (g + 1)]).astype(BF16)

    return _call(
        order, body, [proj] * 8 + [wgrp, scale], name="pool_fwd", grid=(S // tm,),
        in_specs=_pool_specs(S, uoff // gw, gw, tm) + [
            pl.BlockSpec((4, gw, gw), lambda i: (0, 0, 0)), pl.BlockSpec((1, POOL), lambda i: (0, 0))],
        out_specs=pl.BlockSpec((tm, POOL), lambda i: (i, 0)),
        out_shape=jax.ShapeDtypeStruct((S, POOL), BF16), sem=("parallel",))


def _pool_bwd_mix(order, proj, wgrp, scale, dps, S, uoff, POOL):
    gw = POOL // 4
    tm = BLOCK

    def body(*refs):
        main, halo = refs[:4], refs[4:8]
        w_ref, sc_ref, dps_ref, dpl_ref, dw_ref, dsc_ref = refs[8:]
        i = pl.program_id(0)

        @pl.when(i == 0)
        def _():
            dw_ref[...] = jnp.zeros_like(dw_ref)
            dsc_ref[...] = jnp.zeros_like(dsc_ref)

        for g in range(4):
            cols = slice(gw * g, gw * (g + 1))
            pooled = _pooled(main, halo, g)
            mixed = jnp.dot(pooled, w_ref[g], preferred_element_type=F32)
            dps_g = dps_ref[:, cols]
            dsc_ref[:, cols] += jnp.sum(dps_g * mixed, axis=0, keepdims=True)
            dms = (dps_g * sc_ref[:, cols]).astype(BF16)
            dw_ref[g] += lax.dot_general(pooled, dms, _DIMS["tn"], preferred_element_type=F32)
            dpl_ref[:, cols] = lax.dot_general(dms, w_ref[g], _DIMS["nt"], preferred_element_type=F32)

    row = pl.BlockSpec((tm, POOL), lambda i: (i, 0))
    return _call(
        order, body, [proj] * 8 + [wgrp, scale, dps], name="pool_bwd_mix", grid=(S // tm,),
        in_specs=_pool_specs(S, uoff // gw, gw, tm) + [
            pl.BlockSpec((4, gw, gw), lambda i: (0, 0, 0)), pl.BlockSpec((1, POOL), lambda i: (0, 0)), row],
        out_specs=[row, pl.BlockSpec((4, gw, gw), lambda i: (0, 0, 0)), pl.BlockSpec((1, POOL), lambda i: (0, 0))],
        out_shape=[jax.ShapeDtypeStruct((S, POOL), F32), jax.ShapeDtypeStruct((4, gw, gw), F32),
                   jax.ShapeDtypeStruct((1, POOL), F32)],
        sem=("arbitrary",))


def _pool_bwd_window(order, dpl, dproj, S, uoff, POOL):
    gw = POOL // 4
    nb = S // BLOCK
    ub = uoff // gw

    def body(main_ref, halo_ref, dproj_in, o_ref):
        del dproj_in
        b, g = pl.program_id(0), pl.program_id(1)
        main = jnp.where(b < nb, main_ref[...], 0.0)
        halo = jnp.where(b == nb - 1, 0.0, halo_ref[...])
        sums = _window_sums(jnp.concatenate([main, halo], axis=0), up=True)
        du = jnp.zeros((BLOCK, gw), F32)
        for k, w in enumerate(POOL_WINDOWS):
            du = jnp.where(g == k, sums[k][:BLOCK] * (1.0 / w), du)
        du = du - main
        row = lax.broadcasted_iota(jnp.int32, du.shape, 0)
        first_valid = jnp.where(b == nb, META_ROW0, 0)
        o_ref[...] = jnp.where(row >= first_valid, du, 0.0).astype(BF16)

    return _call(
        order, body, [dpl, dpl, dproj], name="pool_bwd_window", grid=(nb + 1, 4),
        in_specs=[pl.BlockSpec((BLOCK, gw), lambda b, g: (jnp.minimum(b, nb - 1), g)),
                  pl.BlockSpec((HALO, gw), lambda b, g: (
                      jnp.where(b == nb, 0, jnp.minimum((b + 1) * (BLOCK // HALO), S // HALO - 1)), g)),
                  ANY],
        out_specs=pl.BlockSpec((BLOCK, gw), lambda b, g: (b, ub + g)),
        out_shape=jax.ShapeDtypeStruct(dproj.shape, BF16), aliases={2: 0}, sem=("parallel", "parallel"))


def _sigmoid(x):
    return 1.0 / (1.0 + jnp.exp(-x))


def _gate_tiles(S, D, goff):
    tc = 512
    while goff % tc or D % tc:
        tc //= 2
    return _pick(S, 512, 8), tc


def _gate_mix(order, proj, bgate, a_out, p_out, S, D, goff):
    tm, tc = _gate_tiles(S, D, goff)
    g0b, nd = goff // tc, D // tc

    def body(l0_ref, l1_ref, b_ref, a_ref, p_ref, o_ref):
        g0 = _sigmoid(l0_ref[...] + b_ref[0:1, :])
        g1 = _sigmoid(l1_ref[...] + b_ref[1:2, :])
        o_ref[...] = (g0 * a_ref[...] + g1 * p_ref[...]).astype(BF16)

    tile = pl.BlockSpec((tm, tc), lambda i, j: (i, j))
    return _call(
        order, body, [proj, proj, bgate, a_out, p_out], name="gate_mix", grid=(S // tm, nd),
        in_specs=[pl.BlockSpec((tm, tc), lambda i, j: (i, g0b + j)),
                  pl.BlockSpec((tm, tc), lambda i, j: (i, g0b + nd + j)),
                  pl.BlockSpec((2, tc), lambda i, j: (0, j)), tile, tile],
        out_specs=tile, out_shape=jax.ShapeDtypeStruct((S, D), BF16), sem=("parallel", "parallel"))


def _gate_bwd(order, proj, bgate, a_out, p_out, dmixed, dproj, S, D, goff):
    tm, tc = _gate_tiles(S, D, goff)
    g0b, nd, ni = goff // tc, D // tc, S // tm
    nsteps = nd * ni

    def body(l0_ref, l1_ref, b_ref, a_ref, p_ref, dm_ref, dproj_in, dap_ref, dproj_ref, db_ref, buf, sems):
        del dproj_in
        j, i = pl.program_id(0), pl.program_id(1)
        step = j * ni + i
        slot = step % 2

        def put(sl, br):
            col = pl.multiple_of((g0b + br * nd + j) * tc, tc)
            return pltpu.make_async_copy(
                buf.at[sl, br], dproj_ref.at[pl.ds(pl.multiple_of(i * tm, tm), tm), pl.ds(col, tc)], sems.at[sl, br])

        @pl.when(step >= 2)
        def _():
            put(slot, 0).wait()
            put(slot, 1).wait()

        @pl.when(i == 0)
        def _():
            db_ref[...] = jnp.zeros_like(db_ref)

        dm = dm_ref[...]
        for br, (l_ref, val_ref) in enumerate(((l0_ref, a_ref), (l1_ref, p_ref))):
            gate = _sigmoid(l_ref[...] + b_ref[br:br + 1, :])
            dap_ref[br] = (dm * gate).astype(BF16)
            dl = dm * val_ref[...] * gate * (1.0 - gate)
            buf[slot, br] = dl.astype(BF16)
            db_ref[br] += jnp.sum(dl, axis=0, keepdims=True)
            put(slot, br).start()

        @pl.when(step == nsteps - 1)
        def _():
            for sl in ((slot, 1 - slot) if nsteps > 1 else (slot,)):
                put(sl, 0).wait()
                put(sl, 1).wait()

    tile = pl.BlockSpec((tm, tc), lambda j, i: (i, j))
    return _call(
        order, body, [proj, proj, bgate, a_out, p_out, dmixed, dproj], name="gate_bwd", grid=(nd, ni),
        in_specs=[pl.BlockSpec((tm, tc), lambda j, i: (i, g0b + j)),
                  pl.BlockSpec((tm, tc), lambda j, i: (i, g0b + nd + j)),
                  pl.BlockSpec((2, tc), lambda j, i: (0, j)), tile, tile, tile, ANY],
        out_specs=[pl.BlockSpec((2, tm, tc), lambda j, i: (0, i, j)), ANY,
                   pl.BlockSpec((2, 1, tc), lambda j, i: (0, 0, j))],
        out_shape=[jax.ShapeDtypeStruct((2, S, D), BF16), jax.ShapeDtypeStruct(dproj.shape, BF16),
                   jax.ShapeDtypeStruct((2, 1, D), F32)],
        scratch=[pltpu.VMEM((2, 2, tm, tc), BF16), pltpu.SemaphoreType.DMA((2, 2))],
        aliases={6: 1}, sem=("arbitrary", "arbitrary"))


def _ffn_in_near(order, h, wT, FF):
    S, D = h.shape
    tm, tn = _pick(S, 2048), _pick(FF, 512)
    nj = FF // tn

    def body(h_ref, w_ref, f_ref):
        f_ref[...] = lax.dot_general(h_ref[...], w_ref[...], _DIMS["nt"], preferred_element_type=F32)

    return _call(
        order, body, [h, wT], name="ffn_in_near", grid=(S // tm, nj),
        in_specs=[pl.BlockSpec((tm, D), lambda i, j: (i, 0)),
                  pl.BlockSpec((tn, D), lambda i, j: (_x_half(False) * nj + j, 0))],
        out_specs=pl.BlockSpec((tm, tn), lambda i, j: (i, j)),
        out_shape=jax.ShapeDtypeStruct((S, FF), F32), sem=("parallel", "parallel"))


def _gate_up(near, far):
    near_is_gate = lax.axis_index("x") == 0
    return jnp.where(near_is_gate, near, far), jnp.where(near_is_gate, far, near)


def _ffn_in_far(order, h, wT, near):
    S, D = h.shape
    FF = near.shape[1]
    tm, tn = _pick(S, 1024), _pick(FF, 512)
    nj = FF // tn

    def body(h_ref, w_ref, near_ref, f_ref, act_ref):
        far = lax.dot_general(h_ref[...], w_ref[...], _DIMS["nt"], preferred_element_type=F32)
        f_ref[...] = far
        gt, up = _gate_up(near_ref[...], far)
        act_ref[...] = (gt * _sigmoid(gt) * up).astype(BF16)

    tile = pl.BlockSpec((tm, tn), lambda i, j: (i, j))
    return _call(
        order, body, [h, wT, near], name="ffn_in_far", grid=(S // tm, nj),
        in_specs=[pl.BlockSpec((tm, D), lambda i, j: (i, 0)),
                  pl.BlockSpec((tn, D), lambda i, j: (_x_half(True) * nj + j, 0)), tile],
        out_specs=[tile, tile],
        out_shape=[jax.ShapeDtypeStruct((S, FF), F32), jax.ShapeDtypeStruct((S, FF), BF16)],
        sem=("parallel", "parallel"))


def _d_act_swiglu(order, dy, wdown, near, far):
    S, D = dy.shape
    FF = wdown.shape[0]
    tm, tn = _pick(S, 1024), _pick(FF, 256)

    def body(dy_ref, w_ref, near_ref, far_ref, o_ref):
        d = lax.dot_general(dy_ref[...], w_ref[...], _DIMS["nt"], preferred_element_type=F32)
        gt, up = _gate_up(near_ref[...], far_ref[...])
        s = _sigmoid(gt)
        o_ref[0] = (d * up * s * (1.0 + gt * (1.0 - s))).astype(BF16)
        o_ref[1] = (d * gt * s).astype(BF16)

    tile = pl.BlockSpec((tm, tn), lambda i, j: (i, j))
    return _call(
        order, body, [dy, wdown, near, far], name="d_act", grid=(S // tm, FF // tn),
        in_specs=[pl.BlockSpec((tm, D), lambda i, j: (i, 0)), pl.BlockSpec((tn, D), lambda i, j: (j, 0)), tile, tile],
        out_specs=pl.BlockSpec((2, tm, tn), lambda i, j: (0, i, j)),
        out_shape=jax.ShapeDtypeStruct((2, S, FF), BF16), sem=("parallel", "parallel"))


def _place():
    return lax.axis_index("x"), lax.axis_index("y"), lax.axis_index("c")


def _xfer_start(order, name, bufs, copies):
    nb = len(bufs)
    n = len(copies([None] * nb, None))
    is_new = [isinstance(b, jax.ShapeDtypeStruct) for b in bufs]
    old = [b for b, fresh in zip(bufs, is_new) if not fresh]
    no = len(old)
    tok = [] if any(order.last is b for b in old) else [order.last]
    first_out = no + len(tok)

    def body(*refs):
        send, recv = refs[first_out:first_out + n], refs[first_out + n:first_out + 2 * n]
        token = refs[-1]
        given, made = iter(refs[:no]), iter(refs[first_out + 2 * n + no:-1])
        logical = [next(made) if fresh else next(given) for fresh in is_new]
        for i, (src, dst, dev) in enumerate(copies(logical, _place())):
            pltpu.make_async_remote_copy(src_ref=src, dst_ref=dst, send_sem=send[i], recv_sem=recv[i],
                                         device_id=dev, device_id_type=MESH).start()
        token[...] = jnp.zeros_like(token)

    fresh_shapes = [b for b, fresh in zip(bufs, is_new) if fresh]
    out = pl.pallas_call(
        body, name=name,
        out_shape=tuple([pltpu.SemaphoreType.DMA(())] * (2 * n)
                        + [pltpu.HBM(b.shape, b.dtype) for b in old + fresh_shapes]
                        + [jax.ShapeDtypeStruct((8, LANES), F32)]),
        in_specs=[HBM] * no + [ANY] * len(tok),
        out_specs=tuple([SEM] * (2 * n) + [HBM] * nb + [pl.BlockSpec(memory_space=pltpu.VMEM)]),
        input_output_aliases={i: 2 * n + i for i in range(no)},
        compiler_params=pltpu.CompilerParams(has_side_effects=EFFECT),
    )(*[pltpu.with_memory_space_constraint(b, pltpu.HBM) for b in old], *tok)
    order.last = out[-1]
    thru, made = iter(out[2 * n:2 * n + no]), iter(out[2 * n + no:2 * n + nb])
    return list(out[:2 * n]), [next(made) if fresh else next(thru) for fresh in is_new]


def _xfer_wait(order, name, sems, bufs, copies):
    nb = len(bufs)
    n = len(sems) // 2
    tok = order.last

    def body(*refs):
        send, recv = refs[nb:nb + n], refs[nb + n:nb + 2 * n]
        token = refs[-1]
        for i, (src, dst, dev) in enumerate(copies(refs[:nb], _place())):
            cp = pltpu.make_async_remote_copy(src_ref=src, dst_ref=dst, send_sem=send[i], recv_sem=recv[i],
                                              device_id=dev, device_id_type=MESH)
            cp.wait_send()
            cp.wait_recv()
        token[...] = jnp.zeros_like(token)

    out = pl.pallas_call(
        body, name=name,
        out_shape=tuple([pltpu.HBM(b.shape, b.dtype) for b in bufs] + [jax.ShapeDtypeStruct((8, LANES), F32)]),
        in_specs=[HBM] * nb + [SEM] * (2 * n) + [ANY],
        out_specs=tuple([HBM] * nb + [pl.BlockSpec(memory_space=pltpu.VMEM)]),
        input_output_aliases={i: i for i in range(nb)},
        compiler_params=pltpu.CompilerParams(has_side_effects=EFFECT),
    )(*bufs, *sems, tok)
    order.last = out[-1]
    return list(out[:nb])


class _Xfer:
    def __init__(self, name, bufs, copies):
        self.name, self.bufs, self.copies = name, list(bufs), copies
        self.sems = None

    def start(self, order):
        self.sems, self.bufs = _xfer_start(order, self.name + "_start", self.bufs, self.copies)

    def wait(self, order, bufs=None):
        self.bufs = _xfer_wait(order, self.name + "_wait", self.sems, bufs or self.bufs, self.copies)
        return self.bufs


def _block_rows(ref, r, d):
    return ref.at[pl.ds(d * r, r)]


NEAR = ("xn", "yn")
ALL_CHIPS = ("xn", "yn", "diag")


def _chip_of(which, x, y):
    return {"xn": (1 - x, y), "yn": (x, 1 - y), "diag": (1 - x, 1 - y)}[which]


def _gather_send(fulls, chips=ALL_CHIPS, sibling=True):
    def copies(refs, place):
        out = []
        for w, full in enumerate(fulls):
            r = full.shape[0] // 8
            if place is None:
                out += [None] * (len(chips) + int(sibling))
                continue
            x, y, c = place
            mine = _block_rows(refs[w], r, 4 * x + 2 * y + c)
            if sibling:
                out.append((mine, mine, (x, y, 1 - c)))
            for which in chips:
                out.append((mine, mine, (*_chip_of(which, x, y), c)))
        return out
    return copies


def _gather_forward(fulls, chips=ALL_CHIPS):
    def copies(refs, place):
        out = []
        for w, full in enumerate(fulls):
            r = full.shape[0] // 8
            if place is None:
                out += [None] * len(chips)
                continue
            x, y, c = place
            for which in chips:
                px, py = _chip_of(which, x, y)
                blk = _block_rows(refs[w], r, 4 * px + 2 * py + c)
                out.append((blk, blk, (x, y, 1 - c)))
        return out
    return copies


def _pair_send(nw):
    def copies(refs, place):
        out = []
        for w in range(nw):
            if place is None:
                out += [None] * 4
                continue
            x, y, c = place
            grad, other = refs[2 * w], refs[2 * w + 1]
            r = other.shape[1]
            for k in range(4):
                out.append((_block_rows(grad, r, 2 * k + 1 - c), other.at[k], (x, y, 1 - c)))
        return out
    return copies


def _chip_send(nw):
    def copies(refs, place):
        out = []
        for w in range(nw):
            if place is None:
                out += [None] * 3
                continue
            x, y, c = place
            psum, parts = refs[2 * w], refs[2 * w + 1]
            for px, py in ((1 - x, y), (x, 1 - y), (1 - x, 1 - y)):
                out.append((psum.at[2 * px + py], parts.at[2 * x + y], (px, py, c)))
        return out
    return copies


def _dev_index():
    x, y, c = _place()
    return 4 * x + 2 * y + c


def _place_own(order, shard, name):
    r, cols = shard.shape
    tr = _pick(r, max(16, (12 << 20) // (4 * cols)), 16)
    nr = r // tr

    def body(s_ref, o_ref):
        o_ref[...] = s_ref[...].astype(BF16)

    return _call(
        order, body, [shard], name=name, grid=(nr,),
        in_specs=[pl.BlockSpec((tr, cols), lambda i: (i, 0))],
        out_specs=pl.BlockSpec((tr, cols), lambda i: (_dev_index() * nr + i, 0)),
        out_shape=jax.ShapeDtypeStruct((8 * r, cols), BF16), sem=("parallel",))


def _pair_sum(order, grad, other, name):
    r, cols = other.shape[1:]
    tr = _pick(r, max(16, (7 << 20) // (2 * cols)), 16)
    nr = r // tr

    def body(g_ref, a_ref, o_ref):
        o_ref[...] = (g_ref[...].astype(F32) + a_ref[...].astype(F32)).astype(BF16)

    blk = pl.BlockSpec((None, tr, cols), lambda k, i: (k, i, 0))
    return _call(
        order, body, [grad, other], name=name, grid=(4, nr),
        in_specs=[pl.BlockSpec((tr, cols), lambda k, i: ((2 * k + lax.axis_index("c")) * nr + i, 0)), blk],
        out_specs=blk, out_shape=jax.ShapeDtypeStruct(other.shape, BF16), sem=("parallel", "parallel"))


def _chip_sum(order, psum, parts, name):
    _, r, cols = parts.shape
    tr = _pick(r, max(16, (1 << 20) // (2 * cols)), 16)

    def my_chip():
        return 2 * lax.axis_index("x") + lax.axis_index("y")

    def body(own_ref, p0, p1, p2, p3, o_ref):
        own = own_ref[...].astype(F32)
        acc = None
        for k, p in enumerate((p0, p1, p2, p3)):
            term = jnp.where(my_chip() == k, own, p[...].astype(F32))
            acc = term if acc is None else acc + term
        o_ref[...] = acc

    def slot(k):
        return pl.BlockSpec((None, tr, cols), lambda i: (jnp.where(my_chip() == k, (k + 1) % 4, k), i, 0))

    return _call(
        order, body, [psum, parts, parts, parts, parts], name=name, grid=(r // tr,),
        in_specs=[pl.BlockSpec((None, tr, cols), lambda i: (my_chip(), i, 0))] + [slot(k) for k in range(4)],
        out_specs=pl.BlockSpec((tr, cols), lambda i: (i, 0)),
        out_shape=jax.ShapeDtypeStruct((r, cols), F32), sem=("parallel",))


def _all_reduce_small(order, pack, name):
    R = pack.shape[0]

    def body(p_ref, o_ref, buf, send_sems, recv_sems):
        x, y, c = _place()
        me = 4 * x + 2 * y + c
        buf[me] = p_ref[...]
        copies = []
        for k in range(1, 8):
            px = 1 - x if k & 4 else x
            py = 1 - y if k & 2 else y
            pc = 1 - c if k & 1 else c
            cp = pltpu.make_async_remote_copy(
                src_ref=p_ref, dst_ref=buf.at[me], send_sem=send_sems.at[k - 1], recv_sem=recv_sems.at[k - 1],
                device_id=(px, py, pc), device_id_type=MESH)
            cp.start()
            copies.append(cp)
        for cp in copies:
            cp.wait_recv()
        acc = buf[0]
        for d in range(1, 8):
            acc = acc + buf[d]
        o_ref[...] = acc
        for cp in copies:
            cp.wait_send()

    vm = pl.BlockSpec(memory_space=pltpu.VMEM)
    return _call(
        order, body, [pack], name=name, in_specs=[vm], out_specs=vm,
        out_shape=jax.ShapeDtypeStruct((R, LANES), F32),
        scratch=[pltpu.VMEM((8, R, LANES), F32), pltpu.SemaphoreType.DMA((7,)), pltpu.SemaphoreType.DMA((7,))])


def _pack(parts):
    flat = []
    for p in parts:
        v = p.reshape(-1).astype(F32)
        flat.append(jnp.pad(v, (0, (-v.shape[0]) % LANES)))
    v = jnp.concatenate(flat)
    v = jnp.pad(v, (0, (-v.shape[0]) % (8 * LANES)))
    return v.reshape(-1, LANES)


def _unpack(pack, shapes):
    v = pack.reshape(-1)
    out, off = [], 0
    for s in shapes:
        n = 1
        for d in s:
            n *= d
        out.append(v[off:off + n].reshape(s))
        off += n + (-n) % LANES
    return out


def _adamw(order, w, g, m, v, name):
    shape = w.shape
    cols = shape[-1]
    w2, g2, m2, v2 = (t.reshape(-1, cols) for t in (w, g, m, v))
    R = w2.shape[0]
    tr = _pick(R, max(8, (1 << 20) // (4 * cols)), 8)

    def body(w_ref, g_ref, m_ref, v_ref, d_ref, mo_ref, vo_ref):
        d_ref[...], mo_ref[...], vo_ref[...] = _adam_math(w_ref[...], g_ref[...], m_ref[...], v_ref[...])

    blk = pl.BlockSpec((tr, cols), lambda i: (i, 0))
    outs = _call(
        order, body, [w2, g2, m2, v2], name=name, grid=(R // tr,), in_specs=[blk] * 4, out_specs=[blk] * 3,
        out_shape=[jax.ShapeDtypeStruct((R, cols), F32)] * 3, sem=("parallel",))
    return tuple(o.reshape(shape) for o in outs)


def _adam_math(w, g, m, v):
    mn = ADAM_B1 * m + (1.0 - ADAM_B1) * g
    vn = ADAM_B2 * v + (1.0 - ADAM_B2) * (g * g)
    m_hat = mn / (1.0 - ADAM_B1 ** ADAM_STEP)
    v_hat = vn / (1.0 - ADAM_B2 ** ADAM_STEP)
    return -ADAM_LR * (m_hat / (jnp.sqrt(v_hat) + ADAM_EPS) + ADAM_WD * w), mn, vn


def _chip_sum_adamw(order, w, psum, parts, m, v, name):
    _, r, cols = parts.shape
    tr = _pick(r, max(16, (6 << 20) // (38 * cols)), 16)

    def my_chip():
        return 2 * lax.axis_index("x") + lax.axis_index("y")

    def body(w_ref, own_ref, p0, p1, p2, p3, m_ref, v_ref, g_ref, d_ref, mo_ref, vo_ref):
        own = own_ref[...].astype(F32)
        g = None
        for k, p in enumerate((p0, p1, p2, p3)):
            term = jnp.where(my_chip() == k, own, p[...].astype(F32))
            g = term if g is None else g + term
        g_ref[...] = g
        d_ref[...], mo_ref[...], vo_ref[...] = _adam_math(w_ref[...], g, m_ref[...], v_ref[...])

    def slot(k):
        return pl.BlockSpec((None, tr, cols), lambda i: (jnp.where(my_chip() == k, (k + 1) % 4, k), i, 0))

    blk = pl.BlockSpec((tr, cols), lambda i: (i, 0))
    return _call(
        order, body, [w, psum, parts, parts, parts, parts, m, v], name=name, grid=(r // tr,),
        in_specs=[blk, pl.BlockSpec((None, tr, cols), lambda i: (my_chip(), i, 0))]
        + [slot(k) for k in range(4)] + [blk, blk],
        out_specs=[blk] * 4, out_shape=[jax.ShapeDtypeStruct((r, cols), F32)] * 4, sem=("parallel",))


class _GradReduce:
    def __init__(self, tag, grads, names):
        self.tag, self.grads, self.names = tag, list(grads), names
        self.pair = self.chip = self.psums = None

    def pair_start(self, order):
        bufs = []
        for g in self.grads:
            bufs += [g, jax.ShapeDtypeStruct((4, g.shape[0] // 8, g.shape[1]), g.dtype)]
        self.pair = _Xfer("pair_" + self.tag, bufs, _pair_send(len(self.grads)))
        self.pair.start(order)

    def pair_sum_chip_start(self, order):
        bufs = self.pair.wait(order)
        self.psums = [_pair_sum(order, bufs[2 * w], bufs[2 * w + 1], "pair_sum_" + nm)
                      for w, nm in enumerate(self.names)]
        cbufs = []
        for p in self.psums:
            cbufs += [p, jax.ShapeDtypeStruct(p.shape, p.dtype)]
        self.chip = _Xfer("chip_" + self.tag, cbufs, _chip_send(len(self.psums)))
        self.chip.start(order)

    def finish(self, order):
        bufs = self.chip.wait(order)
        return [(bufs[2 * w], bufs[2 * w + 1]) for w in range(len(self.names))]


def kernel(x, meta_tokens, ln_in_g, ln_in_b, w_in, b_gate, attn_sinks, w_attn_up, w_pool_grp, pool_scale, w_pool_up, w_out, ln1_g, ln1_b, w_ffn_in, w_ffn_down, ln2_g, ln2_b, loss_target, m_meta_tokens, m_ln_in_g, m_ln_in_b, m_w_in, m_b_gate, m_attn_sinks, m_w_attn_up, m_w_pool_grp, m_pool_scale, m_w_pool_up, m_w_out, m_ln1_g, m_ln1_b, m_w_ffn_in, m_w_ffn_down, m_ln2_g, m_ln2_b, v_meta_tokens, v_ln_in_g, v_ln_in_b, v_w_in, v_b_gate, v_attn_sinks, v_w_attn_up, v_w_pool_grp, v_pool_scale, v_w_pool_up, v_w_out, v_ln1_g, v_ln1_b, v_w_ffn_in, v_w_ffn_down, v_ln2_g, v_ln2_b):
    S, D = x.shape[1], x.shape[2]
    Tp = S + BLOCK
    NQ = attn_sinks.shape[-1]
    ATTN = NQ * HEAD_DIM
    KVW = ATTN // Q_PER_KV
    POOL = pool_scale.shape[-1]
    IN = 8 * w_in.shape[2]
    FF = 8 * w_ffn_down.shape[1]
    uoff = ATTN + 2 * KVW
    goff = uoff + POOL
    gw = POOL // 4
    dcols = D // 8
    assert IN == goff + 2 * D and w_ffn_in.shape[2] * 8 == 2 * FF

    xi, yi, ci = _place()
    dev = 4 * xi + 2 * yi + ci
    x2, tgt = x[0], loss_target[0]
    order = _Order()

    def place_cols(a):
        return lax.dynamic_update_slice(jnp.zeros(a.shape[:-1] + (D,), F32), a, (0,) * (a.ndim - 1) + (dev * dcols,))

    small = _all_reduce_small(order, _pack([place_cols(meta_tokens), place_cols(b_gate[0])]), "small_inputs_gather")
    meta_full, bgate_full = _unpack(small, [(N_META, D), (2, D)])
    meta_pad = jnp.pad(meta_full, ((META_ROW0, 0), (0, 0)))

    wgrp_rows = w_pool_grp[0].reshape(4 * (gw // 8), gw)
    full_in = _place_own(order, w_in[0].T, "own_w_in")
    g_in = _Xfer("gather_w_in_near", [full_in], _gather_send([full_in], NEAR))
    g_in.start(order)
    mix_names = ["w_attn_up", "w_pool_grp", "w_pool_up", "w_out"]
    mix_shards = [w_attn_up[0].T, wgrp_rows, w_pool_up[0].T, w_out[0]]
    full_mix = [_place_own(order, s, "own_" + nm) for s, nm in zip(mix_shards, mix_names)]
    full_ffn = _place_own(order, w_ffn_in[0].T, "own_w_ffn_in")

    ln_in_g2, ln_in_b2 = ln_in_g.reshape(1, D), ln_in_b.reshape(1, D)
    tab = _rope_table(S)

    h0, h0b = _ln_in_fwd(order, x2, meta_pad, ln_in_g2, ln_in_b2)
    bufs = g_in.wait(order)
    d_in = _Xfer("gather_w_in_diag", bufs, _gather_send(bufs, ("diag",), sibling=False))
    d_in.start(order)
    f_in = _Xfer("forward_w_in_near", d_in.bufs, _gather_forward(bufs, NEAR))
    f_in.start(order)
    full_down = _place_own(order, w_ffn_down[0], "own_w_ffn_down")
    bufs = f_in.wait(order)
    proj = _mm_nt_half(order, h0b, bufs[0], far=False, tm=704, tn=1280, name="proj_near")
    bufs = d_in.wait(order, bufs)
    fd_in = _Xfer("forward_w_in_diag", bufs, _gather_forward(bufs, ("diag",)))
    fd_in.start(order)
    ag_mix = _Xfer("gather_mixers", full_mix, _gather_send(full_mix))
    ag_mix.start(order)
    g_ffn = _Xfer("gather_w_ffn_in_near", [full_ffn], _gather_send([full_ffn], NEAR))
    g_ffn.start(order)
    (winT,) = fd_in.wait(order)
    proj = _mm_nt_half(order, h0b, winT, far=True, tm=704, tn=1280, name="proj_far", into=proj)

    att = _attn_fwd(order, proj, tab, attn_sinks, S, ATTN, KVW)
    full_mix = ag_mix.wait(order)
    fw_mix = _Xfer("forward_mixers", full_mix, _gather_forward(full_mix))
    fw_mix.start(order)
    wattT, wgrp_g, wpupT, wout = fw_mix.wait(order)
    wgrp = wgrp_g.reshape(8, 4, gw // 8, gw).transpose(1, 0, 2, 3).reshape(4, gw, gw)

    ps = _pool_fwd(order, proj, wgrp, pool_scale, S, uoff, POOL)
    a_out = _mm(order, att, wattT, kind="nt", out_dtype=F32, tm=1024, tn=1024, name="attn_up")
    p_out = _mm(order, ps, wpupT, kind="nt", out_dtype=F32, tm=1024, tn=1024, name="pool_up")
    mixed = _gate_mix(order, proj, bgate_full, a_out, p_out, S, D, goff)
    y1 = _mm(order, mixed, wout, kind="nn", out_dtype=F32, tm=1024, tn=1024, name="out_proj")

    bufs = g_ffn.wait(order)
    d_ffn = _Xfer("gather_w_ffn_in_diag", bufs, _gather_send(bufs, ("diag",), sibling=False))
    d_ffn.start(order)
    f_ffn = _Xfer("forward_w_ffn_in_near", d_ffn.bufs, _gather_forward(bufs, NEAR))
    f_ffn.start(order)
    h1, h1b = _ln1_fwd(order, h0, y1, ln1_g, ln1_b)
    bufs = f_ffn.wait(order)
    f_near = _ffn_in_near(order, h1b, bufs[0], FF)
    bufs = d_ffn.wait(order, bufs)
    fd_ffn = _Xfer("forward_w_ffn_in_diag", bufs, _gather_forward(bufs, ("diag",)))
    fd_ffn.start(order)
    ag_down = _Xfer("gather_w_ffn_down", [full_down], _gather_send([full_down]))
    ag_down.start(order)
    (wffnT,) = fd_ffn.wait(order)
    f_far, act = _ffn_in_far(order, h1b, wffnT, f_near)

    (full_down,) = ag_down.wait(order)
    fw_down = _Xfer("forward_w_ffn_down", [full_down], _gather_forward([full_down]))
    fw_down.start(order)
    (wdown,) = fw_down.wait(order)
    y2 = _mm(order, act, wdown, kind="nn", out_dtype=F32, tm=1024, tn=1024, tk=5504, name="ffn_down")

    dz2, dz2b, dg2, db2, loss_part = _ln2_loss_bwd(order, h1, y2, tgt, ln2_g, ln2_b)
    df = _d_act_swiglu(order, dz2b, wdown, f_near, f_far)
    gwdown = _mm(order, act, dz2b, kind="tn", out_dtype=BF16, tm=256, tn=2048, name="d_ffn_down")
    rs_down = _GradReduce("w_ffn_down", [gwdown], ["w_ffn_down"])
    rs_down.pair_start(order)
    gwffnT = _mm(order, df, h1b, kind="tn", out_dtype=BF16, tm=256, tn=2048, name="d_ffn_in", a_lead="halves")
    rs_down.pair_sum_chip_start(order)
    rs_ffn = _GradReduce("w_ffn_in", [gwffnT], ["w_ffn_in"])
    rs_ffn.pair_start(order)
    dh1 = _mm(order, df, wffnT, kind="nn", out_dtype=F32, tm=1024, tn=1024, tk=5504, name="d_h1", a_lead="halves")
    rs_ffn.pair_sum_chip_start(order)
    dz1, dz1b, dg1, db1 = _ln1_bwd(order, h0, y1, ln1_g, dh1, dz2)
    gwout = _mm(order, mixed, dz1b, kind="tn", out_dtype=BF16, tm=512, tn=1024, name="d_out_proj")
    rs_out = _GradReduce("w_out", [gwout], ["w_out"])
    rs_out.pair_start(order)
    dmixed = _mm(order, dz1b, wout, kind="nt", out_dtype=F32, tm=1024, tn=1024, name="d_mixed")
    rs_out.pair_sum_chip_start(order)

    dproj = _zero_meta_block(order, Tp, IN)
    dap, dproj, dbgate = _gate_bwd(order, proj, bgate_full, a_out, p_out, dmixed, dproj, S, D, goff)
    gwattT = _mm(order, dap, att, kind="tn", out_dtype=BF16, tm=512, tn=1024, name="d_attn_up", a_lead=0)
    datt = _mm(order, dap, wattT, kind="nn", out_dtype=BF16, tm=1024, tn=1024, name="d_att", a_lead=0)
    gwpupT = _mm(order, dap, ps, kind="tn", out_dtype=BF16, tm=512, tn=1024, name="d_pool_up", a_lead=1)
    dps = _mm(order, dap, wpupT, kind="nn", out_dtype=F32, tm=1024, tn=1024, name="d_ps", a_lead=1)
    dpl, gwgrp, dscale = _pool_bwd_mix(order, proj, wgrp, pool_scale, dps, S, uoff, POOL)
    gwgrp_rows = gwgrp.reshape(4, 8, gw // 8, gw).transpose(1, 0, 2, 3).reshape(8 * 4 * (gw // 8), gw).astype(BF16)
    rs_mix = _GradReduce("mixers", [gwattT, gwgrp_rows, gwpupT], ["w_attn_up", "w_pool_grp", "w_pool_up"])
    rs_mix.pair_start(order)
    dproj = _pool_bwd_window(order, dpl, dproj, S, uoff, POOL)
    rs_mix.pair_sum_chip_start(order)
    dproj, dk, dv, dsink = _attn_bwd(order, proj, tab, attn_sinks, datt, dproj, S, ATTN, KVW)
    dproj = _put_dkv(order, dk, dv, dproj, ATTN)

    weights = dict(meta_tokens=meta_tokens, ln_in_g=ln_in_g, ln_in_b=ln_in_b, w_in=w_in, b_gate=b_gate,
                   attn_sinks=attn_sinks, w_attn_up=w_attn_up, w_pool_grp=w_pool_grp, pool_scale=pool_scale,
                   w_pool_up=w_pool_up, w_out=w_out, ln1_g=ln1_g, ln1_b=ln1_b, w_ffn_in=w_ffn_in,
                   w_ffn_down=w_ffn_down, ln2_g=ln2_g, ln2_b=ln2_b)
    ms = dict(meta_tokens=m_meta_tokens, ln_in_g=m_ln_in_g, ln_in_b=m_ln_in_b, w_in=m_w_in, b_gate=m_b_gate,
              attn_sinks=m_attn_sinks, w_attn_up=m_w_attn_up, w_pool_grp=m_w_pool_grp, pool_scale=m_pool_scale,
              w_pool_up=m_w_pool_up, w_out=m_w_out, ln1_g=m_ln1_g, ln1_b=m_ln1_b, w_ffn_in=m_w_ffn_in,
              w_ffn_down=m_w_ffn_down, ln2_g=m_ln2_g, ln2_b=m_ln2_b)
    vs = dict(meta_tokens=v_meta_tokens, ln_in_g=v_ln_in_g, ln_in_b=v_ln_in_b, w_in=v_w_in, b_gate=v_b_gate,
              attn_sinks=v_attn_sinks, w_attn_up=v_w_attn_up, w_pool_grp=v_w_pool_grp, pool_scale=v_pool_scale,
              w_pool_up=v_w_pool_up, w_out=v_w_out, ln1_g=v_ln1_g, ln1_b=v_ln1_b, w_ffn_in=v_w_ffn_in,
              w_ffn_down=v_w_ffn_down, ln2_g=v_ln2_g, ln2_b=v_ln2_b)
    grads, deltas, new_ms, new_vs = {}, {}, {}, {}

    def update(nm, g):
        g = g.reshape(weights[nm].shape)
        grads[nm] = g
        deltas[nm], new_ms[nm], new_vs[nm] = _adamw(order, weights[nm], g, ms[nm], vs[nm], "adamw_" + nm)

    def update_reduced(nm, bufs, transposed=False):
        psum, parts = bufs
        if transposed:
            to2d, back = (lambda t: t[0].T), (lambda t: t.T[None])
        else:
            to2d, back = (lambda t: t.reshape(parts.shape[1:])), (lambda t: t.reshape(weights[nm].shape))
        outs = _chip_sum_adamw(order, to2d(weights[nm]), psum, parts, to2d(ms[nm]), to2d(vs[nm]), "adamw_" + nm)
        grads[nm], deltas[nm], new_ms[nm], new_vs[nm] = (back(t) for t in outs)

    gwinT = _mm(order, dproj, h0b, kind="tn", out_dtype=BF16, tm=512, tn=1024, name="d_w_in")
    rs_in = _GradReduce("w_in", [gwinT], ["w_in"])
    rs_in.pair_start(order)
    update_reduced("w_ffn_down", rs_down.finish(order)[0])
    rs_in.pair_sum_chip_start(order)
    dh0 = _mm(order, dproj, winT, kind="nn", out_dtype=F32, tm=1408, tn=1024, tk=2560, name="d_h0")
    dxin, dg_in, db_in = _ln_in_bwd(order, x2, meta_pad, ln_in_g2, dh0, dz1)
    grad_x = dxin[:S][None]
    dmeta = dxin[S + META_ROW0:]

    small_shapes = [(D,), (D,), (1, D), (1, D), (1, D), (1, D), (1, POOL), (1, NQ), (), (N_META, D), (2, D)]
    red = _all_reduce_small(order, _pack([dg_in, db_in, dg1, db1, dg2, db2, dscale, dsink[:, :, 0], loss_part,
                                          dmeta, dbgate]), "small_grads_all_reduce")

    update_reduced("w_ffn_in", rs_ffn.finish(order)[0], transposed=True)
    update_reduced("w_out", rs_out.finish(order)[0])
    b_att, b_grp, b_pup = rs_mix.finish(order)
    update("w_attn_up", _chip_sum(order, *b_att, "chip_sum_w_attn_up").T)
    update_reduced("w_pool_grp", b_grp)
    update("w_pool_up", _chip_sum(order, *b_pup, "chip_sum_w_pool_up").T)

    (g_ln_in_g, g_ln_in_b, g_ln1_g, g_ln1_b, g_ln2_g, g_ln2_b, g_scale, g_sinks, loss_sum, g_meta_full,
     g_bgate_full) = _unpack(red, small_shapes)
    loss = 0.5 * loss_sum
    update("meta_tokens", lax.dynamic_slice(g_meta_full, (0, dev * dcols), (N_META, dcols)))
    update("b_gate", lax.dynamic_slice(g_bgate_full, (0, dev * dcols), (2, dcols)))
    for nm, g in (("ln_in_g", g_ln_in_g), ("ln_in_b", g_ln_in_b), ("ln1_g", g_ln1_g), ("ln1_b", g_ln1_b),
                  ("ln2_g", g_ln2_g), ("ln2_b", g_ln2_b), ("pool_scale", g_scale), ("attn_sinks", g_sinks)):
        update(nm, g)

    update_reduced("w_in", rs_in.finish(order)[0], transposed=True)

    names = list(weights)
    return (loss, grad_x, *[grads[n] for n in names], *[deltas[n] for n in names],
            *[new_ms[n] for n in names], *[new_vs[n] for n in names])
```

```python
import jax
import jax.numpy as jnp
from jax import lax
from jax.experimental import pallas as pl
from jax.experimental.pallas import tpu as pltpu

F32 = jnp.float32
BF16 = jnp.bfloat16
MESH = pl.DeviceIdType.MESH

N_META = 16
HEAD_DIM = 64
Q_PER_KV = 8
WINDOW = 128
BLOCK = 128
ATTN_SCALE = HEAD_DIM ** -0.5
ROPE_DIM = HEAD_DIM // 4
ROPE_THETA = 500000.0
NEG_INF = -1e30
POOL_WINDOWS = (2, 4, 8, 16)
LN_EPS = 1e-5
DN_ALPHA = 2.0 ** 0.25
ADAM_LR = 0.001
ADAM_B1 = 0.9
ADAM_B2 = 0.999
ADAM_EPS = 1e-08
ADAM_WD = 0.01
ADAM_STEP = 10

LANES = 128
META_ROW0 = BLOCK - N_META
VMEM_LIMIT = 56 * 1024 * 1024

ANY = pl.BlockSpec(memory_space=pl.ANY)
HBM = pl.BlockSpec(memory_space=pltpu.HBM)
SEM = pl.BlockSpec(memory_space=pltpu.SEMAPHORE)
EFFECT = pltpu.SideEffectType.DATAFLOW_SIDE_EFFECTING


def _params(sem=None, **kw):
    return pltpu.CompilerParams(dimension_semantics=sem, vmem_limit_bytes=VMEM_LIMIT, **kw)


class _Order:
    def __init__(self):
        self.last = None


def _call(order, body, operands, *, name, in_specs, out_specs, out_shape, grid=(), scratch=(), sem=None,
          aliases=None, prefetch=()):
    n_in, npf = len(operands), len(prefetch)
    tok = order.last
    if tok is not None and any(tok is op for op in operands):
        tok = None

    def wrapped(*refs):
        refs = list(refs)
        if tok is not None:
            del refs[npf + n_in]
        body(*refs)

    specs = list(in_specs) + ([ANY] if tok is not None else [])
    ops = list(operands) + ([tok] if tok is not None else [])
    if npf:
        out = pl.pallas_call(
            wrapped, name=name, out_shape=out_shape, compiler_params=_params(sem),
            grid_spec=pltpu.PrefetchScalarGridSpec(num_scalar_prefetch=npf, grid=grid, in_specs=specs,
                                                   out_specs=out_specs, scratch_shapes=list(scratch)),
        )(*prefetch, *ops)
    else:
        out = pl.pallas_call(
            wrapped, name=name, grid=grid, in_specs=specs, out_specs=out_specs, out_shape=out_shape,
            scratch_shapes=list(scratch), input_output_aliases=aliases or {}, compiler_params=_params(sem),
        )(*ops)
    order.last = out[0] if isinstance(out, (list, tuple)) else out
    return out


def _pick(dim, pref, mult=LANES):
    best = None
    t = mult
    while t <= min(dim, pref):
        if dim % t == 0:
            best = t
        t += mult
    return dim if best is None else best


_DIMS = {"nn": (((1,), (0,)), ((), ())), "nt": (((1,), (1,)), ((), ())), "tn": (((0,), (0,)), ((), ()))}


def _mm(order, a, b, *, kind, out_dtype, tm, tn, tk=None, name, a_lead=None):
    a2 = a.shape[-2:]
    halves = a_lead == "halves"
    if halves:
        a2 = (a2[0], 2 * a2[1])
    if kind == "tn":
        K, M = a2
    else:
        M, K = a2
    N = b.shape[0] if kind == "nt" else b.shape[1]
    half_cols = a2[1] // 2
    tm = _pick(half_cols if halves and kind == "tn" else M, tm)
    tn = _pick(N, tn)
    tk = K if tk is None else _pick(half_cols if halves and kind != "tn" else K, tk)
    nm, nn_, nk = M // tm, N // tn, K // tk
    a_bytes = M * K * a.dtype.itemsize
    b_bytes = N * K * b.dtype.itemsize
    i_outer = (a_bytes + nm * b_bytes <= b_bytes + nn_ * a_bytes) if nk == 1 else True

    def ij(g0, g1):
        return (g0, g1) if i_outer else (g1, g0)

    def a_map(g0, g1, k):
        i, _ = ij(g0, g1)
        if halves:
            per = half_cols // (tm if kind == "tn" else tk)
            return (i // per, k, i % per) if kind == "tn" else (k // per, i, k % per)
        idx = (k, i) if kind == "tn" else (i, k)
        return idx if a_lead is None else (a_lead,) + idx

    def b_map(g0, g1, k):
        _, j = ij(g0, g1)
        return (j, k) if kind == "nt" else (k, j)

    def o_map(g0, g1, k):
        return ij(g0, g1)

    a_blk = (tk, tm) if kind == "tn" else (tm, tk)
    if a_lead is not None:
        a_blk = (None,) + a_blk
    b_blk = (tn, tk) if kind == "nt" else (tk, tn)

    in_place = out_dtype == F32

    def body(a_ref, b_ref, o_ref, *acc):
        p = lax.dot_general(a_ref[...], b_ref[...], _DIMS[kind], preferred_element_type=F32)
        if nk == 1:
            o_ref[...] = p.astype(o_ref.dtype)
        else:
            k = pl.program_id(2)
            acc_ref = o_ref if in_place else acc[0]

            @pl.when(k == 0)
            def _():
                acc_ref[...] = p

            @pl.when(k > 0)
            def _():
                acc_ref[...] += p

            if not in_place:
                @pl.when(k == nk - 1)
                def _():
                    o_ref[...] = acc_ref[...].astype(o_ref.dtype)

    grid = (nm, nn_, nk) if i_outer else (nn_, nm, nk)
    return _call(
        order, body, [a, b], name=name, grid=grid,
        in_specs=[pl.BlockSpec(a_blk, a_map), pl.BlockSpec(b_blk, b_map)],
        out_specs=pl.BlockSpec((tm, tn), o_map),
        out_shape=jax.ShapeDtypeStruct((M, N), out_dtype),
        scratch=[] if nk == 1 or in_place else [pltpu.VMEM((tm, tn), F32)],
        sem=("parallel", "parallel", "arbitrary"))


def _x_half(far):
    x = lax.axis_index("x")
    return 1 - x if far else x


def _mm_nt_half(order, a, bT, *, far, tm, tn, name, into=None):
    M, K = a.shape
    N = bT.shape[0]
    tm, tn = _pick(M, tm), _pick(N // 2, tn)
    nh = N // 2 // tn

    def body(a_ref, b_ref, *rest):
        rest[-1][...] = lax.dot_general(a_ref[...], b_ref[...], _DIMS["nt"], preferred_element_type=F32)

    return _call(
        order, body, [a, bT] + ([] if into is None else [into]), name=name, grid=(M // tm, nh),
        in_specs=[pl.BlockSpec((tm, K), lambda i, j: (i, 0)),
                  pl.BlockSpec((tn, K), lambda i, j: (_x_half(far) * nh + j, 0))] + ([] if into is None else [ANY]),
        out_specs=pl.BlockSpec((tm, tn), lambda i, j: (i, _x_half(far) * nh + j)),
        out_shape=jax.ShapeDtypeStruct((M, N), F32), aliases=None if into is None else {2: 0},
        sem=("parallel", "parallel"))


def _ln_stats(z):
    mu = jnp.mean(z, axis=-1, keepdims=True)
    zc = z - mu
    var = jnp.mean(zc * zc, axis=-1, keepdims=True)
    rstd = lax.rsqrt(var + LN_EPS)
    return zc * rstd, rstd


def _ln_bwd(dy, xhat, rstd, g):
    dxh = dy * g
    m1 = jnp.mean(dxh, axis=-1, keepdims=True)
    m2 = jnp.mean(dxh * xhat, axis=-1, keepdims=True)
    return rstd * (dxh - m1 - xhat * m2)


def _ln_in_fwd(order, x, meta_pad, g, b):
    S, D = x.shape
    nb = S // BLOCK

    def body(x_ref, mp_ref, g_ref, b_ref, h_ref, hb_ref):
        is_meta = pl.program_id(0) == nb
        xin = jnp.where(is_meta, mp_ref[...], x_ref[...])
        xhat, _ = _ln_stats(xin)
        y = xhat * g_ref[...] + b_ref[...]
        h_ref[...] = y
        hb_ref[...] = y.astype(BF16)

    row = pl.BlockSpec((BLOCK, D), lambda i: (i, 0))
    vec = pl.BlockSpec((1, D), lambda i: (0, 0))
    return _call(
        order, body, [x, meta_pad, g, b], name="ln_in_fwd", grid=(nb + 1,),
        in_specs=[pl.BlockSpec((BLOCK, D), lambda i: (jnp.minimum(i, nb - 1), 0)),
                  pl.BlockSpec((BLOCK, D), lambda i: (0, 0)), vec, vec],
        out_specs=[row, row],
        out_shape=[jax.ShapeDtypeStruct((S + BLOCK, D), F32), jax.ShapeDtypeStruct((S + BLOCK, D), BF16)],
        sem=("parallel",))


def _ln_in_bwd(order, x, meta_pad, g, dh0, dz1):
    S, D = x.shape
    nb = S // BLOCK

    def body(x_ref, mp_ref, g_ref, dh_ref, dz_ref, dx_ref, dm_ref, dg_ref, db_ref):
        i = pl.program_id(0)
        is_meta = i == nb
        xin = jnp.where(is_meta, mp_ref[...], x_ref[...])
        xhat, rstd = _ln_stats(xin)
        dy = dh_ref[...] + jnp.where(is_meta, 0.0, DN_ALPHA) * dz_ref[...]
        dxin = _ln_bwd(dy, xhat, rstd, g_ref[...])

        @pl.when(i < nb)
        def _():
            dx_ref[...] = dxin

        @pl.when(is_meta)
        def _():
            dm_ref[...] = dxin

        @pl.when(i == 0)
        def _():
            dg_ref[...] = jnp.zeros_like(dg_ref)
            db_ref[...] = jnp.zeros_like(db_ref)

        dg_ref[...] += jnp.sum(dy * xhat, axis=0, keepdims=True)
        db_ref[...] += jnp.sum(dy, axis=0, keepdims=True)

    row = pl.BlockSpec((BLOCK, D), lambda i: (i, 0))
    rowx = pl.BlockSpec((BLOCK, D), lambda i: (jnp.minimum(i, nb - 1), 0))
    vec = pl.BlockSpec((1, D), lambda i: (0, 0))
    return _call(
        order, body, [x, meta_pad, g, dh0, dz1], name="ln_in_bwd", grid=(nb + 1,),
        in_specs=[rowx, pl.BlockSpec((BLOCK, D), lambda i: (0, 0)), vec, row, rowx],
        out_specs=[rowx, pl.BlockSpec((BLOCK, D), lambda i: (0, 0)), vec, vec],
        out_shape=[jax.ShapeDtypeStruct((S, D), F32), jax.ShapeDtypeStruct((BLOCK, D), F32),
                   jax.ShapeDtypeStruct((1, D), F32), jax.ShapeDtypeStruct((1, D), F32)],
        sem=("arbitrary",))


def _ln1_fwd(order, h0, y1, g, b):
    S, D = y1.shape
    tm = _pick(S, BLOCK, 8)

    def body(h_ref, y_ref, g_ref, b_ref, o_ref, ob_ref):
        xhat, _ = _ln_stats(DN_ALPHA * h_ref[...] + y_ref[...])
        y = xhat * g_ref[...] + b_ref[...]
        o_ref[...] = y
        ob_ref[...] = y.astype(BF16)

    row = pl.BlockSpec((tm, D), lambda i: (i, 0))
    vec = pl.BlockSpec((1, D), lambda i: (0, 0))
    return _call(
        order, body, [h0, y1, g, b], name="ln1_fwd", grid=(S // tm,), in_specs=[row, row, vec, vec],
        out_specs=[row, row],
        out_shape=[jax.ShapeDtypeStruct((S, D), F32), jax.ShapeDtypeStruct((S, D), BF16)],
        sem=("parallel",))


def _ln1_bwd(order, h0, y1, g, dh1, dz2):
    S, D = y1.shape
    tm = _pick(S, BLOCK, 8)

    def body(h_ref, y_ref, g_ref, dh_ref, dz2_ref, dz_ref, dzb_ref, dg_ref, db_ref):
        i = pl.program_id(0)
        xhat, rstd = _ln_stats(DN_ALPHA * h_ref[...] + y_ref[...])
        dy = dh_ref[...] + DN_ALPHA * dz2_ref[...]
        dz = _ln_bwd(dy, xhat, rstd, g_ref[...])
        dz_ref[...] = dz
        dzb_ref[...] = dz.astype(BF16)

        @pl.when(i == 0)
        def _():
            dg_ref[...] = jnp.zeros_like(dg_ref)
            db_ref[...] = jnp.zeros_like(db_ref)

        dg_ref[...] += jnp.sum(dy * xhat, axis=0, keepdims=True)
        db_ref[...] += jnp.sum(dy, axis=0, keepdims=True)

    row = pl.BlockSpec((tm, D), lambda i: (i, 0))
    vec = pl.BlockSpec((1, D), lambda i: (0, 0))
    return _call(
        order, body, [h0, y1, g, dh1, dz2], name="ln1_bwd", grid=(S // tm,),
        in_specs=[row, row, vec, row, row], out_specs=[row, row, vec, vec],
        out_shape=[jax.ShapeDtypeStruct((S, D), F32), jax.ShapeDtypeStruct((S, D), BF16),
                   jax.ShapeDtypeStruct((1, D), F32), jax.ShapeDtypeStruct((1, D), F32)],
        sem=("arbitrary",))


def _ln2_loss_bwd(order, h1, y2, target, g, b):
    S, D = y2.shape
    tm = _pick(S, BLOCK, 8)

    def body(h_ref, y_ref, t_ref, g_ref, b_ref, dz_ref, dzb_ref, dg_ref, db_ref, loss_ref):
        i = pl.program_id(0)
        xhat, rstd = _ln_stats(DN_ALPHA * h_ref[...] + y_ref[...])
        diff = xhat * g_ref[...] + b_ref[...] - t_ref[...]
        dy = diff / D
        dz = _ln_bwd(dy, xhat, rstd, g_ref[...])
        dz_ref[...] = dz
        dzb_ref[...] = dz.astype(BF16)

        @pl.when(i == 0)
        def _():
            dg_ref[...] = jnp.zeros_like(dg_ref)
            db_ref[...] = jnp.zeros_like(db_ref)
            loss_ref[...] = jnp.zeros_like(loss_ref)

        dg_ref[...] += jnp.sum(dy * xhat, axis=0, keepdims=True)
        db_ref[...] += jnp.sum(dy, axis=0, keepdims=True)
        loss_ref[...] += jnp.sum(jnp.mean(diff * diff, axis=-1, keepdims=True), axis=0, keepdims=True)

    row = pl.BlockSpec((tm, D), lambda i: (i, 0))
    vec = pl.BlockSpec((1, D), lambda i: (0, 0))
    one = pl.BlockSpec((1, 1), lambda i: (0, 0))
    return _call(
        order, body, [h1, y2, target, g, b], name="ln2_loss_bwd", grid=(S // tm,),
        in_specs=[row, row, row, vec, vec], out_specs=[row, row, vec, vec, one],
        out_shape=[jax.ShapeDtypeStruct((S, D), F32), jax.ShapeDtypeStruct((S, D), BF16),
                   jax.ShapeDtypeStruct((1, D), F32), jax.ShapeDtypeStruct((1, D), F32),
                   jax.ShapeDtypeStruct((1, 1), F32)],
        sem=("arbitrary",))


def _rope_table(S):
    r = jnp.arange(S + BLOCK)
    pos = jnp.where(r < S, r + N_META, jnp.maximum(r - (S + META_ROW0), 0))
    half = ROPE_DIM // 2
    lane = jnp.arange(LANES) % HEAD_DIM
    inv_freq = ROPE_THETA ** (-(lane % half).astype(F32) * 2.0 / ROPE_DIM)
    ang = pos.astype(F32)[:, None] * inv_freq[None, :]
    cos, sin = jnp.cos(ang), jnp.sin(ang)
    c = jnp.where(lane < ROPE_DIM, cos, 1.0)
    sa = jnp.where(lane < half, -sin, 0.0)
    sb = jnp.where((lane >= half) & (lane < ROPE_DIM), sin, 0.0)
    return jnp.concatenate([c, sa, sb], axis=1).astype(F32)


def _rope(x, tab):
    h = ROPE_DIM // 2
    return (x * tab[:, :LANES] + pltpu.roll(x, LANES - h, 1) * tab[:, LANES:2 * LANES]
            + pltpu.roll(x, h, 1) * tab[:, 2 * LANES:])


def _rope_t(dy, tab):
    h = ROPE_DIM // 2
    return (dy * tab[:, :LANES] + pltpu.roll(dy * tab[:, LANES:2 * LANES], h, 1)
            + pltpu.roll(dy * tab[:, 2 * LANES:], LANES - h, 1))


NKEY = N_META + 2 * BLOCK


def _attn_tiles(g, n, S, sink_ref, q_ref, k_ref, v_ref, tab_ref):
    NQG = Q_PER_KV // 2
    R = NQG * BLOCK
    halfsel = (g % 2).astype(F32)
    prev = jnp.maximum(n - 1, 0)
    qrow = pl.ds(pl.multiple_of(n * BLOCK, BLOCK), BLOCK)
    prow = pl.ds(pl.multiple_of(prev * BLOCK, BLOCK), BLOCK)
    mrow = pl.ds(S + META_ROW0, N_META)

    tq = tab_ref[qrow, :]
    qf = q_ref[...]
    q4 = jnp.concatenate([_rope(qf[:, LANES * p:LANES * (p + 1)], tq) for p in range(NQG)], axis=0).astype(BF16)

    tk = jnp.concatenate([tab_ref[mrow, :], tab_ref[prow, :], tq], axis=0)
    kr = _rope(jnp.concatenate([k_ref[mrow, :], k_ref[prow, :], k_ref[qrow, :]], axis=0), tk)
    vr = jnp.concatenate([v_ref[mrow, :], v_ref[prow, :], v_ref[qrow, :]], axis=0)

    lane = lax.broadcasted_iota(jnp.int32, kr.shape, 1)
    own = jnp.where(lane < HEAD_DIM, 1.0 - halfsel, halfsel)

    def lo_hi(t):
        mine = t * own
        other = pltpu.roll(mine, HEAD_DIM, 1)
        lo = mine * (1.0 - halfsel) + other * halfsel
        hi = other * (1.0 - halfsel) + mine * halfsel
        return lo.astype(BF16), hi.astype(BF16)

    klo, khi = lo_hi(kr)
    vlo, vhi = lo_hi(vr)

    jj = lax.broadcasted_iota(jnp.int32, (BLOCK, R), 0)
    qi = lax.broadcasted_iota(jnp.int32, (BLOCK, R), 1) & (BLOCK - 1)
    in_cur = jj <= qi
    band_ok = in_cur | (jj > qi + jnp.where(n >= 1, 0, 2 * BLOCK))

    def soft(kk, parity):
        sk = jnp.concatenate(
            [jnp.full((1, BLOCK), sink_ref[0, Q_PER_KV * g + 2 * p + parity], F32) for p in range(NQG)], axis=1)
        s = lax.dot_general(kk, q4, _DIMS["nt"], preferred_element_type=F32) * ATTN_SCALE
        band = jnp.where(in_cur, s[N_META + BLOCK:], s[N_META:N_META + BLOCK])
        s = jnp.concatenate([s[:N_META], jnp.where(band_ok, band, NEG_INF)], axis=0)
        m = jnp.maximum(jnp.max(s, axis=0, keepdims=True), sk)
        p = jnp.exp(s - m)
        es = jnp.exp(sk - m)
        inv = 1.0 / (jnp.sum(p, axis=0, keepdims=True) + es)
        return p * inv, es * inv

    pe, sink_e = soft(klo, 0)
    po, sink_o = soft(khi, 1)
    return q4, tk, (klo, khi), (vlo, vhi), (pe, po), (sink_e, sink_o), own, in_cur


def _spread(t, in_cur):
    band = t[N_META:]
    return jnp.concatenate([t[:N_META], jnp.where(in_cur, 0.0, band), jnp.where(in_cur, band, 0.0)], axis=0)


def _attn_specs(S, ATTN, KVW):
    Tp = S + BLOCK
    koff, voff = ATTN // LANES, (ATTN + KVW) // LANES
    gw = Q_PER_KV * HEAD_DIM
    return [pl.BlockSpec(memory_space=pltpu.SMEM),
            pl.BlockSpec((BLOCK, gw), lambda g, n: (n, g)),
            pl.BlockSpec((Tp, LANES), lambda g, n: (0, koff + g // 2)),
            pl.BlockSpec((Tp, LANES), lambda g, n: (0, voff + g // 2)),
            pl.BlockSpec((Tp, 3 * LANES), lambda g, n: (0, 0))]


def _attn_fwd(order, proj, tab, sinks, S, ATTN, KVW):
    G = KVW // HEAD_DIM
    nb = S // BLOCK
    gw = Q_PER_KV * HEAD_DIM

    def body(sink_ref, q_ref, k_ref, v_ref, tab_ref, o_ref):
        g, n = pl.program_id(0), pl.program_id(1)
        _, _, _, (vlo, vhi), (pe, po), _, _, in_cur = _attn_tiles(g, n, S, sink_ref, q_ref, k_ref, v_ref, tab_ref)
        o4 = (lax.dot_general(_spread(pe, in_cur).astype(BF16), vlo, _DIMS["tn"], preferred_element_type=F32)
              + lax.dot_general(_spread(po, in_cur).astype(BF16), vhi, _DIMS["tn"], preferred_element_type=F32))
        o_ref[...] = jnp.concatenate(
            [o4[BLOCK * p:BLOCK * (p + 1)] for p in range(Q_PER_KV // 2)], axis=1).astype(BF16)

    return _call(
        order, body, [sinks, proj, proj, proj, tab], name="attn_fwd", grid=(G, nb),
        in_specs=_attn_specs(S, ATTN, KVW),
        out_specs=pl.BlockSpec((BLOCK, gw), lambda g, n: (n, g)),
        out_shape=jax.ShapeDtypeStruct((S, ATTN), BF16),
        sem=("parallel", "arbitrary"))


def _attn_bwd(order, proj, tab, sinks, da, dproj, S, ATTN, KVW):
    G = KVW // HEAD_DIM
    nb = S // BLOCK
    Tp = S + BLOCK
    NQG = Q_PER_KV // 2
    gw = Q_PER_KV * HEAD_DIM

    def body(sink_ref, q_ref, k_ref, v_ref, tab_ref, da_ref, dproj_in, dq_ref, dk_ref, dv_ref, ds_ref):
        del dproj_in
        g, n = pl.program_id(0), pl.program_id(1)
        q4, tk, (klo, khi), (vlo, vhi), (pe, po), (sink_e, sink_o), own, in_cur = _attn_tiles(
            g, n, S, sink_ref, q_ref, k_ref, v_ref, tab_ref)
        dof = da_ref[...]
        do4 = jnp.concatenate([dof[:, LANES * p:LANES * (p + 1)] for p in range(NQG)], axis=0)

        def grads(p, vv):
            dp = lax.dot_general(vv, do4, _DIMS["nt"], preferred_element_type=F32)
            dp = jnp.concatenate(
                [dp[:N_META], jnp.where(in_cur, dp[N_META + BLOCK:], dp[N_META:N_META + BLOCK])], axis=0)
            delta = jnp.sum(p * dp, axis=0, keepdims=True)
            return _spread(p * (dp - delta) * ATTN_SCALE, in_cur).astype(BF16), delta

        dse, delta_e = grads(pe, vlo)
        dso, delta_o = grads(po, vhi)

        dq4 = (lax.dot_general(dse, klo, _DIMS["tn"], preferred_element_type=F32)
               + lax.dot_general(dso, khi, _DIMS["tn"], preferred_element_type=F32))
        tq = tk[N_META + BLOCK:]
        dq_ref[...] = jnp.concatenate(
            [_rope_t(dq4[BLOCK * p:BLOCK * (p + 1)], tq) for p in range(NQG)], axis=1).astype(BF16)

        lane = lax.broadcasted_iota(jnp.int32, (NKEY, LANES), 1)

        def fold(lo_part, hi_part):
            t = jnp.where(lane < HEAD_DIM, lo_part, hi_part)
            return t + pltpu.roll(t, HEAD_DIM, 1)

        dk = _rope_t(fold(jnp.dot(dse, q4, preferred_element_type=F32),
                          jnp.dot(dso, q4, preferred_element_type=F32)), tk) * own
        dv = fold(jnp.dot(_spread(pe, in_cur).astype(BF16), do4, preferred_element_type=F32),
                  jnp.dot(_spread(po, in_cur).astype(BF16), do4, preferred_element_type=F32)) * own

        @pl.when((n == 0) & (g % 2 == 0))
        def _():
            dk_ref[...] = jnp.zeros_like(dk_ref)
            dv_ref[...] = jnp.zeros_like(dv_ref)

        @pl.when(n == 0)
        def _():
            ds_ref[...] = jnp.zeros_like(ds_ref)

        prev = jnp.maximum(n - 1, 0)
        qrow = pl.ds(pl.multiple_of(n * BLOCK, BLOCK), BLOCK)
        prow = pl.ds(pl.multiple_of(prev * BLOCK, BLOCK), BLOCK)
        mrow = pl.ds(S + META_ROW0, N_META)
        for ref, val in ((dk_ref, dk), (dv_ref, dv)):
            ref[mrow, :] += val[:N_META]
            ref[prow, :] += val[N_META:N_META + BLOCK]
            ref[qrow, :] += val[N_META + BLOCK:]

        srow = lax.broadcasted_iota(jnp.int32, (Q_PER_KV, LANES), 0)
        acc = jnp.zeros((Q_PER_KV, LANES), F32)
        for p in range(NQG):
            for parity, (sk, dl) in enumerate(((sink_e, delta_e), (sink_o, delta_o))):
                val = -jnp.sum(sk[:, BLOCK * p:BLOCK * (p + 1)] * dl[:, BLOCK * p:BLOCK * (p + 1)])
                acc = jnp.where(srow == 2 * p + parity, val, acc)
        ds_ref[0] += acc

    in_specs = _attn_specs(S, ATTN, KVW) + [pl.BlockSpec((BLOCK, gw), lambda g, n: (n, g)), ANY]
    slab = pl.BlockSpec((Tp, LANES), lambda g, n: (0, g // 2))
    return _call(
        order, body, [sinks, proj, proj, proj, tab, da, dproj], name="attn_bwd", grid=(G, nb), in_specs=in_specs,
        out_specs=[pl.BlockSpec((BLOCK, gw), lambda g, n: (n, g)), slab, slab,
                   pl.BlockSpec((1, Q_PER_KV, LANES), lambda g, n: (g, 0, 0))],
        out_shape=[jax.ShapeDtypeStruct(dproj.shape, BF16), jax.ShapeDtypeStruct((Tp, KVW), F32),
                   jax.ShapeDtypeStruct((Tp, KVW), F32), jax.ShapeDtypeStruct((G, Q_PER_KV, LANES), F32)],
        aliases={6: 0}, sem=("arbitrary", "arbitrary"))


def _zero_meta_block(order, Tp, IN):
    tc = _pick(IN, 4096)

    def body(o_ref):
        o_ref[...] = jnp.zeros_like(o_ref)

    return _call(
        order, body, [], name="dproj_zero_meta", grid=(IN // tc,), in_specs=[],
        out_specs=pl.BlockSpec((BLOCK, tc), lambda j: (Tp // BLOCK - 1, j)),
        out_shape=jax.ShapeDtypeStruct((Tp, IN), BF16), sem=("parallel",))


def _put_dkv(order, dk, dv, dproj, ATTN):
    Tp, KVW = dk.shape
    nkb = KVW // LANES
    koff = ATTN // LANES

    def body(dk_ref, dv_ref, dproj_in, o_ref):
        del dproj_in
        t = pl.program_id(0)
        o_ref[...] = jnp.where(t < nkb, dk_ref[...], dv_ref[...]).astype(BF16)

    src = pl.BlockSpec((Tp, LANES), lambda t: (0, t % nkb))
    return _call(
        order, body, [dk, dv, dproj], name="dproj_put_dkv", grid=(2 * nkb,), in_specs=[src, src, ANY],
        out_specs=pl.BlockSpec((Tp, LANES), lambda t: (0, koff + t)),
        out_shape=jax.ShapeDtypeStruct(dproj.shape, BF16), aliases={2: 0}, sem=("parallel",))


HALO = 16


def _window_sums(x, up):
    n = x.shape[0]
    out = []
    s = x
    for k in (1, 2, 4, 8):
        s = s + pltpu.roll(s, (n - k) if up else k, 0)
        out.append(s)
    return out


def _pool_specs(S, ub, gw, tm):
    meta_halo = (S + BLOCK - HALO) // HALO

    def main(g):
        return pl.BlockSpec((tm, gw), lambda i: (i, ub + g))

    def halo(g):
        return pl.BlockSpec((HALO, gw), lambda i: (jnp.where(i == 0, meta_halo, i * (tm // HALO) - 1), ub + g))

    return [main(g) for g in range(4)] + [halo(g) for g in range(4)]


def _pooled(main_refs, halo_refs, g):
    x = jnp.concatenate([halo_refs[g][...], main_refs[g][...]], axis=0)
    s = _window_sums(x, up=False)[g]
    return (s[HALO:] * (1.0 / POOL_WINDOWS[g]) - x[HALO:]).astype(BF16)


def _pool_fwd(order, proj, wgrp, scale, S, uoff, POOL):
    gw = POOL // 4
    tm = BLOCK

    def body(*refs):
        main, halo = refs[:4], refs[4:8]
        w_ref, sc_ref, o_ref = refs[8:]
        for g in range(4):
            mixed = jnp.dot(_pooled(main, halo, g), w_ref[g], preferred_element_type=F32)
            o_ref[:, gw * g:gw * (g + 1)] = (mixed * sc_ref[:, gw * g:gw * (g + 1)]).astype(BF16)

    return _call(
        order, body, [proj] * 8 + [wgrp, scale], name="pool_fwd", grid=(S // tm,),
        in_specs=_pool_specs(S, uoff // gw, gw, tm) + [
            pl.BlockSpec((4, gw, gw), lambda i: (0, 0, 0)), pl.BlockSpec((1, POOL), lambda i: (0, 0))],
        out_specs=pl.BlockSpec((tm, POOL), lambda i: (i, 0)),
        out_shape=jax.ShapeDtypeStruct((S, POOL), BF16), sem=("parallel",))


def _pool_bwd_mix(order, proj, wgrp, scale, dps, S, uoff, POOL):
    gw = POOL // 4
    tm = BLOCK

    def body(*refs):
        main, halo = refs[:4], refs[4:8]
        w_ref, sc_ref, dps_ref, dpl_ref, dw_ref, dsc_ref = refs[8:]
        i = pl.program_id(0)

        @pl.when(i == 0)
        def _():
            dw_ref[...] = jnp.zeros_like(dw_ref)
            dsc_ref[...] = jnp.zeros_like(dsc_ref)

        for g in range(4):
            cols = slice(gw * g, gw * (g + 1))
            pooled = _pooled(main, halo, g)
            mixed = jnp.dot(pooled, w_ref[g], preferred_element_type=F32)
            dps_g = dps_ref[:, cols]
            dsc_ref[:, cols] += jnp.sum(dps_g * mixed, axis=0, keepdims=True)
            dms = (dps_g * sc_ref[:, cols]).astype(BF16)
            dw_ref[g] += lax.dot_general(pooled, dms, _DIMS["tn"], preferred_element_type=F32)
            dpl_ref[:, cols] = lax.dot_general(dms, w_ref[g], _DIMS["nt"], preferred_element_type=F32)

    row = pl.BlockSpec((tm, POOL), lambda i: (i, 0))
    return _call(
        order, body, [proj] * 8 + [wgrp, scale, dps], name="pool_bwd_mix", grid=(S // tm,),
        in_specs=_pool_specs(S, uoff // gw, gw, tm) + [
            pl.BlockSpec((4, gw, gw), lambda i: (0, 0, 0)), pl.BlockSpec((1, POOL), lambda i: (0, 0)), row],
        out_specs=[row, pl.BlockSpec((4, gw, gw), lambda i: (0, 0, 0)), pl.BlockSpec((1, POOL), lambda i: (0, 0))],
        out_shape=[jax.ShapeDtypeStruct((S, POOL), F32), jax.ShapeDtypeStruct((4, gw, gw), F32),
                   jax.ShapeDtypeStruct((1, POOL), F32)],
        sem=("arbitrary",))


def _pool_bwd_window(order, dpl, dproj, S, uoff, POOL):
    gw = POOL // 4
    nb = S // BLOCK
    ub = uoff // gw

    def body(main_ref, halo_ref, dproj_in, o_ref):
        del dproj_in
        b, g = pl.program_id(0), pl.program_id(1)
        main = jnp.where(b < nb, main_ref[...], 0.0)
        halo = jnp.where(b == nb - 1, 0.0, halo_ref[...])
        sums = _window_sums(jnp.concatenate([main, halo], axis=0), up=True)
        du = jnp.zeros((BLOCK, gw), F32)
        for k, w in enumerate(POOL_WINDOWS):
            du = jnp.where(g == k, sums[k][:BLOCK] * (1.0 / w), du)
        du = du - main
        row = lax.broadcasted_iota(jnp.int32, du.shape, 0)
        first_valid = jnp.where(b == nb, META_ROW0, 0)
        o_ref[...] = jnp.where(row >= first_valid, du, 0.0).astype(BF16)

    return _call(
        order, body, [dpl, dpl, dproj], name="pool_bwd_window", grid=(nb + 1, 4),
        in_specs=[pl.BlockSpec((BLOCK, gw), lambda b, g: (jnp.minimum(b, nb - 1), g)),
                  pl.BlockSpec((HALO, gw), lambda b, g: (
                      jnp.where(b == nb, 0, jnp.minimum((b + 1) * (BLOCK // HALO), S // HALO - 1)), g)),
                  ANY],
        out_specs=pl.BlockSpec((BLOCK, gw), lambda b, g: (b, ub + g)),
        out_shape=jax.ShapeDtypeStruct(dproj.shape, BF16), aliases={2: 0}, sem=("parallel", "parallel"))


def _sigmoid(x):
    return 1.0 / (1.0 + jnp.exp(-x))


def _gate_tiles(S, D, goff):
    tc = 512
    while goff % tc or D % tc:
        tc //= 2
    return _pick(S, 512, 8), tc


def _gate_mix(order, proj, bgate, a_out, p_out, S, D, goff):
    tm, tc = _gate_tiles(S, D, goff)
    g0b, nd = goff // tc, D // tc

    def body(l0_ref, l1_ref, b_ref, a_ref, p_ref, o_ref):
        g0 = _sigmoid(l0_ref[...] + b_ref[0:1, :])
        g1 = _sigmoid(l1_ref[...] + b_ref[1:2, :])
        o_ref[...] = (g0 * a_ref[...] + g1 * p_ref[...]).astype(BF16)

    tile = pl.BlockSpec((tm, tc), lambda i, j: (i, j))
    return _call(
        order, body, [proj, proj, bgate, a_out, p_out], name="gate_mix", grid=(S // tm, nd),
        in_specs=[pl.BlockSpec((tm, tc), lambda i, j: (i, g0b + j)),
                  pl.BlockSpec((tm, tc), lambda i, j: (i, g0b + nd + j)),
                  pl.BlockSpec((2, tc), lambda i, j: (0, j)), tile, tile],
        out_specs=tile, out_shape=jax.ShapeDtypeStruct((S, D), BF16), sem=("parallel", "parallel"))


def _gate_bwd(order, proj, bgate, a_out, p_out, dmixed, dproj, S, D, goff):
    tm, tc = _gate_tiles(S, D, goff)
    g0b, nd, ni = goff // tc, D // tc, S // tm
    nsteps = nd * ni

    def body(l0_ref, l1_ref, b_ref, a_ref, p_ref, dm_ref, dproj_in, dap_ref, dproj_ref, db_ref, buf, sems):
        del dproj_in
        j, i = pl.program_id(0), pl.program_id(1)
        step = j * ni + i
        slot = step % 2

        def put(sl, br):
            col = pl.multiple_of((g0b + br * nd + j) * tc, tc)
            return pltpu.make_async_copy(
                buf.at[sl, br], dproj_ref.at[pl.ds(pl.multiple_of(i * tm, tm), tm), pl.ds(col, tc)], sems.at[sl, br])

        @pl.when(step >= 2)
        def _():
            put(slot, 0).wait()
            put(slot, 1).wait()

        @pl.when(i == 0)
        def _():
            db_ref[...] = jnp.zeros_like(db_ref)

        dm = dm_ref[...]
        for br, (l_ref, val_ref) in enumerate(((l0_ref, a_ref), (l1_ref, p_ref))):
            gate = _sigmoid(l_ref[...] + b_ref[br:br + 1, :])
            dap_ref[br] = (dm * gate).astype(BF16)
            dl = dm * val_ref[...] * gate * (1.0 - gate)
            buf[slot, br] = dl.astype(BF16)
            db_ref[br] += jnp.sum(dl, axis=0, keepdims=True)
            put(slot, br).start()

        @pl.when(step == nsteps - 1)
        def _():
            for sl in ((slot, 1 - slot) if nsteps > 1 else (slot,)):
                put(sl, 0).wait()
                put(sl, 1).wait()

    tile = pl.BlockSpec((tm, tc), lambda j, i: (i, j))
    return _call(
        order, body, [proj, proj, bgate, a_out, p_out, dmixed, dproj], name="gate_bwd", grid=(nd, ni),
        in_specs=[pl.BlockSpec((tm, tc), lambda j, i: (i, g0b + j)),
                  pl.BlockSpec((tm, tc), lambda j, i: (i, g0b + nd + j)),
                  pl.BlockSpec((2, tc), lambda j, i: (0, j)), tile, tile, tile, ANY],
        out_specs=[pl.BlockSpec((2, tm, tc), lambda j, i: (0, i, j)), ANY,
                   pl.BlockSpec((2, 1, tc), lambda j, i: (0, 0, j))],
        out_shape=[jax.ShapeDtypeStruct((2, S, D), BF16), jax.ShapeDtypeStruct(dproj.shape, BF16),
                   jax.ShapeDtypeStruct((2, 1, D), F32)],
        scratch=[pltpu.VMEM((2, 2, tm, tc), BF16), pltpu.SemaphoreType.DMA((2, 2))],
        aliases={6: 1}, sem=("arbitrary", "arbitrary"))


def _ffn_in_near(order, h, wT, FF):
    S, D = h.shape
    tm, tn = _pick(S, 2048), _pick(FF, 512)
    nj = FF // tn

    def body(h_ref, w_ref, f_ref):
        f_ref[...] = lax.dot_general(h_ref[...], w_ref[...], _DIMS["nt"], preferred_element_type=F32)

    return _call(
        order, body, [h, wT], name="ffn_in_near", grid=(S // tm, nj),
        in_specs=[pl.BlockSpec((tm, D), lambda i, j: (i, 0)),
                  pl.BlockSpec((tn, D), lambda i, j: (_x_half(False) * nj + j, 0))],
        out_specs=pl.BlockSpec((tm, tn), lambda i, j: (i, j)),
        out_shape=jax.ShapeDtypeStruct((S, FF), F32), sem=("parallel", "parallel"))


def _gate_up(near, far):
    near_is_gate = lax.axis_index("x") == 0
    return jnp.where(near_is_gate, near, far), jnp.where(near_is_gate, far, near)


def _ffn_in_far(order, h, wT, near):
    S, D = h.shape
    FF = near.shape[1]
    tm, tn = _pick(S, 1024), _pick(FF, 512)
    nj = FF // tn

    def body(h_ref, w_ref, near_ref, f_ref, act_ref):
        far = lax.dot_general(h_ref[...], w_ref[...], _DIMS["nt"], preferred_element_type=F32)
        f_ref[...] = far
        gt, up = _gate_up(near_ref[...], far)
        act_ref[...] = (gt * _sigmoid(gt) * up).astype(BF16)

    tile = pl.BlockSpec((tm, tn), lambda i, j: (i, j))
    return _call(
        order, body, [h, wT, near], name="ffn_in_far", grid=(S // tm, nj),
        in_specs=[pl.BlockSpec((tm, D), lambda i, j: (i, 0)),
                  pl.BlockSpec((tn, D), lambda i, j: (_x_half(True) * nj + j, 0)), tile],
        out_specs=[tile, tile],
        out_shape=[jax.ShapeDtypeStruct((S, FF), F32), jax.ShapeDtypeStruct((S, FF), BF16)],
        sem=("parallel", "parallel"))


def _d_act_swiglu(order, dy, wdown, near, far):
    S, D = dy.shape
    FF = wdown.shape[0]
    tm, tn = _pick(S, 1024), _pick(FF, 256)

    def body(dy_ref, w_ref, near_ref, far_ref, o_ref):
        d = lax.dot_general(dy_ref[...], w_ref[...], _DIMS["nt"], preferred_element_type=F32)
        gt, up = _gate_up(near_ref[...], far_ref[...])
        s = _sigmoid(gt)
        o_ref[0] = (d * up * s * (1.0 + gt * (1.0 - s))).astype(BF16)
        o_ref[1] = (d * gt * s).astype(BF16)

    tile = pl.BlockSpec((tm, tn), lambda i, j: (i, j))
    return _call(
        order, body, [dy, wdown, near, far], name="d_act", grid=(S // tm, FF // tn),
        in_specs=[pl.BlockSpec((tm, D), lambda i, j: (i, 0)), pl.BlockSpec((tn, D), lambda i, j: (j, 0)), tile, tile],
        out_specs=pl.BlockSpec((2, tm, tn), lambda i, j: (0, i, j)),
        out_shape=jax.ShapeDtypeStruct((2, S, FF), BF16), sem=("parallel", "parallel"))


def _place():
    return lax.axis_index("x"), lax.axis_index("y"), lax.axis_index("c")


def _xfer_start(order, name, bufs, copies):
    nb = len(bufs)
    n = len(copies([None] * nb, None))
    is_new = [isinstance(b, jax.ShapeDtypeStruct) for b in bufs]
    old = [b for b, fresh in zip(bufs, is_new) if not fresh]
    no = len(old)
    tok = [] if any(order.last is b for b in old) else [order.last]
    first_out = no + len(tok)

    def body(*refs):
        send, recv = refs[first_out:first_out + n], refs[first_out + n:first_out + 2 * n]
        token = refs[-1]
        given, made = iter(refs[:no]), iter(refs[first_out + 2 * n + no:-1])
        logical = [next(made) if fresh else next(given) for fresh in is_new]
        for i, (src, dst, dev) in enumerate(copies(logical, _place())):
            pltpu.make_async_remote_copy(src_ref=src, dst_ref=dst, send_sem=send[i], recv_sem=recv[i],
                                         device_id=dev, device_id_type=MESH).start()
        token[...] = jnp.zeros_like(token)

    fresh_shapes = [b for b, fresh in zip(bufs, is_new) if fresh]
    out = pl.pallas_call(
        body, name=name,
        out_shape=tuple([pltpu.SemaphoreType.DMA(())] * (2 * n)
                        + [pltpu.HBM(b.shape, b.dtype) for b in old + fresh_shapes]
                        + [jax.ShapeDtypeStruct((8, LANES), F32)]),
        in_specs=[HBM] * no + [ANY] * len(tok),
        out_specs=tuple([SEM] * (2 * n) + [HBM] * nb + [pl.BlockSpec(memory_space=pltpu.VMEM)]),
        input_output_aliases={i: 2 * n + i for i in range(no)},
        compiler_params=pltpu.CompilerParams(has_side_effects=EFFECT),
    )(*[pltpu.with_memory_space_constraint(b, pltpu.HBM) for b in old], *tok)
    order.last = out[-1]
    thru, made = iter(out[2 * n:2 * n + no]), iter(out[2 * n + no:2 * n + nb])
    return list(out[:2 * n]), [next(made) if fresh else next(thru) for fresh in is_new]


def _xfer_wait(order, name, sems, bufs, copies):
    nb = len(bufs)
    n = len(sems) // 2
    tok = order.last

    def body(*refs):
        send, recv = refs[nb:nb + n], refs[nb + n:nb + 2 * n]
        token = refs[-1]
        for i, (src, dst, dev) in enumerate(copies(refs[:nb], _place())):
            cp = pltpu.make_async_remote_copy(src_ref=src, dst_ref=dst, send_sem=send[i], recv_sem=recv[i],
                                              device_id=dev, device_id_type=MESH)
            cp.wait_send()
            cp.wait_recv()
        token[...] = jnp.zeros_like(token)

    out = pl.pallas_call(
        body, name=name,
        out_shape=tuple([pltpu.HBM(b.shape, b.dtype) for b in bufs] + [jax.ShapeDtypeStruct((8, LANES), F32)]),
        in_specs=[HBM] * nb + [SEM] * (2 * n) + [ANY],
        out_specs=tuple([HBM] * nb + [pl.BlockSpec(memory_space=pltpu.VMEM)]),
        input_output_aliases={i: i for i in range(nb)},
        compiler_params=pltpu.CompilerParams(has_side_effects=EFFECT),
    )(*bufs, *sems, tok)
    order.last = out[-1]
    return list(out[:nb])


class _Xfer:
    def __init__(self, name, bufs, copies):
        self.name, self.bufs, self.copies = name, list(bufs), copies
        self.sems = None

    def start(self, order):
        self.sems, self.bufs = _xfer_start(order, self.name + "_start", self.bufs, self.copies)

    def wait(self, order, bufs=None):
        self.bufs = _xfer_wait(order, self.name + "_wait", self.sems, bufs or self.bufs, self.copies)
        return self.bufs


def _block_rows(ref, r, d):
    return ref.at[pl.ds(d * r, r)]


NEAR = ("xn", "yn")
ALL_CHIPS = ("xn", "yn", "diag")


def _chip_of(which, x, y):
    return {"xn": (1 - x, y), "yn": (x, 1 - y), "diag": (1 - x, 1 - y)}[which]


def _gather_send(fulls, chips=ALL_CHIPS, sibling=True):
    def copies(refs, place):
        out = []
        for w, full in enumerate(fulls):
            r = full.shape[0] // 8
            if place is None:
                out += [None] * (len(chips) + int(sibling))
                continue
            x, y, c = place
            mine = _block_rows(refs[w], r, 4 * x + 2 * y + c)
            if sibling:
                out.append((mine, mine, (x, y, 1 - c)))
            for which in chips:
                out.append((mine, mine, (*_chip_of(which, x, y), c)))
        return out
    return copies


def _gather_forward(fulls, chips=ALL_CHIPS):
    def copies(refs, place):
        out = []
        for w, full in enumerate(fulls):
            r = full.shape[0] // 8
            if place is None:
                out += [None] * len(chips)
                continue
            x, y, c = place
            for which in chips:
                px, py = _chip_of(which, x, y)
                blk = _block_rows(refs[w], r, 4 * px + 2 * py + c)
                out.append((blk, blk, (x, y, 1 - c)))
        return out
    return copies


def _pair_send(nw):
    def copies(refs, place):
        out = []
        for w in range(nw):
            if place is None:
                out += [None] * 4
                continue
            x, y, c = place
            grad, other = refs[2 * w], refs[2 * w + 1]
            r = other.shape[1]
            for k in range(4):
                out.append((_block_rows(grad, r, 2 * k + 1 - c), other.at[k], (x, y, 1 - c)))
        return out
    return copies


def _chip_send(nw):
    def copies(refs, place):
        out = []
        for w in range(nw):
            if place is None:
                out += [None] * 3
                continue
            x, y, c = place
            psum, parts = refs[2 * w], refs[2 * w + 1]
            for px, py in ((1 - x, y), (x, 1 - y), (1 - x, 1 - y)):
                out.append((psum.at[2 * px + py], parts.at[2 * x + y], (px, py, c)))
        return out
    return copies


def _dev_index():
    x, y, c = _place()
    return 4 * x + 2 * y + c


def _place_own(order, shard, name):
    r, cols = shard.shape
    tr = _pick(r, max(16, (12 << 20) // (4 * cols)), 16)
    nr = r // tr

    def body(s_ref, o_ref):
        o_ref[...] = s_ref[...].astype(BF16)

    return _call(
        order, body, [shard], name=name, grid=(nr,),
        in_specs=[pl.BlockSpec((tr, cols), lambda i: (i, 0))],
        out_specs=pl.BlockSpec((tr, cols), lambda i: (_dev_index() * nr + i, 0)),
        out_shape=jax.ShapeDtypeStruct((8 * r, cols), BF16), sem=("parallel",))


def _pair_sum(order, grad, other, name):
    r, cols = other.shape[1:]
    tr = _pick(r, max(16, (7 << 20) // (2 * cols)), 16)
    nr = r // tr

    def body(g_ref, a_ref, o_ref):
        o_ref[...] = (g_ref[...].astype(F32) + a_ref[...].astype(F32)).astype(BF16)

    blk = pl.BlockSpec((None, tr, cols), lambda k, i: (k, i, 0))
    return _call(
        order, body, [grad, other], name=name, grid=(4, nr),
        in_specs=[pl.BlockSpec((tr, cols), lambda k, i: ((2 * k + lax.axis_index("c")) * nr + i, 0)), blk],
        out_specs=blk, out_shape=jax.ShapeDtypeStruct(other.shape, BF16), sem=("parallel", "parallel"))


def _chip_sum(order, psum, parts, name):
    _, r, cols = parts.shape
    tr = _pick(r, max(16, (1 << 20) // (2 * cols)), 16)

    def my_chip():
        return 2 * lax.axis_index("x") + lax.axis_index("y")

    def body(own_ref, p0, p1, p2, p3, o_ref):
        own = own_ref[...].astype(F32)
        acc = None
        for k, p in enumerate((p0, p1, p2, p3)):
            term = jnp.where(my_chip() == k, own, p[...].astype(F32))
            acc = term if acc is None else acc + term
        o_ref[...] = acc

    def slot(k):
        return pl.BlockSpec((None, tr, cols), lambda i: (jnp.where(my_chip() == k, (k + 1) % 4, k), i, 0))

    return _call(
        order, body, [psum, parts, parts, parts, parts], name=name, grid=(r // tr,),
        in_specs=[pl.BlockSpec((None, tr, cols), lambda i: (my_chip(), i, 0))] + [slot(k) for k in range(4)],
        out_specs=pl.BlockSpec((tr, cols), lambda i: (i, 0)),
        out_shape=jax.ShapeDtypeStruct((r, cols), F32), sem=("parallel",))


def _all_reduce_small(order, pack, name):
    R = pack.shape[0]

    def body(p_ref, o_ref, buf, send_sems, recv_sems):
        x, y, c = _place()
        me = 4 * x + 2 * y + c
        buf[me] = p_ref[...]
        copies = []
        for k in range(1, 8):
            px = 1 - x if k & 4 else x
            py = 1 - y if k & 2 else y
            pc = 1 - c if k & 1 else c
            cp = pltpu.make_async_remote_copy(
                src_ref=p_ref, dst_ref=buf.at[me], send_sem=send_sems.at[k - 1], recv_sem=recv_sems.at[k - 1],
                device_id=(px, py, pc), device_id_type=MESH)
            cp.start()
            copies.append(cp)
        for cp in copies:
            cp.wait_recv()
        acc = buf[0]
        for d in range(1, 8):
            acc = acc + buf[d]
        o_ref[...] = acc
        for cp in copies:
            cp.wait_send()

    vm = pl.BlockSpec(memory_space=pltpu.VMEM)
    return _call(
        order, body, [pack], name=name, in_specs=[vm], out_specs=vm,
        out_shape=jax.ShapeDtypeStruct((R, LANES), F32),
        scratch=[pltpu.VMEM((8, R, LANES), F32), pltpu.SemaphoreType.DMA((7,)), pltpu.SemaphoreType.DMA((7,))])


def _pack(parts):
    flat = []
    for p in parts:
        v = p.reshape(-1).astype(F32)
        flat.append(jnp.pad(v, (0, (-v.shape[0]) % LANES)))
    v = jnp.concatenate(flat)
    v = jnp.pad(v, (0, (-v.shape[0]) % (8 * LANES)))
    return v.reshape(-1, LANES)


def _unpack(pack, shapes):
    v = pack.reshape(-1)
    out, off = [], 0
    for s in shapes:
        n = 1
        for d in s:
            n *= d
        out.append(v[off:off + n].reshape(s))
        off += n + (-n) % LANES
    return out


def _adamw(order, w, g, m, v, name):
    shape = w.shape
    cols = shape[-1]
    w2, g2, m2, v2 = (t.reshape(-1, cols) for t in (w, g, m, v))
    R = w2.shape[0]
    tr = _pick(R, max(8, (1 << 20) // (4 * cols)), 8)

    def body(w_ref, g_ref, m_ref, v_ref, d_ref, mo_ref, vo_ref):
        d_ref[...], mo_ref[...], vo_ref[...] = _adam_math(w_ref[...], g_ref[...], m_ref[...], v_ref[...])

    blk = pl.BlockSpec((tr, cols), lambda i: (i, 0))
    outs = _call(
        order, body, [w2, g2, m2, v2], name=name, grid=(R // tr,), in_specs=[blk] * 4, out_specs=[blk] * 3,
        out_shape=[jax.ShapeDtypeStruct((R, cols), F32)] * 3, sem=("parallel",))
    return tuple(o.reshape(shape) for o in outs)


def _adam_math(w, g, m, v):
    mn = ADAM_B1 * m + (1.0 - ADAM_B1) * g
    vn = ADAM_B2 * v + (1.0 - ADAM_B2) * (g * g)
    m_hat = mn / (1.0 - ADAM_B1 ** ADAM_STEP)
    v_hat = vn / (1.0 - ADAM_B2 ** ADAM_STEP)
    return -ADAM_LR * (m_hat / (jnp.sqrt(v_hat) + ADAM_EPS) + ADAM_WD * w), mn, vn


def _chip_sum_adamw(order, w, psum, parts, m, v, name):
    _, r, cols = parts.shape
    tr = _pick(r, max(16, (12 << 20) // (38 * cols)), 16)

    def my_chip():
        return 2 * lax.axis_index("x") + lax.axis_index("y")

    def body(w_ref, own_ref, p0, p1, p2, p3, m_ref, v_ref, g_ref, d_ref, mo_ref, vo_ref):
        own = own_ref[...].astype(F32)
        g = None
        for k, p in enumerate((p0, p1, p2, p3)):
            term = jnp.where(my_chip() == k, own, p[...].astype(F32))
            g = term if g is None else g + term
        g_ref[...] = g
        d_ref[...], mo_ref[...], vo_ref[...] = _adam_math(w_ref[...], g, m_ref[...], v_ref[...])

    def slot(k):
        return pl.BlockSpec((None, tr, cols), lambda i: (jnp.where(my_chip() == k, (k + 1) % 4, k), i, 0))

    blk = pl.BlockSpec((tr, cols), lambda i: (i, 0))
    return _call(
        order, body, [w, psum, parts, parts, parts, parts, m, v], name=name, grid=(r // tr,),
        in_specs=[blk, pl.BlockSpec((None, tr, cols), lambda i: (my_chip(), i, 0))]
        + [slot(k) for k in range(4)] + [blk, blk],
        out_specs=[blk] * 4, out_shape=[jax.ShapeDtypeStruct((r, cols), F32)] * 4, sem=("parallel",))


class _GradReduce:
    def __init__(self, tag, grads, names):
        self.tag, self.grads, self.names = tag, list(grads), names
        self.pair = self.chip = self.psums = None

    def pair_start(self, order):
        bufs = []
        for g in self.grads:
            bufs += [g, jax.ShapeDtypeStruct((4, g.shape[0] // 8, g.shape[1]), g.dtype)]
        self.pair = _Xfer("pair_" + self.tag, bufs, _pair_send(len(self.grads)))
        self.pair.start(order)

    def pair_sum_chip_start(self, order):
        bufs = self.pair.wait(order)
        self.psums = [_pair_sum(order, bufs[2 * w], bufs[2 * w + 1], "pair_sum_" + nm)
                      for w, nm in enumerate(self.names)]
        cbufs = []
        for p in self.psums:
            cbufs += [p, jax.ShapeDtypeStruct(p.shape, p.dtype)]
        self.chip = _Xfer("chip_" + self.tag, cbufs, _chip_send(len(self.psums)))
        self.chip.start(order)

    def finish(self, order):
        bufs = self.chip.wait(order)
        return [(bufs[2 * w], bufs[2 * w + 1]) for w in range(len(self.names))]


def kernel(x, meta_tokens, ln_in_g, ln_in_b, w_in, b_gate, attn_sinks, w_attn_up, w_pool_grp, pool_scale, w_pool_up, w_out, ln1_g, ln1_b, w_ffn_in, w_ffn_down, ln2_g, ln2_b, loss_target, m_meta_tokens, m_ln_in_g, m_ln_in_b, m_w_in, m_b_gate, m_attn_sinks, m_w_attn_up, m_w_pool_grp, m_pool_scale, m_w_pool_up, m_w_out, m_ln1_g, m_ln1_b, m_w_ffn_in, m_w_ffn_down, m_ln2_g, m_ln2_b, v_meta_tokens, v_ln_in_g, v_ln_in_b, v_w_in, v_b_gate, v_attn_sinks, v_w_attn_up, v_w_pool_grp, v_pool_scale, v_w_pool_up, v_w_out, v_ln1_g, v_ln1_b, v_w_ffn_in, v_w_ffn_down, v_ln2_g, v_ln2_b):
    S, D = x.shape[1], x.shape[2]
    Tp = S + BLOCK
    NQ = attn_sinks.shape[-1]
    ATTN = NQ * HEAD_DIM
    KVW = ATTN // Q_PER_KV
    POOL = pool_scale.shape[-1]
    IN = 8 * w_in.shape[2]
    FF = 8 * w_ffn_down.shape[1]
    uoff = ATTN + 2 * KVW
    goff = uoff + POOL
    gw = POOL // 4
    dcols = D // 8
    assert IN == goff + 2 * D and w_ffn_in.shape[2] * 8 == 2 * FF

    xi, yi, ci = _place()
    dev = 4 * xi + 2 * yi + ci
    x2, tgt = x[0], loss_target[0]
    order = _Order()

    def place_cols(a):
        return lax.dynamic_update_slice(jnp.zeros(a.shape[:-1] + (D,), F32), a, (0,) * (a.ndim - 1) + (dev * dcols,))

    small = _all_reduce_small(order, _pack([place_cols(meta_tokens), place_cols(b_gate[0])]), "small_inputs_gather")
    meta_full, bgate_full = _unpack(small, [(N_META, D), (2, D)])
    meta_pad = jnp.pad(meta_full, ((META_ROW0, 0), (0, 0)))

    wgrp_rows = w_pool_grp[0].reshape(4 * (gw // 8), gw)
    full_in = _place_own(order, w_in[0].T, "own_w_in")
    g_in = _Xfer("gather_w_in_near", [full_in], _gather_send([full_in], NEAR))
    g_in.start(order)
    mix_names = ["w_attn_up", "w_pool_grp", "w_pool_up", "w_out"]
    mix_shards = [w_attn_up[0].T, wgrp_rows, w_pool_up[0].T, w_out[0]]
    full_mix = [_place_own(order, s, "own_" + nm) for s, nm in zip(mix_shards, mix_names)]
    full_ffn = _place_own(order, w_ffn_in[0].T, "own_w_ffn_in")

    ln_in_g2, ln_in_b2 = ln_in_g.reshape(1, D), ln_in_b.reshape(1, D)
    tab = _rope_table(S)

    h0, h0b = _ln_in_fwd(order, x2, meta_pad, ln_in_g2, ln_in_b2)
    bufs = g_in.wait(order)
    d_in = _Xfer("gather_w_in_diag", bufs, _gather_send(bufs, ("diag",), sibling=False))
    d_in.start(order)
    f_in = _Xfer("forward_w_in_near", d_in.bufs, _gather_forward(bufs, NEAR))
    f_in.start(order)
    full_down = _place_own(order, w_ffn_down[0], "own_w_ffn_down")
    bufs = f_in.wait(order)
    proj = _mm_nt_half(order, h0b, bufs[0], far=False, tm=704, tn=1280, name="proj_near")
    bufs = d_in.wait(order, bufs)
    fd_in = _Xfer("forward_w_in_diag", bufs, _gather_forward(bufs, ("diag",)))
    fd_in.start(order)
    ag_mix = _Xfer("gather_mixers", full_mix, _gather_send(full_mix))
    ag_mix.start(order)
    g_ffn = _Xfer("gather_w_ffn_in_near", [full_ffn], _gather_send([full_ffn], NEAR))
    g_ffn.start(order)
    (winT,) = fd_in.wait(order)
    proj = _mm_nt_half(order, h0b, winT, far=True, tm=704, tn=1280, name="proj_far", into=proj)

    att = _attn_fwd(order, proj, tab, attn_sinks, S, ATTN, KVW)
    full_mix = ag_mix.wait(order)
    fw_mix = _Xfer("forward_mixers", full_mix, _gather_forward(full_mix))
    fw_mix.start(order)
    wattT, wgrp_g, wpupT, wout = fw_mix.wait(order)
    wgrp = wgrp_g.reshape(8, 4, gw // 8, gw).transpose(1, 0, 2, 3).reshape(4, gw, gw)

    ps = _pool_fwd(order, proj, wgrp, pool_scale, S, uoff, POOL)
    a_out = _mm(order, att, wattT, kind="nt", out_dtype=F32, tm=1024, tn=1024, name="attn_up")
    p_out = _mm(order, ps, wpupT, kind="nt", out_dtype=F32, tm=1024, tn=1024, name="pool_up")
    mixed = _gate_mix(order, proj, bgate_full, a_out, p_out, S, D, goff)
    y1 = _mm(order, mixed, wout, kind="nn", out_dtype=F32, tm=1024, tn=1024, name="out_proj")

    bufs = g_ffn.wait(order)
    d_ffn = _Xfer("gather_w_ffn_in_diag", bufs, _gather_send(bufs, ("diag",), sibling=False))
    d_ffn.start(order)
    f_ffn = _Xfer("forward_w_ffn_in_near", d_ffn.bufs, _gather_forward(bufs, NEAR))
    f_ffn.start(order)
    h1, h1b = _ln1_fwd(order, h0, y1, ln1_g, ln1_b)
    bufs = f_ffn.wait(order)
    f_near = _ffn_in_near(order, h1b, bufs[0], FF)
    bufs = d_ffn.wait(order, bufs)
    fd_ffn = _Xfer("forward_w_ffn_in_diag", bufs, _gather_forward(bufs, ("diag",)))
    fd_ffn.start(order)
    ag_down = _Xfer("gather_w_ffn_down", [full_down], _gather_send([full_down]))
    ag_down.start(order)
    (wffnT,) = fd_ffn.wait(order)
    f_far, act = _ffn_in_far(order, h1b, wffnT, f_near)

    (full_down,) = ag_down.wait(order)
    fw_down = _Xfer("forward_w_ffn_down", [full_down], _gather_forward([full_down]))
    fw_down.start(order)
    (wdown,) = fw_down.wait(order)
    y2 = _mm(order, act, wdown, kind="nn", out_dtype=F32, tm=1024, tn=1024, tk=5504, name="ffn_down")

    dz2, dz2b, dg2, db2, loss_part = _ln2_loss_bwd(order, h1, y2, tgt, ln2_g, ln2_b)
    df = _d_act_swiglu(order, dz2b, wdown, f_near, f_far)
    gwdown = _mm(order, act, dz2b, kind="tn", out_dtype=BF16, tm=256, tn=2048, name="d_ffn_down")
    rs_down = _GradReduce("w_ffn_down", [gwdown], ["w_ffn_down"])
    rs_down.pair_start(order)
    gwffnT = _mm(order, df, h1b, kind="tn", out_dtype=BF16, tm=256, tn=2048, name="d_ffn_in", a_lead="halves")
    rs_down.pair_sum_chip_start(order)
    rs_ffn = _GradReduce("w_ffn_in", [gwffnT], ["w_ffn_in"])
    rs_ffn.pair_start(order)
    dh1 = _mm(order, df, wffnT, kind="nn", out_dtype=F32, tm=1024, tn=1024, tk=5504, name="d_h1", a_lead="halves")
    rs_ffn.pair_sum_chip_start(order)
    dz1, dz1b, dg1, db1 = _ln1_bwd(order, h0, y1, ln1_g, dh1, dz2)
    gwout = _mm(order, mixed, dz1b, kind="tn", out_dtype=BF16, tm=512, tn=1024, name="d_out_proj")
    rs_out = _GradReduce("w_out", [gwout], ["w_out"])
    rs_out.pair_start(order)
    dmixed = _mm(order, dz1b, wout, kind="nt", out_dtype=F32, tm=1024, tn=1024, name="d_mixed")
    rs_out.pair_sum_chip_start(order)

    dproj = _zero_meta_block(order, Tp, IN)
    dap, dproj, dbgate = _gate_bwd(order, proj, bgate_full, a_out, p_out, dmixed, dproj, S, D, goff)
    gwattT = _mm(order, dap, att, kind="tn", out_dtype=BF16, tm=512, tn=1024, name="d_attn_up", a_lead=0)
    datt = _mm(order, dap, wattT, kind="nn", out_dtype=BF16, tm=1024, tn=1024, name="d_att", a_lead=0)
    gwpupT = _mm(order, dap, ps, kind="tn", out_dtype=BF16, tm=512, tn=1024, name="d_pool_up", a_lead=1)
    dps = _mm(order, dap, wpupT, kind="nn", out_dtype=F32, tm=1024, tn=1024, name="d_ps", a_lead=1)
    dpl, gwgrp, dscale = _pool_bwd_mix(order, proj, wgrp, pool_scale, dps, S, uoff, POOL)
    gwgrp_rows = gwgrp.reshape(4, 8, gw // 8, gw).transpose(1, 0, 2, 3).reshape(8 * 4 * (gw // 8), gw).astype(BF16)
    rs_mix = _GradReduce("mixers", [gwattT, gwgrp_rows, gwpupT], ["w_attn_up", "w_pool_grp", "w_pool_up"])
    rs_mix.pair_start(order)
    dproj = _pool_bwd_window(order, dpl, dproj, S, uoff, POOL)
    rs_mix.pair_sum_chip_start(order)
    dproj, dk, dv, dsink = _attn_bwd(order, proj, tab, attn_sinks, datt, dproj, S, ATTN, KVW)
    dproj = _put_dkv(order, dk, dv, dproj, ATTN)

    weights = dict(meta_tokens=meta_tokens, ln_in_g=ln_in_g, ln_in_b=ln_in_b, w_in=w_in, b_gate=b_gate,
                   attn_sinks=attn_sinks, w_attn_up=w_attn_up, w_pool_grp=w_pool_grp, pool_scale=pool_scale,
                   w_pool_up=w_pool_up, w_out=w_out, ln1_g=ln1_g, ln1_b=ln1_b, w_ffn_in=w_ffn_in,
                   w_ffn_down=w_ffn_down, ln2_g=ln2_g, ln2_b=ln2_b)
    ms = dict(meta_tokens=m_meta_tokens, ln_in_g=m_ln_in_g, ln_in_b=m_ln_in_b, w_in=m_w_in, b_gate=m_b_gate,
              attn_sinks=m_attn_sinks, w_attn_up=m_w_attn_up, w_pool_grp=m_w_pool_grp, pool_scale=m_pool_scale,
              w_pool_up=m_w_pool_up, w_out=m_w_out, ln1_g=m_ln1_g, ln1_b=m_ln1_b, w_ffn_in=m_w_ffn_in,
              w_ffn_down=m_w_ffn_down, ln2_g=m_ln2_g, ln2_b=m_ln2_b)
    vs = dict(meta_tokens=v_meta_tokens, ln_in_g=v_ln_in_g, ln_in_b=v_ln_in_b, w_in=v_w_in, b_gate=v_b_gate,
              attn_sinks=v_attn_sinks, w_attn_up=v_w_attn_up, w_pool_grp=v_w_pool_grp, pool_scale=v_pool_scale,
              w_pool_up=v_w_pool_up, w_out=v_w_out, ln1_g=v_ln1_g, ln1_b=v_ln1_b, w_ffn_in=v_w_ffn_in,
              w_ffn_down=v_w_ffn_down, ln2_g=v_ln2_g, ln2_b=v_ln2_b)
    grads, deltas, new_ms, new_vs = {}, {}, {}, {}

    def update(nm, g):
        g = g.reshape(weights[nm].shape)
        grads[nm] = g
        deltas[nm], new_ms[nm], new_vs[nm] = _adamw(order, weights[nm], g, ms[nm], vs[nm], "adamw_" + nm)

    def update_reduced(nm, bufs, transposed=False):
        psum, parts = bufs
        if transposed:
            to2d, back = (lambda t: t[0].T), (lambda t: t.T[None])
        else:
            to2d, back = (lambda t: t.reshape(parts.shape[1:])), (lambda t: t.reshape(weights[nm].shape))
        outs = _chip_sum_adamw(order, to2d(weights[nm]), psum, parts, to2d(ms[nm]), to2d(vs[nm]), "adamw_" + nm)
        grads[nm], deltas[nm], new_ms[nm], new_vs[nm] = (back(t) for t in outs)

    gwinT = _mm(order, dproj, h0b, kind="tn", out_dtype=BF16, tm=512, tn=1024, name="d_w_in")
    rs_in = _GradReduce("w_in", [gwinT], ["w_in"])
    rs_in.pair_start(order)
    update_reduced("w_ffn_down", rs_down.finish(order)[0])
    rs_in.pair_sum_chip_start(order)
    dh0 = _mm(order, dproj, winT, kind="nn", out_dtype=F32, tm=1408, tn=1024, tk=2560, name="d_h0")
    dx, dmeta_block, dg_in, db_in = _ln_in_bwd(order, x2, meta_pad, ln_in_g2, dh0, dz1)
    grad_x = dx[None]
    dmeta = dmeta_block[META_ROW0:]

    small_shapes = [(D,), (D,), (1, D), (1, D), (1, D), (1, D), (1, POOL), (1, NQ), (), (N_META, D), (2, D)]
    red = _all_reduce_small(order, _pack([dg_in, db_in, dg1, db1, dg2, db2, dscale, dsink[:, :, 0], loss_part,
                                          dmeta, dbgate]), "small_grads_all_reduce")

    update_reduced("w_ffn_in", rs_ffn.finish(order)[0], transposed=True)
    update_reduced("w_out", rs_out.finish(order)[0])
    b_att, b_grp, b_pup = rs_mix.finish(order)
    update("w_attn_up", _chip_sum(order, *b_att, "chip_sum_w_attn_up").T)
    update_reduced("w_pool_grp", b_grp)
    update("w_pool_up", _chip_sum(order, *b_pup, "chip_sum_w_pool_up").T)

    (g_ln_in_g, g_ln_in_b, g_ln1_g, g_ln1_b, g_ln2_g, g_ln2_b, g_scale, g_sinks, loss_sum, g_meta_full,
     g_bgate_full) = _unpack(red, small_shapes)
    loss = 0.5 * loss_sum
    update("meta_tokens", lax.dynamic_slice(g_meta_full, (0, dev * dcols), (N_META, dcols)))
    update("b_gate", lax.dynamic_slice(g_bgate_full, (0, dev * dcols), (2, dcols)))
    for nm, g in (("ln_in_g", g_ln_in_g), ("ln_in_b", g_ln_in_b), ("ln1_g", g_ln1_g), ("ln1_b", g_ln1_b),
                  ("ln2_g", g_ln2_g), ("ln2_b", g_ln2_b), ("pool_scale", g_scale), ("attn_sinks", g_sinks)):
        update(nm, g)

    update_reduced("w_in", rs_in.finish(order)[0], transposed=True)

    names = list(weights)
    return (loss, grad_x, *[grads[n] for n in names], *[deltas[n] for n in names],
            *[new_ms[n] for n in names], *[new_vs[n] for n in names])
```

```python
import jax
import jax.numpy as jnp
from jax import lax
from jax.experimental import pallas as pl
from jax.experimental.pallas import tpu as pltpu

F32 = jnp.float32
BF16 = jnp.bfloat16
MESH = pl.DeviceIdType.MESH

N_META = 16
HEAD_DIM = 64
Q_PER_KV = 8
WINDOW = 128
BLOCK = 128
ATTN_SCALE = HEAD_DIM ** -0.5
ROPE_DIM = HEAD_DIM // 4
ROPE_THETA = 500000.0
NEG_INF = -1e30
POOL_WINDOWS = (2, 4, 8, 16)
LN_EPS = 1e-5
DN_ALPHA = 2.0 ** 0.25
ADAM_LR = 0.001
ADAM_B1 = 0.9
ADAM_B2 = 0.999
ADAM_EPS = 1e-08
ADAM_WD = 0.01
ADAM_STEP = 10

LANES = 128
META_ROW0 = BLOCK - N_META
VMEM_LIMIT = 56 * 1024 * 1024

ANY = pl.BlockSpec(memory_space=pl.ANY)
HBM = pl.BlockSpec(memory_space=pltpu.HBM)
SEM = pl.BlockSpec(memory_space=pltpu.SEMAPHORE)
EFFECT = pltpu.SideEffectType.DATAFLOW_SIDE_EFFECTING


def _params(sem=None, **kw):
    return pltpu.CompilerParams(dimension_semantics=sem, vmem_limit_bytes=VMEM_LIMIT, **kw)


class _Order:
    def __init__(self):
        self.last = None


def _call(order, body, operands, *, name, in_specs, out_specs, out_shape, grid=(), scratch=(), sem=None,
          aliases=None, prefetch=()):
    n_in, npf = len(operands), len(prefetch)
    tok = order.last
    if tok is not None and any(tok is op for op in operands):
        tok = None

    def wrapped(*refs):
        refs = list(refs)
        if tok is not None:
            del refs[npf + n_in]
        body(*refs)

    specs = list(in_specs) + ([ANY] if tok is not None else [])
    ops = list(operands) + ([tok] if tok is not None else [])
    if npf:
        out = pl.pallas_call(
            wrapped, name=name, out_shape=out_shape, compiler_params=_params(sem),
            grid_spec=pltpu.PrefetchScalarGridSpec(num_scalar_prefetch=npf, grid=grid, in_specs=specs,
                                                   out_specs=out_specs, scratch_shapes=list(scratch)),
        )(*prefetch, *ops)
    else:
        out = pl.pallas_call(
            wrapped, name=name, grid=grid, in_specs=specs, out_specs=out_specs, out_shape=out_shape,
            scratch_shapes=list(scratch), input_output_aliases=aliases or {}, compiler_params=_params(sem),
        )(*ops)
    order.last = out[0] if isinstance(out, (list, tuple)) else out
    return out


def _pick(dim, pref, mult=LANES):
    best = None
    t = mult
    while t <= min(dim, pref):
        if dim % t == 0:
            best = t
        t += mult
    return dim if best is None else best


_DIMS = {"nn": (((1,), (0,)), ((), ())), "nt": (((1,), (1,)), ((), ())), "tn": (((0,), (0,)), ((), ()))}


def _mm(order, a, b, *, kind, out_dtype, tm, tn, tk=None, name, a_lead=None):
    a2 = a.shape[-2:]
    halves = a_lead == "halves"
    if halves:
        a2 = (a2[0], 2 * a2[1])
    if kind == "tn":
        K, M = a2
    else:
        M, K = a2
    N = b.shape[0] if kind == "nt" else b.shape[1]
    half_cols = a2[1] // 2
    tm = _pick(half_cols if halves and kind == "tn" else M, tm)
    tn = _pick(N, tn)
    tk = K if tk is None else _pick(half_cols if halves and kind != "tn" else K, tk)
    nm, nn_, nk = M // tm, N // tn, K // tk
    a_bytes = M * K * a.dtype.itemsize
    b_bytes = N * K * b.dtype.itemsize
    i_outer = (a_bytes + nm * b_bytes <= b_bytes + nn_ * a_bytes) if nk == 1 else True

    def ij(g0, g1):
        return (g0, g1) if i_outer else (g1, g0)

    def a_map(g0, g1, k):
        i, _ = ij(g0, g1)
        if halves:
            per = half_cols // (tm if kind == "tn" else tk)
            return (i // per, k, i % per) if kind == "tn" else (k // per, i, k % per)
        idx = (k, i) if kind == "tn" else (i, k)
        return idx if a_lead is None else (a_lead,) + idx

    def b_map(g0, g1, k):
        _, j = ij(g0, g1)
        return (j, k) if kind == "nt" else (k, j)

    def o_map(g0, g1, k):
        return ij(g0, g1)

    a_blk = (tk, tm) if kind == "tn" else (tm, tk)
    if a_lead is not None:
        a_blk = (None,) + a_blk
    b_blk = (tn, tk) if kind == "nt" else (tk, tn)

    in_place = out_dtype == F32

    def body(a_ref, b_ref, o_ref, *acc):
        p = lax.dot_general(a_ref[...], b_ref[...], _DIMS[kind], preferred_element_type=F32)
        if nk == 1:
            o_ref[...] = p.astype(o_ref.dtype)
        else:
            k = pl.program_id(2)
            acc_ref = o_ref if in_place else acc[0]

            @pl.when(k == 0)
            def _():
                acc_ref[...] = p

            @pl.when(k > 0)
            def _():
                acc_ref[...] += p

            if not in_place:
                @pl.when(k == nk - 1)
                def _():
                    o_ref[...] = acc_ref[...].astype(o_ref.dtype)

    grid = (nm, nn_, nk) if i_outer else (nn_, nm, nk)
    return _call(
        order, body, [a, b], name=name, grid=grid,
        in_specs=[pl.BlockSpec(a_blk, a_map), pl.BlockSpec(b_blk, b_map)],
        out_specs=pl.BlockSpec((tm, tn), o_map),
        out_shape=jax.ShapeDtypeStruct((M, N), out_dtype),
        scratch=[] if nk == 1 or in_place else [pltpu.VMEM((tm, tn), F32)],
        sem=("parallel", "parallel", "arbitrary"))


def _x_half(far):
    x = lax.axis_index("x")
    return 1 - x if far else x


def _mm_nt_half(order, a, bT, *, far, tm, tn, name, into=None):
    M, K = a.shape
    N = bT.shape[0]
    tm, tn = _pick(M, tm), _pick(N // 2, tn)
    nh = N // 2 // tn

    def body(a_ref, b_ref, *rest):
        rest[-1][...] = lax.dot_general(a_ref[...], b_ref[...], _DIMS["nt"], preferred_element_type=F32)

    return _call(
        order, body, [a, bT] + ([] if into is None else [into]), name=name, grid=(M // tm, nh),
        in_specs=[pl.BlockSpec((tm, K), lambda i, j: (i, 0)),
                  pl.BlockSpec((tn, K), lambda i, j: (_x_half(far) * nh + j, 0))] + ([] if into is None else [ANY]),
        out_specs=pl.BlockSpec((tm, tn), lambda i, j: (i, _x_half(far) * nh + j)),
        out_shape=jax.ShapeDtypeStruct((M, N), F32), aliases=None if into is None else {2: 0},
        sem=("parallel", "parallel"))


def _ln_stats(z):
    mu = jnp.mean(z, axis=-1, keepdims=True)
    zc = z - mu
    var = jnp.mean(zc * zc, axis=-1, keepdims=True)
    rstd = lax.rsqrt(var + LN_EPS)
    return zc * rstd, rstd


def _ln_bwd(dy, xhat, rstd, g):
    dxh = dy * g
    m1 = jnp.mean(dxh, axis=-1, keepdims=True)
    m2 = jnp.mean(dxh * xhat, axis=-1, keepdims=True)
    return rstd * (dxh - m1 - xhat * m2)


def _ln_in_fwd(order, x, meta_pad, g, b):
    S, D = x.shape
    nb = S // BLOCK

    def body(x_ref, mp_ref, g_ref, b_ref, h_ref, hb_ref):
        is_meta = pl.program_id(0) == nb
        xin = jnp.where(is_meta, mp_ref[...], x_ref[...])
        xhat, _ = _ln_stats(xin)
        y = xhat * g_ref[...] + b_ref[...]
        h_ref[...] = y
        hb_ref[...] = y.astype(BF16)

    row = pl.BlockSpec((BLOCK, D), lambda i: (i, 0))
    vec = pl.BlockSpec((1, D), lambda i: (0, 0))
    return _call(
        order, body, [x, meta_pad, g, b], name="ln_in_fwd", grid=(nb + 1,),
        in_specs=[pl.BlockSpec((BLOCK, D), lambda i: (jnp.minimum(i, nb - 1), 0)),
                  pl.BlockSpec((BLOCK, D), lambda i: (0, 0)), vec, vec],
        out_specs=[row, row],
        out_shape=[jax.ShapeDtypeStruct((S + BLOCK, D), F32), jax.ShapeDtypeStruct((S + BLOCK, D), BF16)],
        sem=("parallel",))


def _ln_in_bwd(order, x, meta_pad, g, dh0, dz1):
    S, D = x.shape
    nb = S // BLOCK

    def body(x_ref, mp_ref, g_ref, dh_ref, dz_ref, dx_ref, dm_ref, dg_ref, db_ref):
        i = pl.program_id(0)
        is_meta = i == nb
        xin = jnp.where(is_meta, mp_ref[...], x_ref[...])
        xhat, rstd = _ln_stats(xin)
        dy = dh_ref[...] + jnp.where(is_meta, 0.0, DN_ALPHA) * dz_ref[...]
        dxin = _ln_bwd(dy, xhat, rstd, g_ref[...])

        @pl.when(i < nb)
        def _():
            dx_ref[...] = dxin

        @pl.when(is_meta)
        def _():
            dm_ref[...] = dxin

        @pl.when(i == 0)
        def _():
            dg_ref[...] = jnp.zeros_like(dg_ref)
            db_ref[...] = jnp.zeros_like(db_ref)

        dg_ref[...] += jnp.sum(dy * xhat, axis=0, keepdims=True)
        db_ref[...] += jnp.sum(dy, axis=0, keepdims=True)

    row = pl.BlockSpec((BLOCK, D), lambda i: (i, 0))
    rowx = pl.BlockSpec((BLOCK, D), lambda i: (jnp.minimum(i, nb - 1), 0))
    vec = pl.BlockSpec((1, D), lambda i: (0, 0))
    return _call(
        order, body, [x, meta_pad, g, dh0, dz1], name="ln_in_bwd", grid=(nb + 1,),
        in_specs=[rowx, pl.BlockSpec((BLOCK, D), lambda i: (0, 0)), vec, row, rowx],
        out_specs=[rowx, pl.BlockSpec((BLOCK, D), lambda i: (0, 0)), vec, vec],
        out_shape=[jax.ShapeDtypeStruct((S, D), F32), jax.ShapeDtypeStruct((BLOCK, D), F32),
                   jax.ShapeDtypeStruct((1, D), F32), jax.ShapeDtypeStruct((1, D), F32)],
        sem=("arbitrary",))


def _ln1_fwd(order, h0, y1, g, b):
    S, D = y1.shape
    tm = _pick(S, 2 * BLOCK, 8)

    def body(h_ref, y_ref, g_ref, b_ref, o_ref, ob_ref):
        xhat, _ = _ln_stats(DN_ALPHA * h_ref[...] + y_ref[...])
        y = xhat * g_ref[...] + b_ref[...]
        o_ref[...] = y
        ob_ref[...] = y.astype(BF16)

    row = pl.BlockSpec((tm, D), lambda i: (i, 0))
    vec = pl.BlockSpec((1, D), lambda i: (0, 0))
    return _call(
        order, body, [h0, y1, g, b], name="ln1_fwd", grid=(S // tm,), in_specs=[row, row, vec, vec],
        out_specs=[row, row],
        out_shape=[jax.ShapeDtypeStruct((S, D), F32), jax.ShapeDtypeStruct((S, D), BF16)],
        sem=("parallel",))


def _ln1_bwd(order, h0, y1, g, dh1, dz2):
    S, D = y1.shape
    tm = _pick(S, BLOCK, 8)

    def body(h_ref, y_ref, g_ref, dh_ref, dz2_ref, dz_ref, dzb_ref, dg_ref, db_ref):
        i = pl.program_id(0)
        xhat, rstd = _ln_stats(DN_ALPHA * h_ref[...] + y_ref[...])
        dy = dh_ref[...] + DN_ALPHA * dz2_ref[...]
        dz = _ln_bwd(dy, xhat, rstd, g_ref[...])
        dz_ref[...] = dz
        dzb_ref[...] = dz.astype(BF16)

        @pl.when(i == 0)
        def _():
            dg_ref[...] = jnp.zeros_like(dg_ref)
            db_ref[...] = jnp.zeros_like(db_ref)

        dg_ref[...] += jnp.sum(dy * xhat, axis=0, keepdims=True)
        db_ref[...] += jnp.sum(dy, axis=0, keepdims=True)

    row = pl.BlockSpec((tm, D), lambda i: (i, 0))
    vec = pl.BlockSpec((1, D), lambda i: (0, 0))
    return _call(
        order, body, [h0, y1, g, dh1, dz2], name="ln1_bwd", grid=(S // tm,),
        in_specs=[row, row, vec, row, row], out_specs=[row, row, vec, vec],
        out_shape=[jax.ShapeDtypeStruct((S, D), F32), jax.ShapeDtypeStruct((S, D), BF16),
                   jax.ShapeDtypeStruct((1, D), F32), jax.ShapeDtypeStruct((1, D), F32)],
        sem=("arbitrary",))


def _ln2_loss_bwd(order, h1, y2, target, g, b):
    S, D = y2.shape
    tm = _pick(S, 2 * BLOCK, 8)

    def body(h_ref, y_ref, t_ref, g_ref, b_ref, dz_ref, dzb_ref, dg_ref, db_ref, loss_ref):
        i = pl.program_id(0)
        xhat, rstd = _ln_stats(DN_ALPHA * h_ref[...] + y_ref[...])
        diff = xhat * g_ref[...] + b_ref[...] - t_ref[...]
        dy = diff / D
        dz = _ln_bwd(dy, xhat, rstd, g_ref[...])
        dz_ref[...] = dz
        dzb_ref[...] = dz.astype(BF16)

        @pl.when(i == 0)
        def _():
            dg_ref[...] = jnp.zeros_like(dg_ref)
            db_ref[...] = jnp.zeros_like(db_ref)
            loss_ref[...] = jnp.zeros_like(loss_ref)

        dg_ref[...] += jnp.sum(dy * xhat, axis=0, keepdims=True)
        db_ref[...] += jnp.sum(dy, axis=0, keepdims=True)
        loss_ref[...] += jnp.sum(jnp.mean(diff * diff, axis=-1, keepdims=True), axis=0, keepdims=True)

    row = pl.BlockSpec((tm, D), lambda i: (i, 0))
    vec = pl.BlockSpec((1, D), lambda i: (0, 0))
    one = pl.BlockSpec((1, 1), lambda i: (0, 0))
    return _call(
        order, body, [h1, y2, target, g, b], name="ln2_loss_bwd", grid=(S // tm,),
        in_specs=[row, row, row, vec, vec], out_specs=[row, row, vec, vec, one],
        out_shape=[jax.ShapeDtypeStruct((S, D), F32), jax.ShapeDtypeStruct((S, D), BF16),
                   jax.ShapeDtypeStruct((1, D), F32), jax.ShapeDtypeStruct((1, D), F32),
                   jax.ShapeDtypeStruct((1, 1), F32)],
        sem=("arbitrary",))


def _rope_table(S):
    r = jnp.arange(S + BLOCK)
    pos = jnp.where(r < S, r + N_META, jnp.maximum(r - (S + META_ROW0), 0))
    half = ROPE_DIM // 2
    lane = jnp.arange(LANES) % HEAD_DIM
    inv_freq = ROPE_THETA ** (-(lane % half).astype(F32) * 2.0 / ROPE_DIM)
    ang = pos.astype(F32)[:, None] * inv_freq[None, :]
    cos, sin = jnp.cos(ang), jnp.sin(ang)
    c = jnp.where(lane < ROPE_DIM, cos, 1.0)
    sa = jnp.where(lane < half, -sin, 0.0)
    sb = jnp.where((lane >= half) & (lane < ROPE_DIM), sin, 0.0)
    return jnp.concatenate([c, sa, sb], axis=1).astype(F32)


def _rope(x, tab):
    h = ROPE_DIM // 2
    return (x * tab[:, :LANES] + pltpu.roll(x, LANES - h, 1) * tab[:, LANES:2 * LANES]
            + pltpu.roll(x, h, 1) * tab[:, 2 * LANES:])


def _rope_t(dy, tab):
    h = ROPE_DIM // 2
    return (dy * tab[:, :LANES] + pltpu.roll(dy * tab[:, LANES:2 * LANES], h, 1)
            + pltpu.roll(dy * tab[:, 2 * LANES:], LANES - h, 1))


NKEY = N_META + 2 * BLOCK


def _attn_tiles(g, n, S, sink_ref, q_ref, k_ref, v_ref, tab_ref):
    NQG = Q_PER_KV // 2
    R = NQG * BLOCK
    halfsel = (g % 2).astype(F32)
    prev = jnp.maximum(n - 1, 0)
    qrow = pl.ds(pl.multiple_of(n * BLOCK, BLOCK), BLOCK)
    prow = pl.ds(pl.multiple_of(prev * BLOCK, BLOCK), BLOCK)
    mrow = pl.ds(S + META_ROW0, N_META)

    tq = tab_ref[qrow, :]
    qf = q_ref[...]
    q4 = jnp.concatenate([_rope(qf[:, LANES * p:LANES * (p + 1)], tq) for p in range(NQG)], axis=0).astype(BF16)

    tk = jnp.concatenate([tab_ref[mrow, :], tab_ref[prow, :], tq], axis=0)
    kr = _rope(jnp.concatenate([k_ref[mrow, :], k_ref[prow, :], k_ref[qrow, :]], axis=0), tk)
    vr = jnp.concatenate([v_ref[mrow, :], v_ref[prow, :], v_ref[qrow, :]], axis=0)

    lane = lax.broadcasted_iota(jnp.int32, kr.shape, 1)
    own = jnp.where(lane < HEAD_DIM, 1.0 - halfsel, halfsel)

    def lo_hi(t):
        mine = t * own
        other = pltpu.roll(mine, HEAD_DIM, 1)
        lo = mine * (1.0 - halfsel) + other * halfsel
        hi = other * (1.0 - halfsel) + mine * halfsel
        return lo.astype(BF16), hi.astype(BF16)

    klo, khi = lo_hi(kr)
    vlo, vhi = lo_hi(vr)

    jj = lax.broadcasted_iota(jnp.int32, (BLOCK, R), 0)
    qi = lax.broadcasted_iota(jnp.int32, (BLOCK, R), 1) & (BLOCK - 1)
    in_cur = jj <= qi
    band_ok = in_cur | (jj > qi + jnp.where(n >= 1, 0, 2 * BLOCK))

    def soft(kk, parity):
        sk = jnp.concatenate(
            [jnp.full((1, BLOCK), sink_ref[0, Q_PER_KV * g + 2 * p + parity], F32) for p in range(NQG)], axis=1)
        s = lax.dot_general(kk, q4, _DIMS["nt"], preferred_element_type=F32) * ATTN_SCALE
        band = jnp.where(in_cur, s[N_META + BLOCK:], s[N_META:N_META + BLOCK])
        s = jnp.concatenate([s[:N_META], jnp.where(band_ok, band, NEG_INF)], axis=0)
        m = jnp.maximum(jnp.max(s, axis=0, keepdims=True), sk)
        p = jnp.exp(s - m)
        es = jnp.exp(sk - m)
        inv = 1.0 / (jnp.sum(p, axis=0, keepdims=True) + es)
        return p * inv, es * inv

    pe, sink_e = soft(klo, 0)
    po, sink_o = soft(khi, 1)
    return q4, tk, (klo, khi), (vlo, vhi), (pe, po), (sink_e, sink_o), own, in_cur


def _spread(t, in_cur):
    band = t[N_META:]
    return jnp.concatenate([t[:N_META], jnp.where(in_cur, 0.0, band), jnp.where(in_cur, band, 0.0)], axis=0)


def _attn_specs(S, ATTN, KVW):
    Tp = S + BLOCK
    koff, voff = ATTN // LANES, (ATTN + KVW) // LANES
    gw = Q_PER_KV * HEAD_DIM
    return [pl.BlockSpec(memory_space=pltpu.SMEM),
            pl.BlockSpec((BLOCK, gw), lambda g, n: (n, g)),
            pl.BlockSpec((Tp, LANES), lambda g, n: (0, koff + g // 2)),
            pl.BlockSpec((Tp, LANES), lambda g, n: (0, voff + g // 2)),
            pl.BlockSpec((Tp, 3 * LANES), lambda g, n: (0, 0))]


def _attn_fwd(order, proj, tab, sinks, S, ATTN, KVW):
    G = KVW // HEAD_DIM
    nb = S // BLOCK
    gw = Q_PER_KV * HEAD_DIM

    def body(sink_ref, q_ref, k_ref, v_ref, tab_ref, o_ref):
        g, n = pl.program_id(0), pl.program_id(1)
        _, _, _, (vlo, vhi), (pe, po), _, _, in_cur = _attn_tiles(g, n, S, sink_ref, q_ref, k_ref, v_ref, tab_ref)
        o4 = (lax.dot_general(_spread(pe, in_cur).astype(BF16), vlo, _DIMS["tn"], preferred_element_type=F32)
              + lax.dot_general(_spread(po, in_cur).astype(BF16), vhi, _DIMS["tn"], preferred_element_type=F32))
        o_ref[...] = jnp.concatenate(
            [o4[BLOCK * p:BLOCK * (p + 1)] for p in range(Q_PER_KV // 2)], axis=1).astype(BF16)

    return _call(
        order, body, [sinks, proj, proj, proj, tab], name="attn_fwd", grid=(G, nb),
        in_specs=_attn_specs(S, ATTN, KVW),
        out_specs=pl.BlockSpec((BLOCK, gw), lambda g, n: (n, g)),
        out_shape=jax.ShapeDtypeStruct((S, ATTN), BF16),
        sem=("parallel", "arbitrary"))


def _attn_bwd(order, proj, tab, sinks, da, dproj, S, ATTN, KVW):
    G = KVW // HEAD_DIM
    nb = S // BLOCK
    Tp = S + BLOCK
    NQG = Q_PER_KV // 2
    gw = Q_PER_KV * HEAD_DIM

    def body(sink_ref, q_ref, k_ref, v_ref, tab_ref, da_ref, dproj_in, dq_ref, dk_ref, dv_ref, ds_ref):
        del dproj_in
        g, n = pl.program_id(0), pl.program_id(1)
        q4, tk, (klo, khi), (vlo, vhi), (pe, po), (sink_e, sink_o), own, in_cur = _attn_tiles(
            g, n, S, sink_ref, q_ref, k_ref, v_ref, tab_ref)
        dof = da_ref[...]
        do4 = jnp.concatenate([dof[:, LANES * p:LANES * (p + 1)] for p in range(NQG)], axis=0)

        def grads(p, vv):
            dp = lax.dot_general(vv, do4, _DIMS["nt"], preferred_element_type=F32)
            dp = jnp.concatenate(
                [dp[:N_META], jnp.where(in_cur, dp[N_META + BLOCK:], dp[N_META:N_META + BLOCK])], axis=0)
            delta = jnp.sum(p * dp, axis=0, keepdims=True)
            return _spread(p * (dp - delta) * ATTN_SCALE, in_cur).astype(BF16), delta

        dse, delta_e = grads(pe, vlo)
        dso, delta_o = grads(po, vhi)

        dq4 = (lax.dot_general(dse, klo, _DIMS["tn"], preferred_element_type=F32)
               + lax.dot_general(dso, khi, _DIMS["tn"], preferred_element_type=F32))
        tq = tk[N_META + BLOCK:]
        dq_ref[...] = jnp.concatenate(
            [_rope_t(dq4[BLOCK * p:BLOCK * (p + 1)], tq) for p in range(NQG)], axis=1).astype(BF16)

        lane = lax.broadcasted_iota(jnp.int32, (NKEY, LANES), 1)

        def fold(lo_part, hi_part):
            t = jnp.where(lane < HEAD_DIM, lo_part, hi_part)
            return t + pltpu.roll(t, HEAD_DIM, 1)

        dk = _rope_t(fold(jnp.dot(dse, q4, preferred_element_type=F32),
                          jnp.dot(dso, q4, preferred_element_type=F32)), tk) * own
        dv = fold(jnp.dot(_spread(pe, in_cur).astype(BF16), do4, preferred_element_type=F32),
                  jnp.dot(_spread(po, in_cur).astype(BF16), do4, preferred_element_type=F32)) * own

        @pl.when((n == 0) & (g % 2 == 0))
        def _():
            dk_ref[...] = jnp.zeros_like(dk_ref)
            dv_ref[...] = jnp.zeros_like(dv_ref)

        @pl.when(n == 0)
        def _():
            ds_ref[...] = jnp.zeros_like(ds_ref)

        prev = jnp.maximum(n - 1, 0)
        qrow = pl.ds(pl.multiple_of(n * BLOCK, BLOCK), BLOCK)
        prow = pl.ds(pl.multiple_of(prev * BLOCK, BLOCK), BLOCK)
        mrow = pl.ds(S + META_ROW0, N_META)
        for ref, val in ((dk_ref, dk), (dv_ref, dv)):
            ref[mrow, :] += val[:N_META]
            ref[prow, :] += val[N_META:N_META + BLOCK]
            ref[qrow, :] += val[N_META + BLOCK:]

        srow = lax.broadcasted_iota(jnp.int32, (Q_PER_KV, LANES), 0)
        acc = jnp.zeros((Q_PER_KV, LANES), F32)
        for p in range(NQG):
            for parity, (sk, dl) in enumerate(((sink_e, delta_e), (sink_o, delta_o))):
                val = -jnp.sum(sk[:, BLOCK * p:BLOCK * (p + 1)] * dl[:, BLOCK * p:BLOCK * (p + 1)])
                acc = jnp.where(srow == 2 * p + parity, val, acc)
        ds_ref[0] += acc

    in_specs = _attn_specs(S, ATTN, KVW) + [pl.BlockSpec((BLOCK, gw), lambda g, n: (n, g)), ANY]
    slab = pl.BlockSpec((Tp, LANES), lambda g, n: (0, g // 2))
    return _call(
        order, body, [sinks, proj, proj, proj, tab, da, dproj], name="attn_bwd", grid=(G, nb), in_specs=in_specs,
        out_specs=[pl.BlockSpec((BLOCK, gw), lambda g, n: (n, g)), slab, slab,
                   pl.BlockSpec((1, Q_PER_KV, LANES), lambda g, n: (g, 0, 0))],
        out_shape=[jax.ShapeDtypeStruct(dproj.shape, BF16), jax.ShapeDtypeStruct((Tp, KVW), F32),
                   jax.ShapeDtypeStruct((Tp, KVW), F32), jax.ShapeDtypeStruct((G, Q_PER_KV, LANES), F32)],
        aliases={6: 0}, sem=("arbitrary", "arbitrary"))


def _zero_meta_block(order, Tp, IN):
    tc = _pick(IN, 4096)

    def body(o_ref):
        o_ref[...] = jnp.zeros_like(o_ref)

    return _call(
        order, body, [], name="dproj_zero_meta", grid=(IN // tc,), in_specs=[],
        out_specs=pl.BlockSpec((BLOCK, tc), lambda j: (Tp // BLOCK - 1, j)),
        out_shape=jax.ShapeDtypeStruct((Tp, IN), BF16), sem=("parallel",))


def _put_dkv(order, dk, dv, dproj, ATTN):
    Tp, KVW = dk.shape
    nkb = KVW // LANES
    koff = ATTN // LANES

    def body(dk_ref, dv_ref, dproj_in, o_ref):
        del dproj_in
        t = pl.program_id(0)
        o_ref[...] = jnp.where(t < nkb, dk_ref[...], dv_ref[...]).astype(BF16)

    src = pl.BlockSpec((Tp, LANES), lambda t: (0, t % nkb))
    return _call(
        order, body, [dk, dv, dproj], name="dproj_put_dkv", grid=(2 * nkb,), in_specs=[src, src, ANY],
        out_specs=pl.BlockSpec((Tp, LANES), lambda t: (0, koff + t)),
        out_shape=jax.ShapeDtypeStruct(dproj.shape, BF16), aliases={2: 0}, sem=("parallel",))


HALO = 16


def _window_sums(x, up):
    n = x.shape[0]
    out = []
    s = x
    for k in (1, 2, 4, 8):
        s = s + pltpu.roll(s, (n - k) if up else k, 0)
        out.append(s)
    return out


def _pool_specs(S, ub, gw, tm):
    meta_halo = (S + BLOCK - HALO) // HALO

    def main(g):
        return pl.BlockSpec((tm, gw), lambda i: (i, ub + g))

    def halo(g):
        return pl.BlockSpec((HALO, gw), lambda i: (jnp.where(i == 0, meta_halo, i * (tm // HALO) - 1), ub + g))

    return [main(g) for g in range(4)] + [halo(g) for g in range(4)]


def _pooled(main_refs, halo_refs, g):
    x = jnp.concatenate([halo_refs[g][...], main_refs[g][...]], axis=0)
    s = _window_sums(x, up=False)[g]
    return (s[HALO:] * (1.0 / POOL_WINDOWS[g]) - x[HALO:]).astype(BF16)


def _pool_fwd(order, proj, wgrp, scale, S, uoff, POOL):
    gw = POOL // 4
    tm = BLOCK

    def body(*refs):
        main, halo = refs[:4], refs[4:8]
        w_ref, sc_ref, o_ref = refs[8:]
        for g in range(4):
            mixed = jnp.dot(_pooled(main, halo, g), w_ref[g], preferred_element_type=F32)
            o_ref[:, gw * g:gw * (g + 1)] = (mixed * sc_ref[:, gw * g:gw * (g + 1)]).astype(BF16)

    return _call(
        order, body, [proj] * 8 + [wgrp, scale], name="pool_fwd", grid=(S // tm,),
        in_specs=_pool_specs(S, uoff // gw, gw, tm) + [
            pl.BlockSpec((4, gw, gw), lambda i: (0, 0, 0)), pl.BlockSpec((1, POOL), lambda i: (0, 0))],
        out_specs=pl.BlockSpec((tm, POOL), lambda i: (i, 0)),
        out_shape=jax.ShapeDtypeStruct((S, POOL), BF16), sem=("parallel",))


def _pool_bwd_mix(order, proj, wgrp, scale, dps, S, uoff, POOL):
    gw = POOL // 4
    tm = BLOCK

    def body(*refs):
        main, halo = refs[:4], refs[4:8]
        w_ref, sc_ref, dps_ref, dpl_ref, dw_ref, dsc_ref = refs[8:]
        i = pl.program_id(0)

        @pl.when(i == 0)
        def _():
            dw_ref[...] = jnp.zeros_like(dw_ref)
            dsc_ref[...] = jnp.zeros_like(dsc_ref)

        for g in range(4):
            cols = slice(gw * g, gw * (g + 1))
            pooled = _pooled(main, halo, g)
            mixed = jnp.dot(pooled, w_ref[g], preferred_element_type=F32)
            dps_g = dps_ref[:, cols]
            dsc_ref[:, cols] += jnp.sum(dps_g * mixed, axis=0, keepdims=True)
            dms = (dps_g * sc_ref[:, cols]).astype(BF16)
            dw_ref[g] += lax.dot_general(pooled, dms, _DIMS["tn"], preferred_element_type=F32)
            dpl_ref[:, cols] = lax.dot_general(dms, w_ref[g], _DIMS["nt"], preferred_element_type=F32)

    row = pl.BlockSpec((tm, POOL), lambda i: (i, 0))
    return _call(
        order, body, [proj] * 8 + [wgrp, scale, dps], name="pool_bwd_mix", grid=(S // tm,),
        in_specs=_pool_specs(S, uoff // gw, gw, tm) + [
            pl.BlockSpec((4, gw, gw), lambda i: (0, 0, 0)), pl.BlockSpec((1, POOL), lambda i: (0, 0)), row],
        out_specs=[row, pl.BlockSpec((4, gw, gw), lambda i: (0, 0, 0)), pl.BlockSpec((1, POOL), lambda i: (0, 0))],
        out_shape=[jax.ShapeDtypeStruct((S, POOL), F32), jax.ShapeDtypeStruct((4, gw, gw), F32),
                   jax.ShapeDtypeStruct((1, POOL), F32)],
        sem=("arbitrary",))


def _pool_bwd_window(order, dpl, dproj, S, uoff, POOL):
    gw = POOL // 4
    nb = S // BLOCK
    ub = uoff // gw

    def body(main_ref, halo_ref, dproj_in, o_ref):
        del dproj_in
        b, g = pl.program_id(0), pl.program_id(1)
        main = jnp.where(b < nb, main_ref[...], 0.0)
        halo = jnp.where(b == nb - 1, 0.0, halo_ref[...])
        sums = _window_sums(jnp.concatenate([main, halo], axis=0), up=True)
        du = jnp.zeros((BLOCK, gw), F32)
        for k, w in enumerate(POOL_WINDOWS):
            du = jnp.where(g == k, sums[k][:BLOCK] * (1.0 / w), du)
        du = du - main
        row = lax.broadcasted_iota(jnp.int32, du.shape, 0)
        first_valid = jnp.where(b == nb, META_ROW0, 0)
        o_ref[...] = jnp.where(row >= first_valid, du, 0.0).astype(BF16)

    return _call(
        order, body, [dpl, dpl, dproj], name="pool_bwd_window", grid=(nb + 1, 4),
        in_specs=[pl.BlockSpec((BLOCK, gw), lambda b, g: (jnp.minimum(b, nb - 1), g)),
                  pl.BlockSpec((HALO, gw), lambda b, g: (
                      jnp.where(b == nb, 0, jnp.minimum((b + 1) * (BLOCK // HALO), S // HALO - 1)), g)),
                  ANY],
        out_specs=pl.BlockSpec((BLOCK, gw), lambda b, g: (b, ub + g)),
        out_shape=jax.ShapeDtypeStruct(dproj.shape, BF16), aliases={2: 0}, sem=("parallel", "parallel"))


def _sigmoid(x):
    return 1.0 / (1.0 + jnp.exp(-x))


def _gate_tiles(S, D, goff):
    tc = 512
    while goff % tc or D % tc:
        tc //= 2
    return _pick(S, 1024, 8), tc


def _gate_mix(order, proj, bgate, a_out, p_out, S, D, goff):
    tm, tc = _gate_tiles(S, D, goff)
    g0b, nd = goff // tc, D // tc

    def body(l0_ref, l1_ref, b_ref, a_ref, p_ref, o_ref):
        g0 = _sigmoid(l0_ref[...] + b_ref[0:1, :])
        g1 = _sigmoid(l1_ref[...] + b_ref[1:2, :])
        o_ref[...] = (g0 * a_ref[...] + g1 * p_ref[...]).astype(BF16)

    tile = pl.BlockSpec((tm, tc), lambda i, j: (i, j))
    return _call(
        order, body, [proj, proj, bgate, a_out, p_out], name="gate_mix", grid=(S // tm, nd),
        in_specs=[pl.BlockSpec((tm, tc), lambda i, j: (i, g0b + j)),
                  pl.BlockSpec((tm, tc), lambda i, j: (i, g0b + nd + j)),
                  pl.BlockSpec((2, tc), lambda i, j: (0, j)), tile, tile],
        out_specs=tile, out_shape=jax.ShapeDtypeStruct((S, D), BF16), sem=("parallel", "parallel"))


def _gate_bwd(order, proj, bgate, a_out, p_out, dmixed, dproj, S, D, goff):
    tm, tc = _gate_tiles(S, D, goff)
    g0b, nd, ni = goff // tc, D // tc, S // tm
    nsteps = nd * ni

    def body(l0_ref, l1_ref, b_ref, a_ref, p_ref, dm_ref, dproj_in, dap_ref, dproj_ref, db_ref, buf, sems):
        del dproj_in
        j, i = pl.program_id(0), pl.program_id(1)
        step = j * ni + i
        slot = step % 2

        def put(sl, br):
            col = pl.multiple_of((g0b + br * nd + j) * tc, tc)
            return pltpu.make_async_copy(
                buf.at[sl, br], dproj_ref.at[pl.ds(pl.multiple_of(i * tm, tm), tm), pl.ds(col, tc)], sems.at[sl, br])

        @pl.when(step >= 2)
        def _():
            put(slot, 0).wait()
            put(slot, 1).wait()

        @pl.when(i == 0)
        def _():
            db_ref[...] = jnp.zeros_like(db_ref)

        dm = dm_ref[...]
        for br, (l_ref, val_ref) in enumerate(((l0_ref, a_ref), (l1_ref, p_ref))):
            gate = _sigmoid(l_ref[...] + b_ref[br:br + 1, :])
            dap_ref[br] = (dm * gate).astype(BF16)
            dl = dm * val_ref[...] * gate * (1.0 - gate)
            buf[slot, br] = dl.astype(BF16)
            db_ref[br] += jnp.sum(dl, axis=0, keepdims=True)
            put(slot, br).start()

        @pl.when(step == nsteps - 1)
        def _():
            for sl in ((slot, 1 - slot) if nsteps > 1 else (slot,)):
                put(sl, 0).wait()
                put(sl, 1).wait()

    tile = pl.BlockSpec((tm, tc), lambda j, i: (i, j))
    return _call(
        order, body, [proj, proj, bgate, a_out, p_out, dmixed, dproj], name="gate_bwd", grid=(nd, ni),
        in_specs=[pl.BlockSpec((tm, tc), lambda j, i: (i, g0b + j)),
                  pl.BlockSpec((tm, tc), lambda j, i: (i, g0b + nd + j)),
                  pl.BlockSpec((2, tc), lambda j, i: (0, j)), tile, tile, tile, ANY],
        out_specs=[pl.BlockSpec((2, tm, tc), lambda j, i: (0, i, j)), ANY,
                   pl.BlockSpec((2, 1, tc), lambda j, i: (0, 0, j))],
        out_shape=[jax.ShapeDtypeStruct((2, S, D), BF16), jax.ShapeDtypeStruct(dproj.shape, BF16),
                   jax.ShapeDtypeStruct((2, 1, D), F32)],
        scratch=[pltpu.VMEM((2, 2, tm, tc), BF16), pltpu.SemaphoreType.DMA((2, 2))],
        aliases={6: 1}, sem=("arbitrary", "arbitrary"))


def _ffn_in_near(order, h, wT, FF):
    S, D = h.shape
    tm, tn = _pick(S, 2048), _pick(FF, 512)
    nj = FF // tn

    def body(h_ref, w_ref, f_ref):
        f_ref[...] = lax.dot_general(h_ref[...], w_ref[...], _DIMS["nt"], preferred_element_type=F32)

    return _call(
        order, body, [h, wT], name="ffn_in_near", grid=(S // tm, nj),
        in_specs=[pl.BlockSpec((tm, D), lambda i, j: (i, 0)),
                  pl.BlockSpec((tn, D), lambda i, j: (_x_half(False) * nj + j, 0))],
        out_specs=pl.BlockSpec((tm, tn), lambda i, j: (i, j)),
        out_shape=jax.ShapeDtypeStruct((S, FF), F32), sem=("parallel", "parallel"))


def _gate_up(near, far):
    near_is_gate = lax.axis_index("x") == 0
    return jnp.where(near_is_gate, near, far), jnp.where(near_is_gate, far, near)


def _ffn_in_far(order, h, wT, near):
    S, D = h.shape
    FF = near.shape[1]
    tm, tn = _pick(S, 1024), _pick(FF, 512)
    nj = FF // tn

    def body(h_ref, w_ref, near_ref, f_ref, act_ref):
        far = lax.dot_general(h_ref[...], w_ref[...], _DIMS["nt"], preferred_element_type=F32)
        f_ref[...] = far
        gt, up = _gate_up(near_ref[...], far)
        act_ref[...] = (gt * _sigmoid(gt) * up).astype(BF16)

    tile = pl.BlockSpec((tm, tn), lambda i, j: (i, j))
    return _call(
        order, body, [h, wT, near], name="ffn_in_far", grid=(S // tm, nj),
        in_specs=[pl.BlockSpec((tm, D), lambda i, j: (i, 0)),
                  pl.BlockSpec((tn, D), lambda i, j: (_x_half(True) * nj + j, 0)), tile],
        out_specs=[tile, tile],
        out_shape=[jax.ShapeDtypeStruct((S, FF), F32), jax.ShapeDtypeStruct((S, FF), BF16)],
        sem=("parallel", "parallel"))


def _d_act_swiglu(order, dy, wdown, near, far):
    S, D = dy.shape
    FF = wdown.shape[0]
    tm, tn = _pick(S, 1024), _pick(FF, 256)

    def body(dy_ref, w_ref, near_ref, far_ref, o_ref):
        d = lax.dot_general(dy_ref[...], w_ref[...], _DIMS["nt"], preferred_element_type=F32)
        gt, up = _gate_up(near_ref[...], far_ref[...])
        s = _sigmoid(gt)
        o_ref[0] = (d * up * s * (1.0 + gt * (1.0 - s))).astype(BF16)
        o_ref[1] = (d * gt * s).astype(BF16)

    tile = pl.BlockSpec((tm, tn), lambda i, j: (i, j))
    return _call(
        order, body, [dy, wdown, near, far], name="d_act", grid=(S // tm, FF // tn),
        in_specs=[pl.BlockSpec((tm, D), lambda i, j: (i, 0)), pl.BlockSpec((tn, D), lambda i, j: (j, 0)), tile, tile],
        out_specs=pl.BlockSpec((2, tm, tn), lambda i, j: (0, i, j)),
        out_shape=jax.ShapeDtypeStruct((2, S, FF), BF16), sem=("parallel", "parallel"))


def _place():
    return lax.axis_index("x"), lax.axis_index("y"), lax.axis_index("c")


def _xfer_start(order, name, bufs, copies):
    nb = len(bufs)
    n = len(copies([None] * nb, None))
    is_new = [isinstance(b, jax.ShapeDtypeStruct) for b in bufs]
    old = [b for b, fresh in zip(bufs, is_new) if not fresh]
    no = len(old)
    tok = [] if any(order.last is b for b in old) else [order.last]
    first_out = no + len(tok)

    def body(*refs):
        send, recv = refs[first_out:first_out + n], refs[first_out + n:first_out + 2 * n]
        token = refs[-1]
        given, made = iter(refs[:no]), iter(refs[first_out + 2 * n + no:-1])
        logical = [next(made) if fresh else next(given) for fresh in is_new]
        for i, (src, dst, dev) in enumerate(copies(logical, _place())):
            pltpu.make_async_remote_copy(src_ref=src, dst_ref=dst, send_sem=send[i], recv_sem=recv[i],
                                         device_id=dev, device_id_type=MESH).start()
        token[...] = jnp.zeros_like(token)

    fresh_shapes = [b for b, fresh in zip(bufs, is_new) if fresh]
    out = pl.pallas_call(
        body, name=name,
        out_shape=tuple([pltpu.SemaphoreType.DMA(())] * (2 * n)
                        + [pltpu.HBM(b.shape, b.dtype) for b in old + fresh_shapes]
                        + [jax.ShapeDtypeStruct((8, LANES), F32)]),
        in_specs=[HBM] * no + [ANY] * len(tok),
        out_specs=tuple([SEM] * (2 * n) + [HBM] * nb + [pl.BlockSpec(memory_space=pltpu.VMEM)]),
        input_output_aliases={i: 2 * n + i for i in range(no)},
        compiler_params=pltpu.CompilerParams(has_side_effects=EFFECT),
    )(*[pltpu.with_memory_space_constraint(b, pltpu.HBM) for b in old], *tok)
    order.last = out[-1]
    thru, made = iter(out[2 * n:2 * n + no]), iter(out[2 * n + no:2 * n + nb])
    return list(out[:2 * n]), [next(made) if fresh else next(thru) for fresh in is_new]


def _xfer_wait(order, name, sems, bufs, copies):
    nb = len(bufs)
    n = len(sems) // 2
    tok = order.last

    def body(*refs):
        send, recv = refs[nb:nb + n], refs[nb + n:nb + 2 * n]
        token = refs[-1]
        for i, (src, dst, dev) in enumerate(copies(refs[:nb], _place())):
            cp = pltpu.make_async_remote_copy(src_ref=src, dst_ref=dst, send_sem=send[i], recv_sem=recv[i],
                                              device_id=dev, device_id_type=MESH)
            cp.wait_send()
            cp.wait_recv()
        token[...] = jnp.zeros_like(token)

    out = pl.pallas_call(
        body, name=name,
        out_shape=tuple([pltpu.HBM(b.shape, b.dtype) for b in bufs] + [jax.ShapeDtypeStruct((8, LANES), F32)]),
        in_specs=[HBM] * nb + [SEM] * (2 * n) + [ANY],
        out_specs=tuple([HBM] * nb + [pl.BlockSpec(memory_space=pltpu.VMEM)]),
        input_output_aliases={i: i for i in range(nb)},
        compiler_params=pltpu.CompilerParams(has_side_effects=EFFECT),
    )(*bufs, *sems, tok)
    order.last = out[-1]
    return list(out[:nb])


class _Xfer:
    def __init__(self, name, bufs, copies):
        self.name, self.bufs, self.copies = name, list(bufs), copies
        self.sems = None

    def start(self, order):
        self.sems, self.bufs = _xfer_start(order, self.name + "_start", self.bufs, self.copies)

    def wait(self, order, bufs=None):
        self.bufs = _xfer_wait(order, self.name + "_wait", self.sems, bufs or self.bufs, self.copies)
        return self.bufs


def _block_rows(ref, r, d):
    return ref.at[pl.ds(d * r, r)]


NEAR = ("xn", "yn")
ALL_CHIPS = ("xn", "yn", "diag")


def _chip_of(which, x, y):
    return {"xn": (1 - x, y), "yn": (x, 1 - y), "diag": (1 - x, 1 - y)}[which]


def _gather_send(fulls, chips=ALL_CHIPS, sibling=True):
    def copies(refs, place):
        out = []
        for w, full in enumerate(fulls):
            r = full.shape[0] // 8
            if place is None:
                out += [None] * (len(chips) + int(sibling))
                continue
            x, y, c = place
            mine = _block_rows(refs[w], r, 4 * x + 2 * y + c)
            if sibling:
                out.append((mine, mine, (x, y, 1 - c)))
            for which in chips:
                out.append((mine, mine, (*_chip_of(which, x, y), c)))
        return out
    return copies


def _gather_forward(fulls, chips=ALL_CHIPS):
    def copies(refs, place):
        out = []
        for w, full in enumerate(fulls):
            r = full.shape[0] // 8
            if place is None:
                out += [None] * len(chips)
                continue
            x, y, c = place
            for which in chips:
                px, py = _chip_of(which, x, y)
                blk = _block_rows(refs[w], r, 4 * px + 2 * py + c)
                out.append((blk, blk, (x, y, 1 - c)))
        return out
    return copies


def _pair_send(nw):
    def copies(refs, place):
        out = []
        for w in range(nw):
            if place is None:
                out += [None] * 4
                continue
            x, y, c = place
            grad, other = refs[2 * w], refs[2 * w + 1]
            r = other.shape[1]
            for k in range(4):
                out.append((_block_rows(grad, r, 2 * k + 1 - c), other.at[k], (x, y, 1 - c)))
        return out
    return copies


def _chip_send(nw):
    def copies(refs, place):
        out = []
        for w in range(nw):
            if place is None:
                out += [None] * 3
                continue
            x, y, c = place
            psum, parts = refs[2 * w], refs[2 * w + 1]
            for px, py in ((1 - x, y), (x, 1 - y), (1 - x, 1 - y)):
                out.append((psum.at[2 * px + py], parts.at[2 * x + y], (px, py, c)))
        return out
    return copies


def _dev_index():
    x, y, c = _place()
    return 4 * x + 2 * y + c


def _place_own(order, shard, name):
    r, cols = shard.shape
    tr = _pick(r, max(16, (12 << 20) // (4 * cols)), 16)
    nr = r // tr

    def body(s_ref, o_ref):
        o_ref[...] = s_ref[...].astype(BF16)

    return _call(
        order, body, [shard], name=name, grid=(nr,),
        in_specs=[pl.BlockSpec((tr, cols), lambda i: (i, 0))],
        out_specs=pl.BlockSpec((tr, cols), lambda i: (_dev_index() * nr + i, 0)),
        out_shape=jax.ShapeDtypeStruct((8 * r, cols), BF16), sem=("parallel",))


def _pair_sum(order, grad, other, name):
    r, cols = other.shape[1:]
    tr = _pick(r, max(16, (7 << 20) // (2 * cols)), 16)
    nr = r // tr

    def body(g_ref, a_ref, o_ref):
        o_ref[...] = (g_ref[...].astype(F32) + a_ref[...].astype(F32)).astype(BF16)

    blk = pl.BlockSpec((None, tr, cols), lambda k, i: (k, i, 0))
    return _call(
        order, body, [grad, other], name=name, grid=(4, nr),
        in_specs=[pl.BlockSpec((tr, cols), lambda k, i: ((2 * k + lax.axis_index("c")) * nr + i, 0)), blk],
        out_specs=blk, out_shape=jax.ShapeDtypeStruct(other.shape, BF16), sem=("parallel", "parallel"))


def _chip_sum(order, psum, parts, name):
    _, r, cols = parts.shape
    tr = _pick(r, max(16, (1 << 20) // (2 * cols)), 16)

    def my_chip():
        return 2 * lax.axis_index("x") + lax.axis_index("y")

    def body(own_ref, p0, p1, p2, p3, o_ref):
        own = own_ref[...].astype(F32)
        acc = None
        for k, p in enumerate((p0, p1, p2, p3)):
            term = jnp.where(my_chip() == k, own, p[...].astype(F32))
            acc = term if acc is None else acc + term
        o_ref[...] = acc

    def slot(k):
        return pl.BlockSpec((None, tr, cols), lambda i: (jnp.where(my_chip() == k, (k + 1) % 4, k), i, 0))

    return _call(
        order, body, [psum, parts, parts, parts, parts], name=name, grid=(r // tr,),
        in_specs=[pl.BlockSpec((None, tr, cols), lambda i: (my_chip(), i, 0))] + [slot(k) for k in range(4)],
        out_specs=pl.BlockSpec((tr, cols), lambda i: (i, 0)),
        out_shape=jax.ShapeDtypeStruct((r, cols), F32), sem=("parallel",))


def _all_reduce_small(order, pack, name):
    R = pack.shape[0]

    def body(p_ref, o_ref, buf, send_sems, recv_sems):
        x, y, c = _place()
        me = 4 * x + 2 * y + c
        buf[me] = p_ref[...]
        copies = []
        for k in range(1, 8):
            px = 1 - x if k & 4 else x
            py = 1 - y if k & 2 else y
            pc = 1 - c if k & 1 else c
            cp = pltpu.make_async_remote_copy(
                src_ref=p_ref, dst_ref=buf.at[me], send_sem=send_sems.at[k - 1], recv_sem=recv_sems.at[k - 1],
                device_id=(px, py, pc), device_id_type=MESH)
            cp.start()
            copies.append(cp)
        for cp in copies:
            cp.wait_recv()
        acc = buf[0]
        for d in range(1, 8):
            acc = acc + buf[d]
        o_ref[...] = acc
        for cp in copies:
            cp.wait_send()

    vm = pl.BlockSpec(memory_space=pltpu.VMEM)
    return _call(
        order, body, [pack], name=name, in_specs=[vm], out_specs=vm,
        out_shape=jax.ShapeDtypeStruct((R, LANES), F32),
        scratch=[pltpu.VMEM((8, R, LANES), F32), pltpu.SemaphoreType.DMA((7,)), pltpu.SemaphoreType.DMA((7,))])


def _pack(parts):
    flat = []
    for p in parts:
        v = p.reshape(-1).astype(F32)
        flat.append(jnp.pad(v, (0, (-v.shape[0]) % LANES)))
    v = jnp.concatenate(flat)
    v = jnp.pad(v, (0, (-v.shape[0]) % (8 * LANES)))
    return v.reshape(-1, LANES)


def _unpack(pack, shapes):
    v = pack.reshape(-1)
    out, off = [], 0
    for s in shapes:
        n = 1
        for d in s:
            n *= d
        out.append(v[off:off + n].reshape(s))
        off += n + (-n) % LANES
    return out


def _adamw(order, w, g, m, v, name):
    shape = w.shape
    cols = shape[-1]
    w2, g2, m2, v2 = (t.reshape(-1, cols) for t in (w, g, m, v))
    R = w2.shape[0]
    tr = _pick(R, max(8, (1 << 20) // (4 * cols)), 8)

    def body(w_ref, g_ref, m_ref, v_ref, d_ref, mo_ref, vo_ref):
        d_ref[...], mo_ref[...], vo_ref[...] = _adam_math(w_ref[...], g_ref[...], m_ref[...], v_ref[...])

    blk = pl.BlockSpec((tr, cols), lambda i: (i, 0))
    outs = _call(
        order, body, [w2, g2, m2, v2], name=name, grid=(R // tr,), in_specs=[blk] * 4, out_specs=[blk] * 3,
        out_shape=[jax.ShapeDtypeStruct((R, cols), F32)] * 3, sem=("parallel",))
    return tuple(o.reshape(shape) for o in outs)


def _adam_math(w, g, m, v):
    mn = ADAM_B1 * m + (1.0 - ADAM_B1) * g
    vn = ADAM_B2 * v + (1.0 - ADAM_B2) * (g * g)
    m_hat = mn / (1.0 - ADAM_B1 ** ADAM_STEP)
    v_hat = vn / (1.0 - ADAM_B2 ** ADAM_STEP)
    return -ADAM_LR * (m_hat / (jnp.sqrt(v_hat) + ADAM_EPS) + ADAM_WD * w), mn, vn


def _chip_sum_adamw(order, w, psum, parts, m, v, name):
    _, r, cols = parts.shape
    tr = _pick(r, max(16, (24 << 20) // (38 * cols)), 16)

    def my_chip():
        return 2 * lax.axis_index("x") + lax.axis_index("y")

    def body(w_ref, own_ref, p0, p1, p2, p3, m_ref, v_ref, g_ref, d_ref, mo_ref, vo_ref):
        own = own_ref[...].astype(F32)
        g = None
        for k, p in enumerate((p0, p1, p2, p3)):
            term = jnp.where(my_chip() == k, own, p[...].astype(F32))
            g = term if g is None else g + term
        g_ref[...] = g
        d_ref[...], mo_ref[...], vo_ref[...] = _adam_math(w_ref[...], g, m_ref[...], v_ref[...])

    def slot(k):
        return pl.BlockSpec((None, tr, cols), lambda i: (jnp.where(my_chip() == k, (k + 1) % 4, k), i, 0))

    blk = pl.BlockSpec((tr, cols), lambda i: (i, 0))
    return _call(
        order, body, [w, psum, parts, parts, parts, parts, m, v], name=name, grid=(r // tr,),
        in_specs=[blk, pl.BlockSpec((None, tr, cols), lambda i: (my_chip(), i, 0))]
        + [slot(k) for k in range(4)] + [blk, blk],
        out_specs=[blk] * 4, out_shape=[jax.ShapeDtypeStruct((r, cols), F32)] * 4, sem=("parallel",))


class _GradReduce:
    def __init__(self, tag, grads, names):
        self.tag, self.grads, self.names = tag, list(grads), names
        self.pair = self.chip = self.psums = None

    def pair_start(self, order):
        bufs = []
        for g in self.grads:
            bufs += [g, jax.ShapeDtypeStruct((4, g.shape[0] // 8, g.shape[1]), g.dtype)]
        self.pair = _Xfer("pair_" + self.tag, bufs, _pair_send(len(self.grads)))
        self.pair.start(order)

    def pair_sum_chip_start(self, order):
        bufs = self.pair.wait(order)
        self.psums = [_pair_sum(order, bufs[2 * w], bufs[2 * w + 1], "pair_sum_" + nm)
                      for w, nm in enumerate(self.names)]
        cbufs = []
        for p in self.psums:
            cbufs += [p, jax.ShapeDtypeStruct(p.shape, p.dtype)]
        self.chip = _Xfer("chip_" + self.tag, cbufs, _chip_send(len(self.psums)))
        self.chip.start(order)

    def finish(self, order):
        bufs = self.chip.wait(order)
        return [(bufs[2 * w], bufs[2 * w + 1]) for w in range(len(self.names))]


def kernel(x, meta_tokens, ln_in_g, ln_in_b, w_in, b_gate, attn_sinks, w_attn_up, w_pool_grp, pool_scale, w_pool_up, w_out, ln1_g, ln1_b, w_ffn_in, w_ffn_down, ln2_g, ln2_b, loss_target, m_meta_tokens, m_ln_in_g, m_ln_in_b, m_w_in, m_b_gate, m_attn_sinks, m_w_attn_up, m_w_pool_grp, m_pool_scale, m_w_pool_up, m_w_out, m_ln1_g, m_ln1_b, m_w_ffn_in, m_w_ffn_down, m_ln2_g, m_ln2_b, v_meta_tokens, v_ln_in_g, v_ln_in_b, v_w_in, v_b_gate, v_attn_sinks, v_w_attn_up, v_w_pool_grp, v_pool_scale, v_w_pool_up, v_w_out, v_ln1_g, v_ln1_b, v_w_ffn_in, v_w_ffn_down, v_ln2_g, v_ln2_b):
    S, D = x.shape[1], x.shape[2]
    Tp = S + BLOCK
    NQ = attn_sinks.shape[-1]
    ATTN = NQ * HEAD_DIM
    KVW = ATTN // Q_PER_KV
    POOL = pool_scale.shape[-1]
    IN = 8 * w_in.shape[2]
    FF = 8 * w_ffn_down.shape[1]
    uoff = ATTN + 2 * KVW
    goff = uoff + POOL
    gw = POOL // 4
    dcols = D // 8
    assert IN == goff + 2 * D and w_ffn_in.shape[2] * 8 == 2 * FF

    xi, yi, ci = _place()
    dev = 4 * xi + 2 * yi + ci
    x2, tgt = x[0], loss_target[0]
    order = _Order()

    def place_cols(a):
        return lax.dynamic_update_slice(jnp.zeros(a.shape[:-1] + (D,), F32), a, (0,) * (a.ndim - 1) + (dev * dcols,))

    small = _all_reduce_small(order, _pack([place_cols(meta_tokens), place_cols(b_gate[0])]), "small_inputs_gather")
    meta_full, bgate_full = _unpack(small, [(N_META, D), (2, D)])
    meta_pad = jnp.pad(meta_full, ((META_ROW0, 0), (0, 0)))

    wgrp_rows = w_pool_grp[0].reshape(4 * (gw // 8), gw)
    full_in = _place_own(order, w_in[0].T, "own_w_in")
    g_in = _Xfer("gather_w_in_near", [full_in], _gather_send([full_in], NEAR))
    g_in.start(order)
    mix_names = ["w_attn_up", "w_pool_grp", "w_pool_up", "w_out"]
    mix_shards = [w_attn_up[0].T, wgrp_rows, w_pool_up[0].T, w_out[0]]
    full_mix = [_place_own(order, s, "own_" + nm) for s, nm in zip(mix_shards, mix_names)]
    full_ffn = _place_own(order, w_ffn_in[0].T, "own_w_ffn_in")

    ln_in_g2, ln_in_b2 = ln_in_g.reshape(1, D), ln_in_b.reshape(1, D)
    tab = _rope_table(S)

    h0, h0b = _ln_in_fwd(order, x2, meta_pad, ln_in_g2, ln_in_b2)
    bufs = g_in.wait(order)
    d_in = _Xfer("gather_w_in_diag", bufs, _gather_send(bufs, ("diag",), sibling=False))
    d_in.start(order)
    f_in = _Xfer("forward_w_in_near", d_in.bufs, _gather_forward(bufs, NEAR))
    f_in.start(order)
    full_down = _place_own(order, w_ffn_down[0], "own_w_ffn_down")
    bufs = f_in.wait(order)
    proj = _mm_nt_half(order, h0b, bufs[0], far=False, tm=704, tn=1280, name="proj_near")
    bufs = d_in.wait(order, bufs)
    fd_in = _Xfer("forward_w_in_diag", bufs, _gather_forward(bufs, ("diag",)))
    fd_in.start(order)
    ag_mix = _Xfer("gather_mixers", full_mix, _gather_send(full_mix))
    ag_mix.start(order)
    g_ffn = _Xfer("gather_w_ffn_in_near", [full_ffn], _gather_send([full_ffn], NEAR))
    g_ffn.start(order)
    (winT,) = fd_in.wait(order)
    proj = _mm_nt_half(order, h0b, winT, far=True, tm=704, tn=1280, name="proj_far", into=proj)

    att = _attn_fwd(order, proj, tab, attn_sinks, S, ATTN, KVW)
    full_mix = ag_mix.wait(order)
    fw_mix = _Xfer("forward_mixers", full_mix, _gather_forward(full_mix))
    fw_mix.start(order)
    wattT, wgrp_g, wpupT, wout = fw_mix.wait(order)
    wgrp = wgrp_g.reshape(8, 4, gw // 8, gw).transpose(1, 0, 2, 3).reshape(4, gw, gw)

    ps = _pool_fwd(order, proj, wgrp, pool_scale, S, uoff, POOL)
    a_out = _mm(order, att, wattT, kind="nt", out_dtype=F32, tm=1024, tn=1024, name="attn_up")
    p_out = _mm(order, ps, wpupT, kind="nt", out_dtype=F32, tm=1024, tn=1024, name="pool_up")
    mixed = _gate_mix(order, proj, bgate_full, a_out, p_out, S, D, goff)
    y1 = _mm(order, mixed, wout, kind="nn", out_dtype=F32, tm=1024, tn=1024, name="out_proj")

    bufs = g_ffn.wait(order)
    d_ffn = _Xfer("gather_w_ffn_in_diag", bufs, _gather_send(bufs, ("diag",), sibling=False))
    d_ffn.start(order)
    f_ffn = _Xfer("forward_w_ffn_in_near", d_ffn.bufs, _gather_forward(bufs, NEAR))
    f_ffn.start(order)
    h1, h1b = _ln1_fwd(order, h0, y1, ln1_g, ln1_b)
    bufs = f_ffn.wait(order)
    f_near = _ffn_in_near(order, h1b, bufs[0], FF)
    bufs = d_ffn.wait(order, bufs)
    fd_ffn = _Xfer("forward_w_ffn_in_diag", bufs, _gather_forward(bufs, ("diag",)))
    fd_ffn.start(order)
    ag_down = _Xfer("gather_w_ffn_down", [full_down], _gather_send([full_down]))
    ag_down.start(order)
    (wffnT,) = fd_ffn.wait(order)
    f_far, act = _ffn_in_far(order, h1b, wffnT, f_near)

    (full_down,) = ag_down.wait(order)
    fw_down = _Xfer("forward_w_ffn_down", [full_down], _gather_forward([full_down]))
    fw_down.start(order)
    (wdown,) = fw_down.wait(order)
    y2 = _mm(order, act, wdown, kind="nn", out_dtype=F32, tm=1024, tn=1024, tk=5504, name="ffn_down")

    dz2, dz2b, dg2, db2, loss_part = _ln2_loss_bwd(order, h1, y2, tgt, ln2_g, ln2_b)
    df = _d_act_swiglu(order, dz2b, wdown, f_near, f_far)
    gwdown = _mm(order, act, dz2b, kind="tn", out_dtype=BF16, tm=256, tn=2048, name="d_ffn_down")
    rs_down = _GradReduce("w_ffn_down", [gwdown], ["w_ffn_down"])
    rs_down.pair_start(order)
    gwffnT = _mm(order, df, h1b, kind="tn", out_dtype=BF16, tm=256, tn=2048, name="d_ffn_in", a_lead="halves")
    rs_down.pair_sum_chip_start(order)
    rs_ffn = _GradReduce("w_ffn_in", [gwffnT], ["w_ffn_in"])
    rs_ffn.pair_start(order)
    dh1 = _mm(order, df, wffnT, kind="nn", out_dtype=F32, tm=1024, tn=1024, tk=5504, name="d_h1", a_lead="halves")
    rs_ffn.pair_sum_chip_start(order)
    dz1, dz1b, dg1, db1 = _ln1_bwd(order, h0, y1, ln1_g, dh1, dz2)
    gwout = _mm(order, mixed, dz1b, kind="tn", out_dtype=BF16, tm=512, tn=1024, name="d_out_proj")
    rs_out = _GradReduce("w_out", [gwout], ["w_out"])
    rs_out.pair_start(order)
    dmixed = _mm(order, dz1b, wout, kind="nt", out_dtype=F32, tm=1024, tn=1024, name="d_mixed")
    rs_out.pair_sum_chip_start(order)

    dproj = _zero_meta_block(order, Tp, IN)
    dap, dproj, dbgate = _gate_bwd(order, proj, bgate_full, a_out, p_out, dmixed, dproj, S, D, goff)
    gwattT = _mm(order, dap, att, kind="tn", out_dtype=BF16, tm=512, tn=1024, name="d_attn_up", a_lead=0)
    datt = _mm(order, dap, wattT, kind="nn", out_dtype=BF16, tm=1024, tn=1024, name="d_att", a_lead=0)
    gwpupT = _mm(order, dap, ps, kind="tn", out_dtype=BF16, tm=512, tn=1024, name="d_pool_up", a_lead=1)
    dps = _mm(order, dap, wpupT, kind="nn", out_dtype=F32, tm=1024, tn=1024, name="d_ps", a_lead=1)
    dpl, gwgrp, dscale = _pool_bwd_mix(order, proj, wgrp, pool_scale, dps, S, uoff, POOL)
    gwgrp_rows = gwgrp.reshape(4, 8, gw // 8, gw).transpose(1, 0, 2, 3).reshape(8 * 4 * (gw // 8), gw).astype(BF16)
    rs_mix = _GradReduce("mixers", [gwattT, gwgrp_rows, gwpupT], ["w_attn_up", "w_pool_grp", "w_pool_up"])
    rs_mix.pair_start(order)
    dproj = _pool_bwd_window(order, dpl, dproj, S, uoff, POOL)
    rs_mix.pair_sum_chip_start(order)
    dproj, dk, dv, dsink = _attn_bwd(order, proj, tab, attn_sinks, datt, dproj, S, ATTN, KVW)
    dproj = _put_dkv(order, dk, dv, dproj, ATTN)

    weights = dict(meta_tokens=meta_tokens, ln_in_g=ln_in_g, ln_in_b=ln_in_b, w_in=w_in, b_gate=b_gate,
                   attn_sinks=attn_sinks, w_attn_up=w_attn_up, w_pool_grp=w_pool_grp, pool_scale=pool_scale,
                   w_pool_up=w_pool_up, w_out=w_out, ln1_g=ln1_g, ln1_b=ln1_b, w_ffn_in=w_ffn_in,
                   w_ffn_down=w_ffn_down, ln2_g=ln2_g, ln2_b=ln2_b)
    ms = dict(meta_tokens=m_meta_tokens, ln_in_g=m_ln_in_g, ln_in_b=m_ln_in_b, w_in=m_w_in, b_gate=m_b_gate,
              attn_sinks=m_attn_sinks, w_attn_up=m_w_attn_up, w_pool_grp=m_w_pool_grp, pool_scale=m_pool_scale,
              w_pool_up=m_w_pool_up, w_out=m_w_out, ln1_g=m_ln1_g, ln1_b=m_ln1_b, w_ffn_in=m_w_ffn_in,
              w_ffn_down=m_w_ffn_down, ln2_g=m_ln2_g, ln2_b=m_ln2_b)
    vs = dict(meta_tokens=v_meta_tokens, ln_in_g=v_ln_in_g, ln_in_b=v_ln_in_b, w_in=v_w_in, b_gate=v_b_gate,
              attn_sinks=v_attn_sinks, w_attn_up=v_w_attn_up, w_pool_grp=v_w_pool_grp, pool_scale=v_pool_scale,
              w_pool_up=v_w_pool_up, w_out=v_w_out, ln1_g=v_ln1_g, ln1_b=v_ln1_b, w_ffn_in=v_w_ffn_in,
              w_ffn_down=v_w_ffn_down, ln2_g=v_ln2_g, ln2_b=v_ln2_b)
    grads, deltas, new_ms, new_vs = {}, {}, {}, {}

    def update(nm, g):
        g = g.reshape(weights[nm].shape)
        grads[nm] = g
        deltas[nm], new_ms[nm], new_vs[nm] = _adamw(order, weights[nm], g, ms[nm], vs[nm], "adamw_" + nm)

    def update_reduced(nm, bufs, transposed=False):
        psum, parts = bufs
        if transposed:
            to2d, back = (lambda t: t[0].T), (lambda t: t.T[None])
        else:
            to2d, back = (lambda t: t.reshape(parts.shape[1:])), (lambda t: t.reshape(weights[nm].shape))
        outs = _chip_sum_adamw(order, to2d(weights[nm]), psum, parts, to2d(ms[nm]), to2d(vs[nm]), "adamw_" + nm)
        grads[nm], deltas[nm], new_ms[nm], new_vs[nm] = (back(t) for t in outs)

    gwinT = _mm(order, dproj, h0b, kind="tn", out_dtype=BF16, tm=512, tn=1024, name="d_w_in")
    rs_in = _GradReduce("w_in", [gwinT], ["w_in"])
    rs_in.pair_start(order)
    update_reduced("w_ffn_down", rs_down.finish(order)[0])
    rs_in.pair_sum_chip_start(order)
    dh0 = _mm(order, dproj, winT, kind="nn", out_dtype=F32, tm=1408, tn=1024, tk=2560, name="d_h0")
    dx, dmeta_block, dg_in, db_in = _ln_in_bwd(order, x2, meta_pad, ln_in_g2, dh0, dz1)
    grad_x = dx[None]
    dmeta = dmeta_block[META_ROW0:]

    small_shapes = [(D,), (D,), (1, D), (1, D), (1, D), (1, D), (1, POOL), (1, NQ), (), (N_META, D), (2, D)]
    red = _all_reduce_small(order, _pack([dg_in, db_in, dg1, db1, dg2, db2, dscale, dsink[:, :, 0], loss_part,
                                          dmeta, dbgate]), "small_grads_all_reduce")

    update_reduced("w_ffn_in", rs_ffn.finish(order)[0], transposed=True)
    update_reduced("w_out", rs_out.finish(order)[0])
    b_att, b_grp, b_pup = rs_mix.finish(order)
    update("w_attn_up", _chip_sum(order, *b_att, "chip_sum_w_attn_up").T)
    update_reduced("w_pool_grp", b_grp)
    update("w_pool_up", _chip_sum(order, *b_pup, "chip_sum_w_pool_up").T)

    (g_ln_in_g, g_ln_in_b, g_ln1_g, g_ln1_b, g_ln2_g, g_ln2_b, g_scale, g_sinks, loss_sum, g_meta_full,
     g_bgate_full) = _unpack(red, small_shapes)
    loss = 0.5 * loss_sum
    update("meta_tokens", lax.dynamic_slice(g_meta_full, (0, dev * dcols), (N_META, dcols)))
    update("b_gate", lax.dynamic_slice(g_bgate_full, (0, dev * dcols), (2, dcols)))
    for nm, g in (("ln_in_g", g_ln_in_g), ("ln_in_b", g_ln_in_b), ("ln1_g", g_ln1_g), ("ln1_b", g_ln1_b),
                  ("ln2_g", g_ln2_g), ("ln2_b", g_ln2_b), ("pool_scale", g_scale), ("attn_sinks", g_sinks)):
        update(nm, g)

    update_reduced("w_in", rs_in.finish(order)[0], transposed=True)

    names = list(weights)
    return (loss, grad_x, *[grads[n] for n in names], *[deltas[n] for n in names],
            *[new_ms[n] for n in names], *[new_vs[n] for n in names])
```

```python
import jax
import jax.numpy as jnp
from jax import lax
from jax.experimental import pallas as pl
from jax.experimental.pallas import tpu as pltpu

F32 = jnp.float32
BF16 = jnp.bfloat16
MESH = pl.DeviceIdType.MESH

N_META = 16
HEAD_DIM = 64
Q_PER_KV = 8
WINDOW = 128
BLOCK = 128
ATTN_SCALE = HEAD_DIM ** -0.5
ROPE_DIM = HEAD_DIM // 4
ROPE_THETA = 500000.0
NEG_INF = -1e30
POOL_WINDOWS = (2, 4, 8, 16)
LN_EPS = 1e-5
DN_ALPHA = 2.0 ** 0.25
ADAM_LR = 0.001
ADAM_B1 = 0.9
ADAM_B2 = 0.999
ADAM_EPS = 1e-08
ADAM_WD = 0.01
ADAM_STEP = 10

LANES = 128
META_ROW0 = BLOCK - N_META
VMEM_LIMIT = 56 * 1024 * 1024

ANY = pl.BlockSpec(memory_space=pl.ANY)
HBM = pl.BlockSpec(memory_space=pltpu.HBM)
SEM = pl.BlockSpec(memory_space=pltpu.SEMAPHORE)
EFFECT = pltpu.SideEffectType.DATAFLOW_SIDE_EFFECTING


def _params(sem=None, **kw):
    return pltpu.CompilerParams(dimension_semantics=sem, vmem_limit_bytes=VMEM_LIMIT, **kw)


class _Order:
    def __init__(self):
        self.last = None


def _call(order, body, operands, *, name, in_specs, out_specs, out_shape, grid=(), scratch=(), sem=None,
          aliases=None, prefetch=()):
    n_in, npf = len(operands), len(prefetch)
    tok = order.last
    if tok is not None and any(tok is op for op in operands):
        tok = None

    def wrapped(*refs):
        refs = list(refs)
        if tok is not None:
            del refs[npf + n_in]
        body(*refs)

    specs = list(in_specs) + ([ANY] if tok is not None else [])
    ops = list(operands) + ([tok] if tok is not None else [])
    if npf:
        out = pl.pallas_call(
            wrapped, name=name, out_shape=out_shape, compiler_params=_params(sem),
            grid_spec=pltpu.PrefetchScalarGridSpec(num_scalar_prefetch=npf, grid=grid, in_specs=specs,
                                                   out_specs=out_specs, scratch_shapes=list(scratch)),
        )(*prefetch, *ops)
    else:
        out = pl.pallas_call(
            wrapped, name=name, grid=grid, in_specs=specs, out_specs=out_specs, out_shape=out_shape,
            scratch_shapes=list(scratch), input_output_aliases=aliases or {}, compiler_params=_params(sem),
        )(*ops)
    order.last = out[0] if isinstance(out, (list, tuple)) else out
    return out


def _pick(dim, pref, mult=LANES):
    best = None
    t = mult
    while t <= min(dim, pref):
        if dim % t == 0:
            best = t
        t += mult
    return dim if best is None else best


_DIMS = {"nn": (((1,), (0,)), ((), ())), "nt": (((1,), (1,)), ((), ())), "tn": (((0,), (0,)), ((), ()))}


def _mm(order, a, b, *, kind, out_dtype, tm, tn, tk=None, name, a_lead=None):
    a2 = a.shape[-2:]
    halves = a_lead == "halves"
    if halves:
        a2 = (a2[0], 2 * a2[1])
    if kind == "tn":
        K, M = a2
    else:
        M, K = a2
    N = b.shape[0] if kind == "nt" else b.shape[1]
    half_cols = a2[1] // 2
    tm = _pick(half_cols if halves and kind == "tn" else M, tm)
    tn = _pick(N, tn)
    tk = K if tk is None else _pick(half_cols if halves and kind != "tn" else K, tk)
    nm, nn_, nk = M // tm, N // tn, K // tk
    a_bytes = M * K * a.dtype.itemsize
    b_bytes = N * K * b.dtype.itemsize
    i_outer = (a_bytes + nm * b_bytes <= b_bytes + nn_ * a_bytes) if nk == 1 else True

    def ij(g0, g1):
        return (g0, g1) if i_outer else (g1, g0)

    def a_map(g0, g1, k):
        i, _ = ij(g0, g1)
        if halves:
            per = half_cols // (tm if kind == "tn" else tk)
            return (i // per, k, i % per) if kind == "tn" else (k // per, i, k % per)
        idx = (k, i) if kind == "tn" else (i, k)
        return idx if a_lead is None else (a_lead,) + idx

    def b_map(g0, g1, k):
        _, j = ij(g0, g1)
        return (j, k) if kind == "nt" else (k, j)

    def o_map(g0, g1, k):
        return ij(g0, g1)

    a_blk = (tk, tm) if kind == "tn" else (tm, tk)
    if a_lead is not None:
        a_blk = (None,) + a_blk
    b_blk = (tn, tk) if kind == "nt" else (tk, tn)

    in_place = out_dtype == F32

    def body(a_ref, b_ref, o_ref, *acc):
        p = lax.dot_general(a_ref[...], b_ref[...], _DIMS[kind], preferred_element_type=F32)
        if nk == 1:
            o_ref[...] = p.astype(o_ref.dtype)
        else:
            k = pl.program_id(2)
            acc_ref = o_ref if in_place else acc[0]

            @pl.when(k == 0)
            def _():
                acc_ref[...] = p

            @pl.when(k > 0)
            def _():
                acc_ref[...] += p

            if not in_place:
                @pl.when(k == nk - 1)
                def _():
                    o_ref[...] = acc_ref[...].astype(o_ref.dtype)

    grid = (nm, nn_, nk) if i_outer else (nn_, nm, nk)
    return _call(
        order, body, [a, b], name=name, grid=grid,
        in_specs=[pl.BlockSpec(a_blk, a_map), pl.BlockSpec(b_blk, b_map)],
        out_specs=pl.BlockSpec((tm, tn), o_map),
        out_shape=jax.ShapeDtypeStruct((M, N), out_dtype),
        scratch=[] if nk == 1 or in_place else [pltpu.VMEM((tm, tn), F32)],
        sem=("parallel", "parallel", "arbitrary"))


def _x_half(far):
    x = lax.axis_index("x")
    return 1 - x if far else x


def _chip_quarter(y_neighbour):
    x, y = lax.axis_index("x"), lax.axis_index("y")
    return 2 * x + (1 - y if y_neighbour else y)


def _mm_nt_half(order, a, bT, *, tm, tn, name, far=False, into=None, parts=2, which=None):
    M, K = a.shape
    N = bT.shape[0]
    tm, tn = _pick(M, tm), _pick(N // parts, tn)
    nh = N // parts // tn
    part = which if which is not None else (lambda: _x_half(far))

    def body(a_ref, b_ref, *rest):
        rest[-1][...] = lax.dot_general(a_ref[...], b_ref[...], _DIMS["nt"], preferred_element_type=F32)

    return _call(
        order, body, [a, bT] + ([] if into is None else [into]), name=name, grid=(M // tm, nh),
        in_specs=[pl.BlockSpec((tm, K), lambda i, j: (i, 0)),
                  pl.BlockSpec((tn, K), lambda i, j: (part() * nh + j, 0))] + ([] if into is None else [ANY]),
        out_specs=pl.BlockSpec((tm, tn), lambda i, j: (i, part() * nh + j)),
        out_shape=jax.ShapeDtypeStruct((M, N), F32), aliases=None if into is None else {2: 0},
        sem=("parallel", "parallel"))


def _ln_stats(z):
    mu = jnp.mean(z, axis=-1, keepdims=True)
    zc = z - mu
    var = jnp.mean(zc * zc, axis=-1, keepdims=True)
    rstd = lax.rsqrt(var + LN_EPS)
    return zc * rstd, rstd


def _ln_bwd(dy, xhat, rstd, g):
    dxh = dy * g
    m1 = jnp.mean(dxh, axis=-1, keepdims=True)
    m2 = jnp.mean(dxh * xhat, axis=-1, keepdims=True)
    return rstd * (dxh - m1 - xhat * m2)


def _ln_in_fwd(order, x, meta_pad, g, b):
    S, D = x.shape
    nb = S // BLOCK

    def body(x_ref, mp_ref, g_ref, b_ref, h_ref, hb_ref):
        is_meta = pl.program_id(0) == nb
        xin = jnp.where(is_meta, mp_ref[...], x_ref[...])
        xhat, _ = _ln_stats(xin)
        y = xhat * g_ref[...] + b_ref[...]
        h_ref[...] = y
        hb_ref[...] = y.astype(BF16)

    row = pl.BlockSpec((BLOCK, D), lambda i: (i, 0))
    vec = pl.BlockSpec((1, D), lambda i: (0, 0))
    return _call(
        order, body, [x, meta_pad, g, b], name="ln_in_fwd", grid=(nb + 1,),
        in_specs=[pl.BlockSpec((BLOCK, D), lambda i: (jnp.minimum(i, nb - 1), 0)),
                  pl.BlockSpec((BLOCK, D), lambda i: (0, 0)), vec, vec],
        out_specs=[row, row],
        out_shape=[jax.ShapeDtypeStruct((S + BLOCK, D), F32), jax.ShapeDtypeStruct((S + BLOCK, D), BF16)],
        sem=("parallel",))


def _ln_in_bwd(order, x, meta_pad, g, dh0, dz1):
    S, D = x.shape
    nb = S // BLOCK

    def body(x_ref, mp_ref, g_ref, dh_ref, dz_ref, dx_ref, dm_ref, dg_ref, db_ref):
        i = pl.program_id(0)
        is_meta = i == nb
        xin = jnp.where(is_meta, mp_ref[...], x_ref[...])
        xhat, rstd = _ln_stats(xin)
        dy = dh_ref[...] + jnp.where(is_meta, 0.0, DN_ALPHA) * dz_ref[...]
        dxin = _ln_bwd(dy, xhat, rstd, g_ref[...])

        @pl.when(i < nb)
        def _():
            dx_ref[...] = dxin

        @pl.when(is_meta)
        def _():
            dm_ref[...] = dxin

        @pl.when(i == 0)
        def _():
            dg_ref[...] = jnp.zeros_like(dg_ref)
            db_ref[...] = jnp.zeros_like(db_ref)

        dg_ref[...] += jnp.sum(dy * xhat, axis=0, keepdims=True)
        db_ref[...] += jnp.sum(dy, axis=0, keepdims=True)

    row = pl.BlockSpec((BLOCK, D), lambda i: (i, 0))
    rowx = pl.BlockSpec((BLOCK, D), lambda i: (jnp.minimum(i, nb - 1), 0))
    vec = pl.BlockSpec((1, D), lambda i: (0, 0))
    return _call(
        order, body, [x, meta_pad, g, dh0, dz1], name="ln_in_bwd", grid=(nb + 1,),
        in_specs=[rowx, pl.BlockSpec((BLOCK, D), lambda i: (0, 0)), vec, row, rowx],
        out_specs=[rowx, pl.BlockSpec((BLOCK, D), lambda i: (0, 0)), vec, vec],
        out_shape=[jax.ShapeDtypeStruct((S, D), F32), jax.ShapeDtypeStruct((BLOCK, D), F32),
                   jax.ShapeDtypeStruct((1, D), F32), jax.ShapeDtypeStruct((1, D), F32)],
        sem=("arbitrary",))


def _ln1_fwd(order, h0, y1, g, b):
    S, D = y1.shape
    tm = _pick(S, 2 * BLOCK, 8)

    def body(h_ref, y_ref, g_ref, b_ref, o_ref, ob_ref):
        xhat, _ = _ln_stats(DN_ALPHA * h_ref[...] + y_ref[...])
        y = xhat * g_ref[...] + b_ref[...]
        o_ref[...] = y
        ob_ref[...] = y.astype(BF16)

    row = pl.BlockSpec((tm, D), lambda i: (i, 0))
    vec = pl.BlockSpec((1, D), lambda i: (0, 0))
    return _call(
        order, body, [h0, y1, g, b], name="ln1_fwd", grid=(S // tm,), in_specs=[row, row, vec, vec],
        out_specs=[row, row],
        out_shape=[jax.ShapeDtypeStruct((S, D), F32), jax.ShapeDtypeStruct((S, D), BF16)],
        sem=("parallel",))


def _ln1_bwd(order, h0, y1, g, dh1, dz2):
    S, D = y1.shape
    tm = _pick(S, BLOCK, 8)

    def body(h_ref, y_ref, g_ref, dh_ref, dz2_ref, dz_ref, dzb_ref, dg_ref, db_ref):
        i = pl.program_id(0)
        xhat, rstd = _ln_stats(DN_ALPHA * h_ref[...] + y_ref[...])
        dy = dh_ref[...] + DN_ALPHA * dz2_ref[...]
        dz = _ln_bwd(dy, xhat, rstd, g_ref[...])
        dz_ref[...] = dz
        dzb_ref[...] = dz.astype(BF16)

        @pl.when(i == 0)
        def _():
            dg_ref[...] = jnp.zeros_like(dg_ref)
            db_ref[...] = jnp.zeros_like(db_ref)

        dg_ref[...] += jnp.sum(dy * xhat, axis=0, keepdims=True)
        db_ref[...] += jnp.sum(dy, axis=0, keepdims=True)

    row = pl.BlockSpec((tm, D), lambda i: (i, 0))
    vec = pl.BlockSpec((1, D), lambda i: (0, 0))
    return _call(
        order, body, [h0, y1, g, dh1, dz2], name="ln1_bwd", grid=(S // tm,),
        in_specs=[row, row, vec, row, row], out_specs=[row, row, vec, vec],
        out_shape=[jax.ShapeDtypeStruct((S, D), F32), jax.ShapeDtypeStruct((S, D), BF16),
                   jax.ShapeDtypeStruct((1, D), F32), jax.ShapeDtypeStruct((1, D), F32)],
        sem=("arbitrary",))


def _ln2_loss_bwd(order, h1, y2, target, g, b):
    S, D = y2.shape
    tm = _pick(S, 2 * BLOCK, 8)

    def body(h_ref, y_ref, t_ref, g_ref, b_ref, dz_ref, dzb_ref, dg_ref, db_ref, loss_ref):
        i = pl.program_id(0)
        xhat, rstd = _ln_stats(DN_ALPHA * h_ref[...] + y_ref[...])
        diff = xhat * g_ref[...] + b_ref[...] - t_ref[...]
        dy = diff / D
        dz = _ln_bwd(dy, xhat, rstd, g_ref[...])
        dz_ref[...] = dz
        dzb_ref[...] = dz.astype(BF16)

        @pl.when(i == 0)
        def _():
            dg_ref[...] = jnp.zeros_like(dg_ref)
            db_ref[...] = jnp.zeros_like(db_ref)
            loss_ref[...] = jnp.zeros_like(loss_ref)

        dg_ref[...] += jnp.sum(dy * xhat, axis=0, keepdims=True)
        db_ref[...] += jnp.sum(dy, axis=0, keepdims=True)
        loss_ref[...] += jnp.sum(jnp.mean(diff * diff, axis=-1, keepdims=True), axis=0, keepdims=True)

    row = pl.BlockSpec((tm, D), lambda i: (i, 0))
    vec = pl.BlockSpec((1, D), lambda i: (0, 0))
    one = pl.BlockSpec((1, 1), lambda i: (0, 0))
    return _call(
        order, body, [h1, y2, target, g, b], name="ln2_loss_bwd", grid=(S // tm,),
        in_specs=[row, row, row, vec, vec], out_specs=[row, row, vec, vec, one],
        out_shape=[jax.ShapeDtypeStruct((S, D), F32), jax.ShapeDtypeStruct((S, D), BF16),
                   jax.ShapeDtypeStruct((1, D), F32), jax.ShapeDtypeStruct((1, D), F32),
                   jax.ShapeDtypeStruct((1, 1), F32)],
        sem=("arbitrary",))


def _rope_table(S):
    r = jnp.arange(S + BLOCK)
    pos = jnp.where(r < S, r + N_META, jnp.maximum(r - (S + META_ROW0), 0))
    half = ROPE_DIM // 2
    lane = jnp.arange(LANES) % HEAD_DIM
    inv_freq = ROPE_THETA ** (-(lane % half).astype(F32) * 2.0 / ROPE_DIM)
    ang = pos.astype(F32)[:, None] * inv_freq[None, :]
    cos, sin = jnp.cos(ang), jnp.sin(ang)
    c = jnp.where(lane < ROPE_DIM, cos, 1.0)
    sa = jnp.where(lane < half, -sin, 0.0)
    sb = jnp.where((lane >= half) & (lane < ROPE_DIM), sin, 0.0)
    return jnp.concatenate([c, sa, sb], axis=1).astype(F32)


def _rope(x, tab):
    h = ROPE_DIM // 2
    return (x * tab[:, :LANES] + pltpu.roll(x, LANES - h, 1) * tab[:, LANES:2 * LANES]
            + pltpu.roll(x, h, 1) * tab[:, 2 * LANES:])


def _rope_t(dy, tab):
    h = ROPE_DIM // 2
    return (dy * tab[:, :LANES] + pltpu.roll(dy * tab[:, LANES:2 * LANES], h, 1)
            + pltpu.roll(dy * tab[:, 2 * LANES:], LANES - h, 1))


NKEY = N_META + 2 * BLOCK


def _attn_tiles(g, n, S, sink_ref, q_ref, k_ref, v_ref, tab_ref):
    NQG = Q_PER_KV // 2
    R = NQG * BLOCK
    halfsel = (g % 2).astype(F32)
    prev = jnp.maximum(n - 1, 0)
    qrow = pl.ds(pl.multiple_of(n * BLOCK, BLOCK), BLOCK)
    prow = pl.ds(pl.multiple_of(prev * BLOCK, BLOCK), BLOCK)
    mrow = pl.ds(S + META_ROW0, N_META)

    tq = tab_ref[qrow, :]
    qf = q_ref[...]
    q4 = jnp.concatenate([_rope(qf[:, LANES * p:LANES * (p + 1)], tq) for p in range(NQG)], axis=0).astype(BF16)

    tk = jnp.concatenate([tab_ref[mrow, :], tab_ref[prow, :], tq], axis=0)
    kr = _rope(jnp.concatenate([k_ref[mrow, :], k_ref[prow, :], k_ref[qrow, :]], axis=0), tk)
    vr = jnp.concatenate([v_ref[mrow, :], v_ref[prow, :], v_ref[qrow, :]], axis=0)

    lane = lax.broadcasted_iota(jnp.int32, kr.shape, 1)
    own = jnp.where(lane < HEAD_DIM, 1.0 - halfsel, halfsel)

    def lo_hi(t):
        mine = t * own
        other = pltpu.roll(mine, HEAD_DIM, 1)
        lo = mine * (1.0 - halfsel) + other * halfsel
        hi = other * (1.0 - halfsel) + mine * halfsel
        return lo.astype(BF16), hi.astype(BF16)

    klo, khi = lo_hi(kr)
    vlo, vhi = lo_hi(vr)

    jj = lax.broadcasted_iota(jnp.int32, (BLOCK, R), 0)
    qi = lax.broadcasted_iota(jnp.int32, (BLOCK, R), 1) & (BLOCK - 1)
    in_cur = jj <= qi
    band_ok = in_cur | (jj > qi + jnp.where(n >= 1, 0, 2 * BLOCK))

    def soft(kk, parity):
        sk = jnp.concatenate(
            [jnp.full((1, BLOCK), sink_ref[0, Q_PER_KV * g + 2 * p + parity], F32) for p in range(NQG)], axis=1)
        s = lax.dot_general(kk, q4, _DIMS["nt"], preferred_element_type=F32) * ATTN_SCALE
        band = jnp.where(in_cur, s[N_META + BLOCK:], s[N_META:N_META + BLOCK])
        s = jnp.concatenate([s[:N_META], jnp.where(band_ok, band, NEG_INF)], axis=0)
        m = jnp.maximum(jnp.max(s, axis=0, keepdims=True), sk)
        p = jnp.exp(s - m)
        es = jnp.exp(sk - m)
        inv = 1.0 / (jnp.sum(p, axis=0, keepdims=True) + es)
        return p * inv, es * inv

    pe, sink_e = soft(klo, 0)
    po, sink_o = soft(khi, 1)
    return q4, tk, (klo, khi), (vlo, vhi), (pe, po), (sink_e, sink_o), own, in_cur


def _spread(t, in_cur):
    band = t[N_META:]
    return jnp.concatenate([t[:N_META], jnp.where(in_cur, 0.0, band), jnp.where(in_cur, band, 0.0)], axis=0)


def _attn_specs(S, ATTN, KVW):
    Tp = S + BLOCK
    koff, voff = ATTN // LANES, (ATTN + KVW) // LANES
    gw = Q_PER_KV * HEAD_DIM
    return [pl.BlockSpec(memory_space=pltpu.SMEM),
            pl.BlockSpec((BLOCK, gw), lambda g, n: (n, g)),
            pl.BlockSpec((Tp, LANES), lambda g, n: (0, koff + g // 2)),
            pl.BlockSpec((Tp, LANES), lambda g, n: (0, voff + g // 2)),
            pl.BlockSpec((Tp, 3 * LANES), lambda g, n: (0, 0))]


def _attn_fwd(order, proj, tab, sinks, S, ATTN, KVW):
    G = KVW // HEAD_DIM
    nb = S // BLOCK
    gw = Q_PER_KV * HEAD_DIM

    def body(sink_ref, q_ref, k_ref, v_ref, tab_ref, o_ref):
        g, n = pl.program_id(0), pl.program_id(1)
        _, _, _, (vlo, vhi), (pe, po), _, _, in_cur = _attn_tiles(g, n, S, sink_ref, q_ref, k_ref, v_ref, tab_ref)
        o4 = (lax.dot_general(_spread(pe, in_cur).astype(BF16), vlo, _DIMS["tn"], preferred_element_type=F32)
              + lax.dot_general(_spread(po, in_cur).astype(BF16), vhi, _DIMS["tn"], preferred_element_type=F32))
        o_ref[...] = jnp.concatenate(
            [o4[BLOCK * p:BLOCK * (p + 1)] for p in range(Q_PER_KV // 2)], axis=1).astype(BF16)

    return _call(
        order, body, [sinks, proj, proj, proj, tab], name="attn_fwd", grid=(G, nb),
        in_specs=_attn_specs(S, ATTN, KVW),
        out_specs=pl.BlockSpec((BLOCK, gw), lambda g, n: (n, g)),
        out_shape=jax.ShapeDtypeStruct((S, ATTN), BF16),
        sem=("parallel", "arbitrary"))


def _attn_bwd(order, proj, tab, sinks, da, dproj, S, ATTN, KVW):
    G = KVW // HEAD_DIM
    nb = S // BLOCK
    Tp = S + BLOCK
    NQG = Q_PER_KV // 2
    gw = Q_PER_KV * HEAD_DIM

    def body(sink_ref, q_ref, k_ref, v_ref, tab_ref, da_ref, dproj_in, dq_ref, dk_ref, dv_ref, ds_ref):
        del dproj_in
        g, n = pl.program_id(0), pl.program_id(1)
        q4, tk, (klo, khi), (vlo, vhi), (pe, po), (sink_e, sink_o), own, in_cur = _attn_tiles(
            g, n, S, sink_ref, q_ref, k_ref, v_ref, tab_ref)
        dof = da_ref[...]
        do4 = jnp.concatenate([dof[:, LANES * p:LANES * (p + 1)] for p in range(NQG)], axis=0)

        def grads(p, vv):
            dp = lax.dot_general(vv, do4, _DIMS["nt"], preferred_element_type=F32)
            dp = jnp.concatenate(
                [dp[:N_META], jnp.where(in_cur, dp[N_META + BLOCK:], dp[N_META:N_META + BLOCK])], axis=0)
            delta = jnp.sum(p * dp, axis=0, keepdims=True)
            return _spread(p * (dp - delta) * ATTN_SCALE, in_cur).astype(BF16), delta

        dse, delta_e = grads(pe, vlo)
        dso, delta_o = grads(po, vhi)

        dq4 = (lax.dot_general(dse, klo, _DIMS["tn"], preferred_element_type=F32)
               + lax.dot_general(dso, khi, _DIMS["tn"], preferred_element_type=F32))
        tq = tk[N_META + BLOCK:]
        dq_ref[...] = jnp.concatenate(
            [_rope_t(dq4[BLOCK * p:BLOCK * (p + 1)], tq) for p in range(NQG)], axis=1).astype(BF16)

        lane = lax.broadcasted_iota(jnp.int32, (NKEY, LANES), 1)

        def fold(lo_part, hi_part):
            t = jnp.where(lane < HEAD_DIM, lo_part, hi_part)
            return t + pltpu.roll(t, HEAD_DIM, 1)

        dk = _rope_t(fold(jnp.dot(dse, q4, preferred_element_type=F32),
                          jnp.dot(dso, q4, preferred_element_type=F32)), tk) * own
        dv = fold(jnp.dot(_spread(pe, in_cur).astype(BF16), do4, preferred_element_type=F32),
                  jnp.dot(_spread(po, in_cur).astype(BF16), do4, preferred_element_type=F32)) * own

        @pl.when((n == 0) & (g % 2 == 0))
        def _():
            dk_ref[...] = jnp.zeros_like(dk_ref)
            dv_ref[...] = jnp.zeros_like(dv_ref)

        @pl.when(n == 0)
        def _():
            ds_ref[...] = jnp.zeros_like(ds_ref)

        prev = jnp.maximum(n - 1, 0)
        qrow = pl.ds(pl.multiple_of(n * BLOCK, BLOCK), BLOCK)
        prow = pl.ds(pl.multiple_of(prev * BLOCK, BLOCK), BLOCK)
        mrow = pl.ds(S + META_ROW0, N_META)
        for ref, val in ((dk_ref, dk), (dv_ref, dv)):
            ref[mrow, :] += val[:N_META]
            ref[prow, :] += val[N_META:N_META + BLOCK]
            ref[qrow, :] += val[N_META + BLOCK:]

        srow = lax.broadcasted_iota(jnp.int32, (Q_PER_KV, LANES), 0)
        acc = jnp.zeros((Q_PER_KV, LANES), F32)
        for p in range(NQG):
            for parity, (sk, dl) in enumerate(((sink_e, delta_e), (sink_o, delta_o))):
                val = -jnp.sum(sk[:, BLOCK * p:BLOCK * (p + 1)] * dl[:, BLOCK * p:BLOCK * (p + 1)])
                acc = jnp.where(srow == 2 * p + parity, val, acc)
        ds_ref[0] += acc

    in_specs = _attn_specs(S, ATTN, KVW) + [pl.BlockSpec((BLOCK, gw), lambda g, n: (n, g)), ANY]
    slab = pl.BlockSpec((Tp, LANES), lambda g, n: (0, g // 2))
    return _call(
        order, body, [sinks, proj, proj, proj, tab, da, dproj], name="attn_bwd", grid=(G, nb), in_specs=in_specs,
        out_specs=[pl.BlockSpec((BLOCK, gw), lambda g, n: (n, g)), slab, slab,
                   pl.BlockSpec((1, Q_PER_KV, LANES), lambda g, n: (g, 0, 0))],
        out_shape=[jax.ShapeDtypeStruct(dproj.shape, BF16), jax.ShapeDtypeStruct((Tp, KVW), F32),
                   jax.ShapeDtypeStruct((Tp, KVW), F32), jax.ShapeDtypeStruct((G, Q_PER_KV, LANES), F32)],
        aliases={6: 0}, sem=("arbitrary", "arbitrary"))


def _zero_meta_block(order, Tp, IN):
    tc = _pick(IN, 4096)

    def body(o_ref):
        o_ref[...] = jnp.zeros_like(o_ref)

    return _call(
        order, body, [], name="dproj_zero_meta", grid=(IN // tc,), in_specs=[],
        out_specs=pl.BlockSpec((BLOCK, tc), lambda j: (Tp // BLOCK - 1, j)),
        out_shape=jax.ShapeDtypeStruct((Tp, IN), BF16), sem=("parallel",))


def _put_dkv(order, dk, dv, dproj, ATTN):
    Tp, KVW = dk.shape
    nkb = KVW // LANES
    koff = ATTN // LANES

    def body(dk_ref, dv_ref, dproj_in, o_ref):
        del dproj_in
        t = pl.program_id(0)
        o_ref[...] = jnp.where(t < nkb, dk_ref[...], dv_ref[...]).astype(BF16)

    src = pl.BlockSpec((Tp, LANES), lambda t: (0, t % nkb))
    return _call(
        order, body, [dk, dv, dproj], name="dproj_put_dkv", grid=(2 * nkb,), in_specs=[src, src, ANY],
        out_specs=pl.BlockSpec((Tp, LANES), lambda t: (0, koff + t)),
        out_shape=jax.ShapeDtypeStruct(dproj.shape, BF16), aliases={2: 0}, sem=("parallel",))


HALO = 16


def _window_sums(x, up):
    n = x.shape[0]
    out = []
    s = x
    for k in (1, 2, 4, 8):
        s = s + pltpu.roll(s, (n - k) if up else k, 0)
        out.append(s)
    return out


def _pool_specs(S, ub, gw, tm):
    meta_halo = (S + BLOCK - HALO) // HALO

    def main(g):
        return pl.BlockSpec((tm, gw), lambda i: (i, ub + g))

    def halo(g):
        return pl.BlockSpec((HALO, gw), lambda i: (jnp.where(i == 0, meta_halo, i * (tm // HALO) - 1), ub + g))

    return [main(g) for g in range(4)] + [halo(g) for g in range(4)]


def _pooled(main_refs, halo_refs, g):
    x = jnp.concatenate([halo_refs[g][...], main_refs[g][...]], axis=0)
    s = _window_sums(x, up=False)[g]
    return (s[HALO:] * (1.0 / POOL_WINDOWS[g]) - x[HALO:]).astype(BF16)


def _pool_fwd(order, proj, wgrp, scale, S, uoff, POOL):
    gw = POOL // 4
    tm = BLOCK

    def body(*refs):
        main, halo = refs[:4], refs[4:8]
        w_ref, sc_ref, o_ref = refs[8:]
        for g in range(4):
            mixed = jnp.dot(_pooled(main, halo, g), w_ref[g], preferred_element_type=F32)
            o_ref[:, gw * g:gw * (g + 1)] = (mixed * sc_ref[:, gw * g:gw * (g + 1)]).astype(BF16)

    return _call(
        order, body, [proj] * 8 + [wgrp, scale], name="pool_fwd", grid=(S // tm,),
        in_specs=_pool_specs(S, uoff // gw, gw, tm) + [
            pl.BlockSpec((4, gw, gw), lambda i: (0, 0, 0)), pl.BlockSpec((1, POOL), lambda i: (0, 0))],
        out_specs=pl.BlockSpec((tm, POOL), lambda i: (i, 0)),
        out_shape=jax.ShapeDtypeStruct((S, POOL), BF16), sem=("parallel",))


def _pool_bwd_mix(order, proj, wgrp, scale, dps, S, uoff, POOL):
    gw = POOL // 4
    tm = BLOCK

    def body(*refs):
        main, halo = refs[:4], refs[4:8]
        w_ref, sc_ref, dps_ref, dpl_ref, dw_ref, dsc_ref = refs[8:]
        i = pl.program_id(0)

        @pl.when(i == 0)
        def _():
            dw_ref[...] = jnp.zeros_like(dw_ref)
            dsc_ref[...] = jnp.zeros_like(dsc_ref)

        for g in range(4):
            cols = slice(gw * g, gw * (g + 1))
            pooled = _pooled(main, halo, g)
            mixed = jnp.dot(pooled, w_ref[g], preferred_element_type=F32)
            dps_g = dps_ref[:, cols]
            dsc_ref[:, cols] += jnp.sum(dps_g * mixed, axis=0, keepdims=True)
            dms = (dps_g * sc_ref[:, cols]).astype(BF16)
            dw_ref[g] += lax.dot_general(pooled, dms, _DIMS["tn"], preferred_element_type=F32)
            dpl_ref[:, cols] = lax.dot_general(dms, w_ref[g], _DIMS["nt"], preferred_element_type=F32)

    row = pl.BlockSpec((tm, POOL), lambda i: (i, 0))
    return _call(
        order, body, [proj] * 8 + [wgrp, scale, dps], name="pool_bwd_mix", grid=(S // tm,),
        in_specs=_pool_specs(S, uoff // gw, gw, tm) + [
            pl.BlockSpec((4, gw, gw), lambda i: (0, 0, 0)), pl.BlockSpec((1, POOL), lambda i: (0, 0)), row],
        out_specs=[row, pl.BlockSpec((4, gw, gw), lambda i: (0, 0, 0)), pl.BlockSpec((1, POOL), lambda i: (0, 0))],
        out_shape=[jax.ShapeDtypeStruct((S, POOL), F32), jax.ShapeDtypeStruct((4, gw, gw), F32),
                   jax.ShapeDtypeStruct((1, POOL), F32)],
        sem=("arbitrary",))


def _pool_bwd_window(order, dpl, dproj, S, uoff, POOL):
    gw = POOL // 4
    nb = S // BLOCK
    ub = uoff // gw

    def body(main_ref, halo_ref, dproj_in, o_ref):
        del dproj_in
        b, g = pl.program_id(0), pl.program_id(1)
        main = jnp.where(b < nb, main_ref[...], 0.0)
        halo = jnp.where(b == nb - 1, 0.0, halo_ref[...])
        sums = _window_sums(jnp.concatenate([main, halo], axis=0), up=True)
        du = jnp.zeros((BLOCK, gw), F32)
        for k, w in enumerate(POOL_WINDOWS):
            du = jnp.where(g == k, sums[k][:BLOCK] * (1.0 / w), du)
        du = du - main
        row = lax.broadcasted_iota(jnp.int32, du.shape, 0)
        first_valid = jnp.where(b == nb, META_ROW0, 0)
        o_ref[...] = jnp.where(row >= first_valid, du, 0.0).astype(BF16)

    return _call(
        order, body, [dpl, dpl, dproj], name="pool_bwd_window", grid=(nb + 1, 4),
        in_specs=[pl.BlockSpec((BLOCK, gw), lambda b, g: (jnp.minimum(b, nb - 1), g)),
                  pl.BlockSpec((HALO, gw), lambda b, g: (
                      jnp.where(b == nb, 0, jnp.minimum((b + 1) * (BLOCK // HALO), S // HALO - 1)), g)),
                  ANY],
        out_specs=pl.BlockSpec((BLOCK, gw), lambda b, g: (b, ub + g)),
        out_shape=jax.ShapeDtypeStruct(dproj.shape, BF16), aliases={2: 0}, sem=("parallel", "parallel"))


def _sigmoid(x):
    return 1.0 / (1.0 + jnp.exp(-x))


def _gate_tiles(S, D, goff):
    tc = 512
    while goff % tc or D % tc:
        tc //= 2
    return _pick(S, 1024, 8), tc


def _gate_mix(order, proj, bgate, a_out, p_out, S, D, goff):
    tm, tc = _gate_tiles(S, D, goff)
    g0b, nd = goff // tc, D // tc

    def body(l0_ref, l1_ref, b_ref, a_ref, p_ref, o_ref):
        g0 = _sigmoid(l0_ref[...] + b_ref[0:1, :])
        g1 = _sigmoid(l1_ref[...] + b_ref[1:2, :])
        o_ref[...] = (g0 * a_ref[...] + g1 * p_ref[...]).astype(BF16)

    tile = pl.BlockSpec((tm, tc), lambda i, j: (i, j))
    return _call(
        order, body, [proj, proj, bgate, a_out, p_out], name="gate_mix", grid=(S // tm, nd),
        in_specs=[pl.BlockSpec((tm, tc), lambda i, j: (i, g0b + j)),
                  pl.BlockSpec((tm, tc), lambda i, j: (i, g0b + nd + j)),
                  pl.BlockSpec((2, tc), lambda i, j: (0, j)), tile, tile],
        out_specs=tile, out_shape=jax.ShapeDtypeStruct((S, D), BF16), sem=("parallel", "parallel"))


def _gate_bwd(order, proj, bgate, a_out, p_out, dmixed, dproj, S, D, goff):
    tm, tc = _gate_tiles(S, D, goff)
    g0b, nd, ni = goff // tc, D // tc, S // tm
    nsteps = nd * ni

    def body(l0_ref, l1_ref, b_ref, a_ref, p_ref, dm_ref, dproj_in, dap_ref, dproj_ref, db_ref, buf, sems):
        del dproj_in
        j, i = pl.program_id(0), pl.program_id(1)
        step = j * ni + i
        slot = step % 2

        def put(sl, br):
            col = pl.multiple_of((g0b + br * nd + j) * tc, tc)
            return pltpu.make_async_copy(
                buf.at[sl, br], dproj_ref.at[pl.ds(pl.multiple_of(i * tm, tm), tm), pl.ds(col, tc)], sems.at[sl, br])

        @pl.when(step >= 2)
        def _():
            put(slot, 0).wait()
            put(slot, 1).wait()

        @pl.when(i == 0)
        def _():
            db_ref[...] = jnp.zeros_like(db_ref)

        dm = dm_ref[...]
        for br, (l_ref, val_ref) in enumerate(((l0_ref, a_ref), (l1_ref, p_ref))):
            gate = _sigmoid(l_ref[...] + b_ref[br:br + 1, :])
            dap_ref[br] = (dm * gate).astype(BF16)
            dl = dm * val_ref[...] * gate * (1.0 - gate)
            buf[slot, br] = dl.astype(BF16)
            db_ref[br] += jnp.sum(dl, axis=0, keepdims=True)
            put(slot, br).start()

        @pl.when(step == nsteps - 1)
        def _():
            for sl in ((slot, 1 - slot) if nsteps > 1 else (slot,)):
                put(sl, 0).wait()
                put(sl, 1).wait()

    tile = pl.BlockSpec((tm, tc), lambda j, i: (i, j))
    return _call(
        order, body, [proj, proj, bgate, a_out, p_out, dmixed, dproj], name="gate_bwd", grid=(nd, ni),
        in_specs=[pl.BlockSpec((tm, tc), lambda j, i: (i, g0b + j)),
                  pl.BlockSpec((tm, tc), lambda j, i: (i, g0b + nd + j)),
                  pl.BlockSpec((2, tc), lambda j, i: (0, j)), tile, tile, tile, ANY],
        out_specs=[pl.BlockSpec((2, tm, tc), lambda j, i: (0, i, j)), ANY,
                   pl.BlockSpec((2, 1, tc), lambda j, i: (0, 0, j))],
        out_shape=[jax.ShapeDtypeStruct((2, S, D), BF16), jax.ShapeDtypeStruct(dproj.shape, BF16),
                   jax.ShapeDtypeStruct((2, 1, D), F32)],
        scratch=[pltpu.VMEM((2, 2, tm, tc), BF16), pltpu.SemaphoreType.DMA((2, 2))],
        aliases={6: 1}, sem=("arbitrary", "arbitrary"))


def _ffn_in_near(order, h, wT, FF):
    S, D = h.shape
    tm, tn = _pick(S, 2048), _pick(FF, 512)
    nj = FF // tn

    def body(h_ref, w_ref, f_ref):
        f_ref[...] = lax.dot_general(h_ref[...], w_ref[...], _DIMS["nt"], preferred_element_type=F32)

    return _call(
        order, body, [h, wT], name="ffn_in_near", grid=(S // tm, nj),
        in_specs=[pl.BlockSpec((tm, D), lambda i, j: (i, 0)),
                  pl.BlockSpec((tn, D), lambda i, j: (_x_half(False) * nj + j, 0))],
        out_specs=pl.BlockSpec((tm, tn), lambda i, j: (i, j)),
        out_shape=jax.ShapeDtypeStruct((S, FF), F32), sem=("parallel", "parallel"))


def _gate_up(near, far):
    near_is_gate = lax.axis_index("x") == 0
    return jnp.where(near_is_gate, near, far), jnp.where(near_is_gate, far, near)


def _ffn_in_far(order, h, wT, near):
    S, D = h.shape
    FF = near.shape[1]
    tm, tn = _pick(S, 1024), _pick(FF, 512)
    nj = FF // tn

    def body(h_ref, w_ref, near_ref, f_ref, act_ref):
        far = lax.dot_general(h_ref[...], w_ref[...], _DIMS["nt"], preferred_element_type=F32)
        f_ref[...] = far
        gt, up = _gate_up(near_ref[...], far)
        act_ref[...] = (gt * _sigmoid(gt) * up).astype(BF16)

    tile = pl.BlockSpec((tm, tn), lambda i, j: (i, j))
    return _call(
        order, body, [h, wT, near], name="ffn_in_far", grid=(S // tm, nj),
        in_specs=[pl.BlockSpec((tm, D), lambda i, j: (i, 0)),
                  pl.BlockSpec((tn, D), lambda i, j: (_x_half(True) * nj + j, 0)), tile],
        out_specs=[tile, tile],
        out_shape=[jax.ShapeDtypeStruct((S, FF), F32), jax.ShapeDtypeStruct((S, FF), BF16)],
        sem=("parallel", "parallel"))


def _d_act_swiglu(order, dy, wdown, near, far):
    S, D = dy.shape
    FF = wdown.shape[0]
    tm, tn = _pick(S, 1024), _pick(FF, 256)

    def body(dy_ref, w_ref, near_ref, far_ref, o_ref):
        d = lax.dot_general(dy_ref[...], w_ref[...], _DIMS["nt"], preferred_element_type=F32)
        gt, up = _gate_up(near_ref[...], far_ref[...])
        s = _sigmoid(gt)
        o_ref[0] = (d * up * s * (1.0 + gt * (1.0 - s))).astype(BF16)
        o_ref[1] = (d * gt * s).astype(BF16)

    tile = pl.BlockSpec((tm, tn), lambda i, j: (i, j))
    return _call(
        order, body, [dy, wdown, near, far], name="d_act", grid=(S // tm, FF // tn),
        in_specs=[pl.BlockSpec((tm, D), lambda i, j: (i, 0)), pl.BlockSpec((tn, D), lambda i, j: (j, 0)), tile, tile],
        out_specs=pl.BlockSpec((2, tm, tn), lambda i, j: (0, i, j)),
        out_shape=jax.ShapeDtypeStruct((2, S, FF), BF16), sem=("parallel", "parallel"))


def _place():
    return lax.axis_index("x"), lax.axis_index("y"), lax.axis_index("c")


def _xfer_start(order, name, bufs, copies):
    nb = len(bufs)
    n = len(copies([None] * nb, None))
    is_new = [isinstance(b, jax.ShapeDtypeStruct) for b in bufs]
    old = [b for b, fresh in zip(bufs, is_new) if not fresh]
    no = len(old)
    tok = [] if any(order.last is b for b in old) else [order.last]
    first_out = no + len(tok)

    def body(*refs):
        send, recv = refs[first_out:first_out + n], refs[first_out + n:first_out + 2 * n]
        token = refs[-1]
        given, made = iter(refs[:no]), iter(refs[first_out + 2 * n + no:-1])
        logical = [next(made) if fresh else next(given) for fresh in is_new]
        for i, (src, dst, dev) in enumerate(copies(logical, _place())):
            pltpu.make_async_remote_copy(src_ref=src, dst_ref=dst, send_sem=send[i], recv_sem=recv[i],
                                         device_id=dev, device_id_type=MESH).start()
        token[...] = jnp.zeros_like(token)

    fresh_shapes = [b for b, fresh in zip(bufs, is_new) if fresh]
    out = pl.pallas_call(
        body, name=name,
        out_shape=tuple([pltpu.SemaphoreType.DMA(())] * (2 * n)
                        + [pltpu.HBM(b.shape, b.dtype) for b in old + fresh_shapes]
                        + [jax.ShapeDtypeStruct((8, LANES), F32)]),
        in_specs=[HBM] * no + [ANY] * len(tok),
        out_specs=tuple([SEM] * (2 * n) + [HBM] * nb + [pl.BlockSpec(memory_space=pltpu.VMEM)]),
        input_output_aliases={i: 2 * n + i for i in range(no)},
        compiler_params=pltpu.CompilerParams(has_side_effects=EFFECT),
    )(*[pltpu.with_memory_space_constraint(b, pltpu.HBM) for b in old], *tok)
    order.last = out[-1]
    thru, made = iter(out[2 * n:2 * n + no]), iter(out[2 * n + no:2 * n + nb])
    return list(out[:2 * n]), [next(made) if fresh else next(thru) for fresh in is_new]


def _xfer_wait(order, name, sems, bufs, copies):
    nb = len(bufs)
    n = len(sems) // 2
    tok = order.last

    def body(*refs):
        send, recv = refs[nb:nb + n], refs[nb + n:nb + 2 * n]
        token = refs[-1]
        for i, (src, dst, dev) in enumerate(copies(refs[:nb], _place())):
            cp = pltpu.make_async_remote_copy(src_ref=src, dst_ref=dst, send_sem=send[i], recv_sem=recv[i],
                                              device_id=dev, device_id_type=MESH)
            cp.wait_send()
            cp.wait_recv()
        token[...] = jnp.zeros_like(token)

    out = pl.pallas_call(
        body, name=name,
        out_shape=tuple([pltpu.HBM(b.shape, b.dtype) for b in bufs] + [jax.ShapeDtypeStruct((8, LANES), F32)]),
        in_specs=[HBM] * nb + [SEM] * (2 * n) + [ANY],
        out_specs=tuple([HBM] * nb + [pl.BlockSpec(memory_space=pltpu.VMEM)]),
        input_output_aliases={i: i for i in range(nb)},
        compiler_params=pltpu.CompilerParams(has_side_effects=EFFECT),
    )(*bufs, *sems, tok)
    order.last = out[-1]
    return list(out[:nb])


class _Xfer:
    def __init__(self, name, bufs, copies):
        self.name, self.bufs, self.copies = name, list(bufs), copies
        self.sems = None

    def start(self, order):
        self.sems, self.bufs = _xfer_start(order, self.name + "_start", self.bufs, self.copies)

    def wait(self, order, bufs=None):
        self.bufs = _xfer_wait(order, self.name + "_wait", self.sems, bufs or self.bufs, self.copies)
        return self.bufs


def _block_rows(ref, r, d):
    return ref.at[pl.ds(d * r, r)]


NEAR = ("xn", "yn")
ALL_CHIPS = ("xn", "yn", "diag")


def _chip_of(which, x, y):
    return {"xn": (1 - x, y), "yn": (x, 1 - y), "diag": (1 - x, 1 - y)}[which]


def _gather_send(fulls, chips=ALL_CHIPS, sibling=True):
    def copies(refs, place):
        out = []
        for w, full in enumerate(fulls):
            r = full.shape[0] // 8
            if place is None:
                out += [None] * (len(chips) + int(sibling))
                continue
            x, y, c = place
            mine = _block_rows(refs[w], r, 4 * x + 2 * y + c)
            if sibling:
                out.append((mine, mine, (x, y, 1 - c)))
            for which in chips:
                out.append((mine, mine, (*_chip_of(which, x, y), c)))
        return out
    return copies


def _gather_forward(fulls, chips=ALL_CHIPS):
    def copies(refs, place):
        out = []
        for w, full in enumerate(fulls):
            r = full.shape[0] // 8
            if place is None:
                out += [None] * len(chips)
                continue
            x, y, c = place
            for which in chips:
                px, py = _chip_of(which, x, y)
                blk = _block_rows(refs[w], r, 4 * px + 2 * py + c)
                out.append((blk, blk, (x, y, 1 - c)))
        return out
    return copies


def _pair_send(nw):
    def copies(refs, place):
        out = []
        for w in range(nw):
            if place is None:
                out += [None] * 4
                continue
            x, y, c = place
            grad, other = refs[2 * w], refs[2 * w + 1]
            r = other.shape[1]
            for k in range(4):
                out.append((_block_rows(grad, r, 2 * k + 1 - c), other.at[k], (x, y, 1 - c)))
        return out
    return copies


def _chip_send(nw):
    def copies(refs, place):
        out = []
        for w in range(nw):
            if place is None:
                out += [None] * 3
                continue
            x, y, c = place
            psum, parts = refs[2 * w], refs[2 * w + 1]
            for px, py in ((1 - x, y), (x, 1 - y), (1 - x, 1 - y)):
                out.append((psum.at[2 * px + py], parts.at[2 * x + y], (px, py, c)))
        return out
    return copies


def _dev_index():
    x, y, c = _place()
    return 4 * x + 2 * y + c


def _place_own(order, shard, name):
    r, cols = shard.shape
    tr = _pick(r, max(16, (12 << 20) // (4 * cols)), 16)
    nr = r // tr

    def body(s_ref, o_ref):
        o_ref[...] = s_ref[...].astype(BF16)

    return _call(
        order, body, [shard], name=name, grid=(nr,),
        in_specs=[pl.BlockSpec((tr, cols), lambda i: (i, 0))],
        out_specs=pl.BlockSpec((tr, cols), lambda i: (_dev_index() * nr + i, 0)),
        out_shape=jax.ShapeDtypeStruct((8 * r, cols), BF16), sem=("parallel",))


def _pair_sum(order, grad, other, name):
    r, cols = other.shape[1:]
    tr = _pick(r, max(16, (7 << 20) // (2 * cols)), 16)
    nr = r // tr

    def body(g_ref, a_ref, o_ref):
        o_ref[...] = (g_ref[...].astype(F32) + a_ref[...].astype(F32)).astype(BF16)

    blk = pl.BlockSpec((None, tr, cols), lambda k, i: (k, i, 0))
    return _call(
        order, body, [grad, other], name=name, grid=(4, nr),
        in_specs=[pl.BlockSpec((tr, cols), lambda k, i: ((2 * k + lax.axis_index("c")) * nr + i, 0)), blk],
        out_specs=blk, out_shape=jax.ShapeDtypeStruct(other.shape, BF16), sem=("parallel", "parallel"))


def _chip_sum(order, psum, parts, name):
    _, r, cols = parts.shape
    tr = _pick(r, max(16, (1 << 20) // (2 * cols)), 16)

    def my_chip():
        return 2 * lax.axis_index("x") + lax.axis_index("y")

    def body(own_ref, p0, p1, p2, p3, o_ref):
        own = own_ref[...].astype(F32)
        acc = None
        for k, p in enumerate((p0, p1, p2, p3)):
            term = jnp.where(my_chip() == k, own, p[...].astype(F32))
            acc = term if acc is None else acc + term
        o_ref[...] = acc

    def slot(k):
        return pl.BlockSpec((None, tr, cols), lambda i: (jnp.where(my_chip() == k, (k + 1) % 4, k), i, 0))

    return _call(
        order, body, [psum, parts, parts, parts, parts], name=name, grid=(r // tr,),
        in_specs=[pl.BlockSpec((None, tr, cols), lambda i: (my_chip(), i, 0))] + [slot(k) for k in range(4)],
        out_specs=pl.BlockSpec((tr, cols), lambda i: (i, 0)),
        out_shape=jax.ShapeDtypeStruct((r, cols), F32), sem=("parallel",))


def _all_reduce_small(order, pack, name):
    R = pack.shape[0]

    def body(p_ref, o_ref, buf, send_sems, recv_sems):
        x, y, c = _place()
        me = 4 * x + 2 * y + c
        buf[me] = p_ref[...]
        copies = []
        for k in range(1, 8):
            px = 1 - x if k & 4 else x
            py = 1 - y if k & 2 else y
            pc = 1 - c if k & 1 else c
            cp = pltpu.make_async_remote_copy(
                src_ref=p_ref, dst_ref=buf.at[me], send_sem=send_sems.at[k - 1], recv_sem=recv_sems.at[k - 1],
                device_id=(px, py, pc), device_id_type=MESH)
            cp.start()
            copies.append(cp)
        for cp in copies:
            cp.wait_recv()
        acc = buf[0]
        for d in range(1, 8):
            acc = acc + buf[d]
        o_ref[...] = acc
        for cp in copies:
            cp.wait_send()

    vm = pl.BlockSpec(memory_space=pltpu.VMEM)
    return _call(
        order, body, [pack], name=name, in_specs=[vm], out_specs=vm,
        out_shape=jax.ShapeDtypeStruct((R, LANES), F32),
        scratch=[pltpu.VMEM((8, R, LANES), F32), pltpu.SemaphoreType.DMA((7,)), pltpu.SemaphoreType.DMA((7,))])


def _pack(parts):
    flat = []
    for p in parts:
        v = p.reshape(-1).astype(F32)
        flat.append(jnp.pad(v, (0, (-v.shape[0]) % LANES)))
    v = jnp.concatenate(flat)
    v = jnp.pad(v, (0, (-v.shape[0]) % (8 * LANES)))
    return v.reshape(-1, LANES)


def _unpack(pack, shapes):
    v = pack.reshape(-1)
    out, off = [], 0
    for s in shapes:
        n = 1
        for d in s:
            n *= d
        out.append(v[off:off + n].reshape(s))
        off += n + (-n) % LANES
    return out


def _adamw(order, w, g, m, v, name):
    shape = w.shape
    cols = shape[-1]
    w2, g2, m2, v2 = (t.reshape(-1, cols) for t in (w, g, m, v))
    R = w2.shape[0]
    tr = _pick(R, max(8, (1 << 20) // (4 * cols)), 8)

    def body(w_ref, g_ref, m_ref, v_ref, d_ref, mo_ref, vo_ref):
        d_ref[...], mo_ref[...], vo_ref[...] = _adam_math(w_ref[...], g_ref[...], m_ref[...], v_ref[...])

    blk = pl.BlockSpec((tr, cols), lambda i: (i, 0))
    outs = _call(
        order, body, [w2, g2, m2, v2], name=name, grid=(R // tr,), in_specs=[blk] * 4, out_specs=[blk] * 3,
        out_shape=[jax.ShapeDtypeStruct((R, cols), F32)] * 3, sem=("parallel",))
    return tuple(o.reshape(shape) for o in outs)


def _adam_math(w, g, m, v):
    mn = ADAM_B1 * m + (1.0 - ADAM_B1) * g
    vn = ADAM_B2 * v + (1.0 - ADAM_B2) * (g * g)
    m_hat = mn / (1.0 - ADAM_B1 ** ADAM_STEP)
    v_hat = vn / (1.0 - ADAM_B2 ** ADAM_STEP)
    return -ADAM_LR * (m_hat / (jnp.sqrt(v_hat) + ADAM_EPS) + ADAM_WD * w), mn, vn


def _chip_sum_adamw(order, w, psum, parts, m, v, name):
    _, r, cols = parts.shape
    tr = _pick(r, max(16, (24 << 20) // (38 * cols)), 16)

    def my_chip():
        return 2 * lax.axis_index("x") + lax.axis_index("y")

    def body(w_ref, own_ref, p0, p1, p2, p3, m_ref, v_ref, g_ref, d_ref, mo_ref, vo_ref):
        own = own_ref[...].astype(F32)
        g = None
        for k, p in enumerate((p0, p1, p2, p3)):
            term = jnp.where(my_chip() == k, own, p[...].astype(F32))
            g = term if g is None else g + term
        g_ref[...] = g
        d_ref[...], mo_ref[...], vo_ref[...] = _adam_math(w_ref[...], g, m_ref[...], v_ref[...])

    def slot(k):
        return pl.BlockSpec((None, tr, cols), lambda i: (jnp.where(my_chip() == k, (k + 1) % 4, k), i, 0))

    blk = pl.BlockSpec((tr, cols), lambda i: (i, 0))
    return _call(
        order, body, [w, psum, parts, parts, parts, parts, m, v], name=name, grid=(r // tr,),
        in_specs=[blk, pl.BlockSpec((None, tr, cols), lambda i: (my_chip(), i, 0))]
        + [slot(k) for k in range(4)] + [blk, blk],
        out_specs=[blk] * 4, out_shape=[jax.ShapeDtypeStruct((r, cols), F32)] * 4, sem=("parallel",))


class _GradReduce:
    def __init__(self, tag, grads, names):
        self.tag, self.grads, self.names = tag, list(grads), names
        self.pair = self.chip = self.psums = None

    def pair_start(self, order):
        bufs = []
        for g in self.grads:
            bufs += [g, jax.ShapeDtypeStruct((4, g.shape[0] // 8, g.shape[1]), g.dtype)]
        self.pair = _Xfer("pair_" + self.tag, bufs, _pair_send(len(self.grads)))
        self.pair.start(order)

    def pair_sum_chip_start(self, order):
        bufs = self.pair.wait(order)
        self.psums = [_pair_sum(order, bufs[2 * w], bufs[2 * w + 1], "pair_sum_" + nm)
                      for w, nm in enumerate(self.names)]
        cbufs = []
        for p in self.psums:
            cbufs += [p, jax.ShapeDtypeStruct(p.shape, p.dtype)]
        self.chip = _Xfer("chip_" + self.tag, cbufs, _chip_send(len(self.psums)))
        self.chip.start(order)

    def finish(self, order):
        bufs = self.chip.wait(order)
        return [(bufs[2 * w], bufs[2 * w + 1]) for w in range(len(self.names))]


def kernel(x, meta_tokens, ln_in_g, ln_in_b, w_in, b_gate, attn_sinks, w_attn_up, w_pool_grp, pool_scale, w_pool_up, w_out, ln1_g, ln1_b, w_ffn_in, w_ffn_down, ln2_g, ln2_b, loss_target, m_meta_tokens, m_ln_in_g, m_ln_in_b, m_w_in, m_b_gate, m_attn_sinks, m_w_attn_up, m_w_pool_grp, m_pool_scale, m_w_pool_up, m_w_out, m_ln1_g, m_ln1_b, m_w_ffn_in, m_w_ffn_down, m_ln2_g, m_ln2_b, v_meta_tokens, v_ln_in_g, v_ln_in_b, v_w_in, v_b_gate, v_attn_sinks, v_w_attn_up, v_w_pool_grp, v_pool_scale, v_w_pool_up, v_w_out, v_ln1_g, v_ln1_b, v_w_ffn_in, v_w_ffn_down, v_ln2_g, v_ln2_b):
    S, D = x.shape[1], x.shape[2]
    Tp = S + BLOCK
    NQ = attn_sinks.shape[-1]
    ATTN = NQ * HEAD_DIM
    KVW = ATTN // Q_PER_KV
    POOL = pool_scale.shape[-1]
    IN = 8 * w_in.shape[2]
    FF = 8 * w_ffn_down.shape[1]
    uoff = ATTN + 2 * KVW
    goff = uoff + POOL
    gw = POOL // 4
    dcols = D // 8
    assert IN == goff + 2 * D and w_ffn_in.shape[2] * 8 == 2 * FF

    xi, yi, ci = _place()
    dev = 4 * xi + 2 * yi + ci
    x2, tgt = x[0], loss_target[0]
    order = _Order()

    def place_cols(a):
        return lax.dynamic_update_slice(jnp.zeros(a.shape[:-1] + (D,), F32), a, (0,) * (a.ndim - 1) + (dev * dcols,))

    small = _all_reduce_small(order, _pack([place_cols(meta_tokens), place_cols(b_gate[0])]), "small_inputs_gather")
    meta_full, bgate_full = _unpack(small, [(N_META, D), (2, D)])
    meta_pad = jnp.pad(meta_full, ((META_ROW0, 0), (0, 0)))

    wgrp_rows = w_pool_grp[0].reshape(4 * (gw // 8), gw)
    full_in = _place_own(order, w_in[0].T, "own_w_in")
    g_in = _Xfer("gather_w_in_near", [full_in], _gather_send([full_in], NEAR, sibling=False))
    g_in.start(order)
    s_in = _Xfer("gather_w_in_sibling", g_in.bufs, _gather_send([full_in], (), sibling=True))
    s_in.start(order)
    mix_names = ["w_attn_up", "w_pool_grp", "w_pool_up", "w_out"]
    mix_shards = [w_attn_up[0].T, wgrp_rows, w_pool_up[0].T, w_out[0]]
    full_mix = [_place_own(order, s, "own_" + nm) for s, nm in zip(mix_shards, mix_names)]
    full_ffn = _place_own(order, w_ffn_in[0].T, "own_w_ffn_in")

    ln_in_g2, ln_in_b2 = ln_in_g.reshape(1, D), ln_in_b.reshape(1, D)
    tab = _rope_table(S)

    h0, h0b = _ln_in_fwd(order, x2, meta_pad, ln_in_g2, ln_in_b2)
    bufs = s_in.wait(order)
    proj = _mm_nt_half(order, h0b, bufs[0], parts=4, which=lambda: _chip_quarter(False), tm=704, tn=640,
                       name="proj_own")
    bufs = g_in.wait(order, bufs)
    d_in = _Xfer("gather_w_in_diag", bufs, _gather_send(bufs, ("diag",), sibling=False))
    d_in.start(order)
    f_in = _Xfer("forward_w_in_near", d_in.bufs, _gather_forward(bufs, NEAR))
    f_in.start(order)
    full_down = _place_own(order, w_ffn_down[0], "own_w_ffn_down")
    bufs = f_in.wait(order)
    proj = _mm_nt_half(order, h0b, bufs[0], parts=4, which=lambda: _chip_quarter(True), tm=704, tn=640,
                       name="proj_yn", into=proj)
    bufs = d_in.wait(order, bufs)
    fd_in = _Xfer("forward_w_in_diag", bufs, _gather_forward(bufs, ("diag",)))
    fd_in.start(order)
    ag_mix = _Xfer("gather_mixers", full_mix, _gather_send(full_mix))
    ag_mix.start(order)
    g_ffn = _Xfer("gather_w_ffn_in_near", [full_ffn], _gather_send([full_ffn], NEAR))
    g_ffn.start(order)
    (winT,) = fd_in.wait(order)
    proj = _mm_nt_half(order, h0b, winT, far=True, tm=704, tn=1280, name="proj_far", into=proj)

    att = _attn_fwd(order, proj, tab, attn_sinks, S, ATTN, KVW)
    full_mix = ag_mix.wait(order)
    fw_mix = _Xfer("forward_mixers", full_mix, _gather_forward(full_mix))
    fw_mix.start(order)
    wattT, wgrp_g, wpupT, wout = fw_mix.wait(order)
    wgrp = wgrp_g.reshape(8, 4, gw // 8, gw).transpose(1, 0, 2, 3).reshape(4, gw, gw)

    ps = _pool_fwd(order, proj, wgrp, pool_scale, S, uoff, POOL)
    a_out = _mm(order, att, wattT, kind="nt", out_dtype=F32, tm=1024, tn=1024, name="attn_up")
    p_out = _mm(order, ps, wpupT, kind="nt", out_dtype=F32, tm=1024, tn=1024, name="pool_up")
    mixed = _gate_mix(order, proj, bgate_full, a_out, p_out, S, D, goff)
    y1 = _mm(order, mixed, wout, kind="nn", out_dtype=F32, tm=1024, tn=1024, name="out_proj")

    bufs = g_ffn.wait(order)
    d_ffn = _Xfer("gather_w_ffn_in_diag", bufs, _gather_send(bufs, ("diag",), sibling=False))
    d_ffn.start(order)
    f_ffn = _Xfer("forward_w_ffn_in_near", d_ffn.bufs, _gather_forward(bufs, NEAR))
    f_ffn.start(order)
    h1, h1b = _ln1_fwd(order, h0, y1, ln1_g, ln1_b)
    bufs = f_ffn.wait(order)
    f_near = _ffn_in_near(order, h1b, bufs[0], FF)
    bufs = d_ffn.wait(order, bufs)
    fd_ffn = _Xfer("forward_w_ffn_in_diag", bufs, _gather_forward(bufs, ("diag",)))
    fd_ffn.start(order)
    ag_down = _Xfer("gather_w_ffn_down", [full_down], _gather_send([full_down]))
    ag_down.start(order)
    (wffnT,) = fd_ffn.wait(order)
    f_far, act = _ffn_in_far(order, h1b, wffnT, f_near)

    (full_down,) = ag_down.wait(order)
    fw_down = _Xfer("forward_w_ffn_down", [full_down], _gather_forward([full_down]))
    fw_down.start(order)
    (wdown,) = fw_down.wait(order)
    y2 = _mm(order, act, wdown, kind="nn", out_dtype=F32, tm=1024, tn=1024, tk=5504, name="ffn_down")

    dz2, dz2b, dg2, db2, loss_part = _ln2_loss_bwd(order, h1, y2, tgt, ln2_g, ln2_b)
    df = _d_act_swiglu(order, dz2b, wdown, f_near, f_far)
    gwdown = _mm(order, act, dz2b, kind="tn", out_dtype=BF16, tm=256, tn=2048, name="d_ffn_down")
    rs_down = _GradReduce("w_ffn_down", [gwdown], ["w_ffn_down"])
    rs_down.pair_start(order)
    gwffnT = _mm(order, df, h1b, kind="tn", out_dtype=BF16, tm=256, tn=2048, name="d_ffn_in", a_lead="halves")
    rs_down.pair_sum_chip_start(order)
    rs_ffn = _GradReduce("w_ffn_in", [gwffnT], ["w_ffn_in"])
    rs_ffn.pair_start(order)
    dh1 = _mm(order, df, wffnT, kind="nn", out_dtype=F32, tm=1024, tn=1024, tk=5504, name="d_h1", a_lead="halves")
    rs_ffn.pair_sum_chip_start(order)
    dz1, dz1b, dg1, db1 = _ln1_bwd(order, h0, y1, ln1_g, dh1, dz2)
    gwout = _mm(order, mixed, dz1b, kind="tn", out_dtype=BF16, tm=512, tn=1024, name="d_out_proj")
    rs_out = _GradReduce("w_out", [gwout], ["w_out"])
    rs_out.pair_start(order)
    dmixed = _mm(order, dz1b, wout, kind="nt", out_dtype=F32, tm=1024, tn=1024, name="d_mixed")
    rs_out.pair_sum_chip_start(order)

    dproj = _zero_meta_block(order, Tp, IN)
    dap, dproj, dbgate = _gate_bwd(order, proj, bgate_full, a_out, p_out, dmixed, dproj, S, D, goff)
    gwattT = _mm(order, dap, att, kind="tn", out_dtype=BF16, tm=512, tn=1024, name="d_attn_up", a_lead=0)
    datt = _mm(order, dap, wattT, kind="nn", out_dtype=BF16, tm=1024, tn=1024, name="d_att", a_lead=0)
    gwpupT = _mm(order, dap, ps, kind="tn", out_dtype=BF16, tm=512, tn=1024, name="d_pool_up", a_lead=1)
    dps = _mm(order, dap, wpupT, kind="nn", out_dtype=F32, tm=1024, tn=1024, name="d_ps", a_lead=1)
    dpl, gwgrp, dscale = _pool_bwd_mix(order, proj, wgrp, pool_scale, dps, S, uoff, POOL)
    gwgrp_rows = gwgrp.reshape(4, 8, gw // 8, gw).transpose(1, 0, 2, 3).reshape(8 * 4 * (gw // 8), gw).astype(BF16)
    rs_mix = _GradReduce("mixers", [gwattT, gwgrp_rows, gwpupT], ["w_attn_up", "w_pool_grp", "w_pool_up"])
    rs_mix.pair_start(order)
    dproj = _pool_bwd_window(order, dpl, dproj, S, uoff, POOL)
    rs_mix.pair_sum_chip_start(order)
    dproj, dk, dv, dsink = _attn_bwd(order, proj, tab, attn_sinks, datt, dproj, S, ATTN, KVW)
    dproj = _put_dkv(order, dk, dv, dproj, ATTN)

    weights = dict(meta_tokens=meta_tokens, ln_in_g=ln_in_g, ln_in_b=ln_in_b, w_in=w_in, b_gate=b_gate,
                   attn_sinks=attn_sinks, w_attn_up=w_attn_up, w_pool_grp=w_pool_grp, pool_scale=pool_scale,
                   w_pool_up=w_pool_up, w_out=w_out, ln1_g=ln1_g, ln1_b=ln1_b, w_ffn_in=w_ffn_in,
                   w_ffn_down=w_ffn_down, ln2_g=ln2_g, ln2_b=ln2_b)
    ms = dict(meta_tokens=m_meta_tokens, ln_in_g=m_ln_in_g, ln_in_b=m_ln_in_b, w_in=m_w_in, b_gate=m_b_gate,
              attn_sinks=m_attn_sinks, w_attn_up=m_w_attn_up, w_pool_grp=m_w_pool_grp, pool_scale=m_pool_scale,
              w_pool_up=m_w_pool_up, w_out=m_w_out, ln1_g=m_ln1_g, ln1_b=m_ln1_b, w_ffn_in=m_w_ffn_in,
              w_ffn_down=m_w_ffn_down, ln2_g=m_ln2_g, ln2_b=m_ln2_b)
    vs = dict(meta_tokens=v_meta_tokens, ln_in_g=v_ln_in_g, ln_in_b=v_ln_in_b, w_in=v_w_in, b_gate=v_b_gate,
              attn_sinks=v_attn_sinks, w_attn_up=v_w_attn_up, w_pool_grp=v_w_pool_grp, pool_scale=v_pool_scale,
              w_pool_up=v_w_pool_up, w_out=v_w_out, ln1_g=v_ln1_g, ln1_b=v_ln1_b, w_ffn_in=v_w_ffn_in,
              w_ffn_down=v_w_ffn_down, ln2_g=v_ln2_g, ln2_b=v_ln2_b)
    grads, deltas, new_ms, new_vs = {}, {}, {}, {}

    def update(nm, g):
        g = g.reshape(weights[nm].shape)
        grads[nm] = g
        deltas[nm], new_ms[nm], new_vs[nm] = _adamw(order, weights[nm], g, ms[nm], vs[nm], "adamw_" + nm)

    def update_reduced(nm, bufs, transposed=False):
        psum, parts = bufs
        if transposed:
            to2d, back = (lambda t: t[0].T), (lambda t: t.T[None])
        else:
            to2d, back = (lambda t: t.reshape(parts.shape[1:])), (lambda t: t.reshape(weights[nm].shape))
        outs = _chip_sum_adamw(order, to2d(weights[nm]), psum, parts, to2d(ms[nm]), to2d(vs[nm]), "adamw_" + nm)
        grads[nm], deltas[nm], new_ms[nm], new_vs[nm] = (back(t) for t in outs)

    gwinT = _mm(order, dproj, h0b, kind="tn", out_dtype=BF16, tm=512, tn=1024, name="d_w_in")
    rs_in = _GradReduce("w_in", [gwinT], ["w_in"])
    rs_in.pair_start(order)
    update_reduced("w_ffn_down", rs_down.finish(order)[0])
    rs_in.pair_sum_chip_start(order)
    dh0 = _mm(order, dproj, winT, kind="nn", out_dtype=F32, tm=1408, tn=1024, tk=2560, name="d_h0")
    dx, dmeta_block, dg_in, db_in = _ln_in_bwd(order, x2, meta_pad, ln_in_g2, dh0, dz1)
    grad_x = dx[None]
    dmeta = dmeta_block[META_ROW0:]

    small_shapes = [(D,), (D,), (1, D), (1, D), (1, D), (1, D), (1, POOL), (1, NQ), (), (N_META, D), (2, D)]
    red = _all_reduce_small(order, _pack([dg_in, db_in, dg1, db1, dg2, db2, dscale, dsink[:, :, 0], loss_part,
                                          dmeta, dbgate]), "small_grads_all_reduce")

    update_reduced("w_ffn_in", rs_ffn.finish(order)[0], transposed=True)
    update_reduced("w_out", rs_out.finish(order)[0])
    b_att, b_grp, b_pup = rs_mix.finish(order)
    update("w_attn_up", _chip_sum(order, *b_att, "chip_sum_w_attn_up").T)
    update_reduced("w_pool_grp", b_grp)
    update("w_pool_up", _chip_sum(order, *b_pup, "chip_sum_w_pool_up").T)

    (g_ln_in_g, g_ln_in_b, g_ln1_g, g_ln1_b, g_ln2_g, g_ln2_b, g_scale, g_sinks, loss_sum, g_meta_full,
     g_bgate_full) = _unpack(red, small_shapes)
    loss = 0.5 * loss_sum
    update("meta_tokens", lax.dynamic_slice(g_meta_full, (0, dev * dcols), (N_META, dcols)))
    update("b_gate", lax.dynamic_slice(g_bgate_full, (0, dev * dcols), (2, dcols)))
    for nm, g in (("ln_in_g", g_ln_in_g), ("ln_in_b", g_ln_in_b), ("ln1_g", g_ln1_g), ("ln1_b", g_ln1_b),
                  ("ln2_g", g_ln2_g), ("ln2_b", g_ln2_b), ("pool_scale", g_scale), ("attn_sinks", g_sinks)):
        update(nm, g)

    update_reduced("w_in", rs_in.finish(order)[0], transposed=True)

    names = list(weights)
    return (loss, grad_x, *[grads[n] for n in names], *[deltas[n] for n in names],
            *[new_ms[n] for n in names], *[new_vs[n] for n in names])
```

```python
import jax
import jax.numpy as jnp
from jax import lax
from jax.experimental import pallas as pl
from jax.experimental.pallas import tpu as pltpu

F32 = jnp.float32
BF16 = jnp.bfloat16
MESH = pl.DeviceIdType.MESH

N_META = 16
HEAD_DIM = 64
Q_PER_KV = 8
WINDOW = 128
BLOCK = 128
ATTN_SCALE = HEAD_DIM ** -0.5
ROPE_DIM = HEAD_DIM // 4
ROPE_THETA = 500000.0
NEG_INF = -1e30
POOL_WINDOWS = (2, 4, 8, 16)
LN_EPS = 1e-5
DN_ALPHA = 2.0 ** 0.25
ADAM_LR = 0.001
ADAM_B1 = 0.9
ADAM_B2 = 0.999
ADAM_EPS = 1e-08
ADAM_WD = 0.01
ADAM_STEP = 10

LANES = 128
META_ROW0 = BLOCK - N_META
VMEM_LIMIT = 56 * 1024 * 1024

ANY = pl.BlockSpec(memory_space=pl.ANY)
HBM = pl.BlockSpec(memory_space=pltpu.HBM)
SEM = pl.BlockSpec(memory_space=pltpu.SEMAPHORE)
EFFECT = pltpu.SideEffectType.DATAFLOW_SIDE_EFFECTING


def _params(sem=None, **kw):
    return pltpu.CompilerParams(dimension_semantics=sem, vmem_limit_bytes=VMEM_LIMIT, **kw)


class _Order:
    def __init__(self):
        self.last = None


def _call(order, body, operands, *, name, in_specs, out_specs, out_shape, grid=(), scratch=(), sem=None,
          aliases=None, prefetch=()):
    n_in, npf = len(operands), len(prefetch)
    tok = order.last
    if tok is not None and any(tok is op for op in operands):
        tok = None

    def wrapped(*refs):
        refs = list(refs)
        if tok is not None:
            del refs[npf + n_in]
        body(*refs)

    specs = list(in_specs) + ([ANY] if tok is not None else [])
    ops = list(operands) + ([tok] if tok is not None else [])
    if npf:
        out = pl.pallas_call(
            wrapped, name=name, out_shape=out_shape, compiler_params=_params(sem),
            grid_spec=pltpu.PrefetchScalarGridSpec(num_scalar_prefetch=npf, grid=grid, in_specs=specs,
                                                   out_specs=out_specs, scratch_shapes=list(scratch)),
        )(*prefetch, *ops)
    else:
        out = pl.pallas_call(
            wrapped, name=name, grid=grid, in_specs=specs, out_specs=out_specs, out_shape=out_shape,
            scratch_shapes=list(scratch), input_output_aliases=aliases or {}, compiler_params=_params(sem),
        )(*ops)
    order.last = out[0] if isinstance(out, (list, tuple)) else out
    return out


def _pick(dim, pref, mult=LANES):
    best = None
    t = mult
    while t <= min(dim, pref):
        if dim % t == 0:
            best = t
        t += mult
    return dim if best is None else best


_DIMS = {"nn": (((1,), (0,)), ((), ())), "nt": (((1,), (1,)), ((), ())), "tn": (((0,), (0,)), ((), ()))}


def _mm(order, a, b, *, kind, out_dtype, tm, tn, tk=None, name, a_lead=None):
    a2 = a.shape[-2:]
    halves = a_lead == "halves"
    if halves:
        a2 = (a2[0], 2 * a2[1])
    if kind == "tn":
        K, M = a2
    else:
        M, K = a2
    N = b.shape[0] if kind == "nt" else b.shape[1]
    half_cols = a2[1] // 2
    tm = _pick(half_cols if halves and kind == "tn" else M, tm)
    tn = _pick(N, tn)
    tk = K if tk is None else _pick(half_cols if halves and kind != "tn" else K, tk)
    nm, nn_, nk = M // tm, N // tn, K // tk
    a_bytes = M * K * a.dtype.itemsize
    b_bytes = N * K * b.dtype.itemsize
    i_outer = (a_bytes + nm * b_bytes <= b_bytes + nn_ * a_bytes) if nk == 1 else True

    def ij(g0, g1):
        return (g0, g1) if i_outer else (g1, g0)

    def a_map(g0, g1, k):
        i, _ = ij(g0, g1)
        if halves:
            per = half_cols // (tm if kind == "tn" else tk)
            return (i // per, k, i % per) if kind == "tn" else (k // per, i, k % per)
        idx = (k, i) if kind == "tn" else (i, k)
        return idx if a_lead is None else (a_lead,) + idx

    def b_map(g0, g1, k):
        _, j = ij(g0, g1)
        return (j, k) if kind == "nt" else (k, j)

    def o_map(g0, g1, k):
        return ij(g0, g1)

    a_blk = (tk, tm) if kind == "tn" else (tm, tk)
    if a_lead is not None:
        a_blk = (None,) + a_blk
    b_blk = (tn, tk) if kind == "nt" else (tk, tn)

    in_place = out_dtype == F32

    def body(a_ref, b_ref, o_ref, *acc):
        p = lax.dot_general(a_ref[...], b_ref[...], _DIMS[kind], preferred_element_type=F32)
        if nk == 1:
            o_ref[...] = p.astype(o_ref.dtype)
        else:
            k = pl.program_id(2)
            acc_ref = o_ref if in_place else acc[0]

            @pl.when(k == 0)
            def _():
                acc_ref[...] = p

            @pl.when(k > 0)
            def _():
                acc_ref[...] += p

            if not in_place:
                @pl.when(k == nk - 1)
                def _():
                    o_ref[...] = acc_ref[...].astype(o_ref.dtype)

    grid = (nm, nn_, nk) if i_outer else (nn_, nm, nk)
    return _call(
        order, body, [a, b], name=name, grid=grid,
        in_specs=[pl.BlockSpec(a_blk, a_map), pl.BlockSpec(b_blk, b_map)],
        out_specs=pl.BlockSpec((tm, tn), o_map),
        out_shape=jax.ShapeDtypeStruct((M, N), out_dtype),
        scratch=[] if nk == 1 or in_place else [pltpu.VMEM((tm, tn), F32)],
        sem=("parallel", "parallel", "arbitrary"))


def _x_half(far):
    x = lax.axis_index("x")
    return 1 - x if far else x


def _chip_quarter(y_neighbour):
    x, y = lax.axis_index("x"), lax.axis_index("y")
    return 2 * x + (1 - y if y_neighbour else y)


def _mm_nt_half(order, a, bT, *, tm, tn, name, far=False, into=None, parts=2, which=None):
    M, K = a.shape
    N = bT.shape[0]
    tm, tn = _pick(M, tm), _pick(N // parts, tn)
    nh = N // parts // tn
    part = which if which is not None else (lambda: _x_half(far))

    def body(a_ref, b_ref, *rest):
        rest[-1][...] = lax.dot_general(a_ref[...], b_ref[...], _DIMS["nt"], preferred_element_type=F32)

    return _call(
        order, body, [a, bT] + ([] if into is None else [into]), name=name, grid=(M // tm, nh),
        in_specs=[pl.BlockSpec((tm, K), lambda i, j: (i, 0)),
                  pl.BlockSpec((tn, K), lambda i, j: (part() * nh + j, 0))] + ([] if into is None else [ANY]),
        out_specs=pl.BlockSpec((tm, tn), lambda i, j: (i, part() * nh + j)),
        out_shape=jax.ShapeDtypeStruct((M, N), F32), aliases=None if into is None else {2: 0},
        sem=("parallel", "parallel"))


def _ln_stats(z):
    mu = jnp.mean(z, axis=-1, keepdims=True)
    zc = z - mu
    var = jnp.mean(zc * zc, axis=-1, keepdims=True)
    rstd = lax.rsqrt(var + LN_EPS)
    return zc * rstd, rstd


def _ln_bwd(dy, xhat, rstd, g):
    dxh = dy * g
    m1 = jnp.mean(dxh, axis=-1, keepdims=True)
    m2 = jnp.mean(dxh * xhat, axis=-1, keepdims=True)
    return rstd * (dxh - m1 - xhat * m2)


def _ln_in_fwd(order, x, meta_pad, g, b):
    S, D = x.shape
    nb = S // BLOCK

    def body(x_ref, mp_ref, g_ref, b_ref, h_ref, hb_ref):
        is_meta = pl.program_id(0) == nb
        xin = jnp.where(is_meta, mp_ref[...], x_ref[...])
        xhat, _ = _ln_stats(xin)
        y = xhat * g_ref[...] + b_ref[...]
        h_ref[...] = y
        hb_ref[...] = y.astype(BF16)

    row = pl.BlockSpec((BLOCK, D), lambda i: (i, 0))
    vec = pl.BlockSpec((1, D), lambda i: (0, 0))
    return _call(
        order, body, [x, meta_pad, g, b], name="ln_in_fwd", grid=(nb + 1,),
        in_specs=[pl.BlockSpec((BLOCK, D), lambda i: (jnp.minimum(i, nb - 1), 0)),
                  pl.BlockSpec((BLOCK, D), lambda i: (0, 0)), vec, vec],
        out_specs=[row, row],
        out_shape=[jax.ShapeDtypeStruct((S + BLOCK, D), F32), jax.ShapeDtypeStruct((S + BLOCK, D), BF16)],
        sem=("parallel",))


def _ln_in_bwd(order, x, meta_pad, g, dh0, dz1):
    S, D = x.shape
    nb = S // BLOCK

    def body(x_ref, mp_ref, g_ref, dh_ref, dz_ref, dx_ref, dm_ref, dg_ref, db_ref):
        i = pl.program_id(0)
        is_meta = i == nb
        xin = jnp.where(is_meta, mp_ref[...], x_ref[...])
        xhat, rstd = _ln_stats(xin)
        dy = dh_ref[...] + jnp.where(is_meta, 0.0, DN_ALPHA) * dz_ref[...]
        dxin = _ln_bwd(dy, xhat, rstd, g_ref[...])

        @pl.when(i < nb)
        def _():
            dx_ref[...] = dxin

        @pl.when(is_meta)
        def _():
            dm_ref[...] = dxin

        @pl.when(i == 0)
        def _():
            dg_ref[...] = jnp.zeros_like(dg_ref)
            db_ref[...] = jnp.zeros_like(db_ref)

        dg_ref[...] += jnp.sum(dy * xhat, axis=0, keepdims=True)
        db_ref[...] += jnp.sum(dy, axis=0, keepdims=True)

    row = pl.BlockSpec((BLOCK, D), lambda i: (i, 0))
    rowx = pl.BlockSpec((BLOCK, D), lambda i: (jnp.minimum(i, nb - 1), 0))
    vec = pl.BlockSpec((1, D), lambda i: (0, 0))
    return _call(
        order, body, [x, meta_pad, g, dh0, dz1], name="ln_in_bwd", grid=(nb + 1,),
        in_specs=[rowx, pl.BlockSpec((BLOCK, D), lambda i: (0, 0)), vec, row, rowx],
        out_specs=[rowx, pl.BlockSpec((BLOCK, D), lambda i: (0, 0)), vec, vec],
        out_shape=[jax.ShapeDtypeStruct((S, D), F32), jax.ShapeDtypeStruct((BLOCK, D), F32),
                   jax.ShapeDtypeStruct((1, D), F32), jax.ShapeDtypeStruct((1, D), F32)],
        sem=("arbitrary",))


def _ln1_fwd(order, h0, y1, g, b):
    S, D = y1.shape
    tm = _pick(S, 2 * BLOCK, 8)

    def body(h_ref, y_ref, g_ref, b_ref, o_ref, ob_ref):
        xhat, _ = _ln_stats(DN_ALPHA * h_ref[...] + y_ref[...])
        y = xhat * g_ref[...] + b_ref[...]
        o_ref[...] = y
        ob_ref[...] = y.astype(BF16)

    row = pl.BlockSpec((tm, D), lambda i: (i, 0))
    vec = pl.BlockSpec((1, D), lambda i: (0, 0))
    return _call(
        order, body, [h0, y1, g, b], name="ln1_fwd", grid=(S // tm,), in_specs=[row, row, vec, vec],
        out_specs=[row, row],
        out_shape=[jax.ShapeDtypeStruct((S, D), F32), jax.ShapeDtypeStruct((S, D), BF16)],
        sem=("parallel",))


def _ln1_bwd(order, h0, y1, g, dh1, dz2):
    S, D = y1.shape
    tm = _pick(S, BLOCK, 8)

    def body(h_ref, y_ref, g_ref, dh_ref, dz2_ref, dz_ref, dzb_ref, dg_ref, db_ref):
        i = pl.program_id(0)
        xhat, rstd = _ln_stats(DN_ALPHA * h_ref[...] + y_ref[...])
        dy = dh_ref[...] + DN_ALPHA * dz2_ref[...]
        dz = _ln_bwd(dy, xhat, rstd, g_ref[...])
        dz_ref[...] = dz
        dzb_ref[...] = dz.astype(BF16)

        @pl.when(i == 0)
        def _():
            dg_ref[...] = jnp.zeros_like(dg_ref)
            db_ref[...] = jnp.zeros_like(db_ref)

        dg_ref[...] += jnp.sum(dy * xhat, axis=0, keepdims=True)
        db_ref[...] += jnp.sum(dy, axis=0, keepdims=True)

    row = pl.BlockSpec((tm, D), lambda i: (i, 0))
    vec = pl.BlockSpec((1, D), lambda i: (0, 0))
    return _call(
        order, body, [h0, y1, g, dh1, dz2], name="ln1_bwd", grid=(S // tm,),
        in_specs=[row, row, vec, row, row], out_specs=[row, row, vec, vec],
        out_shape=[jax.ShapeDtypeStruct((S, D), F32), jax.ShapeDtypeStruct((S, D), BF16),
                   jax.ShapeDtypeStruct((1, D), F32), jax.ShapeDtypeStruct((1, D), F32)],
        sem=("arbitrary",))


def _ln2_loss_bwd(order, h1, y2, target, g, b):
    S, D = y2.shape
    tm = _pick(S, 2 * BLOCK, 8)

    def body(h_ref, y_ref, t_ref, g_ref, b_ref, dz_ref, dzb_ref, dg_ref, db_ref, loss_ref):
        i = pl.program_id(0)
        xhat, rstd = _ln_stats(DN_ALPHA * h_ref[...] + y_ref[...])
        diff = xhat * g_ref[...] + b_ref[...] - t_ref[...]
        dy = diff / D
        dz = _ln_bwd(dy, xhat, rstd, g_ref[...])
        dz_ref[...] = dz
        dzb_ref[...] = dz.astype(BF16)

        @pl.when(i == 0)
        def _():
            dg_ref[...] = jnp.zeros_like(dg_ref)
            db_ref[...] = jnp.zeros_like(db_ref)
            loss_ref[...] = jnp.zeros_like(loss_ref)

        dg_ref[...] += jnp.sum(dy * xhat, axis=0, keepdims=True)
        db_ref[...] += jnp.sum(dy, axis=0, keepdims=True)
        loss_ref[...] += jnp.sum(jnp.mean(diff * diff, axis=-1, keepdims=True), axis=0, keepdims=True)

    row = pl.BlockSpec((tm, D), lambda i: (i, 0))
    vec = pl.BlockSpec((1, D), lambda i: (0, 0))
    one = pl.BlockSpec((1, 1), lambda i: (0, 0))
    return _call(
        order, body, [h1, y2, target, g, b], name="ln2_loss_bwd", grid=(S // tm,),
        in_specs=[row, row, row, vec, vec], out_specs=[row, row, vec, vec, one],
        out_shape=[jax.ShapeDtypeStruct((S, D), F32), jax.ShapeDtypeStruct((S, D), BF16),
                   jax.ShapeDtypeStruct((1, D), F32), jax.ShapeDtypeStruct((1, D), F32),
                   jax.ShapeDtypeStruct((1, 1), F32)],
        sem=("arbitrary",))


def _rope_table(S):
    r = jnp.arange(S + BLOCK)
    pos = jnp.where(r < S, r + N_META, jnp.maximum(r - (S + META_ROW0), 0))
    half = ROPE_DIM // 2
    lane = jnp.arange(LANES) % HEAD_DIM
    inv_freq = ROPE_THETA ** (-(lane % half).astype(F32) * 2.0 / ROPE_DIM)
    ang = pos.astype(F32)[:, None] * inv_freq[None, :]
    cos, sin = jnp.cos(ang), jnp.sin(ang)
    c = jnp.where(lane < ROPE_DIM, cos, 1.0)
    sa = jnp.where(lane < half, -sin, 0.0)
    sb = jnp.where((lane >= half) & (lane < ROPE_DIM), sin, 0.0)
    return jnp.concatenate([c, sa, sb], axis=1).astype(F32)


def _rope(x, tab):
    h = ROPE_DIM // 2
    return (x * tab[:, :LANES] + pltpu.roll(x, LANES - h, 1) * tab[:, LANES:2 * LANES]
            + pltpu.roll(x, h, 1) * tab[:, 2 * LANES:])


def _rope_t(dy, tab):
    h = ROPE_DIM // 2
    return (dy * tab[:, :LANES] + pltpu.roll(dy * tab[:, LANES:2 * LANES], h, 1)
            + pltpu.roll(dy * tab[:, 2 * LANES:], LANES - h, 1))


NKEY = N_META + 2 * BLOCK


def _attn_tiles(g, n, S, sink_ref, q_ref, k_ref, v_ref, tab_ref):
    NQG = Q_PER_KV // 2
    R = NQG * BLOCK
    halfsel = (g % 2).astype(F32)
    prev = jnp.maximum(n - 1, 0)
    qrow = pl.ds(pl.multiple_of(n * BLOCK, BLOCK), BLOCK)
    prow = pl.ds(pl.multiple_of(prev * BLOCK, BLOCK), BLOCK)
    mrow = pl.ds(S + META_ROW0, N_META)

    tq = tab_ref[qrow, :]
    qf = q_ref[...]
    q4 = jnp.concatenate([_rope(qf[:, LANES * p:LANES * (p + 1)], tq) for p in range(NQG)], axis=0).astype(BF16)

    tk = jnp.concatenate([tab_ref[mrow, :], tab_ref[prow, :], tq], axis=0)
    kr = _rope(jnp.concatenate([k_ref[mrow, :], k_ref[prow, :], k_ref[qrow, :]], axis=0), tk)
    vr = jnp.concatenate([v_ref[mrow, :], v_ref[prow, :], v_ref[qrow, :]], axis=0)

    lane = lax.broadcasted_iota(jnp.int32, kr.shape, 1)
    own = jnp.where(lane < HEAD_DIM, 1.0 - halfsel, halfsel)

    def lo_hi(t):
        mine = t * own
        other = pltpu.roll(mine, HEAD_DIM, 1)
        lo = mine * (1.0 - halfsel) + other * halfsel
        hi = other * (1.0 - halfsel) + mine * halfsel
        return lo.astype(BF16), hi.astype(BF16)

    klo, khi = lo_hi(kr)
    vlo, vhi = lo_hi(vr)

    jj = lax.broadcasted_iota(jnp.int32, (BLOCK, R), 0)
    qi = lax.broadcasted_iota(jnp.int32, (BLOCK, R), 1) & (BLOCK - 1)
    in_cur = jj <= qi
    band_ok = in_cur | (jj > qi + jnp.where(n >= 1, 0, 2 * BLOCK))

    def soft(kk, parity):
        sk = jnp.concatenate(
            [jnp.full((1, BLOCK), sink_ref[0, Q_PER_KV * g + 2 * p + parity], F32) for p in range(NQG)], axis=1)
        s = lax.dot_general(kk, q4, _DIMS["nt"], preferred_element_type=F32) * ATTN_SCALE
        band = jnp.where(in_cur, s[N_META + BLOCK:], s[N_META:N_META + BLOCK])
        s = jnp.concatenate([s[:N_META], jnp.where(band_ok, band, NEG_INF)], axis=0)
        m = jnp.maximum(jnp.max(s, axis=0, keepdims=True), sk)
        p = jnp.exp(s - m)
        es = jnp.exp(sk - m)
        inv = 1.0 / (jnp.sum(p, axis=0, keepdims=True) + es)
        return p * inv, es * inv

    pe, sink_e = soft(klo, 0)
    po, sink_o = soft(khi, 1)
    return q4, tk, (klo, khi), (vlo, vhi), (pe, po), (sink_e, sink_o), own, in_cur


def _spread(t, in_cur):
    band = t[N_META:]
    return jnp.concatenate([t[:N_META], jnp.where(in_cur, 0.0, band), jnp.where(in_cur, band, 0.0)], axis=0)


def _attn_specs(S, ATTN, KVW):
    Tp = S + BLOCK
    koff, voff = ATTN // LANES, (ATTN + KVW) // LANES
    gw = Q_PER_KV * HEAD_DIM
    return [pl.BlockSpec(memory_space=pltpu.SMEM),
            pl.BlockSpec((BLOCK, gw), lambda g, n: (n, g)),
            pl.BlockSpec((Tp, LANES), lambda g, n: (0, koff + g // 2)),
            pl.BlockSpec((Tp, LANES), lambda g, n: (0, voff + g // 2)),
            pl.BlockSpec((Tp, 3 * LANES), lambda g, n: (0, 0))]


def _attn_fwd(order, proj, tab, sinks, S, ATTN, KVW):
    G = KVW // HEAD_DIM
    nb = S // BLOCK
    gw = Q_PER_KV * HEAD_DIM

    def body(sink_ref, q_ref, k_ref, v_ref, tab_ref, o_ref):
        g, n = pl.program_id(0), pl.program_id(1)
        _, _, _, (vlo, vhi), (pe, po), _, _, in_cur = _attn_tiles(g, n, S, sink_ref, q_ref, k_ref, v_ref, tab_ref)
        o4 = (lax.dot_general(_spread(pe, in_cur).astype(BF16), vlo, _DIMS["tn"], preferred_element_type=F32)
              + lax.dot_general(_spread(po, in_cur).astype(BF16), vhi, _DIMS["tn"], preferred_element_type=F32))
        o_ref[...] = jnp.concatenate(
            [o4[BLOCK * p:BLOCK * (p + 1)] for p in range(Q_PER_KV // 2)], axis=1).astype(BF16)

    return _call(
        order, body, [sinks, proj, proj, proj, tab], name="attn_fwd", grid=(G, nb),
        in_specs=_attn_specs(S, ATTN, KVW),
        out_specs=pl.BlockSpec((BLOCK, gw), lambda g, n: (n, g)),
        out_shape=jax.ShapeDtypeStruct((S, ATTN), BF16),
        sem=("parallel", "arbitrary"))


def _attn_bwd(order, proj, tab, sinks, da, dproj, S, ATTN, KVW):
    G = KVW // HEAD_DIM
    nb = S // BLOCK
    Tp = S + BLOCK
    NQG = Q_PER_KV // 2
    gw = Q_PER_KV * HEAD_DIM

    def body(sink_ref, q_ref, k_ref, v_ref, tab_ref, da_ref, dproj_in, dq_ref, dk_ref, dv_ref, ds_ref):
        del dproj_in
        g, n = pl.program_id(0), pl.program_id(1)
        q4, tk, (klo, khi), (vlo, vhi), (pe, po), (sink_e, sink_o), own, in_cur = _attn_tiles(
            g, n, S, sink_ref, q_ref, k_ref, v_ref, tab_ref)
        dof = da_ref[...]
        do4 = jnp.concatenate([dof[:, LANES * p:LANES * (p + 1)] for p in range(NQG)], axis=0)

        def grads(p, vv):
            dp = lax.dot_general(vv, do4, _DIMS["nt"], preferred_element_type=F32)
            dp = jnp.concatenate(
                [dp[:N_META], jnp.where(in_cur, dp[N_META + BLOCK:], dp[N_META:N_META + BLOCK])], axis=0)
            delta = jnp.sum(p * dp, axis=0, keepdims=True)
            return _spread(p * (dp - delta) * ATTN_SCALE, in_cur).astype(BF16), delta

        dse, delta_e = grads(pe, vlo)
        dso, delta_o = grads(po, vhi)

        dq4 = (lax.dot_general(dse, klo, _DIMS["tn"], preferred_element_type=F32)
               + lax.dot_general(dso, khi, _DIMS["tn"], preferred_element_type=F32))
        tq = tk[N_META + BLOCK:]
        dq_ref[...] = jnp.concatenate(
            [_rope_t(dq4[BLOCK * p:BLOCK * (p + 1)], tq) for p in range(NQG)], axis=1).astype(BF16)

        lane = lax.broadcasted_iota(jnp.int32, (NKEY, LANES), 1)

        def fold(lo_part, hi_part):
            t = jnp.where(lane < HEAD_DIM, lo_part, hi_part)
            return t + pltpu.roll(t, HEAD_DIM, 1)

        dk = _rope_t(fold(jnp.dot(dse, q4, preferred_element_type=F32),
                          jnp.dot(dso, q4, preferred_element_type=F32)), tk) * own
        dv = fold(jnp.dot(_spread(pe, in_cur).astype(BF16), do4, preferred_element_type=F32),
                  jnp.dot(_spread(po, in_cur).astype(BF16), do4, preferred_element_type=F32)) * own

        @pl.when((n == 0) & (g % 2 == 0))
        def _():
            dk_ref[...] = jnp.zeros_like(dk_ref)
            dv_ref[...] = jnp.zeros_like(dv_ref)

        @pl.when(n == 0)
        def _():
            ds_ref[...] = jnp.zeros_like(ds_ref)

        prev = jnp.maximum(n - 1, 0)
        qrow = pl.ds(pl.multiple_of(n * BLOCK, BLOCK), BLOCK)
        prow = pl.ds(pl.multiple_of(prev * BLOCK, BLOCK), BLOCK)
        mrow = pl.ds(S + META_ROW0, N_META)
        for ref, val in ((dk_ref, dk), (dv_ref, dv)):
            ref[mrow, :] += val[:N_META]
            ref[prow, :] += val[N_META:N_META + BLOCK]
            ref[qrow, :] += val[N_META + BLOCK:]

        srow = lax.broadcasted_iota(jnp.int32, (Q_PER_KV, LANES), 0)
        acc = jnp.zeros((Q_PER_KV, LANES), F32)
        for p in range(NQG):
            for parity, (sk, dl) in enumerate(((sink_e, delta_e), (sink_o, delta_o))):
                val = -jnp.sum(sk[:, BLOCK * p:BLOCK * (p + 1)] * dl[:, BLOCK * p:BLOCK * (p + 1)])
                acc = jnp.where(srow == 2 * p + parity, val, acc)
        ds_ref[0] += acc

    in_specs = _attn_specs(S, ATTN, KVW) + [pl.BlockSpec((BLOCK, gw), lambda g, n: (n, g)), ANY]
    slab = pl.BlockSpec((Tp, LANES), lambda g, n: (0, g // 2))
    return _call(
        order, body, [sinks, proj, proj, proj, tab, da, dproj], name="attn_bwd", grid=(G, nb), in_specs=in_specs,
        out_specs=[pl.BlockSpec((BLOCK, gw), lambda g, n: (n, g)), slab, slab,
                   pl.BlockSpec((1, Q_PER_KV, LANES), lambda g, n: (g, 0, 0))],
        out_shape=[jax.ShapeDtypeStruct(dproj.shape, BF16), jax.ShapeDtypeStruct((Tp, KVW), F32),
                   jax.ShapeDtypeStruct((Tp, KVW), F32), jax.ShapeDtypeStruct((G, Q_PER_KV, LANES), F32)],
        aliases={6: 0}, sem=("arbitrary", "arbitrary"))


def _zero_meta_block(order, Tp, IN):
    tc = _pick(IN, 4096)

    def body(o_ref):
        o_ref[...] = jnp.zeros_like(o_ref)

    return _call(
        order, body, [], name="dproj_zero_meta", grid=(IN // tc,), in_specs=[],
        out_specs=pl.BlockSpec((BLOCK, tc), lambda j: (Tp // BLOCK - 1, j)),
        out_shape=jax.ShapeDtypeStruct((Tp, IN), BF16), sem=("parallel",))


def _put_dkv(order, dk, dv, dproj, ATTN):
    Tp, KVW = dk.shape
    nkb = KVW // LANES
    koff = ATTN // LANES

    def body(dk_ref, dv_ref, dproj_in, o_ref):
        del dproj_in
        t = pl.program_id(0)
        o_ref[...] = jnp.where(t < nkb, dk_ref[...], dv_ref[...]).astype(BF16)

    src = pl.BlockSpec((Tp, LANES), lambda t: (0, t % nkb))
    return _call(
        order, body, [dk, dv, dproj], name="dproj_put_dkv", grid=(2 * nkb,), in_specs=[src, src, ANY],
        out_specs=pl.BlockSpec((Tp, LANES), lambda t: (0, koff + t)),
        out_shape=jax.ShapeDtypeStruct(dproj.shape, BF16), aliases={2: 0}, sem=("parallel",))


HALO = 16


def _window_sums(x, up):
    n = x.shape[0]
    out = []
    s = x
    for k in (1, 2, 4, 8):
        s = s + pltpu.roll(s, (n - k) if up else k, 0)
        out.append(s)
    return out


def _pool_specs(S, ub, gw, tm):
    meta_halo = (S + BLOCK - HALO) // HALO

    def main(g):
        return pl.BlockSpec((tm, gw), lambda i: (i, ub + g))

    def halo(g):
        return pl.BlockSpec((HALO, gw), lambda i: (jnp.where(i == 0, meta_halo, i * (tm // HALO) - 1), ub + g))

    return [main(g) for g in range(4)] + [halo(g) for g in range(4)]


def _pooled(main_refs, halo_refs, g):
    x = jnp.concatenate([halo_refs[g][...], main_refs[g][...]], axis=0)
    s = _window_sums(x, up=False)[g]
    return (s[HALO:] * (1.0 / POOL_WINDOWS[g]) - x[HALO:]).astype(BF16)


def _pool_fwd(order, proj, wgrp, scale, S, uoff, POOL):
    gw = POOL // 4
    tm = BLOCK

    def body(*refs):
        main, halo = refs[:4], refs[4:8]
        w_ref, sc_ref, o_ref = refs[8:]
        for g in range(4):
            mixed = jnp.dot(_pooled(main, halo, g), w_ref[g], preferred_element_type=F32)
            o_ref[:, gw * g:gw * (g + 1)] = (mixed * sc_ref[:, gw * g:gw * (g + 1)]).astype(BF16)

    return _call(
        order, body, [proj] * 8 + [wgrp, scale], name="pool_fwd", grid=(S // tm,),
        in_specs=_pool_specs(S, uoff // gw, gw, tm) + [
            pl.BlockSpec((4, gw, gw), lambda i: (0, 0, 0)), pl.BlockSpec((1, POOL), lambda i: (0, 0))],
        out_specs=pl.BlockSpec((tm, POOL), lambda i: (i, 0)),
        out_shape=jax.ShapeDtypeStruct((S, POOL), BF16), sem=("parallel",))


def _pool_bwd_mix(order, proj, wgrp, scale, dps, S, uoff, POOL):
    gw = POOL // 4
    tm = BLOCK

    def body(*refs):
        main, halo = refs[:4], refs[4:8]
        w_ref, sc_ref, dps_ref, dpl_ref, dw_ref, dsc_ref = refs[8:]
        i = pl.program_id(0)

        @pl.when(i == 0)
        def _():
            dw_ref[...] = jnp.zeros_like(dw_ref)
            dsc_ref[...] = jnp.zeros_like(dsc_ref)

        for g in range(4):
            cols = slice(gw * g, gw * (g + 1))
            pooled = _pooled(main, halo, g)
            mixed = jnp.dot(pooled, w_ref[g], preferred_element_type=F32)
            dps_g = dps_ref[:, cols]
            dsc_ref[:, cols] += jnp.sum(dps_g * mixed, axis=0, keepdims=True)
            dms = (dps_g * sc_ref[:, cols]).astype(BF16)
            dw_ref[g] += lax.dot_general(pooled, dms, _DIMS["tn"], preferred_element_type=F32)
            dpl_ref[:, cols] = lax.dot_general(dms, w_ref[g], _DIMS["nt"], preferred_element_type=F32)

    row = pl.BlockSpec((tm, POOL), lambda i: (i, 0))
    return _call(
        order, body, [proj] * 8 + [wgrp, scale, dps], name="pool_bwd_mix", grid=(S // tm,),
        in_specs=_pool_specs(S, uoff // gw, gw, tm) + [
            pl.BlockSpec((4, gw, gw), lambda i: (0, 0, 0)), pl.BlockSpec((1, POOL), lambda i: (0, 0)), row],
        out_specs=[row, pl.BlockSpec((4, gw, gw), lambda i: (0, 0, 0)), pl.BlockSpec((1, POOL), lambda i: (0, 0))],
        out_shape=[jax.ShapeDtypeStruct((S, POOL), F32), jax.ShapeDtypeStruct((4, gw, gw), F32),
                   jax.ShapeDtypeStruct((1, POOL), F32)],
        sem=("arbitrary",))


def _pool_bwd_window(order, dpl, dproj, S, uoff, POOL):
    gw = POOL // 4
    nb = S // BLOCK
    ub = uoff // gw

    def body(main_ref, halo_ref, dproj_in, o_ref):
        del dproj_in
        b, g = pl.program_id(0), pl.program_id(1)
        main = jnp.where(b < nb, main_ref[...], 0.0)
        halo = jnp.where(b == nb - 1, 0.0, halo_ref[...])
        sums = _window_sums(jnp.concatenate([main, halo], axis=0), up=True)
        du = jnp.zeros((BLOCK, gw), F32)
        for k, w in enumerate(POOL_WINDOWS):
            du = jnp.where(g == k, sums[k][:BLOCK] * (1.0 / w), du)
        du = du - main
        row = lax.broadcasted_iota(jnp.int32, du.shape, 0)
        first_valid = jnp.where(b == nb, META_ROW0, 0)
        o_ref[...] = jnp.where(row >= first_valid, du, 0.0).astype(BF16)

    return _call(
        order, body, [dpl, dpl, dproj], name="pool_bwd_window", grid=(nb + 1, 4),
        in_specs=[pl.BlockSpec((BLOCK, gw), lambda b, g: (jnp.minimum(b, nb - 1), g)),
                  pl.BlockSpec((HALO, gw), lambda b, g: (
                      jnp.where(b == nb, 0, jnp.minimum((b + 1) * (BLOCK // HALO), S // HALO - 1)), g)),
                  ANY],
        out_specs=pl.BlockSpec((BLOCK, gw), lambda b, g: (b, ub + g)),
        out_shape=jax.ShapeDtypeStruct(dproj.shape, BF16), aliases={2: 0}, sem=("parallel", "parallel"))


def _sigmoid(x):
    return 1.0 / (1.0 + jnp.exp(-x))


def _gate_tiles(S, D, goff):
    tc = 512
    while goff % tc or D % tc:
        tc //= 2
    return _pick(S, 1024, 8), tc


def _gate_mix(order, proj, bgate, a_out, p_out, S, D, goff):
    tm, tc = _gate_tiles(S, D, goff)
    g0b, nd = goff // tc, D // tc

    def body(l0_ref, l1_ref, b_ref, a_ref, p_ref, o_ref):
        g0 = _sigmoid(l0_ref[...] + b_ref[0:1, :])
        g1 = _sigmoid(l1_ref[...] + b_ref[1:2, :])
        o_ref[...] = (g0 * a_ref[...] + g1 * p_ref[...]).astype(BF16)

    tile = pl.BlockSpec((tm, tc), lambda i, j: (i, j))
    return _call(
        order, body, [proj, proj, bgate, a_out, p_out], name="gate_mix", grid=(S // tm, nd),
        in_specs=[pl.BlockSpec((tm, tc), lambda i, j: (i, g0b + j)),
                  pl.BlockSpec((tm, tc), lambda i, j: (i, g0b + nd + j)),
                  pl.BlockSpec((2, tc), lambda i, j: (0, j)), tile, tile],
        out_specs=tile, out_shape=jax.ShapeDtypeStruct((S, D), BF16), sem=("parallel", "parallel"))


def _gate_bwd(order, proj, bgate, a_out, p_out, dmixed, dproj, S, D, goff):
    tm, tc = _gate_tiles(S, D, goff)
    g0b, nd, ni = goff // tc, D // tc, S // tm
    nsteps = nd * ni

    def body(l0_ref, l1_ref, b_ref, a_ref, p_ref, dm_ref, dproj_in, dap_ref, dproj_ref, db_ref, buf, sems):
        del dproj_in
        j, i = pl.program_id(0), pl.program_id(1)
        step = j * ni + i
        slot = step % 2

        def put(sl, br):
            col = pl.multiple_of((g0b + br * nd + j) * tc, tc)
            return pltpu.make_async_copy(
                buf.at[sl, br], dproj_ref.at[pl.ds(pl.multiple_of(i * tm, tm), tm), pl.ds(col, tc)], sems.at[sl, br])

        @pl.when(step >= 2)
        def _():
            put(slot, 0).wait()
            put(slot, 1).wait()

        @pl.when(i == 0)
        def _():
            db_ref[...] = jnp.zeros_like(db_ref)

        dm = dm_ref[...]
        for br, (l_ref, val_ref) in enumerate(((l0_ref, a_ref), (l1_ref, p_ref))):
            gate = _sigmoid(l_ref[...] + b_ref[br:br + 1, :])
            dap_ref[br] = (dm * gate).astype(BF16)
            dl = dm * val_ref[...] * gate * (1.0 - gate)
            buf[slot, br] = dl.astype(BF16)
            db_ref[br] += jnp.sum(dl, axis=0, keepdims=True)
            put(slot, br).start()

        @pl.when(step == nsteps - 1)
        def _():
            for sl in ((slot, 1 - slot) if nsteps > 1 else (slot,)):
                put(sl, 0).wait()
                put(sl, 1).wait()

    tile = pl.BlockSpec((tm, tc), lambda j, i: (i, j))
    return _call(
        order, body, [proj, proj, bgate, a_out, p_out, dmixed, dproj], name="gate_bwd", grid=(nd, ni),
        in_specs=[pl.BlockSpec((tm, tc), lambda j, i: (i, g0b + j)),
                  pl.BlockSpec((tm, tc), lambda j, i: (i, g0b + nd + j)),
                  pl.BlockSpec((2, tc), lambda j, i: (0, j)), tile, tile, tile, ANY],
        out_specs=[pl.BlockSpec((2, tm, tc), lambda j, i: (0, i, j)), ANY,
                   pl.BlockSpec((2, 1, tc), lambda j, i: (0, 0, j))],
        out_shape=[jax.ShapeDtypeStruct((2, S, D), BF16), jax.ShapeDtypeStruct(dproj.shape, BF16),
                   jax.ShapeDtypeStruct((2, 1, D), F32)],
        scratch=[pltpu.VMEM((2, 2, tm, tc), BF16), pltpu.SemaphoreType.DMA((2, 2))],
        aliases={6: 1}, sem=("arbitrary", "arbitrary"))


def _ffn_in_near(order, h, wT, FF):
    S, D = h.shape
    tm, tn = _pick(S, 2048), _pick(FF, 512)
    nj = FF // tn

    def body(h_ref, w_ref, f_ref):
        f_ref[...] = lax.dot_general(h_ref[...], w_ref[...], _DIMS["nt"], preferred_element_type=F32)

    return _call(
        order, body, [h, wT], name="ffn_in_near", grid=(S // tm, nj),
        in_specs=[pl.BlockSpec((tm, D), lambda i, j: (i, 0)),
                  pl.BlockSpec((tn, D), lambda i, j: (_x_half(False) * nj + j, 0))],
        out_specs=pl.BlockSpec((tm, tn), lambda i, j: (i, j)),
        out_shape=jax.ShapeDtypeStruct((S, FF), F32), sem=("parallel", "parallel"))


def _gate_up(near, far):
    near_is_gate = lax.axis_index("x") == 0
    return jnp.where(near_is_gate, near, far), jnp.where(near_is_gate, far, near)


def _ffn_in_far(order, h, wT, near):
    S, D = h.shape
    FF = near.shape[1]
    tm, tn = _pick(S, 1024), _pick(FF, 512)
    nj = FF // tn

    def body(h_ref, w_ref, near_ref, f_ref, act_ref):
        far = lax.dot_general(h_ref[...], w_ref[...], _DIMS["nt"], preferred_element_type=F32)
        f_ref[...] = far
        gt, up = _gate_up(near_ref[...], far)
        act_ref[...] = (gt * _sigmoid(gt) * up).astype(BF16)

    tile = pl.BlockSpec((tm, tn), lambda i, j: (i, j))
    return _call(
        order, body, [h, wT, near], name="ffn_in_far", grid=(S // tm, nj),
        in_specs=[pl.BlockSpec((tm, D), lambda i, j: (i, 0)),
                  pl.BlockSpec((tn, D), lambda i, j: (_x_half(True) * nj + j, 0)), tile],
        out_specs=[tile, tile],
        out_shape=[jax.ShapeDtypeStruct((S, FF), F32), jax.ShapeDtypeStruct((S, FF), BF16)],
        sem=("parallel", "parallel"))


def _d_act_swiglu(order, dy, wdown, near, far):
    S, D = dy.shape
    FF = wdown.shape[0]
    tm, tn = _pick(S, 1024), _pick(FF, 256)

    def body(dy_ref, w_ref, near_ref, far_ref, o_ref):
        d = lax.dot_general(dy_ref[...], w_ref[...], _DIMS["nt"], preferred_element_type=F32)
        gt, up = _gate_up(near_ref[...], far_ref[...])
        s = _sigmoid(gt)
        o_ref[0] = (d * up * s * (1.0 + gt * (1.0 - s))).astype(BF16)
        o_ref[1] = (d * gt * s).astype(BF16)

    tile = pl.BlockSpec((tm, tn), lambda i, j: (i, j))
    return _call(
        order, body, [dy, wdown, near, far], name="d_act", grid=(S // tm, FF // tn),
        in_specs=[pl.BlockSpec((tm, D), lambda i, j: (i, 0)), pl.BlockSpec((tn, D), lambda i, j: (j, 0)), tile, tile],
        out_specs=pl.BlockSpec((2, tm, tn), lambda i, j: (0, i, j)),
        out_shape=jax.ShapeDtypeStruct((2, S, FF), BF16), sem=("parallel", "parallel"))


def _place():
    return lax.axis_index("x"), lax.axis_index("y"), lax.axis_index("c")


def _xfer_start(order, name, bufs, copies):
    nb = len(bufs)
    n = len(copies([None] * nb, None))
    is_new = [isinstance(b, jax.ShapeDtypeStruct) for b in bufs]
    old = [b for b, fresh in zip(bufs, is_new) if not fresh]
    no = len(old)
    tok = [] if any(order.last is b for b in old) else [order.last]
    first_out = no + len(tok)

    def body(*refs):
        send, recv = refs[first_out:first_out + n], refs[first_out + n:first_out + 2 * n]
        token = refs[-1]
        given, made = iter(refs[:no]), iter(refs[first_out + 2 * n + no:-1])
        logical = [next(made) if fresh else next(given) for fresh in is_new]
        for i, (src, dst, dev) in enumerate(copies(logical, _place())):
            pltpu.make_async_remote_copy(src_ref=src, dst_ref=dst, send_sem=send[i], recv_sem=recv[i],
                                         device_id=dev, device_id_type=MESH).start()
        token[...] = jnp.zeros_like(token)

    fresh_shapes = [b for b, fresh in zip(bufs, is_new) if fresh]
    out = pl.pallas_call(
        body, name=name,
        out_shape=tuple([pltpu.SemaphoreType.DMA(())] * (2 * n)
                        + [pltpu.HBM(b.shape, b.dtype) for b in old + fresh_shapes]
                        + [jax.ShapeDtypeStruct((8, LANES), F32)]),
        in_specs=[HBM] * no + [ANY] * len(tok),
        out_specs=tuple([SEM] * (2 * n) + [HBM] * nb + [pl.BlockSpec(memory_space=pltpu.VMEM)]),
        input_output_aliases={i: 2 * n + i for i in range(no)},
        compiler_params=pltpu.CompilerParams(has_side_effects=EFFECT),
    )(*[pltpu.with_memory_space_constraint(b, pltpu.HBM) for b in old], *tok)
    order.last = out[-1]
    thru, made = iter(out[2 * n:2 * n + no]), iter(out[2 * n + no:2 * n + nb])
    return list(out[:2 * n]), [next(made) if fresh else next(thru) for fresh in is_new]


def _xfer_wait(order, name, sems, bufs, copies):
    nb = len(bufs)
    n = len(sems) // 2
    tok = order.last

    def body(*refs):
        send, recv = refs[nb:nb + n], refs[nb + n:nb + 2 * n]
        token = refs[-1]
        for i, (src, dst, dev) in enumerate(copies(refs[:nb], _place())):
            cp = pltpu.make_async_remote_copy(src_ref=src, dst_ref=dst, send_sem=send[i], recv_sem=recv[i],
                                              device_id=dev, device_id_type=MESH)
            cp.wait_send()
            cp.wait_recv()
        token[...] = jnp.zeros_like(token)

    out = pl.pallas_call(
        body, name=name,
        out_shape=tuple([pltpu.HBM(b.shape, b.dtype) for b in bufs] + [jax.ShapeDtypeStruct((8, LANES), F32)]),
        in_specs=[HBM] * nb + [SEM] * (2 * n) + [ANY],
        out_specs=tuple([HBM] * nb + [pl.BlockSpec(memory_space=pltpu.VMEM)]),
        input_output_aliases={i: i for i in range(nb)},
        compiler_params=pltpu.CompilerParams(has_side_effects=EFFECT),
    )(*bufs, *sems, tok)
    order.last = out[-1]
    return list(out[:nb])


class _Xfer:
    def __init__(self, name, bufs, copies):
        self.name, self.bufs, self.copies = name, list(bufs), copies
        self.sems = None

    def start(self, order):
        self.sems, self.bufs = _xfer_start(order, self.name + "_start", self.bufs, self.copies)

    def wait(self, order, bufs=None):
        self.bufs = _xfer_wait(order, self.name + "_wait", self.sems, bufs or self.bufs, self.copies)
        return self.bufs


def _block_rows(ref, r, d):
    return ref.at[pl.ds(d * r, r)]


NEAR = ("xn", "yn")
ALL_CHIPS = ("xn", "yn", "diag")


def _chip_of(which, x, y):
    return {"xn": (1 - x, y), "yn": (x, 1 - y), "diag": (1 - x, 1 - y)}[which]


def _gather_send(fulls, chips=ALL_CHIPS, sibling=True):
    def copies(refs, place):
        out = []
        for w, full in enumerate(fulls):
            r = full.shape[0] // 8
            if place is None:
                out += [None] * (len(chips) + int(sibling))
                continue
            x, y, c = place
            mine = _block_rows(refs[w], r, 4 * x + 2 * y + c)
            if sibling:
                out.append((mine, mine, (x, y, 1 - c)))
            for which in chips:
                out.append((mine, mine, (*_chip_of(which, x, y), c)))
        return out
    return copies


def _gather_forward(fulls, chips=ALL_CHIPS):
    def copies(refs, place):
        out = []
        for w, full in enumerate(fulls):
            r = full.shape[0] // 8
            if place is None:
                out += [None] * len(chips)
                continue
            x, y, c = place
            for which in chips:
                px, py = _chip_of(which, x, y)
                blk = _block_rows(refs[w], r, 4 * px + 2 * py + c)
                out.append((blk, blk, (x, y, 1 - c)))
        return out
    return copies


def _pair_send(nw):
    def copies(refs, place):
        out = []
        for w in range(nw):
            if place is None:
                out += [None] * 4
                continue
            x, y, c = place
            grad, other = refs[2 * w], refs[2 * w + 1]
            r = other.shape[1]
            for k in range(4):
                out.append((_block_rows(grad, r, 2 * k + 1 - c), other.at[k], (x, y, 1 - c)))
        return out
    return copies


def _chip_send(nw):
    def copies(refs, place):
        out = []
        for w in range(nw):
            if place is None:
                out += [None] * 3
                continue
            x, y, c = place
            psum, parts = refs[2 * w], refs[2 * w + 1]
            for px, py in ((1 - x, y), (x, 1 - y), (1 - x, 1 - y)):
                out.append((psum.at[2 * px + py], parts.at[2 * x + y], (px, py, c)))
        return out
    return copies


def _dev_index():
    x, y, c = _place()
    return 4 * x + 2 * y + c


def _place_own(order, shard, name):
    r, cols = shard.shape
    tr = _pick(r, max(16, (12 << 20) // (4 * cols)), 16)
    nr = r // tr

    def body(s_ref, o_ref):
        o_ref[...] = s_ref[...].astype(BF16)

    return _call(
        order, body, [shard], name=name, grid=(nr,),
        in_specs=[pl.BlockSpec((tr, cols), lambda i: (i, 0))],
        out_specs=pl.BlockSpec((tr, cols), lambda i: (_dev_index() * nr + i, 0)),
        out_shape=jax.ShapeDtypeStruct((8 * r, cols), BF16), sem=("parallel",))


def _pair_sum(order, grad, other, name):
    r, cols = other.shape[1:]
    tr = _pick(r, max(16, (7 << 20) // (2 * cols)), 16)
    nr = r // tr

    def body(g_ref, a_ref, o_ref):
        o_ref[...] = (g_ref[...].astype(F32) + a_ref[...].astype(F32)).astype(BF16)

    blk = pl.BlockSpec((None, tr, cols), lambda k, i: (k, i, 0))
    return _call(
        order, body, [grad, other], name=name, grid=(4, nr),
        in_specs=[pl.BlockSpec((tr, cols), lambda k, i: ((2 * k + lax.axis_index("c")) * nr + i, 0)), blk],
        out_specs=blk, out_shape=jax.ShapeDtypeStruct(other.shape, BF16), sem=("parallel", "parallel"))


def _chip_sum(order, psum, parts, name):
    _, r, cols = parts.shape
    tr = _pick(r, max(16, (1 << 20) // (2 * cols)), 16)

    def my_chip():
        return 2 * lax.axis_index("x") + lax.axis_index("y")

    def body(own_ref, p0, p1, p2, p3, o_ref):
        own = own_ref[...].astype(F32)
        acc = None
        for k, p in enumerate((p0, p1, p2, p3)):
            term = jnp.where(my_chip() == k, own, p[...].astype(F32))
            acc = term if acc is None else acc + term
        o_ref[...] = acc

    def slot(k):
        return pl.BlockSpec((None, tr, cols), lambda i: (jnp.where(my_chip() == k, (k + 1) % 4, k), i, 0))

    return _call(
        order, body, [psum, parts, parts, parts, parts], name=name, grid=(r // tr,),
        in_specs=[pl.BlockSpec((None, tr, cols), lambda i: (my_chip(), i, 0))] + [slot(k) for k in range(4)],
        out_specs=pl.BlockSpec((tr, cols), lambda i: (i, 0)),
        out_shape=jax.ShapeDtypeStruct((r, cols), F32), sem=("parallel",))


def _all_reduce_small(order, pack, name):
    R = pack.shape[0]

    def body(p_ref, o_ref, buf, send_sems, recv_sems):
        x, y, c = _place()
        me = 4 * x + 2 * y + c
        buf[me] = p_ref[...]
        copies = []
        for k in range(1, 8):
            px = 1 - x if k & 4 else x
            py = 1 - y if k & 2 else y
            pc = 1 - c if k & 1 else c
            cp = pltpu.make_async_remote_copy(
                src_ref=p_ref, dst_ref=buf.at[me], send_sem=send_sems.at[k - 1], recv_sem=recv_sems.at[k - 1],
                device_id=(px, py, pc), device_id_type=MESH)
            cp.start()
            copies.append(cp)
        for cp in copies:
            cp.wait_recv()
        acc = buf[0]
        for d in range(1, 8):
            acc = acc + buf[d]
        o_ref[...] = acc
        for cp in copies:
            cp.wait_send()

    vm = pl.BlockSpec(memory_space=pltpu.VMEM)
    return _call(
        order, body, [pack], name=name, in_specs=[vm], out_specs=vm,
        out_shape=jax.ShapeDtypeStruct((R, LANES), F32),
        scratch=[pltpu.VMEM((8, R, LANES), F32), pltpu.SemaphoreType.DMA((7,)), pltpu.SemaphoreType.DMA((7,))])


def _pack(parts):
    flat = []
    for p in parts:
        v = p.reshape(-1).astype(F32)
        flat.append(jnp.pad(v, (0, (-v.shape[0]) % LANES)))
    v = jnp.concatenate(flat)
    v = jnp.pad(v, (0, (-v.shape[0]) % (8 * LANES)))
    return v.reshape(-1, LANES)


def _unpack(pack, shapes):
    v = pack.reshape(-1)
    out, off = [], 0
    for s in shapes:
        n = 1
        for d in s:
            n *= d
        out.append(v[off:off + n].reshape(s))
        off += n + (-n) % LANES
    return out


def _adamw(order, w, g, m, v, name):
    shape = w.shape
    cols = shape[-1]
    w2, g2, m2, v2 = (t.reshape(-1, cols) for t in (w, g, m, v))
    R = w2.shape[0]
    tr = _pick(R, max(8, (1 << 20) // (4 * cols)), 8)

    def body(w_ref, g_ref, m_ref, v_ref, d_ref, mo_ref, vo_ref):
        d_ref[...], mo_ref[...], vo_ref[...] = _adam_math(w_ref[...], g_ref[...], m_ref[...], v_ref[...])

    blk = pl.BlockSpec((tr, cols), lambda i: (i, 0))
    outs = _call(
        order, body, [w2, g2, m2, v2], name=name, grid=(R // tr,), in_specs=[blk] * 4, out_specs=[blk] * 3,
        out_shape=[jax.ShapeDtypeStruct((R, cols), F32)] * 3, sem=("parallel",))
    return tuple(o.reshape(shape) for o in outs)


def _adam_math(w, g, m, v):
    mn = ADAM_B1 * m + (1.0 - ADAM_B1) * g
    vn = ADAM_B2 * v + (1.0 - ADAM_B2) * (g * g)
    m_hat = mn / (1.0 - ADAM_B1 ** ADAM_STEP)
    v_hat = vn / (1.0 - ADAM_B2 ** ADAM_STEP)
    return -ADAM_LR * (m_hat / (jnp.sqrt(v_hat) + ADAM_EPS) + ADAM_WD * w), mn, vn


def _chip_sum_adamw(order, w, psum, parts, m, v, name):
    _, r, cols = parts.shape
    tr = _pick(r, max(16, (24 << 20) // (38 * cols)), 16)

    def my_chip():
        return 2 * lax.axis_index("x") + lax.axis_index("y")

    def body(w_ref, own_ref, p0, p1, p2, p3, m_ref, v_ref, g_ref, d_ref, mo_ref, vo_ref):
        own = own_ref[...].astype(F32)
        g = None
        for k, p in enumerate((p0, p1, p2, p3)):
            term = jnp.where(my_chip() == k, own, p[...].astype(F32))
            g = term if g is None else g + term
        g_ref[...] = g
        d_ref[...], mo_ref[...], vo_ref[...] = _adam_math(w_ref[...], g, m_ref[...], v_ref[...])

    def slot(k):
        return pl.BlockSpec((None, tr, cols), lambda i: (jnp.where(my_chip() == k, (k + 1) % 4, k), i, 0))

    blk = pl.BlockSpec((tr, cols), lambda i: (i, 0))
    return _call(
        order, body, [w, psum, parts, parts, parts, parts, m, v], name=name, grid=(r // tr,),
        in_specs=[blk, pl.BlockSpec((None, tr, cols), lambda i: (my_chip(), i, 0))]
        + [slot(k) for k in range(4)] + [blk, blk],
        out_specs=[blk] * 4, out_shape=[jax.ShapeDtypeStruct((r, cols), F32)] * 4, sem=("parallel",))


class _GradReduce:
    def __init__(self, tag, grads, names):
        self.tag, self.grads, self.names = tag, list(grads), names
        self.pair = self.chip = self.psums = None

    def pair_start(self, order):
        bufs = []
        for g in self.grads:
            bufs += [g, jax.ShapeDtypeStruct((4, g.shape[0] // 8, g.shape[1]), g.dtype)]
        self.pair = _Xfer("pair_" + self.tag, bufs, _pair_send(len(self.grads)))
        self.pair.start(order)

    def pair_sum_chip_start(self, order):
        bufs = self.pair.wait(order)
        self.psums = [_pair_sum(order, bufs[2 * w], bufs[2 * w + 1], "pair_sum_" + nm)
                      for w, nm in enumerate(self.names)]
        cbufs = []
        for p in self.psums:
            cbufs += [p, jax.ShapeDtypeStruct(p.shape, p.dtype)]
        self.chip = _Xfer("chip_" + self.tag, cbufs, _chip_send(len(self.psums)))
        self.chip.start(order)

    def finish(self, order):
        bufs = self.chip.wait(order)
        return [(bufs[2 * w], bufs[2 * w + 1]) for w in range(len(self.names))]


def kernel(x, meta_tokens, ln_in_g, ln_in_b, w_in, b_gate, attn_sinks, w_attn_up, w_pool_grp, pool_scale, w_pool_up, w_out, ln1_g, ln1_b, w_ffn_in, w_ffn_down, ln2_g, ln2_b, loss_target, m_meta_tokens, m_ln_in_g, m_ln_in_b, m_w_in, m_b_gate, m_attn_sinks, m_w_attn_up, m_w_pool_grp, m_pool_scale, m_w_pool_up, m_w_out, m_ln1_g, m_ln1_b, m_w_ffn_in, m_w_ffn_down, m_ln2_g, m_ln2_b, v_meta_tokens, v_ln_in_g, v_ln_in_b, v_w_in, v_b_gate, v_attn_sinks, v_w_attn_up, v_w_pool_grp, v_pool_scale, v_w_pool_up, v_w_out, v_ln1_g, v_ln1_b, v_w_ffn_in, v_w_ffn_down, v_ln2_g, v_ln2_b):
    S, D = x.shape[1], x.shape[2]
    Tp = S + BLOCK
    NQ = attn_sinks.shape[-1]
    ATTN = NQ * HEAD_DIM
    KVW = ATTN // Q_PER_KV
    POOL = pool_scale.shape[-1]
    IN = 8 * w_in.shape[2]
    FF = 8 * w_ffn_down.shape[1]
    uoff = ATTN + 2 * KVW
    goff = uoff + POOL
    gw = POOL // 4
    dcols = D // 8
    assert IN == goff + 2 * D and w_ffn_in.shape[2] * 8 == 2 * FF

    xi, yi, ci = _place()
    dev = 4 * xi + 2 * yi + ci
    x2, tgt = x[0], loss_target[0]
    order = _Order()

    def place_cols(a):
        return lax.dynamic_update_slice(jnp.zeros(a.shape[:-1] + (D,), F32), a, (0,) * (a.ndim - 1) + (dev * dcols,))

    small = _all_reduce_small(order, _pack([place_cols(meta_tokens), place_cols(b_gate[0])]), "small_inputs_gather")
    meta_full, bgate_full = _unpack(small, [(N_META, D), (2, D)])
    meta_pad = jnp.pad(meta_full, ((META_ROW0, 0), (0, 0)))

    wgrp_rows = w_pool_grp[0].reshape(4 * (gw // 8), gw)
    full_in = _place_own(order, w_in[0].T, "own_w_in")
    g_in = _Xfer("gather_w_in_near", [full_in], _gather_send([full_in], NEAR, sibling=False))
    g_in.start(order)
    s_in = _Xfer("gather_w_in_sibling", g_in.bufs, _gather_send([full_in], (), sibling=True))
    s_in.start(order)
    mix_names = ["w_attn_up", "w_pool_grp", "w_pool_up", "w_out"]
    mix_shards = [w_attn_up[0].T, wgrp_rows, w_pool_up[0].T, w_out[0]]
    full_mix = [_place_own(order, s, "own_" + nm) for s, nm in zip(mix_shards, mix_names)]
    full_ffn = _place_own(order, w_ffn_in[0].T, "own_w_ffn_in")

    ln_in_g2, ln_in_b2 = ln_in_g.reshape(1, D), ln_in_b.reshape(1, D)
    tab = _rope_table(S)

    h0, h0b = _ln_in_fwd(order, x2, meta_pad, ln_in_g2, ln_in_b2)
    bufs = s_in.wait(order)
    proj = _mm_nt_half(order, h0b, bufs[0], parts=4, which=lambda: _chip_quarter(False), tm=1408, tn=640,
                       name="proj_own")
    bufs = g_in.wait(order, bufs)
    d_in = _Xfer("gather_w_in_diag", bufs, _gather_send(bufs, ("diag",), sibling=False))
    d_in.start(order)
    f_in = _Xfer("forward_w_in_near", d_in.bufs, _gather_forward(bufs, NEAR))
    f_in.start(order)
    full_down = _place_own(order, w_ffn_down[0], "own_w_ffn_down")
    bufs = f_in.wait(order)
    proj = _mm_nt_half(order, h0b, bufs[0], parts=4, which=lambda: _chip_quarter(True), tm=1408, tn=640,
                       name="proj_yn", into=proj)
    bufs = d_in.wait(order, bufs)
    fd_in = _Xfer("forward_w_in_diag", bufs, _gather_forward(bufs, ("diag",)))
    fd_in.start(order)
    ag_mix = _Xfer("gather_mixers", full_mix, _gather_send(full_mix))
    ag_mix.start(order)
    g_ffn = _Xfer("gather_w_ffn_in_near", [full_ffn], _gather_send([full_ffn], NEAR))
    g_ffn.start(order)
    (winT,) = fd_in.wait(order)
    proj = _mm_nt_half(order, h0b, winT, far=True, tm=704, tn=1280, name="proj_far", into=proj)

    att = _attn_fwd(order, proj, tab, attn_sinks, S, ATTN, KVW)
    full_mix = ag_mix.wait(order)
    fw_mix = _Xfer("forward_mixers", full_mix, _gather_forward(full_mix))
    fw_mix.start(order)
    wattT, wgrp_g, wpupT, wout = fw_mix.wait(order)
    wgrp = wgrp_g.reshape(8, 4, gw // 8, gw).transpose(1, 0, 2, 3).reshape(4, gw, gw)

    ps = _pool_fwd(order, proj, wgrp, pool_scale, S, uoff, POOL)
    a_out = _mm(order, att, wattT, kind="nt", out_dtype=F32, tm=1024, tn=1024, name="attn_up")
    p_out = _mm(order, ps, wpupT, kind="nt", out_dtype=F32, tm=1024, tn=1024, name="pool_up")
    mixed = _gate_mix(order, proj, bgate_full, a_out, p_out, S, D, goff)
    y1 = _mm(order, mixed, wout, kind="nn", out_dtype=F32, tm=1024, tn=1024, name="out_proj")

    bufs = g_ffn.wait(order)
    d_ffn = _Xfer("gather_w_ffn_in_diag", bufs, _gather_send(bufs, ("diag",), sibling=False))
    d_ffn.start(order)
    f_ffn = _Xfer("forward_w_ffn_in_near", d_ffn.bufs, _gather_forward(bufs, NEAR))
    f_ffn.start(order)
    h1, h1b = _ln1_fwd(order, h0, y1, ln1_g, ln1_b)
    bufs = f_ffn.wait(order)
    f_near = _ffn_in_near(order, h1b, bufs[0], FF)
    bufs = d_ffn.wait(order, bufs)
    fd_ffn = _Xfer("forward_w_ffn_in_diag", bufs, _gather_forward(bufs, ("diag",)))
    fd_ffn.start(order)
    ag_down = _Xfer("gather_w_ffn_down", [full_down], _gather_send([full_down]))
    ag_down.start(order)
    (wffnT,) = fd_ffn.wait(order)
    f_far, act = _ffn_in_far(order, h1b, wffnT, f_near)

    (full_down,) = ag_down.wait(order)
    fw_down = _Xfer("forward_w_ffn_down", [full_down], _gather_forward([full_down]))
    fw_down.start(order)
    (wdown,) = fw_down.wait(order)
    y2 = _mm(order, act, wdown, kind="nn", out_dtype=F32, tm=1024, tn=1024, tk=5504, name="ffn_down")

    dz2, dz2b, dg2, db2, loss_part = _ln2_loss_bwd(order, h1, y2, tgt, ln2_g, ln2_b)
    df = _d_act_swiglu(order, dz2b, wdown, f_near, f_far)
    gwdown = _mm(order, act, dz2b, kind="tn", out_dtype=BF16, tm=256, tn=2048, name="d_ffn_down")
    rs_down = _GradReduce("w_ffn_down", [gwdown], ["w_ffn_down"])
    rs_down.pair_start(order)
    gwffnT = _mm(order, df, h1b, kind="tn", out_dtype=BF16, tm=256, tn=2048, name="d_ffn_in", a_lead="halves")
    rs_down.pair_sum_chip_start(order)
    rs_ffn = _GradReduce("w_ffn_in", [gwffnT], ["w_ffn_in"])
    rs_ffn.pair_start(order)
    dh1 = _mm(order, df, wffnT, kind="nn", out_dtype=F32, tm=1024, tn=1024, tk=5504, name="d_h1", a_lead="halves")
    rs_ffn.pair_sum_chip_start(order)
    dz1, dz1b, dg1, db1 = _ln1_bwd(order, h0, y1, ln1_g, dh1, dz2)
    gwout = _mm(order, mixed, dz1b, kind="tn", out_dtype=BF16, tm=512, tn=1024, name="d_out_proj")
    rs_out = _GradReduce("w_out", [gwout], ["w_out"])
    rs_out.pair_start(order)
    dmixed = _mm(order, dz1b, wout, kind="nt", out_dtype=F32, tm=1024, tn=1024, name="d_mixed")
    rs_out.pair_sum_chip_start(order)

    dproj = _zero_meta_block(order, Tp, IN)
    dap, dproj, dbgate = _gate_bwd(order, proj, bgate_full, a_out, p_out, dmixed, dproj, S, D, goff)
    gwattT = _mm(order, dap, att, kind="tn", out_dtype=BF16, tm=512, tn=1024, name="d_attn_up", a_lead=0)
    datt = _mm(order, dap, wattT, kind="nn", out_dtype=BF16, tm=1024, tn=1024, name="d_att", a_lead=0)
    gwpupT = _mm(order, dap, ps, kind="tn", out_dtype=BF16, tm=512, tn=1024, name="d_pool_up", a_lead=1)
    dps = _mm(order, dap, wpupT, kind="nn", out_dtype=F32, tm=1024, tn=1024, name="d_ps", a_lead=1)
    dpl, gwgrp, dscale = _pool_bwd_mix(order, proj, wgrp, pool_scale, dps, S, uoff, POOL)
    gwgrp_rows = gwgrp.reshape(4, 8, gw // 8, gw).transpose(1, 0, 2, 3).reshape(8 * 4 * (gw // 8), gw).astype(BF16)
    rs_mix = _GradReduce("mixers", [gwattT, gwgrp_rows, gwpupT], ["w_attn_up", "w_pool_grp", "w_pool_up"])
    rs_mix.pair_start(order)
    dproj = _pool_bwd_window(order, dpl, dproj, S, uoff, POOL)
    rs_mix.pair_sum_chip_start(order)
    dproj, dk, dv, dsink = _attn_bwd(order, proj, tab, attn_sinks, datt, dproj, S, ATTN, KVW)
    dproj = _put_dkv(order, dk, dv, dproj, ATTN)

    weights = dict(meta_tokens=meta_tokens, ln_in_g=ln_in_g, ln_in_b=ln_in_b, w_in=w_in, b_gate=b_gate,
                   attn_sinks=attn_sinks, w_attn_up=w_attn_up, w_pool_grp=w_pool_grp, pool_scale=pool_scale,
                   w_pool_up=w_pool_up, w_out=w_out, ln1_g=ln1_g, ln1_b=ln1_b, w_ffn_in=w_ffn_in,
                   w_ffn_down=w_ffn_down, ln2_g=ln2_g, ln2_b=ln2_b)
    ms = dict(meta_tokens=m_meta_tokens, ln_in_g=m_ln_in_g, ln_in_b=m_ln_in_b, w_in=m_w_in, b_gate=m_b_gate,
              attn_sinks=m_attn_sinks, w_attn_up=m_w_attn_up, w_pool_grp=m_w_pool_grp, pool_scale=m_pool_scale,
              w_pool_up=m_w_pool_up, w_out=m_w_out, ln1_g=m_ln1_g, ln1_b=m_ln1_b, w_ffn_in=m_w_ffn_in,
              w_ffn_down=m_w_ffn_down, ln2_g=m_ln2_g, ln2_b=m_ln2_b)
    vs = dict(meta_tokens=v_meta_tokens, ln_in_g=v_ln_in_g, ln_in_b=v_ln_in_b, w_in=v_w_in, b_gate=v_b_gate,
              attn_sinks=v_attn_sinks, w_attn_up=v_w_attn_up, w_pool_grp=v_w_pool_grp, pool_scale=v_pool_scale,
              w_pool_up=v_w_pool_up, w_out=v_w_out, ln1_g=v_ln1_g, ln1_b=v_ln1_b, w_ffn_in=v_w_ffn_in,
              w_ffn_down=v_w_ffn_down, ln2_g=v_ln2_g, ln2_b=v_ln2_b)
    grads, deltas, new_ms, new_vs = {}, {}, {}, {}

    def update(nm, g):
        g = g.reshape(weights[nm].shape)
        grads[nm] = g
        deltas[nm], new_ms[nm], new_vs[nm] = _adamw(order, weights[nm], g, ms[nm], vs[nm], "adamw_" + nm)

    def update_reduced(nm, bufs, transposed=False):
        psum, parts = bufs
        if transposed:
            to2d, back = (lambda t: t[0].T), (lambda t: t.T[None])
        else:
            to2d, back = (lambda t: t.reshape(parts.shape[1:])), (lambda t: t.reshape(weights[nm].shape))
        outs = _chip_sum_adamw(order, to2d(weights[nm]), psum, parts, to2d(ms[nm]), to2d(vs[nm]), "adamw_" + nm)
        grads[nm], deltas[nm], new_ms[nm], new_vs[nm] = (back(t) for t in outs)

    gwinT = _mm(order, dproj, h0b, kind="tn", out_dtype=BF16, tm=512, tn=1024, name="d_w_in")
    rs_in = _GradReduce("w_in", [gwinT], ["w_in"])
    rs_in.pair_start(order)
    update_reduced("w_ffn_down", rs_down.finish(order)[0])
    rs_in.pair_sum_chip_start(order)
    dh0 = _mm(order, dproj, winT, kind="nn", out_dtype=F32, tm=1408, tn=1024, tk=2560, name="d_h0")
    dx, dmeta_block, dg_in, db_in = _ln_in_bwd(order, x2, meta_pad, ln_in_g2, dh0, dz1)
    grad_x = dx[None]
    dmeta = dmeta_block[META_ROW0:]

    small_shapes = [(D,), (D,), (1, D), (1, D), (1, D), (1, D), (1, POOL), (1, NQ), (), (N_META, D), (2, D)]
    red = _all_reduce_small(order, _pack([dg_in, db_in, dg1, db1, dg2, db2, dscale, dsink[:, :, 0], loss_part,
                                          dmeta, dbgate]), "small_grads_all_reduce")

    update_reduced("w_ffn_in", rs_ffn.finish(order)[0], transposed=True)
    update_reduced("w_out", rs_out.finish(order)[0])
    b_att, b_grp, b_pup = rs_mix.finish(order)
    update("w_attn_up", _chip_sum(order, *b_att, "chip_sum_w_attn_up").T)
    update_reduced("w_pool_grp", b_grp)
    update("w_pool_up", _chip_sum(order, *b_pup, "chip_sum_w_pool_up").T)

    (g_ln_in_g, g_ln_in_b, g_ln1_g, g_ln1_b, g_ln2_g, g_ln2_b, g_scale, g_sinks, loss_sum, g_meta_full,
     g_bgate_full) = _unpack(red, small_shapes)
    loss = 0.5 * loss_sum
    update("meta_tokens", lax.dynamic_slice(g_meta_full, (0, dev * dcols), (N_META, dcols)))
    update("b_gate", lax.dynamic_slice(g_bgate_full, (0, dev * dcols), (2, dcols)))
    for nm, g in (("ln_in_g", g_ln_in_g), ("ln_in_b", g_ln_in_b), ("ln1_g", g_ln1_g), ("ln1_b", g_ln1_b),
                  ("ln2_g", g_ln2_g), ("ln2_b", g_ln2_b), ("pool_scale", g_scale), ("attn_sinks", g_sinks)):
        update(nm, g)

    update_reduced("w_in", rs_in.finish(order)[0], transposed=True)

    names = list(weights)
    return (loss, grad_x, *[grads[n] for n in names], *[deltas[n] for n in names],
            *[new_ms[n] for n in names], *[new_vs[n] for n in names])
```
